```python
import math
import jax, jax.numpy as jnp
from jax import lax
import numpy as np

D_MODEL = 1024
BATCH = 16
SEQ = 4096
DEPTH = 2

CHUNK = 64
N_MIXERS = 2
DA_HEADS = 8
DA_HEAD_DIM = 64
DA_V_DIM = 2 * DA_HEAD_DIM
DA_PROJ = 3 * DA_HEADS * 2 * DA_HEAD_DIM
Q_BLOCK = 128
GLA_HEADS = 4
GLA_KEY_WIDTH = D_MODEL // 2
GLA_VAL_WIDTH = D_MODEL
GLA_DK = GLA_KEY_WIDTH // GLA_HEADS
GLA_DV = GLA_VAL_WIDTH // GLA_HEADS
GLA_PROJ = 2 * GLA_KEY_WIDTH + 2 * GLA_VAL_WIDTH
GLA_GATE_RANK = 16
GLA_TAU = 16.0
FFN_DIM = 2816
N_EXPERTS = 8
TOP_K = 2
EXPERT_DIM = 3584
EXPERT_BLOCK = 256
NORM_EPS = 1e-6

kernel_name = "hybrid_diffattn_gla_moe_adaln_encoder"


def rms_norm(x, g):
    xf = x.astype(jnp.float32)
    y = xf * lax.rsqrt(jnp.mean(xf * xf, axis=-1, keepdims=True) + NORM_EPS)
    return (y * g.astype(jnp.float32)).astype(x.dtype)


def alibi_slopes(n_heads):
    return 2.0 ** (-8.0 * jnp.arange(1, n_heads + 1, dtype=jnp.float32) / n_heads)


def swiglu(x, w1, w3, w2):
    return (jax.nn.silu(x @ w1) * (x @ w3)) @ w2


def diff_attention(h, w_in, q_gain, k_gain, lam_q1, lam_k1, lam_q2, lam_k2, subln_g, w_out, lambda_init):
    B, S, _ = h.shape
    H, Dh = DA_HEADS, DA_HEAD_DIM
    q, k, v = jnp.split(h @ w_in, 3, axis=-1)
    q = rms_norm(q.reshape(B, S, H, 2, Dh), q_gain)
    k = rms_norm(k.reshape(B, S, H, 2, Dh), k_gain)
    v = v.reshape(B, S, H, DA_V_DIM)
    f32 = jnp.float32
    lam = (jnp.exp(jnp.sum(lam_q1.astype(f32) * lam_k1.astype(f32)))
           - jnp.exp(jnp.sum(lam_q2.astype(f32) * lam_k2.astype(f32))) + lambda_init)
    slopes = alibi_slopes(H)
    key_pos = jnp.arange(S, dtype=jnp.int32)
    n_qb = S // Q_BLOCK
    q_blocks = jnp.moveaxis(q.reshape(B, n_qb, Q_BLOCK, H, 2, Dh), 1, 0)
    scale = Dh ** -0.5

    def attend_block(args):
        q_blk, start = args
        q_pos = start + jnp.arange(Q_BLOCK, dtype=jnp.int32)
        visible = (key_pos // CHUNK)[None, :] <= (q_pos // CHUNK)[:, None]
        dist = jnp.abs(q_pos[:, None] - key_pos[None, :]).astype(f32)
        bias = -slopes[:, None, None] * dist
        scores = jnp.einsum('bqhmd,bshmd->bhmqs', q_blk, k).astype(f32) * scale + bias[None, :, None]
        scores = jnp.where(visible, scores, -jnp.inf)
        p = jax.nn.softmax(scores, axis=-1)
        p_diff = (p[:, :, 0] - lam * p[:, :, 1]).astype(v.dtype)
        return jnp.einsum('bhqs,bshe->bqhe', p_diff, v)

    starts = jnp.arange(n_qb, dtype=jnp.int32) * Q_BLOCK
    o = lax.map(attend_block, (q_blocks, starts))
    o = jnp.moveaxis(o, 0, 1).reshape(B, S, H, DA_V_DIM)
    o = rms_norm(o, subln_g) * (1.0 - lambda_init)
    return o.reshape(B, S, H * DA_V_DIM) @ w_out


def gla_mixer(h, w_in, w_a1, w_a2, b_a, out_g, w_out):
    B, S, _ = h.shape
    H, dk, dv = GLA_HEADS, GLA_DK, GLA_DV
    n = S // CHUNK
    f32 = jnp.float32
    q, k, v, r = jnp.split(h @ w_in, [GLA_KEY_WIDTH, 2 * GLA_KEY_WIDTH, 2 * GLA_KEY_WIDTH + GLA_VAL_WIDTH], axis=-1)
    log_a = jax.nn.log_sigmoid(((h @ w_a1) @ w_a2 + b_a).astype(f32)) / GLA_TAU

    def chunked(t, d):
        return t.astype(f32).reshape(B, n, CHUNK, H, d)

    q = chunked(q, dk) * dk ** -0.5
    k = chunked(k, dk)
    v = chunked(v, dv)
    b = jnp.cumsum(chunked(log_a, dk), axis=2)
    b_last = b[:, :, -1]
    q_dec = q * jnp.exp(b)
    k_inv = k * jnp.exp(-b)
    k_dec = k * jnp.exp(b_last[:, :, None] - b)
    causal = jnp.tril(jnp.ones((CHUNK, CHUNK), dtype=bool))
    attn = jnp.where(causal, jnp.einsum('bnthk,bnshk->bnhts', q_dec, k_inv), 0.0)
    o_intra = jnp.einsum('bnhts,bnshv->bnthv', attn, v)

    def step(state, xs):
        q_c, k_c, v_c, decay = xs
        o_c = jnp.einsum('bthk,bhkv->bthv', q_c, state)
        state = decay[..., None] * state + jnp.einsum('bshk,bshv->bhkv', k_c, v_c)
        return state, o_c

    xs = (jnp.moveaxis(q_dec, 1, 0), jnp.moveaxis(k_dec, 1, 0), jnp.moveaxis(v, 1, 0),
          jnp.moveaxis(jnp.exp(b_last), 1, 0))
    _, o_inter = lax.scan(step, jnp.zeros((B, H, dk, dv), f32), xs)
    o = o_intra + jnp.moveaxis(o_inter, 0, 1)
    o = rms_norm(o.reshape(B, S, H, dv), out_g).astype(h.dtype)
    o = o * jax.nn.silu(r).reshape(B, S, H, dv)
    return o.reshape(B, S, GLA_VAL_WIDTH) @ w_out


def moe_swiglu(h, router_w, w1, w3, w2):
    B, S, D = h.shape
    N = B * S
    A = N * TOP_K
    xf = h.reshape(N, D)
    logits = (xf @ router_w).astype(jnp.float32)
    top_val, top_idx = lax.top_k(logits, TOP_K)
    gates = jax.nn.softmax(top_val, axis=-1)
    e_flat = top_idx.reshape(A).astype(jnp.int32)
    tok_flat = jnp.repeat(jnp.arange(N, dtype=jnp.int32), TOP_K)
    g_flat = gates.reshape(A)
    e_s, tok_s, g_s = lax.sort((e_flat, tok_flat, g_flat), num_keys=1)
    counts = jax.ops.segment_sum(jnp.ones((A,), jnp.int32), e_flat, num_segments=N_EXPERTS)
    padded = (counts + EXPERT_BLOCK - 1) // EXPERT_BLOCK * EXPERT_BLOCK
    pend = jnp.cumsum(padded)
    pstart = pend - padded
    ustart = jnp.cumsum(counts) - counts
    dest = pstart[e_s] + (jnp.arange(A, dtype=jnp.int32) - ustart[e_s])
    P = A + N_EXPERTS * EXPERT_BLOCK
    buf_tok = jnp.full((P,), N, jnp.int32).at[dest].set(tok_s)
    buf_g = jnp.zeros((P,), jnp.float32).at[dest].set(g_s)
    n_blocks = P // EXPERT_BLOCK
    block_starts = jnp.arange(n_blocks, dtype=jnp.int32) * EXPERT_BLOCK
    block_exp = jnp.clip(jnp.searchsorted(pend, block_starts, side='right'), 0, N_EXPERTS - 1)
    x_pad = jnp.concatenate([xf, jnp.zeros((1, D), xf.dtype)], axis=0)

    def expert_block(args):
        tok_b, e = args
        return swiglu(x_pad[tok_b], w1[e], w3[e], w2[e])

    y = lax.map(expert_block, (buf_tok.reshape(n_blocks, EXPERT_BLOCK), block_exp))
    y = y.reshape(P, D) * buf_g[:, None].astype(y.dtype)
    out = jax.ops.segment_sum(y, buf_tok, num_segments=N + 1)[:N]
    return out.reshape(B, S, D)


def setup_inputs(seed: int = 0) -> dict:
    key = jax.random.key(seed)
    ks = iter(jax.random.split(key, 40))
    n_a = (DEPTH + 1) // 2
    n_b = DEPTH // 2
    D = D_MODEL

    def nrm(shape, scale):
        return jax.random.normal(next(ks), shape, jnp.float32) * scale

    def gain(shape):
        return 1.0 + nrm(shape, 0.05)

    return {
        "x": nrm((BATCH, SEQ, D), 1.0),
        "c": nrm((BATCH, D), 1.0),
        "ada_w": nrm((DEPTH, D, 6 * D), D ** -0.5),
        "ada_b": nrm((DEPTH, 6 * D), 0.02),
        "norm1_g": gain((DEPTH, D)),
        "norm2_g": gain((DEPTH, D)),
        "da_w_in": nrm((n_a, D, DA_PROJ), D ** -0.5),
        "da_q_gain": gain((n_a, 2, DA_HEAD_DIM)),
        "da_k_gain": gain((n_a, 2, DA_HEAD_DIM)),
        "da_lam_q1": nrm((n_a, DA_HEAD_DIM), 0.1),
        "da_lam_k1": nrm((n_a, DA_HEAD_DIM), 0.1),
        "da_lam_q2": nrm((n_a, DA_HEAD_DIM), 0.1),
        "da_lam_k2": nrm((n_a, DA_HEAD_DIM), 0.1),
        "da_subln_g": gain((n_a, DA_V_DIM)),
        "da_w_out": nrm((n_a, DA_HEADS * DA_V_DIM, D), (DA_HEADS * DA_V_DIM) ** -0.5),
        "gla_w_in": nrm((n_b, D, GLA_PROJ), D ** -0.5),
        "gla_w_a1": nrm((n_b, D, GLA_GATE_RANK), D ** -0.5),
        "gla_w_a2": nrm((n_b, GLA_GATE_RANK, GLA_KEY_WIDTH), GLA_GATE_RANK ** -0.5),
        "gla_b_a": nrm((n_b, GLA_KEY_WIDTH), 0.1),
        "gla_out_g": gain((n_b, GLA_DV)),
        "gla_w_out": nrm((n_b, GLA_VAL_WIDTH, D), GLA_VAL_WIDTH ** -0.5),
        "ffn_w1": nrm((n_a, D, FFN_DIM), D ** -0.5),
        "ffn_w3": nrm((n_a, D, FFN_DIM), D ** -0.5),
        "ffn_w2": nrm((n_a, FFN_DIM, D), FFN_DIM ** -0.5),
        "moe_router": nrm((n_b, D, N_EXPERTS), D ** -0.5),
        "moe_w1": nrm((n_b, N_EXPERTS, D, EXPERT_DIM), D ** -0.5),
        "moe_w3": nrm((n_b, N_EXPERTS, D, EXPERT_DIM), D ** -0.5),
        "moe_w2": nrm((n_b, N_EXPERTS, EXPERT_DIM, D), EXPERT_DIM ** -0.5),
    }


def reference(x, c, ada_w, ada_b, norm1_g, norm2_g,
              da_w_in, da_q_gain, da_k_gain, da_lam_q1, da_lam_k1, da_lam_q2, da_lam_k2, da_subln_g, da_w_out,
              gla_w_in, gla_w_a1, gla_w_a2, gla_b_a, gla_out_g, gla_w_out,
              ffn_w1, ffn_w3, ffn_w2,
              moe_router, moe_w1, moe_w3, moe_w2):
    c_act = jax.nn.silu(c)
    for i in range(DEPTH):
        j = i // N_MIXERS
        mod = c_act @ ada_w[i] + ada_b[i]
        shift1, scale1, gate1, shift2, scale2, gate2 = jnp.split(mod[:, None, :], 6, axis=-1)
        h = rms_norm(x, norm1_g[i]) * (1.0 + scale1) + shift1
        if i % N_MIXERS == 0:
            lambda_init = 0.8 - 0.6 * math.exp(-0.3 * i)
            y = diff_attention(h, da_w_in[j], da_q_gain[j], da_k_gain[j], da_lam_q1[j], da_lam_k1[j],
                               da_lam_q2[j], da_lam_k2[j], da_subln_g[j], da_w_out[j], lambda_init)
        else:
            y = gla_mixer(h, gla_w_in[j], gla_w_a1[j], gla_w_a2[j], gla_b_a[j], gla_out_g[j], gla_w_out[j])
        x = x + gate1 * y
        h = rms_norm(x, norm2_g[i]) * (1.0 + scale2) + shift2
        if i % 2 == 0:
            y = swiglu(h, ffn_w1[j], ffn_w3[j], ffn_w2[j])
        else:
            y = moe_swiglu(h, moe_router[j], moe_w1[j], moe_w3[j], moe_w2[j])
        x = x + gate2 * y
    return x
```

```python
import functools
import math

import jax
import jax.numpy as jnp
from jax import lax
from jax.experimental import pallas as pl
from jax.experimental.pallas import tpu as pltpu

F32 = jnp.float32
BF16 = jnp.bfloat16

D = 1024
EPS = 1e-6
NEG = -1e30
CHUNK = 64
DA_HEADS = 8
DA_HEAD_DIM = 64
GLA_HEADS = 4
GLA_DK = 128
GLA_DV = 256
GLA_TAU = 16.0
N_EXPERTS = 8
LANES = 128

VMEM_LIMIT = 56 * 1024 * 1024


def _cp(*sem):
    return pltpu.CompilerParams(dimension_semantics=sem, vmem_limit_bytes=VMEM_LIMIT)


def _sigmoid(x):
    return 1.0 / (1.0 + jnp.exp(-x))


def _modnorm(x, g, sc, sh):
    ms = jnp.mean(x * x, axis=-1, keepdims=True)
    return (x * lax.rsqrt(ms + EPS) * g) * (1.0 + sc) + sh


def _adaln_kernel(c_ref, w_ref, b_ref, o_ref):
    c = c_ref[...]
    ca = (c * _sigmoid(c)).astype(BF16)
    o_ref[...] = jnp.dot(ca, w_ref[...].astype(BF16), preferred_element_type=F32) + b_ref[...]


def _adaln(c, ada_w, ada_b):
    L, _, n6 = ada_w.shape
    B = c.shape[0]
    tn = 1536
    return pl.pallas_call(
        _adaln_kernel,
        grid=(L, n6 // tn),
        in_specs=[pl.BlockSpec((B, D), lambda l, n: (0, 0)),
                  pl.BlockSpec((None, D, tn), lambda l, n: (l, 0, n)),
                  pl.BlockSpec((None, 1, tn), lambda l, n: (l, 0, n))],
        out_specs=pl.BlockSpec((None, B, tn), lambda l, n: (l, 0, n)),
        out_shape=jax.ShapeDtypeStruct((L, B, n6), F32),
        compiler_params=_cp("arbitrary", "arbitrary"),
        name="adaln",
    )(c, ada_w, ada_b.reshape(L, 1, n6))


def _da_inproj_kernel(x_ref, g_ref, sc_ref, sh_ref, w_ref, gain_ref, gsum_ref, o_ref, h_scr, *, n_qk_blocks):
    j = pl.program_id(1)

    @pl.when(j == 0)
    def _():
        h_scr[...] = _modnorm(x_ref[...], g_ref[...], sc_ref[...], sh_ref[...]).astype(BF16)

    acc = jnp.dot(h_scr[...], w_ref[...], preferred_element_type=F32)
    y2 = (acc * acc).astype(BF16)
    gsum = gsum_ref[...]
    tn = acc.shape[1]
    ss = jnp.concatenate(
        [jnp.dot(y2[:, c * 256:(c + 1) * 256], gsum, preferred_element_type=F32) for c in range(tn // 256)],
        axis=1)
    rs = lax.rsqrt(ss * (1.0 / DA_HEAD_DIM) + EPS)
    rs = jnp.where(j < n_qk_blocks, rs, 1.0)
    o_ref[...] = (acc * rs * gain_ref[...]).astype(BF16)


def _da_inproj(x2d, S, g, sc, sh, w_bf, gain_row, tm=1024, tn=512):
    N = x2d.shape[0]
    n_out = w_bf.shape[1]
    r = jnp.arange(256) // DA_HEAD_DIM
    gsum = (r[:, None] == r[None, :]).astype(BF16)
    vec = pl.BlockSpec((None, 1, D), lambda i, j: ((i * tm) // S, 0, 0))
    return pl.pallas_call(
        functools.partial(_da_inproj_kernel, n_qk_blocks=(2 * D) // tn),
        grid=(N // tm, n_out // tn),
        in_specs=[pl.BlockSpec((tm, D), lambda i, j: (i, 0)),
                  pl.BlockSpec((1, D), lambda i, j: (0, 0)),
                  vec, vec,
                  pl.BlockSpec((D, tn), lambda i, j: (0, j)),
                  pl.BlockSpec((1, tn), lambda i, j: (0, j)),
                  pl.BlockSpec((256, 256), lambda i, j: (0, 0))],
        out_specs=pl.BlockSpec((tm, tn), lambda i, j: (i, j)),
        out_shape=jax.ShapeDtypeStruct((N, n_out), BF16),
        scratch_shapes=[pltpu.VMEM((tm, D), BF16)],
        compiler_params=_cp("arbitrary", "arbitrary"),
        name="da_inproj",
    )(x2d, g, sc, sh, w_bf, gain_row, gsum)


def _da_attn_kernel(slope_ref, q_ref, k_ref, v_ref, lq1_ref, lk1_ref, lq2_ref, lk2_ref, subg_ref, o_ref,
                    bfull_scr, bdiag_scr, m1, l1, a1, m2, l2, a2, *, T, S, lambda_init):
    h = pl.program_id(1)
    slope = slope_ref[h]
    ii = lax.broadcasted_iota(jnp.int32, (T, T), 0)
    jj = lax.broadcasted_iota(jnp.int32, (T, T), 1)
    rel = (jj - ii).astype(F32)
    bfull_scr[...] = slope * rel
    visible = jnp.right_shift(jj, 6) <= jnp.right_shift(ii, 6)
    bdiag_scr[...] = jnp.where(visible, -slope * jnp.abs(rel), NEG)

    lam = (jnp.exp(jnp.sum(lq1_ref[...] * lk1_ref[...], axis=-1, keepdims=True))
           - jnp.exp(jnp.sum(lq2_ref[...] * lk2_ref[...], axis=-1, keepdims=True)) + lambda_init)
    lane = lax.broadcasted_iota(jnp.int32, (T, LANES), 1)
    first_map = lane < DA_HEAD_DIM
    maps = ((m1, l1, a1), (m2, l2, a2))

    def step(qs, k, v, bias, c_s):
        for qm, (m_ref, l_ref, a_ref) in zip(qs, maps):
            s = lax.dot_general(qm, k, (((1,), (1,)), ((), ())), preferred_element_type=F32) + bias
            m_prev = m_ref[...]
            m_cur = jnp.max(s, axis=1, keepdims=True) + c_s
            m_next = jnp.maximum(m_prev, m_cur)
            p = jnp.exp(s + (c_s - m_next[:, :1]))
            alpha = jnp.exp(m_prev - m_next)
            l_ref[...] = alpha * l_ref[...] + jnp.sum(p, axis=1, keepdims=True)
            a_ref[...] = a_ref[...] * alpha + jnp.dot(p.astype(BF16), v, preferred_element_type=F32)
            m_ref[...] = m_next

    def q_block(qi, carry):
        q0 = pl.multiple_of(qi * T, T)
        q = q_ref[pl.ds(q0, T), :]
        zero = jnp.zeros_like(q)
        qs = (jnp.where(first_map, q, zero), jnp.where(first_map, zero, q))
        for m_ref, l_ref, a_ref in maps:
            m_ref[...] = jnp.full((T, LANES), NEG, F32)
            l_ref[...] = jnp.zeros((T, LANES), F32)
            a_ref[...] = jnp.zeros((T, LANES), F32)

        def kv_block(j, c):
            k0 = pl.multiple_of(j * T, T)
            c_s = -slope * lax.convert_element_type((qi - j) * T, F32)
            step(qs, k_ref[pl.ds(k0, T), :], v_ref[pl.ds(k0, T), :], bfull_scr[...], c_s)
            return c

        lax.fori_loop(0, qi, kv_block, 0)
        step(qs, k_ref[pl.ds(q0, T), :], v_ref[pl.ds(q0, T), :], bdiag_scr[...], jnp.float32(0.0))

        o = a1[...] / l1[...] - lam * (a2[...] / l2[...])
        ms = jnp.mean(o * o, axis=-1, keepdims=True)
        o = o * lax.rsqrt(ms + EPS) * subg_ref[...] * (1.0 - lambda_init)
        o_ref[pl.ds(q0, T), :] = o.astype(o_ref.dtype)
        return carry

    lax.fori_loop(0, S // T, q_block, 0)


def _da_attention(qkv, slopes, lq1, lk1, lq2, lk2, subg, lambda_init, T=256):
    B, S, _ = qkv.shape
    H = DA_HEADS
    vec64 = pl.BlockSpec((1, DA_HEAD_DIM), lambda b, h: (0, 0))
    return pl.pallas_call(
        functools.partial(_da_attn_kernel, T=T, S=S, lambda_init=lambda_init),
        grid=(B, H),
        in_specs=[pl.BlockSpec(memory_space=pltpu.SMEM),
                  pl.BlockSpec((None, S, LANES), lambda b, h: (b, 0, h)),
                  pl.BlockSpec((None, S, LANES), lambda b, h: (b, 0, H + h)),
                  pl.BlockSpec((None, S, LANES), lambda b, h: (b, 0, 2 * H + h)),
                  vec64, vec64, vec64, vec64,
                  pl.BlockSpec((1, LANES), lambda b, h: (0, 0))],
        out_specs=pl.BlockSpec((None, S, LANES), lambda b, h: (b, 0, h)),
        out_shape=jax.ShapeDtypeStruct((B, S, H * LANES), BF16),
        scratch_shapes=[pltpu.VMEM((T, T), F32), pltpu.VMEM((T, T), F32)]
        + [pltpu.VMEM((T, LANES), F32) for _ in range(6)],
        compiler_params=_cp("arbitrary", "arbitrary"),
        name="da_attention",
    )(slopes, qkv, qkv, qkv, lq1, lk1, lq2, lk2, subg)


def _post0_kernel(x_ref, o_ref, wo_ref, g1_ref, ng_ref, sc_ref, sh_ref, g2_ref, w1_ref, w3_ref, w2_ref,
                  out_ref, x1_scr, h_scr, acc_scr):
    j = pl.program_id(1)

    @pl.when(j == 0)
    def _():
        y = jnp.dot(o_ref[...], wo_ref[...], preferred_element_type=F32)
        x1 = x_ref[...] + g1_ref[...] * y
        x1_scr[...] = x1
        h_scr[...] = _modnorm(x1, ng_ref[...], sc_ref[...], sh_ref[...]).astype(BF16)
        acc_scr[...] = jnp.zeros_like(acc_scr)

    h = h_scr[...]
    a = jnp.dot(h, w1_ref[...], preferred_element_type=F32)
    b = jnp.dot(h, w3_ref[...], preferred_element_type=F32)
    gact = (a * _sigmoid(a) * b).astype(BF16)
    acc_scr[...] += jnp.dot(gact, w2_ref[...], preferred_element_type=F32)

    @pl.when(j == pl.num_programs(1) - 1)
    def _():
        out_ref[...] = x1_scr[...] + g2_ref[...] * acc_scr[...]


def _post0(x2d, o2d, S, wo, g1, ng, sc, sh, g2, w1, w3, w2, tm=512, tf=1408):
    N = x2d.shape[0]
    F = w1.shape[1]
    vec = pl.BlockSpec((None, 1, D), lambda i, j: ((i * tm) // S, 0, 0))
    row = pl.BlockSpec((tm, D), lambda i, j: (i, 0))
    return pl.pallas_call(
        _post0_kernel,
        grid=(N // tm, F // tf),
        in_specs=[row, row,
                  pl.BlockSpec((D, D), lambda i, j: (0, 0)),
                  vec,
                  pl.BlockSpec((1, D), lambda i, j: (0, 0)),
                  vec, vec, vec,
                  pl.BlockSpec((D, tf), lambda i, j: (0, j)),
                  pl.BlockSpec((D, tf), lambda i, j: (0, j)),
                  pl.BlockSpec((tf, D), lambda i, j: (j, 0))],
        out_specs=row,
        out_shape=jax.ShapeDtypeStruct((N, D), F32),
        scratch_shapes=[pltpu.VMEM((tm, D), F32), pltpu.VMEM((tm, D), BF16), pltpu.VMEM((tm, D), F32)],
        compiler_params=_cp("arbitrary", "arbitrary"),
        name="post0_ffn",
    )(x2d, o2d, wo, g1, ng, sc, sh, g2, w1, w3, w2)


def _gla_inproj_kernel(x_ref, g_ref, sc_ref, sh_ref, w_ref, mult_ref, wa1_ref, wa2_ref, ba_ref,
                       o_ref, la_ref, h_scr, *, n_plain_blocks):
    j = pl.program_id(1)

    @pl.when(j == 0)
    def _():
        h = _modnorm(x_ref[...], g_ref[...], sc_ref[...], sh_ref[...]).astype(BF16)
        h_scr[...] = h
        low = jnp.dot(h, wa1_ref[...], preferred_element_type=F32).astype(BF16)
        z = jnp.dot(low, wa2_ref[...], preferred_element_type=F32) + ba_ref[...]
        log_sig = jnp.minimum(z, 0.0) - jnp.log(1.0 + jnp.exp(-jnp.abs(z)))
        la_ref[...] = log_sig * (1.0 / GLA_TAU)

    acc = jnp.dot(h_scr[...], w_ref[...], preferred_element_type=F32)

    @pl.when(j < n_plain_blocks)
    def _():
        o_ref[...] = (acc * mult_ref[...]).astype(BF16)

    @pl.when(j >= n_plain_blocks)
    def _():
        o_ref[...] = (acc * _sigmoid(acc)).astype(BF16)


def _gla_inproj(x2d, S, g, sc, sh, w_bf, mult_row, wa1, wa2, ba, tm=1024, tn=512):
    N = x2d.shape[0]
    n_out = w_bf.shape[1]
    kw = wa2.shape[1]
    vec = pl.BlockSpec((None, 1, D), lambda i, j: ((i * tm) // S, 0, 0))
    return pl.pallas_call(
        functools.partial(_gla_inproj_kernel, n_plain_blocks=(2 * D) // tn),
        grid=(N // tm, n_out // tn),
        in_specs=[pl.BlockSpec((tm, D), lambda i, j: (i, 0)),
                  pl.BlockSpec((1, D), lambda i, j: (0, 0)),
                  vec, vec,
                  pl.BlockSpec((D, tn), lambda i, j: (0, j)),
                  pl.BlockSpec((1, tn), lambda i, j: (0, j)),
                  pl.BlockSpec((D, LANES), lambda i, j: (0, 0)),
                  pl.BlockSpec((LANES, kw), lambda i, j: (0, 0)),
                  pl.BlockSpec((1, kw), lambda i, j: (0, 0))],
        out_specs=[pl.BlockSpec((tm, tn), lambda i, j: (i, j)),
                   pl.BlockSpec((tm, kw), lambda i, j: (i, 0))],
        out_shape=[jax.ShapeDtypeStruct((N, n_out), BF16), jax.ShapeDtypeStruct((N, kw), F32)],
        scratch_shapes=[pltpu.VMEM((tm, D), BF16)],
        compiler_params=_cp("arbitrary", "arbitrary"),
        name="gla_inproj",
    )(x2d, g, sc, sh, w_bf, mult_row, wa1, wa2, ba)


def _gla_kernel(q_ref, k_ref, v_ref, r_ref, la_ref, tri_ref, og_ref, o_ref, state_scr, *, S):
    state_scr[...] = jnp.zeros_like(state_scr)
    tri = tri_ref[...]
    ti = lax.broadcasted_iota(jnp.int32, (CHUNK, CHUNK), 0)
    si = lax.broadcasted_iota(jnp.int32, (CHUNK, CHUNK), 1)
    causal = si <= ti
    og = og_ref[...]

    def chunk(c, carry):
        r0 = pl.multiple_of(c * CHUNK, CHUNK)
        la = la_ref[pl.ds(r0, CHUNK), :]
        la_hi = la.astype(BF16)
        la_lo = (la - la_hi.astype(F32)).astype(BF16)
        b = (jnp.dot(tri, la_hi, preferred_element_type=F32)
             + jnp.dot(tri, la_lo, preferred_element_type=F32))
        b_last = b[CHUNK - 1:CHUNK, :]
        q = q_ref[pl.ds(r0, CHUNK), :].astype(F32)
        k = k_ref[pl.ds(r0, CHUNK), :].astype(F32)
        v = v_ref[pl.ds(r0, CHUNK), :]
        q_dec = (q * jnp.exp(b)).astype(BF16)
        k_inv = (k * jnp.exp(-b)).astype(BF16)
        k_dec = (k * jnp.exp(b_last - b)).astype(BF16)
        attn = lax.dot_general(q_dec, k_inv, (((1,), (1,)), ((), ())), preferred_element_type=F32)
        attn = jnp.where(causal, attn, 0.0).astype(BF16)
        state = state_scr[...]
        o = (jnp.dot(attn, v, preferred_element_type=F32)
             + lax.dot_general(q_dec, state.astype(BF16), (((1,), (1,)), ((), ())), preferred_element_type=F32))
        kv_t = lax.dot_general(v, k_dec, (((0,), (0,)), ((), ())), preferred_element_type=F32)
        state_scr[...] = state * jnp.exp(b_last) + kv_t
        ms = jnp.mean(o * o, axis=-1, keepdims=True)
        o = o * lax.rsqrt(ms + EPS) * og * r_ref[pl.ds(r0, CHUNK), :].astype(F32)
        o_ref[pl.ds(r0, CHUNK), :] = o.astype(o_ref.dtype)
        return carry

    lax.fori_loop(0, S // CHUNK, chunk, 0)


def _gla(qkvr, la, out_g):
    B, S, _ = qkvr.shape
    H = GLA_HEADS
    tri = (jnp.arange(CHUNK)[None, :] <= jnp.arange(CHUNK)[:, None]).astype(BF16)
    kb = (H * GLA_DK) // GLA_DK
    vb = (2 * H * GLA_DK) // GLA_DV
    rb = vb + H
    return pl.pallas_call(
        functools.partial(_gla_kernel, S=S),
        grid=(B, H),
        in_specs=[pl.BlockSpec((None, S, GLA_DK), lambda b, h: (b, 0, h)),
                  pl.BlockSpec((None, S, GLA_DK), lambda b, h: (b, 0, kb + h)),
                  pl.BlockSpec((None, S, GLA_DV), lambda b, h: (b, 0, vb + h)),
                  pl.BlockSpec((None, S, GLA_DV), lambda b, h: (b, 0, rb + h)),
                  pl.BlockSpec((None, S, GLA_DK), lambda b, h: (b, 0, h)),
                  pl.BlockSpec((CHUNK, CHUNK), lambda b, h: (0, 0)),
                  pl.BlockSpec((1, GLA_DV), lambda b, h: (0, 0))],
        out_specs=pl.BlockSpec((None, S, GLA_DV), lambda b, h: (b, 0, h)),
        out_shape=jax.ShapeDtypeStruct((B, S, H * GLA_DV), BF16),
        scratch_shapes=[pltpu.VMEM((GLA_DV, GLA_DK), F32)],
        compiler_params=_cp("arbitrary", "arbitrary"),
        name="gla",
    )(qkvr, qkvr, qkvr, qkvr, la, tri, out_g)


def _post1_kernel(x_ref, o_ref, wo_ref, g1_ref, ng_ref, sc_ref, sh_ref, rw_ref, ltri_ref,
                  x3_ref, hp_ref, route_ref, cnt_ref, carry_scr, *, sub):
    i = pl.program_id(0)

    @pl.when(i == 0)
    def _():
        carry_scr[...] = jnp.zeros_like(carry_scr)

    y = jnp.dot(o_ref[...], wo_ref[...], preferred_element_type=F32)
    x3 = x_ref[...] + g1_ref[...] * y
    x3_ref[...] = x3
    hb = _modnorm(x3, ng_ref[...], sc_ref[...], sh_ref[...]).astype(BF16)

    half = D // 2
    lo = pltpu.bitcast(hb[:, :half].astype(F32), jnp.uint32)
    hi = pltpu.bitcast(hb[:, half:].astype(F32), jnp.uint32)
    hp_ref[...] = jnp.right_shift(lo, jnp.uint32(16)) | (hi & jnp.uint32(0xFFFF0000))

    tm = hb.shape[0]
    lane = lax.broadcasted_iota(jnp.int32, (tm, LANES), 1)
    lanef = lane.astype(F32)
    logits = jnp.dot(hb, rw_ref[...], preferred_element_type=F32)
    logits = jnp.where(lane < N_EXPERTS, logits, NEG)
    m1 = jnp.max(logits, axis=-1, keepdims=True)
    i1 = jnp.min(jnp.where(logits == m1, lanef, float(LANES)), axis=-1, keepdims=True)
    oh1 = lanef == i1
    rest = jnp.where(oh1, NEG, logits)
    m2 = jnp.max(rest, axis=-1, keepdims=True)
    i2 = jnp.min(jnp.where(rest == m2, lanef, float(LANES)), axis=-1, keepdims=True)
    oh2 = lanef == i2
    e = jnp.exp(m2 - m1)
    gate1 = 1.0 / (1.0 + e)
    gate2 = e / (1.0 + e)

    cnt = jnp.where(oh1, 1.0, 0.0) + jnp.where(oh2, 1.0, 0.0)
    ltri = ltri_ref[...]
    carry = carry_scr[...]
    pres = []
    for s in range(tm // sub):
        c_sub = cnt[s * sub:(s + 1) * sub, :]
        pres.append(jnp.dot(ltri, c_sub.astype(BF16), preferred_element_type=F32) + carry)
        carry = carry + jnp.sum(c_sub, axis=0, keepdims=True)
    pre = jnp.concatenate(pres, axis=0)
    carry_scr[...] = carry
    cnt_ref[...] = carry
    r1 = jnp.sum(jnp.where(oh1, pre, 0.0), axis=-1, keepdims=True)
    r2 = jnp.sum(jnp.where(oh2, pre, 0.0), axis=-1, keepdims=True)
    route = jnp.where(lane == 0, i1, 0.0)
    for idx, val in ((1, i2), (2, r1), (3, r2), (4, gate1), (5, gate2)):
        route = jnp.where(lane == idx, val, route)
    route_ref[...] = route


def _post1(x2d, o2d, S, wo, g1, ng, sc, sh, router_pad, tm=512, sub=256):
    N = x2d.shape[0]
    ltri = (jnp.arange(sub)[None, :] < jnp.arange(sub)[:, None]).astype(BF16)
    vec = pl.BlockSpec((None, 1, D), lambda i: ((i * tm) // S, 0, 0))
    row = pl.BlockSpec((tm, D), lambda i: (i, 0))
    return pl.pallas_call(
        functools.partial(_post1_kernel, sub=sub),
        grid=(N // tm,),
        in_specs=[row, row,
                  pl.BlockSpec((D, D), lambda i: (0, 0)),
                  vec,
                  pl.BlockSpec((1, D), lambda i: (0, 0)),
                  vec, vec,
                  pl.BlockSpec((D, LANES), lambda i: (0, 0)),
                  pl.BlockSpec((sub, sub), lambda i: (0, 0))],
        out_specs=[row,
                   pl.BlockSpec((tm, D // 2), lambda i: (i, 0)),
                   pl.BlockSpec((tm, LANES), lambda i: (i, 0)),
                   pl.BlockSpec((1, LANES), lambda i: (0, 0))],
        out_shape=[jax.ShapeDtypeStruct((N, D), F32),
                   jax.ShapeDtypeStruct((N, D // 2), jnp.uint32),
                   jax.ShapeDtypeStruct((N, LANES), F32),
                   jax.ShapeDtypeStruct((1, LANES), F32)],
        scratch_shapes=[pltpu.VMEM((1, LANES), F32)],
        compiler_params=_cp("arbitrary"),
        name="post1_router",
    )(x2d, o2d, wo, g1, ng, sc, sh, router_pad, ltri)


def _dispatch_kernel(pos_ref, h_ref, xs_in_ref, xs_ref, sem, *, td):
    del xs_in_ref

    def row(r, carry):
        for kk in range(2):
            p = pos_ref[0, 2 * r + kk]
            pltpu.make_async_copy(h_ref.at[pl.ds(r, 1)], xs_ref.at[pl.ds(p, 1)], sem).start()
        return carry

    lax.fori_loop(0, td, row, 0, unroll=8)
    for _ in range(2):
        pltpu.make_async_copy(h_ref, xs_ref.at[pl.ds(0, td)], sem).wait()


def _dispatch(hp, pos, P, td=512):
    N, W = hp.shape
    xs0 = jnp.zeros((P, W), jnp.uint32)
    return pl.pallas_call(
        functools.partial(_dispatch_kernel, td=td),
        grid=(N // td,),
        in_specs=[pl.BlockSpec((None, 1, 2 * td), lambda i: (i, 0, 0), memory_space=pltpu.SMEM),
                  pl.BlockSpec((td, W), lambda i: (i, 0)),
                  pl.BlockSpec(memory_space=pl.ANY)],
        out_specs=pl.BlockSpec(memory_space=pl.ANY),
        out_shape=jax.ShapeDtypeStruct((P, W), jnp.uint32),
        scratch_shapes=[pltpu.SemaphoreType.DMA(())],
        input_output_aliases={2: 0},
        compiler_params=_cp("arbitrary"),
        name="moe_dispatch",
    )(pos.reshape(N // td, 1, 2 * td), hp, xs0)


def _experts_kernel(be_ref, nv_ref, xs_ref, w1_ref, w3_ref, w2_ref, y_ref, xb_scr):
    i = pl.program_id(0)
    j = pl.program_id(1)
    valid = i < nv_ref[0]

    @pl.when(valid & (j == 0))
    def _():
        w = xs_ref[...]
        half = D // 2
        xb_scr[:, :half] = pltpu.bitcast(jnp.left_shift(w, jnp.uint32(16)), F32).astype(BF16)
        xb_scr[:, half:] = pltpu.bitcast(w & jnp.uint32(0xFFFF0000), F32).astype(BF16)
        y_ref[...] = jnp.zeros_like(y_ref)

    @pl.when(jnp.logical_not(valid) & (j == 0))
    def _():
        y_ref[...] = jnp.zeros_like(y_ref)

    @pl.when(valid)
    def _():
        xb = xb_scr[...]
        a = jnp.dot(xb, w1_ref[...], preferred_element_type=F32)
        b = jnp.dot(xb, w3_ref[...], preferred_element_type=F32)
        gact = (a * _sigmoid(a) * b).astype(BF16)
        y_ref[...] += jnp.dot(gact, w2_ref[...], preferred_element_type=F32)


def _experts(xs, blk_expert, n_valid, w1, w3, w2, tm, tf=512):
    P, W = xs.shape
    E, _, F = w1.shape
    nblk = P // tm
    nj = F // tf

    def _i(i, nv):
        return jnp.minimum(i, nv[0] - 1)

    def _j(i, j, nv):
        return jnp.where(i < nv[0], j, nj - 1)

    grid_spec = pltpu.PrefetchScalarGridSpec(
        num_scalar_prefetch=2,
        grid=(nblk, nj),
        in_specs=[pl.BlockSpec((tm, W), lambda i, j, be, nv: (_i(i, nv), 0)),
                  pl.BlockSpec((None, D, tf), lambda i, j, be, nv: (be[_i(i, nv)], 0, _j(i, j, nv))),
                  pl.BlockSpec((None, D, tf), lambda i, j, be, nv: (be[_i(i, nv)], 0, _j(i, j, nv))),
                  pl.BlockSpec((None, tf, D), lambda i, j, be, nv: (be[_i(i, nv)], _j(i, j, nv), 0))],
        out_specs=pl.BlockSpec((tm, D), lambda i, j, be, nv: (i, 0)),
        scratch_shapes=[pltpu.VMEM((tm, D), BF16)],
    )
    return pl.pallas_call(
        _experts_kernel,
        grid_spec=grid_spec,
        out_shape=jax.ShapeDtypeStruct((P, D), F32),
        compiler_params=_cp("arbitrary", "arbitrary"),
        name="moe_experts",
    )(blk_expert, n_valid, xs, w1, w3, w2)


def _combine_kernel(pos_ref, route_ref, x_ref, g2_ref, y_ref, out_ref, ybuf, sem, *, tc):
    def row(r, carry):
        for kk in range(2):
            p = pos_ref[0, 2 * r + kk]
            pltpu.make_async_copy(y_ref.at[pl.ds(p, 1)], ybuf.at[kk, pl.ds(r, 1)], sem).start()
        return carry

    lax.fori_loop(0, tc, row, 0, unroll=8)
    for kk in range(2):
        pltpu.make_async_copy(y_ref.at[pl.ds(0, tc)], ybuf.at[kk], sem).wait()
    route = route_ref[...]
    moe = route[:, 4:5] * ybuf[0] + route[:, 5:6] * ybuf[1]
    out_ref[...] = x_ref[...] + g2_ref[...] * moe


def _combine(y, pos, route, x3, S, g2, tc=256):
    N = x3.shape[0]
    return pl.pallas_call(
        functools.partial(_combine_kernel, tc=tc),
        grid=(N // tc,),
        in_specs=[pl.BlockSpec((None, 1, 2 * tc), lambda i: (i, 0, 0), memory_space=pltpu.SMEM),
                  pl.BlockSpec((tc, LANES), lambda i: (i, 0)),
                  pl.BlockSpec((tc, D), lambda i: (i, 0)),
                  pl.BlockSpec((None, 1, D), lambda i: ((i * tc) // S, 0, 0)),
                  pl.BlockSpec(memory_space=pl.ANY)],
        out_specs=pl.BlockSpec((tc, D), lambda i: (i, 0)),
        out_shape=jax.ShapeDtypeStruct((N, D), F32),
        scratch_shapes=[pltpu.VMEM((2, tc, D), F32), pltpu.SemaphoreType.DMA(())],
        compiler_params=_cp("arbitrary"),
        name="moe_combine",
    )(pos.reshape(N // tc, 1, 2 * tc), route, x3, g2, y)


def _moe(hp, route, counts, x3, S, g2, w1, w3, w2, tm=1024):
    N = x3.shape[0]
    cnt = counts[0, :N_EXPERTS].astype(jnp.int32)
    nblk_e = (cnt + tm - 1) // tm
    blk_end = jnp.cumsum(nblk_e)
    row_start = (blk_end - nblk_e) * tm
    e_idx = route[:, 0:2].astype(jnp.int32)
    rank = route[:, 2:4].astype(jnp.int32)
    pos = row_start[e_idx] + rank
    P = 2 * N + N_EXPERTS * tm
    nblk = P // tm
    blk_expert = jnp.clip(jnp.searchsorted(blk_end, jnp.arange(nblk, dtype=jnp.int32), side="right"),
                          0, N_EXPERTS - 1).astype(jnp.int32)
    n_valid = blk_end[-1:].astype(jnp.int32)
    xs = _dispatch(hp, pos, P)
    y = _experts(xs, blk_expert, n_valid, w1, w3, w2, tm)
    return _combine(y, pos, route, x3, S, g2)


def kernel(x, c, ada_w, ada_b, norm1_g, norm2_g, da_w_in, da_q_gain, da_k_gain, da_lam_q1, da_lam_k1, da_lam_q2,
           da_lam_k2, da_subln_g, da_w_out, gla_w_in, gla_w_a1, gla_w_a2, gla_b_a, gla_out_g, gla_w_out,
           ffn_w1, ffn_w3, ffn_w2, moe_router, moe_w1, moe_w3, moe_w2):
    B, S, _ = x.shape
    N = B * S
    mod = _adaln(c, ada_w, ada_b)
    mods = [[mod[l, :, k * D:(k + 1) * D].reshape(B, 1, D) for k in range(6)] for l in range(2)]
    x2d = x.reshape(N, D)

    sh1, sc1, gt1, sh2, sc2, gt2 = mods[0]
    lambda_init = 0.8 - 0.6 * math.exp(-0.3 * 0)
    qk_scale = DA_HEAD_DIM ** -0.5
    gain_row = jnp.concatenate([jnp.tile(da_q_gain[0].reshape(-1) * qk_scale, DA_HEADS),
                                jnp.tile(da_k_gain[0].reshape(-1), DA_HEADS),
                                jnp.ones((D,), F32)]).reshape(1, 3 * D)
    qkv = _da_inproj(x2d, S, norm1_g[0].reshape(1, D), sc1, sh1, da_w_in[0].astype(BF16), gain_row)
    slopes = 2.0 ** (-8.0 * jnp.arange(1, DA_HEADS + 1, dtype=F32) / DA_HEADS)
    o = _da_attention(qkv.reshape(B, S, 3 * D), slopes,
                      da_lam_q1[0].reshape(1, -1), da_lam_k1[0].reshape(1, -1),
                      da_lam_q2[0].reshape(1, -1), da_lam_k2[0].reshape(1, -1),
                      da_subln_g[0].reshape(1, -1), lambda_init)
    x2d = _post0(x2d, o.reshape(N, D), S, da_w_out[0].astype(BF16), gt1, norm2_g[0].reshape(1, D), sc2, sh2, gt2,
                 ffn_w1[0].astype(BF16), ffn_w3[0].astype(BF16), ffn_w2[0].astype(BF16))

    sh1, sc1, gt1, sh2, sc2, gt2 = mods[1]
    kw = GLA_HEADS * GLA_DK
    mult_row = jnp.concatenate([jnp.full((kw,), GLA_DK ** -0.5, F32), jnp.ones((3 * D - kw,), F32)]).reshape(1, -1)
    rank = gla_w_a1.shape[-1]
    wa1 = jnp.zeros((D, LANES), BF16).at[:, :rank].set(gla_w_a1[0].astype(BF16))
    wa2 = jnp.zeros((LANES, kw), BF16).at[:rank, :].set(gla_w_a2[0].astype(BF16))
    qkvr, la = _gla_inproj(x2d, S, norm1_g[1].reshape(1, D), sc1, sh1, gla_w_in[0].astype(BF16), mult_row,
                           wa1, wa2, gla_b_a[0].reshape(1, kw))
    o = _gla(qkvr.reshape(B, S, 3 * D), la.reshape(B, S, kw), gla_out_g[0].reshape(1, -1))
    router_pad = jnp.zeros((D, LANES), BF16).at[:, :N_EXPERTS].set(moe_router[0].astype(BF16))
    x3, hp, route, counts = _post1(x2d, o.reshape(N, D), S, gla_w_out[0].astype(BF16), gt1,
                                   norm2_g[1].reshape(1, D), sc2, sh2, router_pad)
    out = _moe(hp, route, counts, x3, S, gt2,
               moe_w1[0].astype(BF16), moe_w3[0].astype(BF16), moe_w2[0].astype(BF16))
    return out.reshape(B, S, D)
```

```python
import functools
import math

import jax
import jax.numpy as jnp
from jax import lax
from jax.experimental import pallas as pl
from jax.experimental.pallas import tpu as pltpu

F32 = jnp.float32
BF16 = jnp.bfloat16

D = 1024
EPS = 1e-6
NEG = -1e30
CHUNK = 64
DA_HEADS = 8
DA_HEAD_DIM = 64
GLA_HEADS = 4
GLA_DK = 128
GLA_DV = 256
GLA_TAU = 16.0
N_EXPERTS = 8
LANES = 128

VMEM_LIMIT = 56 * 1024 * 1024


def _cp(*sem):
    return pltpu.CompilerParams(dimension_semantics=sem, vmem_limit_bytes=VMEM_LIMIT)


def _sigmoid(x):
    return 1.0 / (1.0 + jnp.exp(-x))


def _modnorm(x, g, sc, sh):
    ms = jnp.mean(x * x, axis=-1, keepdims=True)
    return (x * lax.rsqrt(ms + EPS) * g) * (1.0 + sc) + sh


def _adaln_kernel(c_ref, w_ref, b_ref, o_ref):
    c = c_ref[...]
    ca = (c * _sigmoid(c)).astype(BF16)
    o_ref[...] = jnp.dot(ca, w_ref[...].astype(BF16), preferred_element_type=F32) + b_ref[...]


def _adaln(c, ada_w, ada_b):
    L, _, n6 = ada_w.shape
    B = c.shape[0]
    tn = 1536
    return pl.pallas_call(
        _adaln_kernel,
        grid=(L, n6 // tn),
        in_specs=[pl.BlockSpec((B, D), lambda l, n: (0, 0)),
                  pl.BlockSpec((None, D, tn), lambda l, n: (l, 0, n)),
                  pl.BlockSpec((None, 1, tn), lambda l, n: (l, 0, n))],
        out_specs=pl.BlockSpec((None, B, tn), lambda l, n: (l, 0, n)),
        out_shape=jax.ShapeDtypeStruct((L, B, n6), F32),
        compiler_params=_cp("arbitrary", "arbitrary"),
        name="adaln",
    )(c, ada_w, ada_b.reshape(L, 1, n6))


def _da_inproj_kernel(x_ref, g_ref, sc_ref, sh_ref, w_ref, gain_ref, gsum_ref, o_ref, h_scr, *, n_qk_blocks):
    j = pl.program_id(1)

    @pl.when(j == 0)
    def _():
        h_scr[...] = _modnorm(x_ref[...], g_ref[...], sc_ref[...], sh_ref[...]).astype(BF16)

    acc = jnp.dot(h_scr[...], w_ref[...], preferred_element_type=F32)
    y2 = (acc * acc).astype(BF16)
    gsum = gsum_ref[...]
    tn = acc.shape[1]
    ss = jnp.concatenate(
        [jnp.dot(y2[:, c * 256:(c + 1) * 256], gsum, preferred_element_type=F32) for c in range(tn // 256)],
        axis=1)
    rs = lax.rsqrt(ss * (1.0 / DA_HEAD_DIM) + EPS)
    rs = jnp.where(j < n_qk_blocks, rs, 1.0)
    o_ref[...] = (acc * rs * gain_ref[...]).astype(BF16)


def _da_inproj(x2d, S, g, sc, sh, w_bf, gain_row, tm=1024, tn=512):
    N = x2d.shape[0]
    n_out = w_bf.shape[1]
    r = jnp.arange(256) // DA_HEAD_DIM
    gsum = (r[:, None] == r[None, :]).astype(BF16)
    vec = pl.BlockSpec((None, 1, D), lambda i, j: ((i * tm) // S, 0, 0))
    return pl.pallas_call(
        functools.partial(_da_inproj_kernel, n_qk_blocks=(2 * D) // tn),
        grid=(N // tm, n_out // tn),
        in_specs=[pl.BlockSpec((tm, D), lambda i, j: (i, 0)),
                  pl.BlockSpec((1, D), lambda i, j: (0, 0)),
                  vec, vec,
                  pl.BlockSpec((D, tn), lambda i, j: (0, j)),
                  pl.BlockSpec((1, tn), lambda i, j: (0, j)),
                  pl.BlockSpec((256, 256), lambda i, j: (0, 0))],
        out_specs=pl.BlockSpec((tm, tn), lambda i, j: (i, j)),
        out_shape=jax.ShapeDtypeStruct((N, n_out), BF16),
        scratch_shapes=[pltpu.VMEM((tm, D), BF16)],
        compiler_params=_cp("arbitrary", "arbitrary"),
        name="da_inproj",
    )(x2d, g, sc, sh, w_bf, gain_row, gsum)


def _da_attn_kernel(slope_ref, q_ref, k_ref, v_ref, lq1_ref, lk1_ref, lq2_ref, lk2_ref, subg_ref, o_ref,
                    k1_scr, k2_scr, vt_scr, corr_scr, s1_scr, s2_scr, a1, a2, *, T, S, lambda_init):
    h = pl.program_id(1)
    slope = slope_ref[h]
    lane = lax.broadcasted_iota(jnp.int32, (T, LANES), 1)
    local = lax.broadcasted_iota(jnp.int32, (T, LANES), 0)
    lo = jnp.bitwise_and(local, 255).astype(F32)
    hi = (local - jnp.bitwise_and(local, 255)).astype(F32)
    first_half = lane < DA_HEAD_DIM

    def aug(base, vals):
        out = jnp.zeros((T, LANES), F32)
        for off, val in enumerate(vals):
            out = jnp.where(lane == base + off, val, out)
        return out.astype(BF16)

    k_aug1 = aug(DA_HEAD_DIM, (slope * lo, slope * hi, 1.0, 1.0))
    k_aug2 = aug(0, (slope * lo, slope * hi, 1.0, 1.0))
    q_aug1 = aug(DA_HEAD_DIM, (1.0, 1.0, -slope * lo, -slope * hi))
    q_aug2 = aug(0, (1.0, 1.0, -slope * lo, -slope * hi))

    def prep(j, carry):
        r0 = pl.multiple_of(j * T, T)
        k = k_ref[pl.ds(r0, T), :]
        k1_scr[pl.ds(r0, T), :] = jnp.where(first_half, k, k_aug1)
        k2_scr[pl.ds(r0, T), :] = jnp.where(first_half, k_aug2, k)
        vt_scr[:, pl.ds(r0, T)] = v_ref[pl.ds(r0, T), :].astype(F32).T.astype(BF16)
        return carry

    lax.fori_loop(0, S // T, prep, 0)

    kj = lax.broadcasted_iota(jnp.int32, (T, T), 0)
    qj = lax.broadcasted_iota(jnp.int32, (T, T), 1)
    visible = jnp.right_shift(kj, 6) <= jnp.right_shift(qj, 6)
    ahead = jnp.maximum(kj - qj, 0).astype(F32)
    corr_scr[...] = jnp.where(visible, -2.0 * slope * ahead, NEG)

    lam = (jnp.exp(jnp.sum(lq1_ref[...] * lk1_ref[...], axis=-1, keepdims=True))
           - jnp.exp(jnp.sum(lq2_ref[...] * lk2_ref[...], axis=-1, keepdims=True)) + lambda_init)
    k_scrs = (k1_scr, k2_scr)
    s_scrs = (s1_scr, s2_scr)
    accs = (a1, a2)

    def q_block(qi, carry):
        q0 = pl.multiple_of(qi * T, T)
        q = q_ref[pl.ds(q0, T), :]
        qts = tuple(qa.astype(F32).T.astype(BF16)
                    for qa in (jnp.where(first_half, q, q_aug1), jnp.where(first_half, q_aug2, q)))

        def block_shift(j):
            return -slope * lax.convert_element_type((qi - j) * T, F32)

        def scores(j, ms, corr):
            k0 = pl.multiple_of(j * T, T)
            out = []
            for mp in range(2):
                s = jnp.dot(k_scrs[mp][pl.ds(k0, T), :], qts[mp], preferred_element_type=F32)
                if corr is not None:
                    s = s + corr
                s_scrs[mp][pl.ds(k0, T), :] = s
                out.append(jnp.maximum(ms[mp], jnp.max(s, axis=0, keepdims=True) + block_shift(j)))
            return tuple(out)

        neg = jnp.full((1, T), NEG, F32)
        ms = lax.fori_loop(0, qi, lambda j, c: scores(j, c, None), (neg, neg))
        ms = scores(qi, ms, corr_scr[...])

        a1[...] = jnp.zeros_like(a1)
        a2[...] = jnp.zeros_like(a2)

        def weighted(j, ls):
            k0 = pl.multiple_of(j * T, T)
            vt = vt_scr[:, pl.ds(k0, T)]
            out = []
            for mp in range(2):
                p = jnp.exp(s_scrs[mp][pl.ds(k0, T), :] + (block_shift(j) - ms[mp]))
                out.append(ls[mp] + jnp.sum(p, axis=0, keepdims=True))
                accs[mp][...] += jnp.dot(vt, p.astype(BF16), preferred_element_type=F32)
            return tuple(out)

        zero = jnp.zeros((1, T), F32)
        l1, l2 = lax.fori_loop(0, qi + 1, weighted, (zero, zero))

        o = (a1[...] / l1 - lam * (a2[...] / l2)).T
        msq = jnp.mean(o * o, axis=-1, keepdims=True)
        o = o * lax.rsqrt(msq + EPS) * subg_ref[...] * (1.0 - lambda_init)
        o_ref[pl.ds(q0, T), :] = o.astype(o_ref.dtype)
        return carry

    lax.fori_loop(0, S // T, q_block, 0)


def _da_attention(qkv, slopes, lq1, lk1, lq2, lk2, subg, lambda_init, T=512):
    B, S, _ = qkv.shape
    H = DA_HEADS
    vec64 = pl.BlockSpec((1, DA_HEAD_DIM), lambda b, h: (0, 0))
    return pl.pallas_call(
        functools.partial(_da_attn_kernel, T=T, S=S, lambda_init=lambda_init),
        grid=(B, H),
        in_specs=[pl.BlockSpec(memory_space=pltpu.SMEM),
                  pl.BlockSpec((None, S, LANES), lambda b, h: (b, 0, h)),
                  pl.BlockSpec((None, S, LANES), lambda b, h: (b, 0, H + h)),
                  pl.BlockSpec((None, S, LANES), lambda b, h: (b, 0, 2 * H + h)),
                  vec64, vec64, vec64, vec64,
                  pl.BlockSpec((1, LANES), lambda b, h: (0, 0))],
        out_specs=pl.BlockSpec((None, S, LANES), lambda b, h: (b, 0, h)),
        out_shape=jax.ShapeDtypeStruct((B, S, H * LANES), BF16),
        scratch_shapes=[pltpu.VMEM((S, LANES), BF16), pltpu.VMEM((S, LANES), BF16), pltpu.VMEM((LANES, S), BF16),
                        pltpu.VMEM((T, T), F32), pltpu.VMEM((S, T), F32), pltpu.VMEM((S, T), F32),
                        pltpu.VMEM((LANES, T), F32), pltpu.VMEM((LANES, T), F32)],
        compiler_params=_cp("arbitrary", "arbitrary"),
        name="da_attention",
    )(slopes, qkv, qkv, qkv, lq1, lk1, lq2, lk2, subg)


def _post0_kernel(x_ref, o_ref, wo_ref, g1_ref, ng_ref, sc_ref, sh_ref, g2_ref, w1_ref, w3_ref, w2_ref,
                  out_ref, x1_scr, h_scr, acc_scr):
    j = pl.program_id(1)

    @pl.when(j == 0)
    def _():
        y = jnp.dot(o_ref[...], wo_ref[...], preferred_element_type=F32)
        x1 = x_ref[...] + g1_ref[...] * y
        x1_scr[...] = x1
        h_scr[...] = _modnorm(x1, ng_ref[...], sc_ref[...], sh_ref[...]).astype(BF16)
        acc_scr[...] = jnp.zeros_like(acc_scr)

    h = h_scr[...]
    a = jnp.dot(h, w1_ref[...], preferred_element_type=F32)
    b = jnp.dot(h, w3_ref[...], preferred_element_type=F32)
    gact = (a * _sigmoid(a) * b).astype(BF16)
    acc_scr[...] += jnp.dot(gact, w2_ref[...], preferred_element_type=F32)

    @pl.when(j == pl.num_programs(1) - 1)
    def _():
        out_ref[...] = x1_scr[...] + g2_ref[...] * acc_scr[...]


def _post0(x2d, o2d, S, wo, g1, ng, sc, sh, g2, w1, w3, w2, tm=512, tf=1408):
    N = x2d.shape[0]
    F = w1.shape[1]
    vec = pl.BlockSpec((None, 1, D), lambda i, j: ((i * tm) // S, 0, 0))
    row = pl.BlockSpec((tm, D), lambda i, j: (i, 0))
    return pl.pallas_call(
        _post0_kernel,
        grid=(N // tm, F // tf),
        in_specs=[row, row,
                  pl.BlockSpec((D, D), lambda i, j: (0, 0)),
                  vec,
                  pl.BlockSpec((1, D), lambda i, j: (0, 0)),
                  vec, vec, vec,
                  pl.BlockSpec((D, tf), lambda i, j: (0, j)),
                  pl.BlockSpec((D, tf), lambda i, j: (0, j)),
                  pl.BlockSpec((tf, D), lambda i, j: (j, 0))],
        out_specs=row,
        out_shape=jax.ShapeDtypeStruct((N, D), F32),
        scratch_shapes=[pltpu.VMEM((tm, D), F32), pltpu.VMEM((tm, D), BF16), pltpu.VMEM((tm, D), F32)],
        compiler_params=_cp("arbitrary", "arbitrary"),
        name="post0_ffn",
    )(x2d, o2d, wo, g1, ng, sc, sh, g2, w1, w3, w2)


def _gla_inproj_kernel(x_ref, g_ref, sc_ref, sh_ref, w_ref, mult_ref, wa1_ref, wa2_ref, ba_ref,
                       o_ref, la_ref, h_scr, *, n_plain_blocks):
    j = pl.program_id(1)

    @pl.when(j == 0)
    def _():
        h = _modnorm(x_ref[...], g_ref[...], sc_ref[...], sh_ref[...]).astype(BF16)
        h_scr[...] = h
        low = jnp.dot(h, wa1_ref[...], preferred_element_type=F32).astype(BF16)
        z = jnp.dot(low, wa2_ref[...], preferred_element_type=F32) + ba_ref[...]
        log_sig = jnp.minimum(z, 0.0) - jnp.log(1.0 + jnp.exp(-jnp.abs(z)))
        la_ref[...] = log_sig * (1.0 / GLA_TAU)

    acc = jnp.dot(h_scr[...], w_ref[...], preferred_element_type=F32)

    @pl.when(j < n_plain_blocks)
    def _():
        o_ref[...] = (acc * mult_ref[...]).astype(BF16)

    @pl.when(j >= n_plain_blocks)
    def _():
        o_ref[...] = (acc * _sigmoid(acc)).astype(BF16)


def _gla_inproj(x2d, S, g, sc, sh, w_bf, mult_row, wa1, wa2, ba, tm=1024, tn=512):
    N = x2d.shape[0]
    n_out = w_bf.shape[1]
    kw = wa2.shape[1]
    vec = pl.BlockSpec((None, 1, D), lambda i, j: ((i * tm) // S, 0, 0))
    return pl.pallas_call(
        functools.partial(_gla_inproj_kernel, n_plain_blocks=(2 * D) // tn),
        grid=(N // tm, n_out // tn),
        in_specs=[pl.BlockSpec((tm, D), lambda i, j: (i, 0)),
                  pl.BlockSpec((1, D), lambda i, j: (0, 0)),
                  vec, vec,
                  pl.BlockSpec((D, tn), lambda i, j: (0, j)),
                  pl.BlockSpec((1, tn), lambda i, j: (0, j)),
                  pl.BlockSpec((D, LANES), lambda i, j: (0, 0)),
                  pl.BlockSpec((LANES, kw), lambda i, j: (0, 0)),
                  pl.BlockSpec((1, kw), lambda i, j: (0, 0))],
        out_specs=[pl.BlockSpec((tm, tn), lambda i, j: (i, j)),
                   pl.BlockSpec((tm, kw), lambda i, j: (i, 0))],
        out_shape=[jax.ShapeDtypeStruct((N, n_out), BF16), jax.ShapeDtypeStruct((N, kw), F32)],
        scratch_shapes=[pltpu.VMEM((tm, D), BF16)],
        compiler_params=_cp("arbitrary", "arbitrary"),
        name="gla_inproj",
    )(x2d, g, sc, sh, w_bf, mult_row, wa1, wa2, ba)


def _gla_kernel(q_ref, k_ref, v_ref, r_ref, la_ref, tri_ref, og_ref, o_ref, state_scr, *, S):
    state_scr[...] = jnp.zeros_like(state_scr)
    tri = tri_ref[...]
    ti = lax.broadcasted_iota(jnp.int32, (CHUNK, CHUNK), 0)
    si = lax.broadcasted_iota(jnp.int32, (CHUNK, CHUNK), 1)
    causal = si <= ti
    og = og_ref[...]

    def chunk(c, carry):
        r0 = pl.multiple_of(c * CHUNK, CHUNK)
        la = la_ref[pl.ds(r0, CHUNK), :]
        la_hi = la.astype(BF16)
        la_lo = (la - la_hi.astype(F32)).astype(BF16)
        b = (jnp.dot(tri, la_hi, preferred_element_type=F32)
             + jnp.dot(tri, la_lo, preferred_element_type=F32))
        b_last = b[CHUNK - 1:CHUNK, :]
        q = q_ref[pl.ds(r0, CHUNK), :].astype(F32)
        k = k_ref[pl.ds(r0, CHUNK), :].astype(F32)
        v = v_ref[pl.ds(r0, CHUNK), :]
        q_dec = (q * jnp.exp(b)).astype(BF16)
        k_inv = (k * jnp.exp(-b)).astype(BF16)
        k_dec = (k * jnp.exp(b_last - b)).astype(BF16)
        attn = lax.dot_general(q_dec, k_inv, (((1,), (1,)), ((), ())), preferred_element_type=F32)
        attn = jnp.where(causal, attn, 0.0).astype(BF16)
        state = state_scr[...]
        o = (jnp.dot(attn, v, preferred_element_type=F32)
             + lax.dot_general(q_dec, state.astype(BF16), (((1,), (1,)), ((), ())), preferred_element_type=F32))
        kv_t = lax.dot_general(v, k_dec, (((0,), (0,)), ((), ())), preferred_element_type=F32)
        state_scr[...] = state * jnp.exp(b_last) + kv_t
        ms = jnp.mean(o * o, axis=-1, keepdims=True)
        o = o * lax.rsqrt(ms + EPS) * og * r_ref[pl.ds(r0, CHUNK), :].astype(F32)
        o_ref[pl.ds(r0, CHUNK), :] = o.astype(o_ref.dtype)
        return carry

    lax.fori_loop(0, S // CHUNK, chunk, 0)


def _gla(qkvr, la, out_g):
    B, S, _ = qkvr.shape
    H = GLA_HEADS
    tri = (jnp.arange(CHUNK)[None, :] <= jnp.arange(CHUNK)[:, None]).astype(BF16)
    kb = (H * GLA_DK) // GLA_DK
    vb = (2 * H * GLA_DK) // GLA_DV
    rb = vb + H
    return pl.pallas_call(
        functools.partial(_gla_kernel, S=S),
        grid=(B, H),
        in_specs=[pl.BlockSpec((None, S, GLA_DK), lambda b, h: (b, 0, h)),
                  pl.BlockSpec((None, S, GLA_DK), lambda b, h: (b, 0, kb + h)),
                  pl.BlockSpec((None, S, GLA_DV), lambda b, h: (b, 0, vb + h)),
                  pl.BlockSpec((None, S, GLA_DV), lambda b, h: (b, 0, rb + h)),
                  pl.BlockSpec((None, S, GLA_DK), lambda b, h: (b, 0, h)),
                  pl.BlockSpec((CHUNK, CHUNK), lambda b, h: (0, 0)),
                  pl.BlockSpec((1, GLA_DV), lambda b, h: (0, 0))],
        out_specs=pl.BlockSpec((None, S, GLA_DV), lambda b, h: (b, 0, h)),
        out_shape=jax.ShapeDtypeStruct((B, S, H * GLA_DV), BF16),
        scratch_shapes=[pltpu.VMEM((GLA_DV, GLA_DK), F32)],
        compiler_params=_cp("arbitrary", "arbitrary"),
        name="gla",
    )(qkvr, qkvr, qkvr, qkvr, la, tri, out_g)


def _post1_kernel(x_ref, o_ref, wo_ref, g1_ref, ng_ref, sc_ref, sh_ref, rw_ref, ltri_ref,
                  x3_ref, hp_ref, route_ref, cnt_ref, carry_scr, *, sub):
    i = pl.program_id(0)

    @pl.when(i == 0)
    def _():
        carry_scr[...] = jnp.zeros_like(carry_scr)

    y = jnp.dot(o_ref[...], wo_ref[...], preferred_element_type=F32)
    x3 = x_ref[...] + g1_ref[...] * y
    x3_ref[...] = x3
    hb = _modnorm(x3, ng_ref[...], sc_ref[...], sh_ref[...]).astype(BF16)

    half = D // 2
    lo = pltpu.bitcast(hb[:, :half].astype(F32), jnp.uint32)
    hi = pltpu.bitcast(hb[:, half:].astype(F32), jnp.uint32)
    hp_ref[...] = jnp.right_shift(lo, jnp.uint32(16)) | (hi & jnp.uint32(0xFFFF0000))

    tm = hb.shape[0]
    lane = lax.broadcasted_iota(jnp.int32, (tm, LANES), 1)
    lanef = lane.astype(F32)
    logits = jnp.dot(hb, rw_ref[...], preferred_element_type=F32)
    logits = jnp.where(lane < N_EXPERTS, logits, NEG)
    m1 = jnp.max(logits, axis=-1, keepdims=True)
    i1 = jnp.min(jnp.where(logits == m1, lanef, float(LANES)), axis=-1, keepdims=True)
    oh1 = lanef == i1
    rest = jnp.where(oh1, NEG, logits)
    m2 = jnp.max(rest, axis=-1, keepdims=True)
    i2 = jnp.min(jnp.where(rest == m2, lanef, float(LANES)), axis=-1, keepdims=True)
    oh2 = lanef == i2
    e = jnp.exp(m2 - m1)
    gate1 = 1.0 / (1.0 + e)
    gate2 = e / (1.0 + e)

    cnt = jnp.where(oh1, 1.0, 0.0) + jnp.where(oh2, 1.0, 0.0)
    ltri = ltri_ref[...]
    carry = carry_scr[...]
    pres = []
    for s in range(tm // sub):
        c_sub = cnt[s * sub:(s + 1) * sub, :]
        pres.append(jnp.dot(ltri, c_sub.astype(BF16), preferred_element_type=F32) + carry)
        carry = carry + jnp.sum(c_sub, axis=0, keepdims=True)
    pre = jnp.concatenate(pres, axis=0)
    carry_scr[...] = carry
    cnt_ref[...] = carry
    r1 = jnp.sum(jnp.where(oh1, pre, 0.0), axis=-1, keepdims=True)
    r2 = jnp.sum(jnp.where(oh2, pre, 0.0), axis=-1, keepdims=True)
    route = jnp.where(lane == 0, i1, 0.0)
    for idx, val in ((1, i2), (2, r1), (3, r2), (4, gate1), (5, gate2)):
        route = jnp.where(lane == idx, val, route)
    route_ref[...] = route


def _post1(x2d, o2d, S, wo, g1, ng, sc, sh, router_pad, tm=512, sub=256):
    N = x2d.shape[0]
    ltri = (jnp.arange(sub)[None, :] < jnp.arange(sub)[:, None]).astype(BF16)
    vec = pl.BlockSpec((None, 1, D), lambda i: ((i * tm) // S, 0, 0))
    row = pl.BlockSpec((tm, D), lambda i: (i, 0))
    return pl.pallas_call(
        functools.partial(_post1_kernel, sub=sub),
        grid=(N // tm,),
        in_specs=[row, row,
                  pl.BlockSpec((D, D), lambda i: (0, 0)),
                  vec,
                  pl.BlockSpec((1, D), lambda i: (0, 0)),
                  vec, vec,
                  pl.BlockSpec((D, LANES), lambda i: (0, 0)),
                  pl.BlockSpec((sub, sub), lambda i: (0, 0))],
        out_specs=[row,
                   pl.BlockSpec((tm, D // 2), lambda i: (i, 0)),
                   pl.BlockSpec((tm, LANES), lambda i: (i, 0)),
                   pl.BlockSpec((1, LANES), lambda i: (0, 0))],
        out_shape=[jax.ShapeDtypeStruct((N, D), F32),
                   jax.ShapeDtypeStruct((N, D // 2), jnp.uint32),
                   jax.ShapeDtypeStruct((N, LANES), F32),
                   jax.ShapeDtypeStruct((1, LANES), F32)],
        scratch_shapes=[pltpu.VMEM((1, LANES), F32)],
        compiler_params=_cp("arbitrary"),
        name="post1_router",
    )(x2d, o2d, wo, g1, ng, sc, sh, router_pad, ltri)


def _dispatch_kernel(pos_ref, h_ref, xs_in_ref, xs_ref, sem, *, td):
    del xs_in_ref

    def row(r, carry):
        for kk in range(2):
            p = pos_ref[0, 2 * r + kk]
            pltpu.make_async_copy(h_ref.at[pl.ds(r, 1)], xs_ref.at[pl.ds(p, 1)], sem).start()
        return carry

    lax.fori_loop(0, td, row, 0, unroll=8)
    for _ in range(2):
        pltpu.make_async_copy(h_ref, xs_ref.at[pl.ds(0, td)], sem).wait()


def _dispatch(hp, pos, P, td=512):
    N, W = hp.shape
    xs0 = jnp.zeros((P, W), jnp.uint32)
    return pl.pallas_call(
        functools.partial(_dispatch_kernel, td=td),
        grid=(N // td,),
        in_specs=[pl.BlockSpec((None, 1, 2 * td), lambda i: (i, 0, 0), memory_space=pltpu.SMEM),
                  pl.BlockSpec((td, W), lambda i: (i, 0)),
                  pl.BlockSpec(memory_space=pl.ANY)],
        out_specs=pl.BlockSpec(memory_space=pl.ANY),
        out_shape=jax.ShapeDtypeStruct((P, W), jnp.uint32),
        scratch_shapes=[pltpu.SemaphoreType.DMA(())],
        input_output_aliases={2: 0},
        compiler_params=_cp("arbitrary"),
        name="moe_dispatch",
    )(pos.reshape(N // td, 1, 2 * td), hp, xs0)


def _experts_kernel(be_ref, nv_ref, xs_ref, w1_ref, w3_ref, w2_ref, y_ref, xb_scr):
    i = pl.program_id(0)
    j = pl.program_id(1)
    valid = i < nv_ref[0]

    @pl.when(valid & (j == 0))
    def _():
        w = xs_ref[...]
        half = D // 2
        xb_scr[:, :half] = pltpu.bitcast(jnp.left_shift(w, jnp.uint32(16)), F32).astype(BF16)
        xb_scr[:, half:] = pltpu.bitcast(w & jnp.uint32(0xFFFF0000), F32).astype(BF16)
        y_ref[...] = jnp.zeros_like(y_ref)

    @pl.when(jnp.logical_not(valid) & (j == 0))
    def _():
        y_ref[...] = jnp.zeros_like(y_ref)

    @pl.when(valid)
    def _():
        xb = xb_scr[...]
        a = jnp.dot(xb, w1_ref[...], preferred_element_type=F32)
        b = jnp.dot(xb, w3_ref[...], preferred_element_type=F32)
        gact = (a * _sigmoid(a) * b).astype(BF16)
        y_ref[...] += jnp.dot(gact, w2_ref[...], preferred_element_type=F32)


def _experts(xs, blk_expert, n_valid, w1, w3, w2, tm, tf=512):
    P, W = xs.shape
    E, _, F = w1.shape
    nblk = P // tm
    nj = F // tf

    def _i(i, nv):
        return jnp.minimum(i, nv[0] - 1)

    def _j(i, j, nv):
        return jnp.where(i < nv[0], j, nj - 1)

    grid_spec = pltpu.PrefetchScalarGridSpec(
        num_scalar_prefetch=2,
        grid=(nblk, nj),
        in_specs=[pl.BlockSpec((tm, W), lambda i, j, be, nv: (_i(i, nv), 0)),
                  pl.BlockSpec((None, D, tf), lambda i, j, be, nv: (be[_i(i, nv)], 0, _j(i, j, nv))),
                  pl.BlockSpec((None, D, tf), lambda i, j, be, nv: (be[_i(i, nv)], 0, _j(i, j, nv))),
                  pl.BlockSpec((None, tf, D), lambda i, j, be, nv: (be[_i(i, nv)], _j(i, j, nv), 0))],
        out_specs=pl.BlockSpec((tm, D), lambda i, j, be, nv: (i, 0)),
        scratch_shapes=[pltpu.VMEM((tm, D), BF16)],
    )
    return pl.pallas_call(
        _experts_kernel,
        grid_spec=grid_spec,
        out_shape=jax.ShapeDtypeStruct((P, D), F32),
        compiler_params=_cp("arbitrary", "arbitrary"),
        name="moe_experts",
    )(blk_expert, n_valid, xs, w1, w3, w2)


def _combine_kernel(pos_ref, route_ref, x_ref, g2_ref, y_ref, out_ref, ybuf, sem, *, tc):
    def row(r, carry):
        for kk in range(2):
            p = pos_ref[0, 2 * r + kk]
            pltpu.make_async_copy(y_ref.at[pl.ds(p, 1)], ybuf.at[kk, pl.ds(r, 1)], sem).start()
        return carry

    lax.fori_loop(0, tc, row, 0, unroll=8)
    for kk in range(2):
        pltpu.make_async_copy(y_ref.at[pl.ds(0, tc)], ybuf.at[kk], sem).wait()
    route = route_ref[...]
    moe = route[:, 4:5] * ybuf[0] + route[:, 5:6] * ybuf[1]
    out_ref[...] = x_ref[...] + g2_ref[...] * moe


def _combine(y, pos, route, x3, S, g2, tc=256):
    N = x3.shape[0]
    return pl.pallas_call(
        functools.partial(_combine_kernel, tc=tc),
        grid=(N // tc,),
        in_specs=[pl.BlockSpec((None, 1, 2 * tc), lambda i: (i, 0, 0), memory_space=pltpu.SMEM),
                  pl.BlockSpec((tc, LANES), lambda i: (i, 0)),
                  pl.BlockSpec((tc, D), lambda i: (i, 0)),
                  pl.BlockSpec((None, 1, D), lambda i: ((i * tc) // S, 0, 0)),
                  pl.BlockSpec(memory_space=pl.ANY)],
        out_specs=pl.BlockSpec((tc, D), lambda i: (i, 0)),
        out_shape=jax.ShapeDtypeStruct((N, D), F32),
        scratch_shapes=[pltpu.VMEM((2, tc, D), F32), pltpu.SemaphoreType.DMA(())],
        compiler_params=_cp("arbitrary"),
        name="moe_combine",
    )(pos.reshape(N // tc, 1, 2 * tc), route, x3, g2, y)


def _moe(hp, route, counts, x3, S, g2, w1, w3, w2, tm=1024):
    N = x3.shape[0]
    cnt = counts[0, :N_EXPERTS].astype(jnp.int32)
    nblk_e = (cnt + tm - 1) // tm
    blk_end = jnp.cumsum(nblk_e)
    row_start = (blk_end - nblk_e) * tm
    e_idx = route[:, 0:2].astype(jnp.int32)
    rank = route[:, 2:4].astype(jnp.int32)
    pos = row_start[e_idx] + rank
    P = 2 * N + N_EXPERTS * tm
    nblk = P // tm
    blk_expert = jnp.clip(jnp.searchsorted(blk_end, jnp.arange(nblk, dtype=jnp.int32), side="right"),
                          0, N_EXPERTS - 1).astype(jnp.int32)
    n_valid = blk_end[-1:].astype(jnp.int32)
    xs = _dispatch(hp, pos, P)
    y = _experts(xs, blk_expert, n_valid, w1, w3, w2, tm)
    return _combine(y, pos, route, x3, S, g2)


def kernel(x, c, ada_w, ada_b, norm1_g, norm2_g, da_w_in, da_q_gain, da_k_gain, da_lam_q1, da_lam_k1, da_lam_q2,
           da_lam_k2, da_subln_g, da_w_out, gla_w_in, gla_w_a1, gla_w_a2, gla_b_a, gla_out_g, gla_w_out,
           ffn_w1, ffn_w3, ffn_w2, moe_router, moe_w1, moe_w3, moe_w2):
    B, S, _ = x.shape
    N = B * S
    mod = _adaln(c, ada_w, ada_b)
    mods = [[mod[l, :, k * D:(k + 1) * D].reshape(B, 1, D) for k in range(6)] for l in range(2)]
    x2d = x.reshape(N, D)

    sh1, sc1, gt1, sh2, sc2, gt2 = mods[0]
    lambda_init = 0.8 - 0.6 * math.exp(-0.3 * 0)
    qk_scale = DA_HEAD_DIM ** -0.5
    gain_row = jnp.concatenate([jnp.tile(da_q_gain[0].reshape(-1) * qk_scale, DA_HEADS),
                                jnp.tile(da_k_gain[0].reshape(-1), DA_HEADS),
                                jnp.ones((D,), F32)]).reshape(1, 3 * D)
    qkv = _da_inproj(x2d, S, norm1_g[0].reshape(1, D), sc1, sh1, da_w_in[0].astype(BF16), gain_row)
    slopes = 2.0 ** (-8.0 * jnp.arange(1, DA_HEADS + 1, dtype=F32) / DA_HEADS)
    o = _da_attention(qkv.reshape(B, S, 3 * D), slopes,
                      da_lam_q1[0].reshape(1, -1), da_lam_k1[0].reshape(1, -1),
                      da_lam_q2[0].reshape(1, -1), da_lam_k2[0].reshape(1, -1),
                      da_subln_g[0].reshape(1, -1), lambda_init)
    x2d = _post0(x2d, o.reshape(N, D), S, da_w_out[0].astype(BF16), gt1, norm2_g[0].reshape(1, D), sc2, sh2, gt2,
                 ffn_w1[0].astype(BF16), ffn_w3[0].astype(BF16), ffn_w2[0].astype(BF16))

    sh1, sc1, gt1, sh2, sc2, gt2 = mods[1]
    kw = GLA_HEADS * GLA_DK
    mult_row = jnp.concatenate([jnp.full((kw,), GLA_DK ** -0.5, F32), jnp.ones((3 * D - kw,), F32)]).reshape(1, -1)
    rank = gla_w_a1.shape[-1]
    wa1 = jnp.zeros((D, LANES), BF16).at[:, :rank].set(gla_w_a1[0].astype(BF16))
    wa2 = jnp.zeros((LANES, kw), BF16).at[:rank, :].set(gla_w_a2[0].astype(BF16))
    qkvr, la = _gla_inproj(x2d, S, norm1_g[1].reshape(1, D), sc1, sh1, gla_w_in[0].astype(BF16), mult_row,
                           wa1, wa2, gla_b_a[0].reshape(1, kw))
    o = _gla(qkvr.reshape(B, S, 3 * D), la.reshape(B, S, kw), gla_out_g[0].reshape(1, -1))
    router_pad = jnp.zeros((D, LANES), BF16).at[:, :N_EXPERTS].set(moe_router[0].astype(BF16))
    x3, hp, route, counts = _post1(x2d, o.reshape(N, D), S, gla_w_out[0].astype(BF16), gt1,
                                   norm2_g[1].reshape(1, D), sc2, sh2, router_pad)
    out = _moe(hp, route, counts, x3, S, gt2,
               moe_w1[0].astype(BF16), moe_w3[0].astype(BF16), moe_w2[0].astype(BF16))
    return out.reshape(B, S, D)
```

```python
import functools
import math

import jax
import jax.numpy as jnp
from jax import lax
from jax.experimental import pallas as pl
from jax.experimental.pallas import tpu as pltpu

F32 = jnp.float32
BF16 = jnp.bfloat16

D = 1024
EPS = 1e-6
NEG = -1e30
CHUNK = 64
DA_HEADS = 8
DA_HEAD_DIM = 64
GLA_HEADS = 4
GLA_DK = 128
GLA_DV = 256
GLA_TAU = 16.0
N_EXPERTS = 8
LANES = 128
LOG2E = 1.4426950408889634
VT_ROWS = LANES + 16

VMEM_LIMIT = 56 * 1024 * 1024


def _cp(*sem):
    return pltpu.CompilerParams(dimension_semantics=sem, vmem_limit_bytes=VMEM_LIMIT)


def _sigmoid(x):
    return 1.0 / (1.0 + jnp.exp(-x))


def _modnorm(x, g, sc, sh):
    ms = jnp.mean(x * x, axis=-1, keepdims=True)
    return (x * lax.rsqrt(ms + EPS) * g) * (1.0 + sc) + sh


def _adaln_kernel(c_ref, w_ref, b_ref, o_ref):
    c = c_ref[...]
    ca = (c * _sigmoid(c)).astype(BF16)
    o_ref[...] = jnp.dot(ca, w_ref[...].astype(BF16), preferred_element_type=F32) + b_ref[...]


def _adaln(c, ada_w, ada_b):
    L, _, n6 = ada_w.shape
    B = c.shape[0]
    tn = 1536
    return pl.pallas_call(
        _adaln_kernel,
        grid=(L, n6 // tn),
        in_specs=[pl.BlockSpec((B, D), lambda l, n: (0, 0)),
                  pl.BlockSpec((None, D, tn), lambda l, n: (l, 0, n)),
                  pl.BlockSpec((None, 1, tn), lambda l, n: (l, 0, n))],
        out_specs=pl.BlockSpec((None, B, tn), lambda l, n: (l, 0, n)),
        out_shape=jax.ShapeDtypeStruct((L, B, n6), F32),
        compiler_params=_cp("arbitrary", "arbitrary"),
        name="adaln",
    )(c, ada_w, ada_b.reshape(L, 1, n6))


def _da_inproj_kernel(x_ref, g_ref, sc_ref, sh_ref, w_ref, gain_ref, gsum_ref, o_ref, h_scr, *, n_qk_blocks):
    j = pl.program_id(1)

    @pl.when(j == 0)
    def _():
        h_scr[...] = _modnorm(x_ref[...], g_ref[...], sc_ref[...], sh_ref[...]).astype(BF16)

    acc = jnp.dot(h_scr[...], w_ref[...], preferred_element_type=F32)
    y2 = (acc * acc).astype(BF16)
    gsum = gsum_ref[...]
    tn = acc.shape[1]
    ss = jnp.concatenate(
        [jnp.dot(y2[:, c * 256:(c + 1) * 256], gsum, preferred_element_type=F32) for c in range(tn // 256)],
        axis=1)
    rs = lax.rsqrt(ss * (1.0 / DA_HEAD_DIM) + EPS)
    rs = jnp.where(j < n_qk_blocks, rs, 1.0)
    o_ref[...] = (acc * rs * gain_ref[...]).astype(BF16)


def _da_inproj(x2d, S, g, sc, sh, w_bf, gain_row, tm=1024, tn=512):
    N = x2d.shape[0]
    n_out = w_bf.shape[1]
    r = jnp.arange(256) // DA_HEAD_DIM
    gsum = (r[:, None] == r[None, :]).astype(BF16)
    vec = pl.BlockSpec((None, 1, D), lambda i, j: ((i * tm) // S, 0, 0))
    return pl.pallas_call(
        functools.partial(_da_inproj_kernel, n_qk_blocks=(2 * D) // tn),
        grid=(N // tm, n_out // tn),
        in_specs=[pl.BlockSpec((tm, D), lambda i, j: (i, 0)),
                  pl.BlockSpec((1, D), lambda i, j: (0, 0)),
                  vec, vec,
                  pl.BlockSpec((D, tn), lambda i, j: (0, j)),
                  pl.BlockSpec((1, tn), lambda i, j: (0, j)),
                  pl.BlockSpec((256, 256), lambda i, j: (0, 0))],
        out_specs=pl.BlockSpec((tm, tn), lambda i, j: (i, j)),
        out_shape=jax.ShapeDtypeStruct((N, n_out), BF16),
        scratch_shapes=[pltpu.VMEM((tm, D), BF16)],
        compiler_params=_cp("arbitrary", "arbitrary"),
        name="da_inproj",
    )(x2d, g, sc, sh, w_bf, gain_row, gsum)


def _da_attn_kernel(slope_ref, q_ref, k_ref, v_ref, lq1_ref, lk1_ref, lq2_ref, lk2_ref, subg_ref, o_ref,
                    k1_scr, k2_scr, vt_scr, corr_scr, s1e_scr, s2e_scr, s1o_scr, s2o_scr, qt1_scr, qt2_scr, a1, a2,
                    *, T, SUB, S, lambda_init):
    h = pl.program_id(1)
    slope2 = slope_ref[h] * LOG2E
    lane = lax.broadcasted_iota(jnp.int32, (T, LANES), 1)
    local = lax.broadcasted_iota(jnp.int32, (T, LANES), 0)
    lo = jnp.bitwise_and(local, 255).astype(F32)
    hi = (local - jnp.bitwise_and(local, 255)).astype(F32)
    first_half = lane < DA_HEAD_DIM

    def aug(base, vals):
        out = jnp.zeros((T, LANES), F32)
        for off, val in enumerate(vals):
            out = jnp.where(lane == base + off, val, out)
        return out.astype(BF16)

    c_hi = (slope2 + jnp.zeros((T, LANES), F32)).astype(BF16).astype(F32)
    c_mid = (slope2 - c_hi).astype(BF16).astype(F32)
    c_lo = slope2 - c_hi - c_mid
    k_vals = (lo, lo, lo, hi, hi, hi, -c_hi, -c_mid, -c_lo, -c_hi, -c_mid, -c_lo)
    q_vals = (c_hi, c_mid, c_lo, c_hi, c_mid, c_lo, lo, lo, lo, hi, hi, hi)
    k_aug1 = aug(DA_HEAD_DIM, k_vals)
    k_aug2 = aug(0, k_vals)
    q_aug1 = aug(DA_HEAD_DIM, q_vals)
    q_aug2 = aug(0, q_vals)
    ones_rows = jnp.where(lax.broadcasted_iota(jnp.int32, (VT_ROWS - LANES, T), 0) == 0, 1.0, 0.0).astype(BF16)

    def prep(j, carry):
        r0 = pl.multiple_of(j * T, T)
        k = k_ref[pl.ds(r0, T), :]
        k1_scr[pl.ds(r0, T), :] = jnp.where(first_half, k, k_aug1)
        k2_scr[pl.ds(r0, T), :] = jnp.where(first_half, k_aug2, k)
        vt_scr[:LANES, pl.ds(r0, T)] = v_ref[pl.ds(r0, T), :].astype(F32).T.astype(BF16)
        vt_scr[LANES:, pl.ds(r0, T)] = ones_rows
        return carry

    lax.fori_loop(0, S // T, prep, 0)

    kj = lax.broadcasted_iota(jnp.int32, (T, T), 0)
    qj = lax.broadcasted_iota(jnp.int32, (T, T), 1)
    visible = jnp.right_shift(kj, 6) <= jnp.right_shift(qj, 6)
    ahead = jnp.maximum(kj - qj, 0).astype(F32)
    corr_scr[...] = jnp.where(visible, -2.0 * slope2 * ahead, NEG)

    lam = (jnp.exp(jnp.sum(lq1_ref[...] * lk1_ref[...], axis=-1, keepdims=True))
           - jnp.exp(jnp.sum(lq2_ref[...] * lk2_ref[...], axis=-1, keepdims=True)) + lambda_init)
    k_scrs = (k1_scr, k2_scr)
    s_scrs = ((s1e_scr, s2e_scr), (s1o_scr, s2o_scr))
    qt_scrs = (qt1_scr, qt2_scr)
    accs = (a1, a2)
    n_q = S // T
    TK = T // SUB
    neg = jnp.full((1, T), NEG, F32)

    def load_queries(qi):
        q = q_ref[pl.ds(pl.multiple_of(qi * T, T), T), :]
        for qt_scr, qa in zip(qt_scrs, (jnp.where(first_half, q, q_aug1), jnp.where(first_half, q_aug2, q))):
            qt_scr[...] = qa.astype(F32).T.astype(BF16)

    def block_shift(qi, j):
        return -slope2 * lax.convert_element_type((qi - j) * T, F32)

    def score_chunk(qi, par, j, part, ms, corr):
        k0 = pl.multiple_of(j * T + part * TK, TK)
        out = []
        for mp in range(2):
            s = jnp.dot(k_scrs[mp][pl.ds(k0, TK), :], qt_scrs[mp][...], preferred_element_type=F32)
            if corr is not None:
                s = s + corr[part * TK:(part + 1) * TK, :]
            s_scrs[par][mp][pl.ds(k0, TK), :] = s
            out.append(jnp.maximum(ms[mp], jnp.max(s, axis=0, keepdims=True) + block_shift(qi, j)))
        return tuple(out)

    def weight_chunk(qi, par, j, part, ms):
        k0 = pl.multiple_of(j * T + part * TK, TK)
        vt = vt_scr[:, pl.ds(k0, TK)]
        for mp in range(2):
            p = jnp.exp2(s_scrs[par][mp][pl.ds(k0, TK), :] + (block_shift(qi, j) - ms[mp]))
            accs[mp][...] += jnp.dot(vt, p.astype(BF16), preferred_element_type=F32)

    def finalize(qi):
        acc1 = a1[...]
        acc2 = a2[...]
        o = (acc1[:LANES] / acc1[LANES:LANES + 1] - lam * (acc2[:LANES] / acc2[LANES:LANES + 1])).T
        msq = jnp.mean(o * o, axis=-1, keepdims=True)
        o = o * lax.rsqrt(msq + EPS) * subg_ref[...] * (1.0 - lambda_init)
        o_ref[pl.ds(pl.multiple_of(qi * T, T), T), :] = o.astype(o_ref.dtype)
        a1[...] = jnp.zeros_like(a1)
        a2[...] = jnp.zeros_like(a2)

    a1[...] = jnp.zeros_like(a1)
    a2[...] = jnp.zeros_like(a2)
    load_queries(0)
    ms0 = (neg, neg)
    for part in range(SUB):
        ms0 = score_chunk(0, 0, 0, part, ms0, corr_scr)

    def q_block(qi, par, ms):
        load_queries(qi + 1)

        def chunk(j, ms_next):
            for part in range(SUB):
                ms_next = score_chunk(qi + 1, 1 - par, j, part, ms_next, None)
                weight_chunk(qi, par, j, part, ms)
            return ms_next

        ms_next = lax.fori_loop(0, qi + 1, chunk, (neg, neg))
        for part in range(SUB):
            ms_next = score_chunk(qi + 1, 1 - par, qi + 1, part, ms_next, corr_scr)
        finalize(qi)
        return ms_next

    def q_pair(qq, ms):
        return q_block(2 * qq + 1, 1, q_block(2 * qq, 0, ms))

    ms_even = lax.fori_loop(0, n_q // 2 - 1, q_pair, ms0)
    ms_last = q_block(n_q - 2, 0, ms_even)

    def last_chunk(j, carry):
        for part in range(SUB):
            weight_chunk(n_q - 1, 1, j, part, ms_last)
        return carry

    lax.fori_loop(0, n_q, last_chunk, 0)
    finalize(n_q - 1)


def _da_attention(qkv, slopes, lq1, lk1, lq2, lk2, subg, lambda_init, T=512, sub=2):
    B, S, _ = qkv.shape
    H = DA_HEADS
    vec64 = pl.BlockSpec((1, DA_HEAD_DIM), lambda b, h: (0, 0))
    return pl.pallas_call(
        functools.partial(_da_attn_kernel, T=T, SUB=sub, S=S, lambda_init=lambda_init),
        grid=(B, H),
        in_specs=[pl.BlockSpec(memory_space=pltpu.SMEM),
                  pl.BlockSpec((None, S, LANES), lambda b, h: (b, 0, h)),
                  pl.BlockSpec((None, S, LANES), lambda b, h: (b, 0, H + h)),
                  pl.BlockSpec((None, S, LANES), lambda b, h: (b, 0, 2 * H + h)),
                  vec64, vec64, vec64, vec64,
                  pl.BlockSpec((1, LANES), lambda b, h: (0, 0))],
        out_specs=pl.BlockSpec((None, S, LANES), lambda b, h: (b, 0, h)),
        out_shape=jax.ShapeDtypeStruct((B, S, H * LANES), BF16),
        scratch_shapes=[pltpu.VMEM((S, LANES), BF16), pltpu.VMEM((S, LANES), BF16), pltpu.VMEM((VT_ROWS, S), BF16),
                        pltpu.VMEM((T, T), F32)] + [pltpu.VMEM((S, T), F32) for _ in range(4)] + [
                        pltpu.VMEM((LANES, T), BF16), pltpu.VMEM((LANES, T), BF16),
                        pltpu.VMEM((VT_ROWS, T), F32), pltpu.VMEM((VT_ROWS, T), F32)],
        compiler_params=_cp("arbitrary", "arbitrary"),
        name="da_attention",
    )(slopes, qkv, qkv, qkv, lq1, lk1, lq2, lk2, subg)


def _post0_kernel(x_ref, o_ref, wo_ref, g1_ref, ng_ref, sc_ref, sh_ref, g2_ref, w1_ref, w3_ref, w2_ref,
                  out_ref, x1_scr, h_scr, acc_scr):
    j = pl.program_id(1)

    @pl.when(j == 0)
    def _():
        y = jnp.dot(o_ref[...], wo_ref[...], preferred_element_type=F32)
        x1 = x_ref[...] + g1_ref[...] * y
        x1_scr[...] = x1
        h_scr[...] = _modnorm(x1, ng_ref[...], sc_ref[...], sh_ref[...]).astype(BF16)
        acc_scr[...] = jnp.zeros_like(acc_scr)

    h = h_scr[...]
    a = jnp.dot(h, w1_ref[...], preferred_element_type=F32)
    b = jnp.dot(h, w3_ref[...], preferred_element_type=F32)
    gact = (a * _sigmoid(a) * b).astype(BF16)
    acc_scr[...] += jnp.dot(gact, w2_ref[...], preferred_element_type=F32)

    @pl.when(j == pl.num_programs(1) - 1)
    def _():
        out_ref[...] = x1_scr[...] + g2_ref[...] * acc_scr[...]


def _post0(x2d, o2d, S, wo, g1, ng, sc, sh, g2, w1, w3, w2, tm=512, tf=1408):
    N = x2d.shape[0]
    F = w1.shape[1]
    vec = pl.BlockSpec((None, 1, D), lambda i, j: ((i * tm) // S, 0, 0))
    row = pl.BlockSpec((tm, D), lambda i, j: (i, 0))
    return pl.pallas_call(
        _post0_kernel,
        grid=(N // tm, F // tf),
        in_specs=[row, row,
                  pl.BlockSpec((D, D), lambda i, j: (0, 0)),
                  vec,
                  pl.BlockSpec((1, D), lambda i, j: (0, 0)),
                  vec, vec, vec,
                  pl.BlockSpec((D, tf), lambda i, j: (0, j)),
                  pl.BlockSpec((D, tf), lambda i, j: (0, j)),
                  pl.BlockSpec((tf, D), lambda i, j: (j, 0))],
        out_specs=row,
        out_shape=jax.ShapeDtypeStruct((N, D), F32),
        scratch_shapes=[pltpu.VMEM((tm, D), F32), pltpu.VMEM((tm, D), BF16), pltpu.VMEM((tm, D), F32)],
        compiler_params=_cp("arbitrary", "arbitrary"),
        name="post0_ffn",
    )(x2d, o2d, wo, g1, ng, sc, sh, g2, w1, w3, w2)


def _gla_inproj_kernel(x_ref, g_ref, sc_ref, sh_ref, w_ref, mult_ref, wa1_ref, wa2_ref, ba_ref,
                       o_ref, la_ref, h_scr, *, n_plain_blocks):
    j = pl.program_id(1)

    @pl.when(j == 0)
    def _():
        h = _modnorm(x_ref[...], g_ref[...], sc_ref[...], sh_ref[...]).astype(BF16)
        h_scr[...] = h
        low = jnp.dot(h, wa1_ref[...], preferred_element_type=F32).astype(BF16)
        z = jnp.dot(low, wa2_ref[...], preferred_element_type=F32) + ba_ref[...]
        log_sig = jnp.minimum(z, 0.0) - jnp.log(1.0 + jnp.exp(-jnp.abs(z)))
        la_ref[...] = log_sig * (1.0 / GLA_TAU)

    acc = jnp.dot(h_scr[...], w_ref[...], preferred_element_type=F32)

    @pl.when(j < n_plain_blocks)
    def _():
        o_ref[...] = (acc * mult_ref[...]).astype(BF16)

    @pl.when(j >= n_plain_blocks)
    def _():
        o_ref[...] = (acc * _sigmoid(acc)).astype(BF16)


def _gla_inproj(x2d, S, g, sc, sh, w_bf, mult_row, wa1, wa2, ba, tm=1024, tn=512):
    N = x2d.shape[0]
    n_out = w_bf.shape[1]
    kw = wa2.shape[1]
    vec = pl.BlockSpec((None, 1, D), lambda i, j: ((i * tm) // S, 0, 0))
    return pl.pallas_call(
        functools.partial(_gla_inproj_kernel, n_plain_blocks=(2 * D) // tn),
        grid=(N // tm, n_out // tn),
        in_specs=[pl.BlockSpec((tm, D), lambda i, j: (i, 0)),
                  pl.BlockSpec((1, D), lambda i, j: (0, 0)),
                  vec, vec,
                  pl.BlockSpec((D, tn), lambda i, j: (0, j)),
                  pl.BlockSpec((1, tn), lambda i, j: (0, j)),
                  pl.BlockSpec((D, LANES), lambda i, j: (0, 0)),
                  pl.BlockSpec((LANES, kw), lambda i, j: (0, 0)),
                  pl.BlockSpec((1, kw), lambda i, j: (0, 0))],
        out_specs=[pl.BlockSpec((tm, tn), lambda i, j: (i, j)),
                   pl.BlockSpec((tm, kw), lambda i, j: (i, 0))],
        out_shape=[jax.ShapeDtypeStruct((N, n_out), BF16), jax.ShapeDtypeStruct((N, kw), F32)],
        scratch_shapes=[pltpu.VMEM((tm, D), BF16)],
        compiler_params=_cp("arbitrary", "arbitrary"),
        name="gla_inproj",
    )(x2d, g, sc, sh, w_bf, mult_row, wa1, wa2, ba)


def _gla_kernel(q_ref, k_ref, v_ref, r_ref, la_ref, tri_ref, og_ref, o_ref, state_scr, *, S):
    state_scr[...] = jnp.zeros_like(state_scr)
    tri = tri_ref[...]
    ti = lax.broadcasted_iota(jnp.int32, (CHUNK, CHUNK), 0)
    si = lax.broadcasted_iota(jnp.int32, (CHUNK, CHUNK), 1)
    causal = si <= ti
    og = og_ref[...]

    def chunk(c, carry):
        r0 = pl.multiple_of(c * CHUNK, CHUNK)
        la = la_ref[pl.ds(r0, CHUNK), :]
        la_hi = la.astype(BF16)
        la_lo = (la - la_hi.astype(F32)).astype(BF16)
        b = (jnp.dot(tri, la_hi, preferred_element_type=F32)
             + jnp.dot(tri, la_lo, preferred_element_type=F32))
        b_last = b[CHUNK - 1:CHUNK, :]
        q = q_ref[pl.ds(r0, CHUNK), :].astype(F32)
        k = k_ref[pl.ds(r0, CHUNK), :].astype(F32)
        v = v_ref[pl.ds(r0, CHUNK), :]
        q_dec = (q * jnp.exp(b)).astype(BF16)
        k_inv = (k * jnp.exp(-b)).astype(BF16)
        k_dec = (k * jnp.exp(b_last - b)).astype(BF16)
        attn = lax.dot_general(q_dec, k_inv, (((1,), (1,)), ((), ())), preferred_element_type=F32)
        attn = jnp.where(causal, attn, 0.0).astype(BF16)
        state = state_scr[...]
        o = (jnp.dot(attn, v, preferred_element_type=F32)
             + lax.dot_general(q_dec, state.astype(BF16), (((1,), (1,)), ((), ())), preferred_element_type=F32))
        kv_t = lax.dot_general(v, k_dec, (((0,), (0,)), ((), ())), preferred_element_type=F32)
        state_scr[...] = state * jnp.exp(b_last) + kv_t
        ms = jnp.mean(o * o, axis=-1, keepdims=True)
        o = o * lax.rsqrt(ms + EPS) * og * r_ref[pl.ds(r0, CHUNK), :].astype(F32)
        o_ref[pl.ds(r0, CHUNK), :] = o.astype(o_ref.dtype)
        return carry

    lax.fori_loop(0, S // CHUNK, chunk, 0)


def _gla(qkvr, la, out_g):
    B, S, _ = qkvr.shape
    H = GLA_HEADS
    tri = (jnp.arange(CHUNK)[None, :] <= jnp.arange(CHUNK)[:, None]).astype(BF16)
    kb = (H * GLA_DK) // GLA_DK
    vb = (2 * H * GLA_DK) // GLA_DV
    rb = vb + H
    return pl.pallas_call(
        functools.partial(_gla_kernel, S=S),
        grid=(B, H),
        in_specs=[pl.BlockSpec((None, S, GLA_DK), lambda b, h: (b, 0, h)),
                  pl.BlockSpec((None, S, GLA_DK), lambda b, h: (b, 0, kb + h)),
                  pl.BlockSpec((None, S, GLA_DV), lambda b, h: (b, 0, vb + h)),
                  pl.BlockSpec((None, S, GLA_DV), lambda b, h: (b, 0, rb + h)),
                  pl.BlockSpec((None, S, GLA_DK), lambda b, h: (b, 0, h)),
                  pl.BlockSpec((CHUNK, CHUNK), lambda b, h: (0, 0)),
                  pl.BlockSpec((1, GLA_DV), lambda b, h: (0, 0))],
        out_specs=pl.BlockSpec((None, S, GLA_DV), lambda b, h: (b, 0, h)),
        out_shape=jax.ShapeDtypeStruct((B, S, H * GLA_DV), BF16),
        scratch_shapes=[pltpu.VMEM((GLA_DV, GLA_DK), F32)],
        compiler_params=_cp("arbitrary", "arbitrary"),
        name="gla",
    )(qkvr, qkvr, qkvr, qkvr, la, tri, out_g)


def _post1_kernel(x_ref, o_ref, wo_ref, g1_ref, ng_ref, sc_ref, sh_ref, rw_ref, ltri_ref,
                  x3_ref, hp_ref, route_ref, cnt_ref, carry_scr, *, sub):
    i = pl.program_id(0)

    @pl.when(i == 0)
    def _():
        carry_scr[...] = jnp.zeros_like(carry_scr)

    y = jnp.dot(o_ref[...], wo_ref[...], preferred_element_type=F32)
    x3 = x_ref[...] + g1_ref[...] * y
    x3_ref[...] = x3
    hb = _modnorm(x3, ng_ref[...], sc_ref[...], sh_ref[...]).astype(BF16)

    half = D // 2
    lo = pltpu.bitcast(hb[:, :half].astype(F32), jnp.uint32)
    hi = pltpu.bitcast(hb[:, half:].astype(F32), jnp.uint32)
    hp_ref[...] = jnp.right_shift(lo, jnp.uint32(16)) | (hi & jnp.uint32(0xFFFF0000))

    tm = hb.shape[0]
    lane = lax.broadcasted_iota(jnp.int32, (tm, LANES), 1)
    lanef = lane.astype(F32)
    logits = jnp.dot(hb, rw_ref[...], preferred_element_type=F32)
    logits = jnp.where(lane < N_EXPERTS, logits, NEG)
    m1 = jnp.max(logits, axis=-1, keepdims=True)
    i1 = jnp.min(jnp.where(logits == m1, lanef, float(LANES)), axis=-1, keepdims=True)
    oh1 = lanef == i1
    rest = jnp.where(oh1, NEG, logits)
    m2 = jnp.max(rest, axis=-1, keepdims=True)
    i2 = jnp.min(jnp.where(rest == m2, lanef, float(LANES)), axis=-1, keepdims=True)
    oh2 = lanef == i2
    e = jnp.exp(m2 - m1)
    gate1 = 1.0 / (1.0 + e)
    gate2 = e / (1.0 + e)

    cnt = jnp.where(oh1, 1.0, 0.0) + jnp.where(oh2, 1.0, 0.0)
    ltri = ltri_ref[...]
    carry = carry_scr[...]
    pres = []
    for s in range(tm // sub):
        c_sub = cnt[s * sub:(s + 1) * sub, :]
        pres.append(jnp.dot(ltri, c_sub.astype(BF16), preferred_element_type=F32) + carry)
        carry = carry + jnp.sum(c_sub, axis=0, keepdims=True)
    pre = jnp.concatenate(pres, axis=0)
    carry_scr[...] = carry
    cnt_ref[...] = carry
    r1 = jnp.sum(jnp.where(oh1, pre, 0.0), axis=-1, keepdims=True)
    r2 = jnp.sum(jnp.where(oh2, pre, 0.0), axis=-1, keepdims=True)
    route = jnp.where(lane == 0, i1, 0.0)
    for idx, val in ((1, i2), (2, r1), (3, r2), (4, gate1), (5, gate2)):
        route = jnp.where(lane == idx, val, route)
    route_ref[...] = route


def _post1(x2d, o2d, S, wo, g1, ng, sc, sh, router_pad, tm=512, sub=256):
    N = x2d.shape[0]
    ltri = (jnp.arange(sub)[None, :] < jnp.arange(sub)[:, None]).astype(BF16)
    vec = pl.BlockSpec((None, 1, D), lambda i: ((i * tm) // S, 0, 0))
    row = pl.BlockSpec((tm, D), lambda i: (i, 0))
    return pl.pallas_call(
        functools.partial(_post1_kernel, sub=sub),
        grid=(N // tm,),
        in_specs=[row, row,
                  pl.BlockSpec((D, D), lambda i: (0, 0)),
                  vec,
                  pl.BlockSpec((1, D), lambda i: (0, 0)),
                  vec, vec,
                  pl.BlockSpec((D, LANES), lambda i: (0, 0)),
                  pl.BlockSpec((sub, sub), lambda i: (0, 0))],
        out_specs=[row,
                   pl.BlockSpec((tm, D // 2), lambda i: (i, 0)),
                   pl.BlockSpec((tm, LANES), lambda i: (i, 0)),
                   pl.BlockSpec((1, LANES), lambda i: (0, 0))],
        out_shape=[jax.ShapeDtypeStruct((N, D), F32),
                   jax.ShapeDtypeStruct((N, D // 2), jnp.uint32),
                   jax.ShapeDtypeStruct((N, LANES), F32),
                   jax.ShapeDtypeStruct((1, LANES), F32)],
        scratch_shapes=[pltpu.VMEM((1, LANES), F32)],
        compiler_params=_cp("arbitrary"),
        name="post1_router",
    )(x2d, o2d, wo, g1, ng, sc, sh, router_pad, ltri)


def _dispatch_kernel(pos_ref, h_ref, xs_in_ref, xs_ref, sem, *, td):
    del xs_in_ref

    def row(r, carry):
        for kk in range(2):
            p = pos_ref[0, 2 * r + kk]
            pltpu.make_async_copy(h_ref.at[pl.ds(r, 1)], xs_ref.at[pl.ds(p, 1)], sem).start()
        return carry

    lax.fori_loop(0, td, row, 0, unroll=8)
    for _ in range(2):
        pltpu.make_async_copy(h_ref, xs_ref.at[pl.ds(0, td)], sem).wait()


def _dispatch(hp, pos, P, td=512):
    N, W = hp.shape
    xs0 = jnp.zeros((P, W), jnp.uint32)
    return pl.pallas_call(
        functools.partial(_dispatch_kernel, td=td),
        grid=(N // td,),
        in_specs=[pl.BlockSpec((None, 1, 2 * td), lambda i: (i, 0, 0), memory_space=pltpu.SMEM),
                  pl.BlockSpec((td, W), lambda i: (i, 0)),
                  pl.BlockSpec(memory_space=pl.ANY)],
        out_specs=pl.BlockSpec(memory_space=pl.ANY),
        out_shape=jax.ShapeDtypeStruct((P, W), jnp.uint32),
        scratch_shapes=[pltpu.SemaphoreType.DMA(())],
        input_output_aliases={2: 0},
        compiler_params=_cp("arbitrary"),
        name="moe_dispatch",
    )(pos.reshape(N // td, 1, 2 * td), hp, xs0)


def _experts_kernel(be_ref, nv_ref, xs_ref, w1_ref, w3_ref, w2_ref, y_ref, xb_scr):
    i = pl.program_id(0)
    j = pl.program_id(1)
    valid = i < nv_ref[0]

    @pl.when(valid & (j == 0))
    def _():
        w = xs_ref[...]
        half = D // 2
        xb_scr[:, :half] = pltpu.bitcast(jnp.left_shift(w, jnp.uint32(16)), F32).astype(BF16)
        xb_scr[:, half:] = pltpu.bitcast(w & jnp.uint32(0xFFFF0000), F32).astype(BF16)
        y_ref[...] = jnp.zeros_like(y_ref)

    @pl.when(jnp.logical_not(valid) & (j == 0))
    def _():
        y_ref[...] = jnp.zeros_like(y_ref)

    @pl.when(valid)
    def _():
        xb = xb_scr[...]
        a = jnp.dot(xb, w1_ref[...], preferred_element_type=F32)
        b = jnp.dot(xb, w3_ref[...], preferred_element_type=F32)
        gact = (a * _sigmoid(a) * b).astype(BF16)
        y_ref[...] += jnp.dot(gact, w2_ref[...], preferred_element_type=F32)


def _experts(xs, blk_expert, n_valid, w1, w3, w2, tm, tf=512):
    P, W = xs.shape
    E, _, F = w1.shape
    nblk = P // tm
    nj = F // tf

    def _i(i, nv):
        return jnp.minimum(i, nv[0] - 1)

    def _j(i, j, nv):
        return jnp.where(i < nv[0], j, nj - 1)

    grid_spec = pltpu.PrefetchScalarGridSpec(
        num_scalar_prefetch=2,
        grid=(nblk, nj),
        in_specs=[pl.BlockSpec((tm, W), lambda i, j, be, nv: (_i(i, nv), 0)),
                  pl.BlockSpec((None, D, tf), lambda i, j, be, nv: (be[_i(i, nv)], 0, _j(i, j, nv))),
                  pl.BlockSpec((None, D, tf), lambda i, j, be, nv: (be[_i(i, nv)], 0, _j(i, j, nv))),
                  pl.BlockSpec((None, tf, D), lambda i, j, be, nv: (be[_i(i, nv)], _j(i, j, nv), 0))],
        out_specs=pl.BlockSpec((tm, D), lambda i, j, be, nv: (i, 0)),
        scratch_shapes=[pltpu.VMEM((tm, D), BF16)],
    )
    return pl.pallas_call(
        _experts_kernel,
        grid_spec=grid_spec,
        out_shape=jax.ShapeDtypeStruct((P, D), F32),
        compiler_params=_cp("arbitrary", "arbitrary"),
        name="moe_experts",
    )(blk_expert, n_valid, xs, w1, w3, w2)


def _combine_kernel(pos_ref, route_ref, x_ref, g2_ref, y_ref, out_ref, ybuf, sem, *, tc):
    def row(r, carry):
        for kk in range(2):
            p = pos_ref[0, 2 * r + kk]
            pltpu.make_async_copy(y_ref.at[pl.ds(p, 1)], ybuf.at[kk, pl.ds(r, 1)], sem).start()
        return carry

    lax.fori_loop(0, tc, row, 0, unroll=8)
    for kk in range(2):
        pltpu.make_async_copy(y_ref.at[pl.ds(0, tc)], ybuf.at[kk], sem).wait()
    route = route_ref[...]
    moe = route[:, 4:5] * ybuf[0] + route[:, 5:6] * ybuf[1]
    out_ref[...] = x_ref[...] + g2_ref[...] * moe


def _combine(y, pos, route, x3, S, g2, tc=256):
    N = x3.shape[0]
    return pl.pallas_call(
        functools.partial(_combine_kernel, tc=tc),
        grid=(N // tc,),
        in_specs=[pl.BlockSpec((None, 1, 2 * tc), lambda i: (i, 0, 0), memory_space=pltpu.SMEM),
                  pl.BlockSpec((tc, LANES), lambda i: (i, 0)),
                  pl.BlockSpec((tc, D), lambda i: (i, 0)),
                  pl.BlockSpec((None, 1, D), lambda i: ((i * tc) // S, 0, 0)),
                  pl.BlockSpec(memory_space=pl.ANY)],
        out_specs=pl.BlockSpec((tc, D), lambda i: (i, 0)),
        out_shape=jax.ShapeDtypeStruct((N, D), F32),
        scratch_shapes=[pltpu.VMEM((2, tc, D), F32), pltpu.SemaphoreType.DMA(())],
        compiler_params=_cp("arbitrary"),
        name="moe_combine",
    )(pos.reshape(N // tc, 1, 2 * tc), route, x3, g2, y)


def _moe(hp, route, counts, x3, S, g2, w1, w3, w2, tm=1024):
    N = x3.shape[0]
    cnt = counts[0, :N_EXPERTS].astype(jnp.int32)
    nblk_e = (cnt + tm - 1) // tm
    blk_end = jnp.cumsum(nblk_e)
    row_start = (blk_end - nblk_e) * tm
    e_idx = route[:, 0:2].astype(jnp.int32)
    rank = route[:, 2:4].astype(jnp.int32)
    pos = row_start[e_idx] + rank
    P = 2 * N + N_EXPERTS * tm
    nblk = P // tm
    blk_expert = jnp.clip(jnp.searchsorted(blk_end, jnp.arange(nblk, dtype=jnp.int32), side="right"),
                          0, N_EXPERTS - 1).astype(jnp.int32)
    n_valid = blk_end[-1:].astype(jnp.int32)
    xs = _dispatch(hp, pos, P)
    y = _experts(xs, blk_expert, n_valid, w1, w3, w2, tm)
    return _combine(y, pos, route, x3, S, g2)


def kernel(x, c, ada_w, ada_b, norm1_g, norm2_g, da_w_in, da_q_gain, da_k_gain, da_lam_q1, da_lam_k1, da_lam_q2,
           da_lam_k2, da_subln_g, da_w_out, gla_w_in, gla_w_a1, gla_w_a2, gla_b_a, gla_out_g, gla_w_out,
           ffn_w1, ffn_w3, ffn_w2, moe_router, moe_w1, moe_w3, moe_w2):
    B, S, _ = x.shape
    N = B * S
    mod = _adaln(c, ada_w, ada_b)
    mods = [[mod[l, :, k * D:(k + 1) * D].reshape(B, 1, D) for k in range(6)] for l in range(2)]
    x2d = x.reshape(N, D)

    sh1, sc1, gt1, sh2, sc2, gt2 = mods[0]
    lambda_init = 0.8 - 0.6 * math.exp(-0.3 * 0)
    qk_scale = DA_HEAD_DIM ** -0.5 * LOG2E
    gain_row = jnp.concatenate([jnp.tile(da_q_gain[0].reshape(-1) * qk_scale, DA_HEADS),
                                jnp.tile(da_k_gain[0].reshape(-1), DA_HEADS),
                                jnp.ones((D,), F32)]).reshape(1, 3 * D)
    qkv = _da_inproj(x2d, S, norm1_g[0].reshape(1, D), sc1, sh1, da_w_in[0].astype(BF16), gain_row)
    slopes = 2.0 ** (-8.0 * jnp.arange(1, DA_HEADS + 1, dtype=F32) / DA_HEADS)
    o = _da_attention(qkv.reshape(B, S, 3 * D), slopes,
                      da_lam_q1[0].reshape(1, -1), da_lam_k1[0].reshape(1, -1),
                      da_lam_q2[0].reshape(1, -1), da_lam_k2[0].reshape(1, -1),
                      da_subln_g[0].reshape(1, -1), lambda_init)
    x2d = _post0(x2d, o.reshape(N, D), S, da_w_out[0].astype(BF16), gt1, norm2_g[0].reshape(1, D), sc2, sh2, gt2,
                 ffn_w1[0].astype(BF16), ffn_w3[0].astype(BF16), ffn_w2[0].astype(BF16))

    sh1, sc1, gt1, sh2, sc2, gt2 = mods[1]
    kw = GLA_HEADS * GLA_DK
    mult_row = jnp.concatenate([jnp.full((kw,), GLA_DK ** -0.5, F32), jnp.ones((3 * D - kw,), F32)]).reshape(1, -1)
    rank = gla_w_a1.shape[-1]
    wa1 = jnp.zeros((D, LANES), BF16).at[:, :rank].set(gla_w_a1[0].astype(BF16))
    wa2 = jnp.zeros((LANES, kw), BF16).at[:rank, :].set(gla_w_a2[0].astype(BF16))
    qkvr, la = _gla_inproj(x2d, S, norm1_g[1].reshape(1, D), sc1, sh1, gla_w_in[0].astype(BF16), mult_row,
                           wa1, wa2, gla_b_a[0].reshape(1, kw))
    o = _gla(qkvr.reshape(B, S, 3 * D), la.reshape(B, S, kw), gla_out_g[0].reshape(1, -1))
    router_pad = jnp.zeros((D, LANES), BF16).at[:, :N_EXPERTS].set(moe_router[0].astype(BF16))
    x3, hp, route, counts = _post1(x2d, o.reshape(N, D), S, gla_w_out[0].astype(BF16), gt1,
                                   norm2_g[1].reshape(1, D), sc2, sh2, router_pad)
    out = _moe(hp, route, counts, x3, S, gt2,
               moe_w1[0].astype(BF16), moe_w3[0].astype(BF16), moe_w2[0].astype(BF16))
    return out.reshape(B, S, D)
```

```python
import functools
import math

import jax
import jax.numpy as jnp
from jax import lax
from jax.experimental import pallas as pl
from jax.experimental.pallas import tpu as pltpu

F32 = jnp.float32
BF16 = jnp.bfloat16

D = 1024
EPS = 1e-6
NEG = -1e30
CHUNK = 64
DA_HEADS = 8
DA_HEAD_DIM = 64
GLA_HEADS = 4
GLA_DK = 128
GLA_DV = 256
GLA_TAU = 16.0
N_EXPERTS = 8
LANES = 128
LOG2E = 1.4426950408889634
VT_ROWS = LANES + 16

VMEM_LIMIT = 56 * 1024 * 1024


def _cp(*sem):
    return pltpu.CompilerParams(dimension_semantics=sem, vmem_limit_bytes=VMEM_LIMIT)


def _sigmoid(x):
    return 1.0 / (1.0 + jnp.exp(-x))


def _modnorm(x, g, sc, sh):
    ms = jnp.mean(x * x, axis=-1, keepdims=True)
    return (x * lax.rsqrt(ms + EPS) * g) * (1.0 + sc) + sh


def _adaln_kernel(c_ref, w_ref, b_ref, o_ref):
    c = c_ref[...]
    ca = (c * _sigmoid(c)).astype(BF16)
    o_ref[...] = jnp.dot(ca, w_ref[...].astype(BF16), preferred_element_type=F32) + b_ref[...]


def _adaln(c, ada_w, ada_b):
    L, _, n6 = ada_w.shape
    B = c.shape[0]
    tn = 1536
    return pl.pallas_call(
        _adaln_kernel,
        grid=(L, n6 // tn),
        in_specs=[pl.BlockSpec((B, D), lambda l, n: (0, 0)),
                  pl.BlockSpec((None, D, tn), lambda l, n: (l, 0, n)),
                  pl.BlockSpec((None, 1, tn), lambda l, n: (l, 0, n))],
        out_specs=pl.BlockSpec((None, B, tn), lambda l, n: (l, 0, n)),
        out_shape=jax.ShapeDtypeStruct((L, B, n6), F32),
        compiler_params=_cp("arbitrary", "arbitrary"),
        name="adaln",
    )(c, ada_w, ada_b.reshape(L, 1, n6))


def _da_inproj_kernel(x_ref, g_ref, sc_ref, sh_ref, w_ref, gain_ref, gsum_ref, o_ref, h_scr, *, n_qk_blocks):
    j = pl.program_id(1)

    @pl.when(j == 0)
    def _():
        h_scr[...] = _modnorm(x_ref[...], g_ref[...], sc_ref[...], sh_ref[...]).astype(BF16)

    acc = jnp.dot(h_scr[...], w_ref[...], preferred_element_type=F32)
    y2 = (acc * acc).astype(BF16)
    gsum = gsum_ref[...]
    tn = acc.shape[1]
    ss = jnp.concatenate(
        [jnp.dot(y2[:, c * 256:(c + 1) * 256], gsum, preferred_element_type=F32) for c in range(tn // 256)],
        axis=1)
    rs = lax.rsqrt(ss * (1.0 / DA_HEAD_DIM) + EPS)
    rs = jnp.where(j < n_qk_blocks, rs, 1.0)
    o_ref[...] = (acc * rs * gain_ref[...]).astype(BF16)


def _da_inproj(x2d, S, g, sc, sh, w_bf, gain_row, tm=1024, tn=512):
    N = x2d.shape[0]
    n_out = w_bf.shape[1]
    r = jnp.arange(256) // DA_HEAD_DIM
    gsum = (r[:, None] == r[None, :]).astype(BF16)
    vec = pl.BlockSpec((None, 1, D), lambda i, j: ((i * tm) // S, 0, 0))
    return pl.pallas_call(
        functools.partial(_da_inproj_kernel, n_qk_blocks=(2 * D) // tn),
        grid=(N // tm, n_out // tn),
        in_specs=[pl.BlockSpec((tm, D), lambda i, j: (i, 0)),
                  pl.BlockSpec((1, D), lambda i, j: (0, 0)),
                  vec, vec,
                  pl.BlockSpec((D, tn), lambda i, j: (0, j)),
                  pl.BlockSpec((1, tn), lambda i, j: (0, j)),
                  pl.BlockSpec((256, 256), lambda i, j: (0, 0))],
        out_specs=pl.BlockSpec((tm, tn), lambda i, j: (i, j)),
        out_shape=jax.ShapeDtypeStruct((N, n_out), BF16),
        scratch_shapes=[pltpu.VMEM((tm, D), BF16)],
        compiler_params=_cp("arbitrary", "arbitrary"),
        name="da_inproj",
    )(x2d, g, sc, sh, w_bf, gain_row, gsum)


def _da_attn_kernel(slope_ref, q_ref, k_ref, v_ref, lq1_ref, lk1_ref, lq2_ref, lk2_ref, subg_ref, o_ref,
                    k1_scr, k2_scr, vt_scr, corr_scr, s1e_scr, s2e_scr, s1o_scr, s2o_scr, qt1_scr, qt2_scr, a1, a2,
                    *, T, SUB, S, lambda_init):
    h = pl.program_id(1)
    slope2 = slope_ref[h] * LOG2E
    lane = lax.broadcasted_iota(jnp.int32, (T, LANES), 1)
    local = lax.broadcasted_iota(jnp.int32, (T, LANES), 0)
    lo = jnp.bitwise_and(local, 255).astype(F32)
    hi = (local - jnp.bitwise_and(local, 255)).astype(F32)
    first_half = lane < DA_HEAD_DIM

    def aug(base, vals):
        out = jnp.zeros((T, LANES), F32)
        for off, val in enumerate(vals):
            out = jnp.where(lane == base + off, val, out)
        return out.astype(BF16)

    c_hi = (slope2 + jnp.zeros((T, LANES), F32)).astype(BF16).astype(F32)
    c_mid = (slope2 - c_hi).astype(BF16).astype(F32)
    c_lo = slope2 - c_hi - c_mid
    k_vals = (lo, lo, lo, hi, hi, hi, -c_hi, -c_mid, -c_lo, -c_hi, -c_mid, -c_lo)
    q_vals = (c_hi, c_mid, c_lo, c_hi, c_mid, c_lo, lo, lo, lo, hi, hi, hi)
    k_aug1 = aug(DA_HEAD_DIM, k_vals)
    k_aug2 = aug(0, k_vals)
    q_aug1 = aug(DA_HEAD_DIM, q_vals)
    q_aug2 = aug(0, q_vals)
    ones_rows = jnp.where(lax.broadcasted_iota(jnp.int32, (VT_ROWS - LANES, T), 0) == 0, 1.0, 0.0).astype(BF16)

    def prep(j, carry):
        r0 = pl.multiple_of(j * T, T)
        k = k_ref[pl.ds(r0, T), :]
        k1_scr[pl.ds(r0, T), :] = jnp.where(first_half, k, k_aug1)
        k2_scr[pl.ds(r0, T), :] = jnp.where(first_half, k_aug2, k)
        vt_scr[:LANES, pl.ds(r0, T)] = v_ref[pl.ds(r0, T), :].astype(F32).T.astype(BF16)
        vt_scr[LANES:, pl.ds(r0, T)] = ones_rows
        return carry

    lax.fori_loop(0, S // T, prep, 0)

    kj = lax.broadcasted_iota(jnp.int32, (T, T), 0)
    qj = lax.broadcasted_iota(jnp.int32, (T, T), 1)
    visible = jnp.right_shift(kj, 6) <= jnp.right_shift(qj, 6)
    ahead = jnp.maximum(kj - qj, 0).astype(F32)
    corr_scr[...] = jnp.where(visible, -2.0 * slope2 * ahead, NEG)

    lam = (jnp.exp(jnp.sum(lq1_ref[...] * lk1_ref[...], axis=-1, keepdims=True))
           - jnp.exp(jnp.sum(lq2_ref[...] * lk2_ref[...], axis=-1, keepdims=True)) + lambda_init)
    k_scrs = (k1_scr, k2_scr)
    s_scrs = ((s1e_scr, s2e_scr), (s1o_scr, s2o_scr))
    qt_scrs = (qt1_scr, qt2_scr)
    accs = (a1, a2)
    n_q = S // T
    TK = T // SUB
    neg = jnp.full((1, T), NEG, F32)

    def load_queries(qi):
        q = q_ref[pl.ds(pl.multiple_of(qi * T, T), T), :]
        for qt_scr, qa in zip(qt_scrs, (jnp.where(first_half, q, q_aug1), jnp.where(first_half, q_aug2, q))):
            qt_scr[...] = qa.astype(F32).T.astype(BF16)

    def block_shift(qi, j):
        return -slope2 * lax.convert_element_type((qi - j) * T, F32)

    def score_chunk(qi, par, j, part, ms, corr):
        k0 = pl.multiple_of(j * T + part * TK, TK)
        out = []
        for mp in range(2):
            s = jnp.dot(k_scrs[mp][pl.ds(k0, TK), :], qt_scrs[mp][...], preferred_element_type=F32)
            if corr is not None:
                s = s + corr[part * TK:(part + 1) * TK, :]
            s_scrs[par][mp][pl.ds(k0, TK), :] = s
            out.append(jnp.maximum(ms[mp], jnp.max(s, axis=0, keepdims=True) + block_shift(qi, j)))
        return tuple(out)

    def weight_chunk(qi, par, j, part, ms):
        k0 = pl.multiple_of(j * T + part * TK, TK)
        vt = vt_scr[:, pl.ds(k0, TK)]
        for mp in range(2):
            p = jnp.exp2(s_scrs[par][mp][pl.ds(k0, TK), :] + (block_shift(qi, j) - ms[mp]))
            accs[mp][...] += jnp.dot(vt, p.astype(BF16), preferred_element_type=F32)

    def finalize(qi):
        acc1 = a1[...]
        acc2 = a2[...]
        o = (acc1[:LANES] / acc1[LANES:LANES + 1] - lam * (acc2[:LANES] / acc2[LANES:LANES + 1])).T
        msq = jnp.mean(o * o, axis=-1, keepdims=True)
        o = o * lax.rsqrt(msq + EPS) * subg_ref[...] * (1.0 - lambda_init)
        o_ref[pl.ds(pl.multiple_of(qi * T, T), T), :] = o.astype(o_ref.dtype)
        a1[...] = jnp.zeros_like(a1)
        a2[...] = jnp.zeros_like(a2)

    a1[...] = jnp.zeros_like(a1)
    a2[...] = jnp.zeros_like(a2)
    load_queries(0)
    ms0 = (neg, neg)
    for part in range(SUB):
        ms0 = score_chunk(0, 0, 0, part, ms0, corr_scr)

    def q_block(qi, par, ms):
        load_queries(qi + 1)

        def chunk(j, ms_next):
            for part in range(SUB):
                ms_next = score_chunk(qi + 1, 1 - par, j, part, ms_next, None)
                weight_chunk(qi, par, j, part, ms)
            return ms_next

        ms_next = lax.fori_loop(0, qi + 1, chunk, (neg, neg))
        for part in range(SUB):
            ms_next = score_chunk(qi + 1, 1 - par, qi + 1, part, ms_next, corr_scr)
        finalize(qi)
        return ms_next

    def q_pair(qq, ms):
        return q_block(2 * qq + 1, 1, q_block(2 * qq, 0, ms))

    ms_even = lax.fori_loop(0, n_q // 2 - 1, q_pair, ms0)
    ms_last = q_block(n_q - 2, 0, ms_even)

    def last_chunk(j, carry):
        for part in range(SUB):
            weight_chunk(n_q - 1, 1, j, part, ms_last)
        return carry

    lax.fori_loop(0, n_q, last_chunk, 0)
    finalize(n_q - 1)


def _da_attention(qkv, slopes, lq1, lk1, lq2, lk2, subg, lambda_init, T=512, sub=2):
    B, S, _ = qkv.shape
    H = DA_HEADS
    vec64 = pl.BlockSpec((1, DA_HEAD_DIM), lambda b, h: (0, 0))
    return pl.pallas_call(
        functools.partial(_da_attn_kernel, T=T, SUB=sub, S=S, lambda_init=lambda_init),
        grid=(B, H),
        in_specs=[pl.BlockSpec(memory_space=pltpu.SMEM),
                  pl.BlockSpec((None, S, LANES), lambda b, h: (b, 0, h)),
                  pl.BlockSpec((None, S, LANES), lambda b, h: (b, 0, H + h)),
                  pl.BlockSpec((None, S, LANES), lambda b, h: (b, 0, 2 * H + h)),
                  vec64, vec64, vec64, vec64,
                  pl.BlockSpec((1, LANES), lambda b, h: (0, 0))],
        out_specs=pl.BlockSpec((None, S, LANES), lambda b, h: (b, 0, h)),
        out_shape=jax.ShapeDtypeStruct((B, S, H * LANES), BF16),
        scratch_shapes=[pltpu.VMEM((S, LANES), BF16), pltpu.VMEM((S, LANES), BF16), pltpu.VMEM((VT_ROWS, S), BF16),
                        pltpu.VMEM((T, T), F32)] + [pltpu.VMEM((S, T), F32) for _ in range(4)] + [
                        pltpu.VMEM((LANES, T), BF16), pltpu.VMEM((LANES, T), BF16),
                        pltpu.VMEM((VT_ROWS, T), F32), pltpu.VMEM((VT_ROWS, T), F32)],
        compiler_params=_cp("arbitrary", "arbitrary"),
        name="da_attention",
    )(slopes, qkv, qkv, qkv, lq1, lk1, lq2, lk2, subg)


def _post0_kernel(x_ref, o_ref, wo_ref, g1_ref, ng_ref, sc_ref, sh_ref, g2_ref, w1_ref, w3_ref, w2_ref,
                  out_ref, x1_scr, h_scr, acc_scr):
    j = pl.program_id(1)

    @pl.when(j == 0)
    def _():
        y = jnp.dot(o_ref[...], wo_ref[...], preferred_element_type=F32)
        x1 = x_ref[...] + g1_ref[...] * y
        x1_scr[...] = x1
        h_scr[...] = _modnorm(x1, ng_ref[...], sc_ref[...], sh_ref[...]).astype(BF16)
        acc_scr[...] = jnp.zeros_like(acc_scr)

    h = h_scr[...]
    a = jnp.dot(h, w1_ref[...], preferred_element_type=F32)
    b = jnp.dot(h, w3_ref[...], preferred_element_type=F32)
    gact = (a * _sigmoid(a) * b).astype(BF16)
    acc_scr[...] += jnp.dot(gact, w2_ref[...], preferred_element_type=F32)

    @pl.when(j == pl.num_programs(1) - 1)
    def _():
        out_ref[...] = x1_scr[...] + g2_ref[...] * acc_scr[...]


def _post0(x2d, o2d, S, wo, g1, ng, sc, sh, g2, w1, w3, w2, tm=512, tf=1408):
    N = x2d.shape[0]
    F = w1.shape[1]
    vec = pl.BlockSpec((None, 1, D), lambda i, j: ((i * tm) // S, 0, 0))
    row = pl.BlockSpec((tm, D), lambda i, j: (i, 0))
    return pl.pallas_call(
        _post0_kernel,
        grid=(N // tm, F // tf),
        in_specs=[row, row,
                  pl.BlockSpec((D, D), lambda i, j: (0, 0)),
                  vec,
                  pl.BlockSpec((1, D), lambda i, j: (0, 0)),
                  vec, vec, vec,
                  pl.BlockSpec((D, tf), lambda i, j: (0, j)),
                  pl.BlockSpec((D, tf), lambda i, j: (0, j)),
                  pl.BlockSpec((tf, D), lambda i, j: (j, 0))],
        out_specs=row,
        out_shape=jax.ShapeDtypeStruct((N, D), F32),
        scratch_shapes=[pltpu.VMEM((tm, D), F32), pltpu.VMEM((tm, D), BF16), pltpu.VMEM((tm, D), F32)],
        compiler_params=_cp("arbitrary", "arbitrary"),
        name="post0_ffn",
    )(x2d, o2d, wo, g1, ng, sc, sh, g2, w1, w3, w2)


def _gla_inproj_kernel(x_ref, g_ref, sc_ref, sh_ref, w_ref, mult_ref, wa1_ref, wa2_ref, ba_ref,
                       o_ref, la_ref, h_scr, *, n_plain_blocks):
    j = pl.program_id(1)

    @pl.when(j == 0)
    def _():
        h = _modnorm(x_ref[...], g_ref[...], sc_ref[...], sh_ref[...]).astype(BF16)
        h_scr[...] = h
        low = jnp.dot(h, wa1_ref[...], preferred_element_type=F32).astype(BF16)
        z = jnp.dot(low, wa2_ref[...], preferred_element_type=F32) + ba_ref[...]
        log_sig = jnp.minimum(z, 0.0) - jnp.log(1.0 + jnp.exp(-jnp.abs(z)))
        la_ref[...] = log_sig * (1.0 / GLA_TAU)

    acc = jnp.dot(h_scr[...], w_ref[...], preferred_element_type=F32)

    @pl.when(j < n_plain_blocks)
    def _():
        o_ref[...] = (acc * mult_ref[...]).astype(BF16)

    @pl.when(j >= n_plain_blocks)
    def _():
        o_ref[...] = (acc * _sigmoid(acc)).astype(BF16)


def _gla_inproj(x2d, S, g, sc, sh, w_bf, mult_row, wa1, wa2, ba, tm=1024, tn=512):
    N = x2d.shape[0]
    n_out = w_bf.shape[1]
    kw = wa2.shape[1]
    vec = pl.BlockSpec((None, 1, D), lambda i, j: ((i * tm) // S, 0, 0))
    return pl.pallas_call(
        functools.partial(_gla_inproj_kernel, n_plain_blocks=(2 * D) // tn),
        grid=(N // tm, n_out // tn),
        in_specs=[pl.BlockSpec((tm, D), lambda i, j: (i, 0)),
                  pl.BlockSpec((1, D), lambda i, j: (0, 0)),
                  vec, vec,
                  pl.BlockSpec((D, tn), lambda i, j: (0, j)),
                  pl.BlockSpec((1, tn), lambda i, j: (0, j)),
                  pl.BlockSpec((D, LANES), lambda i, j: (0, 0)),
                  pl.BlockSpec((LANES, kw), lambda i, j: (0, 0)),
                  pl.BlockSpec((1, kw), lambda i, j: (0, 0))],
        out_specs=[pl.BlockSpec((tm, tn), lambda i, j: (i, j)),
                   pl.BlockSpec((tm, kw), lambda i, j: (i, 0))],
        out_shape=[jax.ShapeDtypeStruct((N, n_out), BF16), jax.ShapeDtypeStruct((N, kw), F32)],
        scratch_shapes=[pltpu.VMEM((tm, D), BF16)],
        compiler_params=_cp("arbitrary", "arbitrary"),
        name="gla_inproj",
    )(x2d, g, sc, sh, w_bf, mult_row, wa1, wa2, ba)


def _gla_kernel(q_ref, k_ref, v_ref, r_ref, la_ref, tri_ref, og_ref, o_ref, state_scr, *, NB, SB):
    @pl.when(pl.program_id(1) == 0)
    def _():
        state_scr[...] = jnp.zeros_like(state_scr)

    tri = tri_ref[...]
    ti = lax.broadcasted_iota(jnp.int32, (CHUNK, CHUNK), 0)
    si = lax.broadcasted_iota(jnp.int32, (CHUNK, CHUNK), 1)
    causal = si <= ti
    og = og_ref[...]
    chains = [(nb, h) for nb in range(NB) for h in range(GLA_HEADS)]
    nt = (((1,), (1,)), ((), ()))
    tn = (((0,), (0,)), ((), ()))

    def chunk(c, carry):
        r0 = pl.multiple_of(c * CHUNK, CHUNK)
        rows = pl.ds(r0, CHUNK)
        kcol = [slice(h * GLA_DK, (h + 1) * GLA_DK) for _, h in chains]
        vcol = [slice(h * GLA_DV, (h + 1) * GLA_DV) for _, h in chains]
        las = [la_ref[nb, rows, kc] for (nb, _), kc in zip(chains, kcol)]
        his = [la.astype(BF16) for la in las]
        los = [(la - hi.astype(F32)).astype(BF16) for la, hi in zip(las, his)]
        bs = [jnp.dot(tri, hi, preferred_element_type=F32) + jnp.dot(tri, lo, preferred_element_type=F32)
              for hi, lo in zip(his, los)]
        b_lasts = [b[CHUNK - 1:CHUNK, :] for b in bs]
        qs = [q_ref[nb, rows, kc].astype(F32) for (nb, _), kc in zip(chains, kcol)]
        ks = [k_ref[nb, rows, kc].astype(F32) for (nb, _), kc in zip(chains, kcol)]
        vs = [v_ref[nb, rows, vc] for (nb, _), vc in zip(chains, vcol)]
        q_decs = [(q * jnp.exp(b)).astype(BF16) for q, b in zip(qs, bs)]
        k_invs = [(k * jnp.exp(-b)).astype(BF16) for k, b in zip(ks, bs)]
        k_decs = [(k * jnp.exp(bl - b)).astype(BF16) for k, b, bl in zip(ks, bs, b_lasts)]
        states = [state_scr[nb, h] for nb, h in chains]
        attns = [lax.dot_general(qd, ki, nt, preferred_element_type=F32) for qd, ki in zip(q_decs, k_invs)]
        o_inters = [lax.dot_general(qd, st.astype(BF16), nt, preferred_element_type=F32)
                    for qd, st in zip(q_decs, states)]
        kv_ts = [lax.dot_general(v, kd, tn, preferred_element_type=F32) for v, kd in zip(vs, k_decs)]
        attns = [jnp.where(causal, a, 0.0).astype(BF16) for a in attns]
        o_intras = [jnp.dot(a, v, preferred_element_type=F32) for a, v in zip(attns, vs)]
        for (nb, h), st, bl, kv_t, oi, oe, vc in zip(chains, states, b_lasts, kv_ts, o_intras, o_inters, vcol):
            state_scr[nb, h] = st * jnp.exp(bl) + kv_t
            o = oi + oe
            ms = jnp.mean(o * o, axis=-1, keepdims=True)
            o = o * lax.rsqrt(ms + EPS) * og * r_ref[nb, rows, vc].astype(F32)
            o_ref[nb, rows, vc] = o.astype(o_ref.dtype)
        return carry

    lax.fori_loop(0, SB // CHUNK, chunk, 0)


def _gla(qkvr, la, out_g, nb=2, sb=512):
    B, S, _ = qkvr.shape
    H = GLA_HEADS
    kw = H * GLA_DK
    vw = H * GLA_DV
    tri = (jnp.arange(CHUNK)[None, :] <= jnp.arange(CHUNK)[:, None]).astype(BF16)
    return pl.pallas_call(
        functools.partial(_gla_kernel, NB=nb, SB=sb),
        grid=(B // nb, S // sb),
        in_specs=[pl.BlockSpec((nb, sb, kw), lambda b, s: (b, s, 0)),
                  pl.BlockSpec((nb, sb, kw), lambda b, s: (b, s, 1)),
                  pl.BlockSpec((nb, sb, vw), lambda b, s: (b, s, (2 * kw) // vw)),
                  pl.BlockSpec((nb, sb, vw), lambda b, s: (b, s, (2 * kw) // vw + 1)),
                  pl.BlockSpec((nb, sb, kw), lambda b, s: (b, s, 0)),
                  pl.BlockSpec((CHUNK, CHUNK), lambda b, s: (0, 0)),
                  pl.BlockSpec((1, GLA_DV), lambda b, s: (0, 0))],
        out_specs=pl.BlockSpec((nb, sb, vw), lambda b, s: (b, s, 0)),
        out_shape=jax.ShapeDtypeStruct((B, S, vw), BF16),
        scratch_shapes=[pltpu.VMEM((nb, H, GLA_DV, GLA_DK), F32)],
        compiler_params=_cp("arbitrary", "arbitrary"),
        name="gla",
    )(qkvr, qkvr, qkvr, qkvr, la, tri, out_g)


def _post1_kernel(x_ref, o_ref, wo_ref, g1_ref, ng_ref, sc_ref, sh_ref, rw_ref, ltri_ref,
                  x3_ref, hp_ref, route_ref, cnt_ref, carry_scr, *, sub):
    i = pl.program_id(0)

    @pl.when(i == 0)
    def _():
        carry_scr[...] = jnp.zeros_like(carry_scr)

    y = jnp.dot(o_ref[...], wo_ref[...], preferred_element_type=F32)
    x3 = x_ref[...] + g1_ref[...] * y
    x3_ref[...] = x3
    hb = _modnorm(x3, ng_ref[...], sc_ref[...], sh_ref[...]).astype(BF16)

    half = D // 2
    lo = pltpu.bitcast(hb[:, :half].astype(F32), jnp.uint32)
    hi = pltpu.bitcast(hb[:, half:].astype(F32), jnp.uint32)
    hp_ref[...] = jnp.right_shift(lo, jnp.uint32(16)) | (hi & jnp.uint32(0xFFFF0000))

    tm = hb.shape[0]
    lane = lax.broadcasted_iota(jnp.int32, (tm, LANES), 1)
    lanef = lane.astype(F32)
    logits = jnp.dot(hb, rw_ref[...], preferred_element_type=F32)
    logits = jnp.where(lane < N_EXPERTS, logits, NEG)
    m1 = jnp.max(logits, axis=-1, keepdims=True)
    i1 = jnp.min(jnp.where(logits == m1, lanef, float(LANES)), axis=-1, keepdims=True)
    oh1 = lanef == i1
    rest = jnp.where(oh1, NEG, logits)
    m2 = jnp.max(rest, axis=-1, keepdims=True)
    i2 = jnp.min(jnp.where(rest == m2, lanef, float(LANES)), axis=-1, keepdims=True)
    oh2 = lanef == i2
    e = jnp.exp(m2 - m1)
    gate1 = 1.0 / (1.0 + e)
    gate2 = e / (1.0 + e)

    cnt = jnp.where(oh1, 1.0, 0.0) + jnp.where(oh2, 1.0, 0.0)
    ltri = ltri_ref[...]
    carry = carry_scr[...]
    pres = []
    for s in range(tm // sub):
        c_sub = cnt[s * sub:(s + 1) * sub, :]
        pres.append(jnp.dot(ltri, c_sub.astype(BF16), preferred_element_type=F32) + carry)
        carry = carry + jnp.sum(c_sub, axis=0, keepdims=True)
    pre = jnp.concatenate(pres, axis=0)
    carry_scr[...] = carry
    cnt_ref[...] = carry
    r1 = jnp.sum(jnp.where(oh1, pre, 0.0), axis=-1, keepdims=True)
    r2 = jnp.sum(jnp.where(oh2, pre, 0.0), axis=-1, keepdims=True)
    route = jnp.where(lane == 0, i1, 0.0)
    for idx, val in ((1, i2), (2, r1), (3, r2), (4, gate1), (5, gate2)):
        route = jnp.where(lane == idx, val, route)
    route_ref[...] = route


def _post1(x2d, o2d, S, wo, g1, ng, sc, sh, router_pad, tm=512, sub=256):
    N = x2d.shape[0]
    ltri = (jnp.arange(sub)[None, :] < jnp.arange(sub)[:, None]).astype(BF16)
    vec = pl.BlockSpec((None, 1, D), lambda i: ((i * tm) // S, 0, 0))
    row = pl.BlockSpec((tm, D), lambda i: (i, 0))
    return pl.pallas_call(
        functools.partial(_post1_kernel, sub=sub),
        grid=(N // tm,),
        in_specs=[row, row,
                  pl.BlockSpec((D, D), lambda i: (0, 0)),
                  vec,
                  pl.BlockSpec((1, D), lambda i: (0, 0)),
                  vec, vec,
                  pl.BlockSpec((D, LANES), lambda i: (0, 0)),
                  pl.BlockSpec((sub, sub), lambda i: (0, 0))],
        out_specs=[row,
                   pl.BlockSpec((tm, D // 2), lambda i: (i, 0)),
                   pl.BlockSpec((tm, LANES), lambda i: (i, 0)),
                   pl.BlockSpec((1, LANES), lambda i: (0, 0))],
        out_shape=[jax.ShapeDtypeStruct((N, D), F32),
                   jax.ShapeDtypeStruct((N, D // 2), jnp.uint32),
                   jax.ShapeDtypeStruct((N, LANES), F32),
                   jax.ShapeDtypeStruct((1, LANES), F32)],
        scratch_shapes=[pltpu.VMEM((1, LANES), F32)],
        compiler_params=_cp("arbitrary"),
        name="post1_router",
    )(x2d, o2d, wo, g1, ng, sc, sh, router_pad, ltri)


def _dispatch_kernel(pos_ref, h_ref, xs_in_ref, xs_ref, sem, *, td):
    del xs_in_ref

    def row(r, carry):
        for kk in range(2):
            p = pos_ref[0, 2 * r + kk]
            pltpu.make_async_copy(h_ref.at[pl.ds(r, 1)], xs_ref.at[pl.ds(p, 1)], sem).start()
        return carry

    lax.fori_loop(0, td, row, 0, unroll=8)
    for _ in range(2):
        pltpu.make_async_copy(h_ref, xs_ref.at[pl.ds(0, td)], sem).wait()


def _dispatch(hp, pos, P, td=512):
    N, W = hp.shape
    xs0 = jnp.zeros((P, W), jnp.uint32)
    return pl.pallas_call(
        functools.partial(_dispatch_kernel, td=td),
        grid=(N // td,),
        in_specs=[pl.BlockSpec((None, 1, 2 * td), lambda i: (i, 0, 0), memory_space=pltpu.SMEM),
                  pl.BlockSpec((td, W), lambda i: (i, 0)),
                  pl.BlockSpec(memory_space=pl.ANY)],
        out_specs=pl.BlockSpec(memory_space=pl.ANY),
        out_shape=jax.ShapeDtypeStruct((P, W), jnp.uint32),
        scratch_shapes=[pltpu.SemaphoreType.DMA(())],
        input_output_aliases={2: 0},
        compiler_params=_cp("arbitrary"),
        name="moe_dispatch",
    )(pos.reshape(N // td, 1, 2 * td), hp, xs0)


def _experts_kernel(be_ref, nv_ref, xs_ref, w1_ref, w3_ref, w2_ref, y_ref, xb_scr):
    i = pl.program_id(0)
    j = pl.program_id(1)
    valid = i < nv_ref[0]

    @pl.when(valid & (j == 0))
    def _():
        w = xs_ref[...]
        half = D // 2
        xb_scr[:, :half] = pltpu.bitcast(jnp.left_shift(w, jnp.uint32(16)), F32).astype(BF16)
        xb_scr[:, half:] = pltpu.bitcast(w & jnp.uint32(0xFFFF0000), F32).astype(BF16)
        y_ref[...] = jnp.zeros_like(y_ref)

    @pl.when(jnp.logical_not(valid) & (j == 0))
    def _():
        y_ref[...] = jnp.zeros_like(y_ref)

    @pl.when(valid)
    def _():
        xb = xb_scr[...]
        a = jnp.dot(xb, w1_ref[...], preferred_element_type=F32)
        b = jnp.dot(xb, w3_ref[...], preferred_element_type=F32)
        gact = (a * _sigmoid(a) * b).astype(BF16)
        y_ref[...] += jnp.dot(gact, w2_ref[...], preferred_element_type=F32)


def _experts(xs, blk_expert, n_valid, w1, w3, w2, tm, tf=512):
    P, W = xs.shape
    E, _, F = w1.shape
    nblk = P // tm
    nj = F // tf

    def _i(i, nv):
        return jnp.minimum(i, nv[0] - 1)

    def _j(i, j, nv):
        return jnp.where(i < nv[0], j, nj - 1)

    grid_spec = pltpu.PrefetchScalarGridSpec(
        num_scalar_prefetch=2,
        grid=(nblk, nj),
        in_specs=[pl.BlockSpec((tm, W), lambda i, j, be, nv: (_i(i, nv), 0)),
                  pl.BlockSpec((None, D, tf), lambda i, j, be, nv: (be[_i(i, nv)], 0, _j(i, j, nv))),
                  pl.BlockSpec((None, D, tf), lambda i, j, be, nv: (be[_i(i, nv)], 0, _j(i, j, nv))),
                  pl.BlockSpec((None, tf, D), lambda i, j, be, nv: (be[_i(i, nv)], _j(i, j, nv), 0))],
        out_specs=pl.BlockSpec((tm, D), lambda i, j, be, nv: (i, 0)),
        scratch_shapes=[pltpu.VMEM((tm, D), BF16)],
    )
    return pl.pallas_call(
        _experts_kernel,
        grid_spec=grid_spec,
        out_shape=jax.ShapeDtypeStruct((P, D), F32),
        compiler_params=_cp("arbitrary", "arbitrary"),
        name="moe_experts",
    )(blk_expert, n_valid, xs, w1, w3, w2)


def _combine_kernel(pos_ref, route_ref, x_ref, g2_ref, y_ref, out_ref, ybuf, sem, *, tc):
    def row(r, carry):
        for kk in range(2):
            p = pos_ref[0, 2 * r + kk]
            pltpu.make_async_copy(y_ref.at[pl.ds(p, 1)], ybuf.at[kk, pl.ds(r, 1)], sem).start()
        return carry

    lax.fori_loop(0, tc, row, 0, unroll=8)
    for kk in range(2):
        pltpu.make_async_copy(y_ref.at[pl.ds(0, tc)], ybuf.at[kk], sem).wait()
    route = route_ref[...]
    moe = route[:, 4:5] * ybuf[0] + route[:, 5:6] * ybuf[1]
    out_ref[...] = x_ref[...] + g2_ref[...] * moe


def _combine(y, pos, route, x3, S, g2, tc=256):
    N = x3.shape[0]
    return pl.pallas_call(
        functools.partial(_combine_kernel, tc=tc),
        grid=(N // tc,),
        in_specs=[pl.BlockSpec((None, 1, 2 * tc), lambda i: (i, 0, 0), memory_space=pltpu.SMEM),
                  pl.BlockSpec((tc, LANES), lambda i: (i, 0)),
                  pl.BlockSpec((tc, D), lambda i: (i, 0)),
                  pl.BlockSpec((None, 1, D), lambda i: ((i * tc) // S, 0, 0)),
                  pl.BlockSpec(memory_space=pl.ANY)],
        out_specs=pl.BlockSpec((tc, D), lambda i: (i, 0)),
        out_shape=jax.ShapeDtypeStruct((N, D), F32),
        scratch_shapes=[pltpu.VMEM((2, tc, D), F32), pltpu.SemaphoreType.DMA(())],
        compiler_params=_cp("arbitrary"),
        name="moe_combine",
    )(pos.reshape(N // tc, 1, 2 * tc), route, x3, g2, y)


def _moe(hp, route, counts, x3, S, g2, w1, w3, w2, tm=1024):
    N = x3.shape[0]
    cnt = counts[0, :N_EXPERTS].astype(jnp.int32)
    nblk_e = (cnt + tm - 1) // tm
    blk_end = jnp.cumsum(nblk_e)
    row_start = (blk_end - nblk_e) * tm
    e_idx = route[:, 0:2].astype(jnp.int32)
    rank = route[:, 2:4].astype(jnp.int32)
    pos = row_start[e_idx] + rank
    P = 2 * N + N_EXPERTS * tm
    nblk = P // tm
    blk_expert = jnp.clip(jnp.searchsorted(blk_end, jnp.arange(nblk, dtype=jnp.int32), side="right"),
                          0, N_EXPERTS - 1).astype(jnp.int32)
    n_valid = blk_end[-1:].astype(jnp.int32)
    xs = _dispatch(hp, pos, P)
    y = _experts(xs, blk_expert, n_valid, w1, w3, w2, tm)
    return _combine(y, pos, route, x3, S, g2)


def kernel(x, c, ada_w, ada_b, norm1_g, norm2_g, da_w_in, da_q_gain, da_k_gain, da_lam_q1, da_lam_k1, da_lam_q2,
           da_lam_k2, da_subln_g, da_w_out, gla_w_in, gla_w_a1, gla_w_a2, gla_b_a, gla_out_g, gla_w_out,
           ffn_w1, ffn_w3, ffn_w2, moe_router, moe_w1, moe_w3, moe_w2):
    B, S, _ = x.shape
    N = B * S
    mod = _adaln(c, ada_w, ada_b)
    mods = [[mod[l, :, k * D:(k + 1) * D].reshape(B, 1, D) for k in range(6)] for l in range(2)]
    x2d = x.reshape(N, D)

    sh1, sc1, gt1, sh2, sc2, gt2 = mods[0]
    lambda_init = 0.8 - 0.6 * math.exp(-0.3 * 0)
    qk_scale = DA_HEAD_DIM ** -0.5 * LOG2E
    gain_row = jnp.concatenate([jnp.tile(da_q_gain[0].reshape(-1) * qk_scale, DA_HEADS),
                                jnp.tile(da_k_gain[0].reshape(-1), DA_HEADS),
                                jnp.ones((D,), F32)]).reshape(1, 3 * D)
    qkv = _da_inproj(x2d, S, norm1_g[0].reshape(1, D), sc1, sh1, da_w_in[0].astype(BF16), gain_row)
    slopes = 2.0 ** (-8.0 * jnp.arange(1, DA_HEADS + 1, dtype=F32) / DA_HEADS)
    o = _da_attention(qkv.reshape(B, S, 3 * D), slopes,
                      da_lam_q1[0].reshape(1, -1), da_lam_k1[0].reshape(1, -1),
                      da_lam_q2[0].reshape(1, -1), da_lam_k2[0].reshape(1, -1),
                      da_subln_g[0].reshape(1, -1), lambda_init)
    x2d = _post0(x2d, o.reshape(N, D), S, da_w_out[0].astype(BF16), gt1, norm2_g[0].reshape(1, D), sc2, sh2, gt2,
                 ffn_w1[0].astype(BF16), ffn_w3[0].astype(BF16), ffn_w2[0].astype(BF16))

    sh1, sc1, gt1, sh2, sc2, gt2 = mods[1]
    kw = GLA_HEADS * GLA_DK
    mult_row = jnp.concatenate([jnp.full((kw,), GLA_DK ** -0.5, F32), jnp.ones((3 * D - kw,), F32)]).reshape(1, -1)
    rank = gla_w_a1.shape[-1]
    wa1 = jnp.zeros((D, LANES), BF16).at[:, :rank].set(gla_w_a1[0].astype(BF16))
    wa2 = jnp.zeros((LANES, kw), BF16).at[:rank, :].set(gla_w_a2[0].astype(BF16))
    qkvr, la = _gla_inproj(x2d, S, norm1_g[1].reshape(1, D), sc1, sh1, gla_w_in[0].astype(BF16), mult_row,
                           wa1, wa2, gla_b_a[0].reshape(1, kw))
    o = _gla(qkvr.reshape(B, S, 3 * D), la.reshape(B, S, kw), gla_out_g[0].reshape(1, -1))
    router_pad = jnp.zeros((D, LANES), BF16).at[:, :N_EXPERTS].set(moe_router[0].astype(BF16))
    x3, hp, route, counts = _post1(x2d, o.reshape(N, D), S, gla_w_out[0].astype(BF16), gt1,
                                   norm2_g[1].reshape(1, D), sc2, sh2, router_pad)
    out = _moe(hp, route, counts, x3, S, gt2,
               moe_w1[0].astype(BF16), moe_w3[0].astype(BF16), moe_w2[0].astype(BF16))
    return out.reshape(B, S, D)
```

```python
import functools
import math

import jax
import jax.numpy as jnp
from jax import lax
from jax.experimental import pallas as pl
from jax.experimental.pallas import tpu as pltpu

F32 = jnp.float32
BF16 = jnp.bfloat16

D = 1024
EPS = 1e-6
NEG = -1e30
CHUNK = 64
DA_HEADS = 8
DA_HEAD_DIM = 64
GLA_HEADS = 4
GLA_DK = 128
GLA_DV = 256
GLA_TAU = 16.0
N_EXPERTS = 8
LANES = 128
LOG2E = 1.4426950408889634
VT_ROWS = LANES + 16

VMEM_LIMIT = 56 * 1024 * 1024


def _cp(*sem):
    return pltpu.CompilerParams(dimension_semantics=sem, vmem_limit_bytes=VMEM_LIMIT)


def _sigmoid(x):
    return 1.0 / (1.0 + jnp.exp(-x))


def _modnorm(x, g, sc, sh):
    ms = jnp.mean(x * x, axis=-1, keepdims=True)
    return (x * lax.rsqrt(ms + EPS) * g) * (1.0 + sc) + sh


def _adaln_kernel(c_ref, w_ref, b_ref, o_ref):
    c = c_ref[...]
    ca = (c * _sigmoid(c)).astype(BF16)
    o_ref[...] = jnp.dot(ca, w_ref[...].astype(BF16), preferred_element_type=F32) + b_ref[...]


def _adaln(c, ada_w, ada_b):
    L, _, n6 = ada_w.shape
    B = c.shape[0]
    tn = 1536
    return pl.pallas_call(
        _adaln_kernel,
        grid=(L, n6 // tn),
        in_specs=[pl.BlockSpec((B, D), lambda l, n: (0, 0)),
                  pl.BlockSpec((None, D, tn), lambda l, n: (l, 0, n)),
                  pl.BlockSpec((None, 1, tn), lambda l, n: (l, 0, n))],
        out_specs=pl.BlockSpec((None, B, tn), lambda l, n: (l, 0, n)),
        out_shape=jax.ShapeDtypeStruct((L, B, n6), F32),
        compiler_params=_cp("arbitrary", "arbitrary"),
        name="adaln",
    )(c, ada_w, ada_b.reshape(L, 1, n6))


def _row_pipeline(parts, rows_total, matmul, finish):
    rp = rows_total // parts
    prev = None
    for p in range(parts):
        rows = slice(p * rp, (p + 1) * rp)
        acc = matmul(rows)
        if prev is not None:
            finish(*prev)
        prev = (rows, acc)
    finish(*prev)


def _da_inproj_kernel(x_ref, g_ref, sc_ref, sh_ref, w_ref, gain_ref, gsum_ref, o_ref, h_scr, *, n_qk_blocks, parts):
    j = pl.program_id(1)
    tm, tn = o_ref.shape

    def project_new(rows):
        h = _modnorm(x_ref[rows, :], g_ref[...], sc_ref[...], sh_ref[...]).astype(BF16)
        h_scr[rows, :] = h
        return jnp.dot(h, w_ref[...], preferred_element_type=F32)

    def project(rows):
        return jnp.dot(h_scr[rows, :], w_ref[...], preferred_element_type=F32)

    def finish_qk(rows, acc):
        y2 = (acc * acc).astype(BF16)
        gsum = gsum_ref[...]
        ms = jnp.concatenate(
            [jnp.dot(y2[:, c * 256:(c + 1) * 256], gsum, preferred_element_type=F32) for c in range(tn // 256)],
            axis=1)
        o_ref[rows, :] = (acc * lax.rsqrt(ms + EPS) * gain_ref[...]).astype(BF16)

    def finish_v(rows, acc):
        o_ref[rows, :] = (acc * gain_ref[...]).astype(BF16)

    @pl.when(j == 0)
    def _():
        _row_pipeline(parts, tm, project_new, finish_qk)

    @pl.when((j > 0) & (j < n_qk_blocks))
    def _():
        _row_pipeline(parts, tm, project, finish_qk)

    @pl.when(j >= n_qk_blocks)
    def _():
        _row_pipeline(parts, tm, project, finish_v)


def _da_inproj(x2d, S, g, sc, sh, w_bf, gain_row, tm=1024, tn=512, parts=4):
    N = x2d.shape[0]
    n_out = w_bf.shape[1]
    r = jnp.arange(256) // DA_HEAD_DIM
    gsum = jnp.where(r[:, None] == r[None, :], 1.0 / DA_HEAD_DIM, 0.0).astype(BF16)
    vec = pl.BlockSpec((None, 1, D), lambda i, j: ((i * tm) // S, 0, 0))
    return pl.pallas_call(
        functools.partial(_da_inproj_kernel, n_qk_blocks=(2 * D) // tn, parts=parts),
        grid=(N // tm, n_out // tn),
        in_specs=[pl.BlockSpec((tm, D), lambda i, j: (i, 0)),
                  pl.BlockSpec((1, D), lambda i, j: (0, 0)),
                  vec, vec,
                  pl.BlockSpec((D, tn), lambda i, j: (0, j)),
                  pl.BlockSpec((1, tn), lambda i, j: (0, j)),
                  pl.BlockSpec((256, 256), lambda i, j: (0, 0))],
        out_specs=pl.BlockSpec((tm, tn), lambda i, j: (i, j)),
        out_shape=jax.ShapeDtypeStruct((N, n_out), BF16),
        scratch_shapes=[pltpu.VMEM((tm, D), BF16)],
        compiler_params=_cp("arbitrary", "arbitrary"),
        name="da_inproj",
    )(x2d, g, sc, sh, w_bf, gain_row, gsum)


def _da_attn_kernel(slope_ref, q_ref, k_ref, v_ref, lq1_ref, lk1_ref, lq2_ref, lk2_ref, subg_ref, o_ref,
                    k1_scr, k2_scr, vt_scr, corr_scr, s1e_scr, s2e_scr, s1o_scr, s2o_scr, qt1_scr, qt2_scr, a1, a2,
                    *, T, SUB, S, lambda_init):
    h = pl.program_id(1)
    slope2 = slope_ref[h] * LOG2E
    lane = lax.broadcasted_iota(jnp.int32, (T, LANES), 1)
    local = lax.broadcasted_iota(jnp.int32, (T, LANES), 0)
    lo = jnp.bitwise_and(local, 255).astype(F32)
    hi = (local - jnp.bitwise_and(local, 255)).astype(F32)
    first_half = lane < DA_HEAD_DIM

    def aug(base, vals):
        out = jnp.zeros((T, LANES), F32)
        for off, val in enumerate(vals):
            out = jnp.where(lane == base + off, val, out)
        return out.astype(BF16)

    c_hi = (slope2 + jnp.zeros((T, LANES), F32)).astype(BF16).astype(F32)
    c_mid = (slope2 - c_hi).astype(BF16).astype(F32)
    c_lo = slope2 - c_hi - c_mid
    k_vals = (lo, lo, lo, hi, hi, hi, -c_hi, -c_mid, -c_lo, -c_hi, -c_mid, -c_lo)
    q_vals = (c_hi, c_mid, c_lo, c_hi, c_mid, c_lo, lo, lo, lo, hi, hi, hi)
    k_aug1 = aug(DA_HEAD_DIM, k_vals)
    k_aug2 = aug(0, k_vals)
    q_aug1 = aug(DA_HEAD_DIM, q_vals)
    q_aug2 = aug(0, q_vals)
    ones_rows = jnp.where(lax.broadcasted_iota(jnp.int32, (VT_ROWS - LANES, T), 0) == 0, 1.0, 0.0).astype(BF16)

    def prep(j, carry):
        r0 = pl.multiple_of(j * T, T)
        k = k_ref[pl.ds(r0, T), :]
        k1_scr[pl.ds(r0, T), :] = jnp.where(first_half, k, k_aug1)
        k2_scr[pl.ds(r0, T), :] = jnp.where(first_half, k_aug2, k)
        vt_scr[:LANES, pl.ds(r0, T)] = v_ref[pl.ds(r0, T), :].astype(F32).T.astype(BF16)
        vt_scr[LANES:, pl.ds(r0, T)] = ones_rows
        return carry

    lax.fori_loop(0, S // T, prep, 0)

    kj = lax.broadcasted_iota(jnp.int32, (T, T), 0)
    qj = lax.broadcasted_iota(jnp.int32, (T, T), 1)
    visible = jnp.right_shift(kj, 6) <= jnp.right_shift(qj, 6)
    ahead = jnp.maximum(kj - qj, 0).astype(F32)
    corr_scr[...] = jnp.where(visible, -2.0 * slope2 * ahead, NEG)

    lam = (jnp.exp(jnp.sum(lq1_ref[...] * lk1_ref[...], axis=-1, keepdims=True))
           - jnp.exp(jnp.sum(lq2_ref[...] * lk2_ref[...], axis=-1, keepdims=True)) + lambda_init)
    k_scrs = (k1_scr, k2_scr)
    s_scrs = ((s1e_scr, s2e_scr), (s1o_scr, s2o_scr))
    qt_scrs = (qt1_scr, qt2_scr)
    accs = (a1, a2)
    n_q = S // T
    TK = T // SUB
    neg = jnp.full((1, T), NEG, F32)

    def load_queries(qi):
        q = q_ref[pl.ds(pl.multiple_of(qi * T, T), T), :]
        for qt_scr, qa in zip(qt_scrs, (jnp.where(first_half, q, q_aug1), jnp.where(first_half, q_aug2, q))):
            qt_scr[...] = qa.astype(F32).T.astype(BF16)

    def block_shift(qi, j):
        return -slope2 * lax.convert_element_type((qi - j) * T, F32)

    def score_chunk(qi, par, j, part, ms, corr):
        k0 = pl.multiple_of(j * T + part * TK, TK)
        out = []
        for mp in range(2):
            s = jnp.dot(k_scrs[mp][pl.ds(k0, TK), :], qt_scrs[mp][...], preferred_element_type=F32)
            if corr is not None:
                s = s + corr[part * TK:(part + 1) * TK, :]
            s_scrs[par][mp][pl.ds(k0, TK), :] = s
            out.append(jnp.maximum(ms[mp], jnp.max(s, axis=0, keepdims=True) + block_shift(qi, j)))
        return tuple(out)

    def weight_chunk(qi, par, j, part, ms):
        k0 = pl.multiple_of(j * T + part * TK, TK)
        vt = vt_scr[:, pl.ds(k0, TK)]
        for mp in range(2):
            p = jnp.exp2(s_scrs[par][mp][pl.ds(k0, TK), :] + (block_shift(qi, j) - ms[mp]))
            accs[mp][...] += jnp.dot(vt, p.astype(BF16), preferred_element_type=F32)

    def finalize(qi):
        acc1 = a1[...]
        acc2 = a2[...]
        o = (acc1[:LANES] / acc1[LANES:LANES + 1] - lam * (acc2[:LANES] / acc2[LANES:LANES + 1])).T
        msq = jnp.mean(o * o, axis=-1, keepdims=True)
        o = o * lax.rsqrt(msq + EPS) * subg_ref[...] * (1.0 - lambda_init)
        o_ref[pl.ds(pl.multiple_of(qi * T, T), T), :] = o.astype(o_ref.dtype)
        a1[...] = jnp.zeros_like(a1)
        a2[...] = jnp.zeros_like(a2)

    a1[...] = jnp.zeros_like(a1)
    a2[...] = jnp.zeros_like(a2)
    load_queries(0)
    ms0 = (neg, neg)
    for part in range(SUB):
        ms0 = score_chunk(0, 0, 0, part, ms0, corr_scr)

    def q_block(qi, par, ms):
        load_queries(qi + 1)

        def chunk(j, ms_next):
            for part in range(SUB):
                ms_next = score_chunk(qi + 1, 1 - par, j, part, ms_next, None)
                weight_chunk(qi, par, j, part, ms)
            return ms_next

        ms_next = lax.fori_loop(0, qi + 1, chunk, (neg, neg))
        for part in range(SUB):
            ms_next = score_chunk(qi + 1, 1 - par, qi + 1, part, ms_next, corr_scr)
        finalize(qi)
        return ms_next

    def q_pair(qq, ms):
        return q_block(2 * qq + 1, 1, q_block(2 * qq, 0, ms))

    ms_even = lax.fori_loop(0, n_q // 2 - 1, q_pair, ms0)
    ms_last = q_block(n_q - 2, 0, ms_even)

    def last_chunk(j, carry):
        for part in range(SUB):
            weight_chunk(n_q - 1, 1, j, part, ms_last)
        return carry

    lax.fori_loop(0, n_q, last_chunk, 0)
    finalize(n_q - 1)


def _da_attention(qkv, slopes, lq1, lk1, lq2, lk2, subg, lambda_init, T=512, sub=2):
    B, S, _ = qkv.shape
    H = DA_HEADS
    vec64 = pl.BlockSpec((1, DA_HEAD_DIM), lambda b, h: (0, 0))
    return pl.pallas_call(
        functools.partial(_da_attn_kernel, T=T, SUB=sub, S=S, lambda_init=lambda_init),
        grid=(B, H),
        in_specs=[pl.BlockSpec(memory_space=pltpu.SMEM),
                  pl.BlockSpec((None, S, LANES), lambda b, h: (b, 0, h)),
                  pl.BlockSpec((None, S, LANES), lambda b, h: (b, 0, H + h)),
                  pl.BlockSpec((None, S, LANES), lambda b, h: (b, 0, 2 * H + h)),
                  vec64, vec64, vec64, vec64,
                  pl.BlockSpec((1, LANES), lambda b, h: (0, 0))],
        out_specs=pl.BlockSpec((None, S, LANES), lambda b, h: (b, 0, h)),
        out_shape=jax.ShapeDtypeStruct((B, S, H * LANES), BF16),
        scratch_shapes=[pltpu.VMEM((S, LANES), BF16), pltpu.VMEM((S, LANES), BF16), pltpu.VMEM((VT_ROWS, S), BF16),
                        pltpu.VMEM((T, T), F32)] + [pltpu.VMEM((S, T), F32) for _ in range(4)] + [
                        pltpu.VMEM((LANES, T), BF16), pltpu.VMEM((LANES, T), BF16),
                        pltpu.VMEM((VT_ROWS, T), F32), pltpu.VMEM((VT_ROWS, T), F32)],
        compiler_params=_cp("arbitrary", "arbitrary"),
        name="da_attention",
    )(slopes, qkv, qkv, qkv, lq1, lk1, lq2, lk2, subg)


def _post0_kernel(x_ref, o_ref, wo_ref, g1_ref, ng_ref, sc_ref, sh_ref, g2_ref, w1_ref, w3_ref, w2_ref,
                  out_ref, x1_scr, h_scr, acc_scr, *, parts):
    j = pl.program_id(1)
    tm = out_ref.shape[0]

    def out_proj(rows):
        return jnp.dot(o_ref[rows, :], wo_ref[...], preferred_element_type=F32)

    def residual_norm(rows, y):
        x1 = x_ref[rows, :] + g1_ref[...] * y
        x1_scr[rows, :] = x1
        h_scr[rows, :] = _modnorm(x1, ng_ref[...], sc_ref[...], sh_ref[...]).astype(BF16)
        acc_scr[rows, :] = jnp.zeros((y.shape[0], D), F32)

    @pl.when(j == 0)
    def _():
        _row_pipeline(parts, tm, out_proj, residual_norm)

    def up(rows):
        h = h_scr[rows, :]
        return (jnp.dot(h, w1_ref[...], preferred_element_type=F32),
                jnp.dot(h, w3_ref[...], preferred_element_type=F32))

    def down(rows, ab):
        a, b = ab
        gact = (a * _sigmoid(a) * b).astype(BF16)
        acc_scr[rows, :] += jnp.dot(gact, w2_ref[...], preferred_element_type=F32)

    _row_pipeline(parts, tm, up, down)

    @pl.when(j == pl.num_programs(1) - 1)
    def _():
        out_ref[...] = x1_scr[...] + g2_ref[...] * acc_scr[...]


def _post0(x2d, o2d, S, wo, g1, ng, sc, sh, g2, w1, w3, w2, tm=512, tf=1408, parts=2):
    N = x2d.shape[0]
    F = w1.shape[1]
    vec = pl.BlockSpec((None, 1, D), lambda i, j: ((i * tm) // S, 0, 0))
    row = pl.BlockSpec((tm, D), lambda i, j: (i, 0))
    return pl.pallas_call(
        functools.partial(_post0_kernel, parts=parts),
        grid=(N // tm, F // tf),
        in_specs=[row, row,
                  pl.BlockSpec((D, D), lambda i, j: (0, 0)),
                  vec,
                  pl.BlockSpec((1, D), lambda i, j: (0, 0)),
                  vec, vec, vec,
                  pl.BlockSpec((D, tf), lambda i, j: (0, j)),
                  pl.BlockSpec((D, tf), lambda i, j: (0, j)),
                  pl.BlockSpec((tf, D), lambda i, j: (j, 0))],
        out_specs=row,
        out_shape=jax.ShapeDtypeStruct((N, D), F32),
        scratch_shapes=[pltpu.VMEM((tm, D), F32), pltpu.VMEM((tm, D), BF16), pltpu.VMEM((tm, D), F32)],
        compiler_params=_cp("arbitrary", "arbitrary"),
        name="post0_ffn",
    )(x2d, o2d, wo, g1, ng, sc, sh, g2, w1, w3, w2)


def _gla_inproj_kernel(x_ref, g_ref, sc_ref, sh_ref, w_ref, mult_ref, wa1_ref, wa2_ref, ba_ref,
                       o_ref, la_ref, h_scr, *, n_plain_blocks, parts):
    j = pl.program_id(1)
    tm = o_ref.shape[0]

    def project_new(rows):
        h = _modnorm(x_ref[rows, :], g_ref[...], sc_ref[...], sh_ref[...]).astype(BF16)
        h_scr[rows, :] = h
        low = jnp.dot(h, wa1_ref[...], preferred_element_type=F32).astype(BF16)
        z = jnp.dot(low, wa2_ref[...], preferred_element_type=F32) + ba_ref[...]
        log_sig = jnp.minimum(z, 0.0) - jnp.log(1.0 + jnp.exp(-jnp.abs(z)))
        la_ref[rows, :] = log_sig * (1.0 / GLA_TAU)
        return jnp.dot(h, w_ref[...], preferred_element_type=F32)

    def project(rows):
        return jnp.dot(h_scr[rows, :], w_ref[...], preferred_element_type=F32)

    def finish_plain(rows, acc):
        o_ref[rows, :] = (acc * mult_ref[...]).astype(BF16)

    def finish_gate(rows, acc):
        o_ref[rows, :] = (acc * _sigmoid(acc)).astype(BF16)

    @pl.when(j == 0)
    def _():
        _row_pipeline(parts, tm, project_new, finish_plain)

    @pl.when((j > 0) & (j < n_plain_blocks))
    def _():
        _row_pipeline(parts, tm, project, finish_plain)

    @pl.when(j >= n_plain_blocks)
    def _():
        _row_pipeline(parts, tm, project, finish_gate)


def _gla_inproj(x2d, S, g, sc, sh, w_bf, mult_row, wa1, wa2, ba, tm=1024, tn=512, parts=4):
    N = x2d.shape[0]
    n_out = w_bf.shape[1]
    kw = wa2.shape[1]
    vec = pl.BlockSpec((None, 1, D), lambda i, j: ((i * tm) // S, 0, 0))
    return pl.pallas_call(
        functools.partial(_gla_inproj_kernel, n_plain_blocks=(2 * D) // tn, parts=parts),
        grid=(N // tm, n_out // tn),
        in_specs=[pl.BlockSpec((tm, D), lambda i, j: (i, 0)),
                  pl.BlockSpec((1, D), lambda i, j: (0, 0)),
                  vec, vec,
                  pl.BlockSpec((D, tn), lambda i, j: (0, j)),
                  pl.BlockSpec((1, tn), lambda i, j: (0, j)),
                  pl.BlockSpec((D, LANES), lambda i, j: (0, 0)),
                  pl.BlockSpec((LANES, kw), lambda i, j: (0, 0)),
                  pl.BlockSpec((1, kw), lambda i, j: (0, 0))],
        out_specs=[pl.BlockSpec((tm, tn), lambda i, j: (i, j)),
                   pl.BlockSpec((tm, kw), lambda i, j: (i, 0))],
        out_shape=[jax.ShapeDtypeStruct((N, n_out), BF16), jax.ShapeDtypeStruct((N, kw), F32)],
        scratch_shapes=[pltpu.VMEM((tm, D), BF16)],
        compiler_params=_cp("arbitrary", "arbitrary"),
        name="gla_inproj",
    )(x2d, g, sc, sh, w_bf, mult_row, wa1, wa2, ba)


def _gla_kernel(q_ref, k_ref, v_ref, r_ref, la_ref, tri_ref, og_ref, o_ref, state_scr, *, NB, SB):
    @pl.when(pl.program_id(1) == 0)
    def _():
        state_scr[...] = jnp.zeros_like(state_scr)

    tri = tri_ref[...]
    ti = lax.broadcasted_iota(jnp.int32, (CHUNK, CHUNK), 0)
    si = lax.broadcasted_iota(jnp.int32, (CHUNK, CHUNK), 1)
    causal = si <= ti
    og = og_ref[...]
    chains = [(nb, h) for nb in range(NB) for h in range(GLA_HEADS)]
    nt = (((1,), (1,)), ((), ()))
    tn = (((0,), (0,)), ((), ()))

    def chunk(c, carry):
        r0 = pl.multiple_of(c * CHUNK, CHUNK)
        rows = pl.ds(r0, CHUNK)
        kcol = [slice(h * GLA_DK, (h + 1) * GLA_DK) for _, h in chains]
        vcol = [slice(h * GLA_DV, (h + 1) * GLA_DV) for _, h in chains]
        las = [la_ref[nb, rows, kc] for (nb, _), kc in zip(chains, kcol)]
        his = [la.astype(BF16) for la in las]
        los = [(la - hi.astype(F32)).astype(BF16) for la, hi in zip(las, his)]
        bs = [jnp.dot(tri, hi, preferred_element_type=F32) + jnp.dot(tri, lo, preferred_element_type=F32)
              for hi, lo in zip(his, los)]
        b_lasts = [b[CHUNK - 1:CHUNK, :] for b in bs]
        qs = [q_ref[nb, rows, kc].astype(F32) for (nb, _), kc in zip(chains, kcol)]
        ks = [k_ref[nb, rows, kc].astype(F32) for (nb, _), kc in zip(chains, kcol)]
        vs = [v_ref[nb, rows, vc] for (nb, _), vc in zip(chains, vcol)]
        q_decs = [(q * jnp.exp(b)).astype(BF16) for q, b in zip(qs, bs)]
        k_invs = [(k * jnp.exp(-b)).astype(BF16) for k, b in zip(ks, bs)]
        k_decs = [(k * jnp.exp(bl - b)).astype(BF16) for k, b, bl in zip(ks, bs, b_lasts)]
        states = [state_scr[nb, h] for nb, h in chains]
        attns = [lax.dot_general(qd, ki, nt, preferred_element_type=F32) for qd, ki in zip(q_decs, k_invs)]
        o_inters = [lax.dot_general(qd, st.astype(BF16), nt, preferred_element_type=F32)
                    for qd, st in zip(q_decs, states)]
        kv_ts = [lax.dot_general(v, kd, tn, preferred_element_type=F32) for v, kd in zip(vs, k_decs)]
        attns = [jnp.where(causal, a, 0.0).astype(BF16) for a in attns]
        o_intras = [jnp.dot(a, v, preferred_element_type=F32) for a, v in zip(attns, vs)]
        for (nb, h), st, bl, kv_t, oi, oe, vc in zip(chains, states, b_lasts, kv_ts, o_intras, o_inters, vcol):
            state_scr[nb, h] = st * jnp.exp(bl) + kv_t
            o = oi + oe
            ms = jnp.mean(o * o, axis=-1, keepdims=True)
            o = o * lax.rsqrt(ms + EPS) * og * r_ref[nb, rows, vc].astype(F32)
            o_ref[nb, rows, vc] = o.astype(o_ref.dtype)
        return carry

    lax.fori_loop(0, SB // CHUNK, chunk, 0)


def _gla(qkvr, la, out_g, nb=2, sb=512):
    B, S, _ = qkvr.shape
    H = GLA_HEADS
    kw = H * GLA_DK
    vw = H * GLA_DV
    tri = (jnp.arange(CHUNK)[None, :] <= jnp.arange(CHUNK)[:, None]).astype(BF16)
    return pl.pallas_call(
        functools.partial(_gla_kernel, NB=nb, SB=sb),
        grid=(B // nb, S // sb),
        in_specs=[pl.BlockSpec((nb, sb, kw), lambda b, s: (b, s, 0)),
                  pl.BlockSpec((nb, sb, kw), lambda b, s: (b, s, 1)),
                  pl.BlockSpec((nb, sb, vw), lambda b, s: (b, s, (2 * kw) // vw)),
                  pl.BlockSpec((nb, sb, vw), lambda b, s: (b, s, (2 * kw) // vw + 1)),
                  pl.BlockSpec((nb, sb, kw), lambda b, s: (b, s, 0)),
                  pl.BlockSpec((CHUNK, CHUNK), lambda b, s: (0, 0)),
                  pl.BlockSpec((1, GLA_DV), lambda b, s: (0, 0))],
        out_specs=pl.BlockSpec((nb, sb, vw), lambda b, s: (b, s, 0)),
        out_shape=jax.ShapeDtypeStruct((B, S, vw), BF16),
        scratch_shapes=[pltpu.VMEM((nb, H, GLA_DV, GLA_DK), F32)],
        compiler_params=_cp("arbitrary", "arbitrary"),
        name="gla",
    )(qkvr, qkvr, qkvr, qkvr, la, tri, out_g)


def _post1_kernel(x_ref, o_ref, wo_ref, g1_ref, ng_ref, sc_ref, sh_ref, rw_ref, ltri_ref,
                  x3_ref, hp_ref, route_ref, cnt_ref, carry_scr, *, sub):
    i = pl.program_id(0)

    @pl.when(i == 0)
    def _():
        carry_scr[...] = jnp.zeros_like(carry_scr)

    y = jnp.dot(o_ref[...], wo_ref[...], preferred_element_type=F32)
    x3 = x_ref[...] + g1_ref[...] * y
    x3_ref[...] = x3
    hb = _modnorm(x3, ng_ref[...], sc_ref[...], sh_ref[...]).astype(BF16)

    half = D // 2
    lo = pltpu.bitcast(hb[:, :half].astype(F32), jnp.uint32)
    hi = pltpu.bitcast(hb[:, half:].astype(F32), jnp.uint32)
    hp_ref[...] = jnp.right_shift(lo, jnp.uint32(16)) | (hi & jnp.uint32(0xFFFF0000))

    tm = hb.shape[0]
    lane = lax.broadcasted_iota(jnp.int32, (tm, LANES), 1)
    lanef = lane.astype(F32)
    logits = jnp.dot(hb, rw_ref[...], preferred_element_type=F32)
    logits = jnp.where(lane < N_EXPERTS, logits, NEG)
    m1 = jnp.max(logits, axis=-1, keepdims=True)
    i1 = jnp.min(jnp.where(logits == m1, lanef, float(LANES)), axis=-1, keepdims=True)
    oh1 = lanef == i1
    rest = jnp.where(oh1, NEG, logits)
    m2 = jnp.max(rest, axis=-1, keepdims=True)
    i2 = jnp.min(jnp.where(rest == m2, lanef, float(LANES)), axis=-1, keepdims=True)
    oh2 = lanef == i2
    e = jnp.exp(m2 - m1)
    gate1 = 1.0 / (1.0 + e)
    gate2 = e / (1.0 + e)

    cnt = jnp.where(oh1, 1.0, 0.0) + jnp.where(oh2, 1.0, 0.0)
    ltri = ltri_ref[...]
    carry = carry_scr[...]
    pres = []
    for s in range(tm // sub):
        c_sub = cnt[s * sub:(s + 1) * sub, :]
        pres.append(jnp.dot(ltri, c_sub.astype(BF16), preferred_element_type=F32) + carry)
        carry = carry + jnp.sum(c_sub, axis=0, keepdims=True)
    pre = jnp.concatenate(pres, axis=0)
    carry_scr[...] = carry
    cnt_ref[...] = carry
    r1 = jnp.sum(jnp.where(oh1, pre, 0.0), axis=-1, keepdims=True)
    r2 = jnp.sum(jnp.where(oh2, pre, 0.0), axis=-1, keepdims=True)
    route = jnp.where(lane == 0, i1, 0.0)
    for idx, val in ((1, i2), (2, r1), (3, r2), (4, gate1), (5, gate2)):
        route = jnp.where(lane == idx, val, route)
    route_ref[...] = route


def _post1(x2d, o2d, S, wo, g1, ng, sc, sh, router_pad, tm=512, sub=256):
    N = x2d.shape[0]
    ltri = (jnp.arange(sub)[None, :] < jnp.arange(sub)[:, None]).astype(BF16)
    vec = pl.BlockSpec((None, 1, D), lambda i: ((i * tm) // S, 0, 0))
    row = pl.BlockSpec((tm, D), lambda i: (i, 0))
    return pl.pallas_call(
        functools.partial(_post1_kernel, sub=sub),
        grid=(N // tm,),
        in_specs=[row, row,
                  pl.BlockSpec((D, D), lambda i: (0, 0)),
                  vec,
                  pl.BlockSpec((1, D), lambda i: (0, 0)),
                  vec, vec,
                  pl.BlockSpec((D, LANES), lambda i: (0, 0)),
                  pl.BlockSpec((sub, sub), lambda i: (0, 0))],
        out_specs=[row,
                   pl.BlockSpec((tm, D // 2), lambda i: (i, 0)),
                   pl.BlockSpec((tm, LANES), lambda i: (i, 0)),
                   pl.BlockSpec((1, LANES), lambda i: (0, 0))],
        out_shape=[jax.ShapeDtypeStruct((N, D), F32),
                   jax.ShapeDtypeStruct((N, D // 2), jnp.uint32),
                   jax.ShapeDtypeStruct((N, LANES), F32),
                   jax.ShapeDtypeStruct((1, LANES), F32)],
        scratch_shapes=[pltpu.VMEM((1, LANES), F32)],
        compiler_params=_cp("arbitrary"),
        name="post1_router",
    )(x2d, o2d, wo, g1, ng, sc, sh, router_pad, ltri)


def _dispatch_kernel(pos_ref, h_ref, xs_in_ref, xs_ref, sem, *, td):
    del xs_in_ref

    def row(r, carry):
        for kk in range(2):
            p = pos_ref[0, 2 * r + kk]
            pltpu.make_async_copy(h_ref.at[pl.ds(r, 1)], xs_ref.at[pl.ds(p, 1)], sem).start()
        return carry

    lax.fori_loop(0, td, row, 0, unroll=8)
    for _ in range(2):
        pltpu.make_async_copy(h_ref, xs_ref.at[pl.ds(0, td)], sem).wait()


def _dispatch(hp, pos, P, td=512):
    N, W = hp.shape
    xs0 = jnp.zeros((P, W), jnp.uint32)
    return pl.pallas_call(
        functools.partial(_dispatch_kernel, td=td),
        grid=(N // td,),
        in_specs=[pl.BlockSpec((None, 1, 2 * td), lambda i: (i, 0, 0), memory_space=pltpu.SMEM),
                  pl.BlockSpec((td, W), lambda i: (i, 0)),
                  pl.BlockSpec(memory_space=pl.ANY)],
        out_specs=pl.BlockSpec(memory_space=pl.ANY),
        out_shape=jax.ShapeDtypeStruct((P, W), jnp.uint32),
        scratch_shapes=[pltpu.SemaphoreType.DMA(())],
        input_output_aliases={2: 0},
        compiler_params=_cp("arbitrary"),
        name="moe_dispatch",
    )(pos.reshape(N // td, 1, 2 * td), hp, xs0)


def _experts_kernel(be_ref, nv_ref, xs_ref, w1_ref, w3_ref, w2_ref, y_ref, xb_scr):
    i = pl.program_id(0)
    j = pl.program_id(1)
    valid = i < nv_ref[0]

    @pl.when(valid & (j == 0))
    def _():
        w = xs_ref[...]
        half = D // 2
        xb_scr[:, :half] = pltpu.bitcast(jnp.left_shift(w, jnp.uint32(16)), F32).astype(BF16)
        xb_scr[:, half:] = pltpu.bitcast(w & jnp.uint32(0xFFFF0000), F32).astype(BF16)
        y_ref[...] = jnp.zeros_like(y_ref)

    @pl.when(jnp.logical_not(valid) & (j == 0))
    def _():
        y_ref[...] = jnp.zeros_like(y_ref)

    @pl.when(valid)
    def _():
        xb = xb_scr[...]
        a = jnp.dot(xb, w1_ref[...], preferred_element_type=F32)
        b = jnp.dot(xb, w3_ref[...], preferred_element_type=F32)
        gact = (a * _sigmoid(a) * b).astype(BF16)
        y_ref[...] += jnp.dot(gact, w2_ref[...], preferred_element_type=F32)


def _experts(xs, blk_expert, n_valid, w1, w3, w2, tm, tf=512):
    P, W = xs.shape
    E, _, F = w1.shape
    nblk = P // tm
    nj = F // tf

    def _i(i, nv):
        return jnp.minimum(i, nv[0] - 1)

    def _j(i, j, nv):
        return jnp.where(i < nv[0], j, nj - 1)

    grid_spec = pltpu.PrefetchScalarGridSpec(
        num_scalar_prefetch=2,
        grid=(nblk, nj),
        in_specs=[pl.BlockSpec((tm, W), lambda i, j, be, nv: (_i(i, nv), 0)),
                  pl.BlockSpec((None, D, tf), lambda i, j, be, nv: (be[_i(i, nv)], 0, _j(i, j, nv))),
                  pl.BlockSpec((None, D, tf), lambda i, j, be, nv: (be[_i(i, nv)], 0, _j(i, j, nv))),
                  pl.BlockSpec((None, tf, D), lambda i, j, be, nv: (be[_i(i, nv)], _j(i, j, nv), 0))],
        out_specs=pl.BlockSpec((tm, D), lambda i, j, be, nv: (i, 0)),
        scratch_shapes=[pltpu.VMEM((tm, D), BF16)],
    )
    return pl.pallas_call(
        _experts_kernel,
        grid_spec=grid_spec,
        out_shape=jax.ShapeDtypeStruct((P, D), F32),
        compiler_params=_cp("arbitrary", "arbitrary"),
        name="moe_experts",
    )(blk_expert, n_valid, xs, w1, w3, w2)


def _combine_kernel(pos_ref, pos_next_ref, route_ref, x_ref, g2_ref, y_ref, out_ref, ybuf, sems, *, tc):
    i = pl.program_id(0)
    slot = jnp.bitwise_and(i, 1)

    def gather_copy(p, s, kk, r):
        return pltpu.make_async_copy(y_ref.at[pl.ds(p, 1)], ybuf.at[s, kk, pl.ds(r, 1)], sems.at[s])

    def start_gathers(table_ref, s):
        def row(r, carry):
            for kk in range(2):
                gather_copy(table_ref[0, 2 * r + kk], s, kk, r).start()
            return carry

        lax.fori_loop(0, tc, row, 0, unroll=8)

    @pl.when(i == 0)
    def _():
        start_gathers(pos_ref, 0)

    @pl.when(i + 1 < pl.num_programs(0))
    def _():
        start_gathers(pos_next_ref, 1 - slot)

    for kk in range(2):
        pltpu.make_async_copy(y_ref.at[pl.ds(0, tc)], ybuf.at[slot, kk], sems.at[slot]).wait()
    route = route_ref[...]
    moe = route[:, 4:5] * ybuf[slot, 0] + route[:, 5:6] * ybuf[slot, 1]
    out_ref[...] = x_ref[...] + g2_ref[...] * moe


def _combine(y, pos, route, x3, S, g2, tc=256):
    N = x3.shape[0]
    n_steps = N // tc
    pos_blocks = pos.reshape(n_steps, 1, 2 * tc)
    return pl.pallas_call(
        functools.partial(_combine_kernel, tc=tc),
        grid=(n_steps,),
        in_specs=[pl.BlockSpec((None, 1, 2 * tc), lambda i: (i, 0, 0), memory_space=pltpu.SMEM),
                  pl.BlockSpec((None, 1, 2 * tc), lambda i: (jnp.minimum(i + 1, n_steps - 1), 0, 0),
                               memory_space=pltpu.SMEM),
                  pl.BlockSpec((tc, LANES), lambda i: (i, 0)),
                  pl.BlockSpec((tc, D), lambda i: (i, 0)),
                  pl.BlockSpec((None, 1, D), lambda i: ((i * tc) // S, 0, 0)),
                  pl.BlockSpec(memory_space=pl.ANY)],
        out_specs=pl.BlockSpec((tc, D), lambda i: (i, 0)),
        out_shape=jax.ShapeDtypeStruct((N, D), F32),
        scratch_shapes=[pltpu.VMEM((2, 2, tc, D), F32), pltpu.SemaphoreType.DMA((2,))],
        compiler_params=_cp("arbitrary"),
        name="moe_combine",
    )(pos_blocks, pos_blocks, route, x3, g2, y)


def _moe(hp, route, counts, x3, S, g2, w1, w3, w2, tm=1024):
    N = x3.shape[0]
    cnt = counts[0, :N_EXPERTS].astype(jnp.int32)
    nblk_e = (cnt + tm - 1) // tm
    blk_end = jnp.cumsum(nblk_e)
    row_start = (blk_end - nblk_e) * tm
    e_idx = route[:, 0:2].astype(jnp.int32)
    rank = route[:, 2:4].astype(jnp.int32)
    pos = row_start[e_idx] + rank
    P = 2 * N + N_EXPERTS * tm
    nblk = P // tm
    blk_expert = jnp.clip(jnp.searchsorted(blk_end, jnp.arange(nblk, dtype=jnp.int32), side="right"),
                          0, N_EXPERTS - 1).astype(jnp.int32)
    n_valid = blk_end[-1:].astype(jnp.int32)
    xs = _dispatch(hp, pos, P)
    y = _experts(xs, blk_expert, n_valid, w1, w3, w2, tm)
    return _combine(y, pos, route, x3, S, g2)


def kernel(x, c, ada_w, ada_b, norm1_g, norm2_g, da_w_in, da_q_gain, da_k_gain, da_lam_q1, da_lam_k1, da_lam_q2,
           da_lam_k2, da_subln_g, da_w_out, gla_w_in, gla_w_a1, gla_w_a2, gla_b_a, gla_out_g, gla_w_out,
           ffn_w1, ffn_w3, ffn_w2, moe_router, moe_w1, moe_w3, moe_w2):
    B, S, _ = x.shape
    N = B * S
    mod = _adaln(c, ada_w, ada_b)
    mods = [[mod[l, :, k * D:(k + 1) * D].reshape(B, 1, D) for k in range(6)] for l in range(2)]
    x2d = x.reshape(N, D)

    sh1, sc1, gt1, sh2, sc2, gt2 = mods[0]
    lambda_init = 0.8 - 0.6 * math.exp(-0.3 * 0)
    qk_scale = DA_HEAD_DIM ** -0.5 * LOG2E
    gain_row = jnp.concatenate([jnp.tile(da_q_gain[0].reshape(-1) * qk_scale, DA_HEADS),
                                jnp.tile(da_k_gain[0].reshape(-1), DA_HEADS),
                                jnp.ones((D,), F32)]).reshape(1, 3 * D)
    qkv = _da_inproj(x2d, S, norm1_g[0].reshape(1, D), sc1, sh1, da_w_in[0].astype(BF16), gain_row)
    slopes = 2.0 ** (-8.0 * jnp.arange(1, DA_HEADS + 1, dtype=F32) / DA_HEADS)
    o = _da_attention(qkv.reshape(B, S, 3 * D), slopes,
                      da_lam_q1[0].reshape(1, -1), da_lam_k1[0].reshape(1, -1),
                      da_lam_q2[0].reshape(1, -1), da_lam_k2[0].reshape(1, -1),
                      da_subln_g[0].reshape(1, -1), lambda_init)
    x2d = _post0(x2d, o.reshape(N, D), S, da_w_out[0].astype(BF16), gt1, norm2_g[0].reshape(1, D), sc2, sh2, gt2,
                 ffn_w1[0].astype(BF16), ffn_w3[0].astype(BF16), ffn_w2[0].astype(BF16))

    sh1, sc1, gt1, sh2, sc2, gt2 = mods[1]
    kw = GLA_HEADS * GLA_DK
    mult_row = jnp.concatenate([jnp.full((kw,), GLA_DK ** -0.5, F32), jnp.ones((3 * D - kw,), F32)]).reshape(1, -1)
    rank = gla_w_a1.shape[-1]
    wa1 = jnp.zeros((D, LANES), BF16).at[:, :rank].set(gla_w_a1[0].astype(BF16))
    wa2 = jnp.zeros((LANES, kw), BF16).at[:rank, :].set(gla_w_a2[0].astype(BF16))
    qkvr, la = _gla_inproj(x2d, S, norm1_g[1].reshape(1, D), sc1, sh1, gla_w_in[0].astype(BF16), mult_row,
                           wa1, wa2, gla_b_a[0].reshape(1, kw))
    o = _gla(qkvr.reshape(B, S, 3 * D), la.reshape(B, S, kw), gla_out_g[0].reshape(1, -1))
    router_pad = jnp.zeros((D, LANES), BF16).at[:, :N_EXPERTS].set(moe_router[0].astype(BF16))
    x3, hp, route, counts = _post1(x2d, o.reshape(N, D), S, gla_w_out[0].astype(BF16), gt1,
                                   norm2_g[1].reshape(1, D), sc2, sh2, router_pad)
    out = _moe(hp, route, counts, x3, S, gt2,
               moe_w1[0].astype(BF16), moe_w3[0].astype(BF16), moe_w2[0].astype(BF16))
    return out.reshape(B, S, D)
```

```python
import functools
import math

import jax
import jax.numpy as jnp
from jax import lax
from jax.experimental import pallas as pl
from jax.experimental.pallas import tpu as pltpu

F32 = jnp.float32
BF16 = jnp.bfloat16

D = 1024
EPS = 1e-6
NEG = -1e30
CHUNK = 64
DA_HEADS = 8
DA_HEAD_DIM = 64
GLA_HEADS = 4
GLA_DK = 128
GLA_DV = 256
GLA_TAU = 16.0
N_EXPERTS = 8
LANES = 128
LOG2E = 1.4426950408889634
VT_ROWS = LANES + 16

VMEM_LIMIT = 56 * 1024 * 1024


def _cp(*sem):
    return pltpu.CompilerParams(dimension_semantics=sem, vmem_limit_bytes=VMEM_LIMIT)


def _sigmoid(x):
    return 1.0 / (1.0 + jnp.exp(-x))


def _modnorm(x, g, sc, sh):
    ms = jnp.mean(x * x, axis=-1, keepdims=True)
    return (x * lax.rsqrt(ms + EPS) * g) * (1.0 + sc) + sh


def _adaln_kernel(c_ref, w_ref, b_ref, o_ref):
    c = c_ref[...]
    ca = (c * _sigmoid(c)).astype(BF16)
    o_ref[...] = jnp.dot(ca, w_ref[...].astype(BF16), preferred_element_type=F32) + b_ref[...]


def _adaln(c, ada_w, ada_b):
    L, _, n6 = ada_w.shape
    B = c.shape[0]
    tn = 1536
    return pl.pallas_call(
        _adaln_kernel,
        grid=(L, n6 // tn),
        in_specs=[pl.BlockSpec((B, D), lambda l, n: (0, 0)),
                  pl.BlockSpec((None, D, tn), lambda l, n: (l, 0, n)),
                  pl.BlockSpec((None, 1, tn), lambda l, n: (l, 0, n))],
        out_specs=pl.BlockSpec((None, B, tn), lambda l, n: (l, 0, n)),
        out_shape=jax.ShapeDtypeStruct((L, B, n6), F32),
        compiler_params=_cp("arbitrary", "arbitrary"),
        name="adaln",
    )(c, ada_w, ada_b.reshape(L, 1, n6))


def _row_pipeline(parts, rows_total, matmul, finish):
    rp = rows_total // parts
    prev = None
    for p in range(parts):
        rows = slice(p * rp, (p + 1) * rp)
        acc = matmul(rows)
        if prev is not None:
            finish(*prev)
        prev = (rows, acc)
    finish(*prev)


def _da_inproj_kernel(x_ref, g_ref, sc_ref, sh_ref, w_ref, gain_ref, gsum_ref, o_ref, h_scr, *, n_qk_blocks, parts):
    j = pl.program_id(1)
    tm, tn = o_ref.shape

    def project_new(rows):
        h = _modnorm(x_ref[rows, :], g_ref[...], sc_ref[...], sh_ref[...]).astype(BF16)
        h_scr[rows, :] = h
        return jnp.dot(h, w_ref[...], preferred_element_type=F32)

    def project(rows):
        return jnp.dot(h_scr[rows, :], w_ref[...], preferred_element_type=F32)

    def finish_qk(rows, acc):
        y2 = (acc * acc).astype(BF16)
        gsum = gsum_ref[...]
        ms = jnp.concatenate(
            [jnp.dot(y2[:, c * 256:(c + 1) * 256], gsum, preferred_element_type=F32) for c in range(tn // 256)],
            axis=1)
        o_ref[rows, :] = (acc * lax.rsqrt(ms + EPS) * gain_ref[...]).astype(BF16)

    def finish_v(rows, acc):
        o_ref[rows, :] = (acc * gain_ref[...]).astype(BF16)

    @pl.when(j == 0)
    def _():
        _row_pipeline(parts, tm, project_new, finish_qk)

    @pl.when((j > 0) & (j < n_qk_blocks))
    def _():
        _row_pipeline(parts, tm, project, finish_qk)

    @pl.when(j >= n_qk_blocks)
    def _():
        _row_pipeline(parts, tm, project, finish_v)


def _da_inproj(x2d, S, g, sc, sh, w_bf, gain_row, tm=1024, tn=1024, parts=4):
    N = x2d.shape[0]
    n_out = w_bf.shape[1]
    r = jnp.arange(256) // DA_HEAD_DIM
    gsum = jnp.where(r[:, None] == r[None, :], 1.0 / DA_HEAD_DIM, 0.0).astype(BF16)
    vec = pl.BlockSpec((None, 1, D), lambda i, j: ((i * tm) // S, 0, 0))
    return pl.pallas_call(
        functools.partial(_da_inproj_kernel, n_qk_blocks=(2 * D) // tn, parts=parts),
        grid=(N // tm, n_out // tn),
        in_specs=[pl.BlockSpec((tm, D), lambda i, j: (i, 0)),
                  pl.BlockSpec((1, D), lambda i, j: (0, 0)),
                  vec, vec,
                  pl.BlockSpec((D, tn), lambda i, j: (0, j)),
                  pl.BlockSpec((1, tn), lambda i, j: (0, j)),
                  pl.BlockSpec((256, 256), lambda i, j: (0, 0))],
        out_specs=pl.BlockSpec((tm, tn), lambda i, j: (i, j)),
        out_shape=jax.ShapeDtypeStruct((N, n_out), BF16),
        scratch_shapes=[pltpu.VMEM((tm, D), BF16)],
        compiler_params=_cp("arbitrary", "arbitrary"),
        name="da_inproj",
    )(x2d, g, sc, sh, w_bf, gain_row, gsum)


def _da_attn_kernel(slope_ref, q_ref, k_ref, v_ref, lq1_ref, lk1_ref, lq2_ref, lk2_ref, subg_ref, o_ref,
                    k1_scr, k2_scr, vt_scr, corr_scr, s1e_scr, s2e_scr, s1o_scr, s2o_scr, qt1_scr, qt2_scr, a1, a2,
                    *, T, SUB, S, lambda_init):
    h = pl.program_id(1)
    slope2 = slope_ref[h] * LOG2E
    lane = lax.broadcasted_iota(jnp.int32, (T, LANES), 1)
    local = lax.broadcasted_iota(jnp.int32, (T, LANES), 0)
    lo = jnp.bitwise_and(local, 255).astype(F32)
    hi = (local - jnp.bitwise_and(local, 255)).astype(F32)
    first_half = lane < DA_HEAD_DIM

    def aug(base, vals):
        out = jnp.zeros((T, LANES), F32)
        for off, val in enumerate(vals):
            out = jnp.where(lane == base + off, val, out)
        return out.astype(BF16)

    c_hi = (slope2 + jnp.zeros((T, LANES), F32)).astype(BF16).astype(F32)
    c_mid = (slope2 - c_hi).astype(BF16).astype(F32)
    c_lo = slope2 - c_hi - c_mid
    k_vals = (lo, lo, lo, hi, hi, hi, -c_hi, -c_mid, -c_lo, -c_hi, -c_mid, -c_lo)
    q_vals = (c_hi, c_mid, c_lo, c_hi, c_mid, c_lo, lo, lo, lo, hi, hi, hi)
    k_aug1 = aug(DA_HEAD_DIM, k_vals)
    k_aug2 = aug(0, k_vals)
    q_aug1 = aug(DA_HEAD_DIM, q_vals)
    q_aug2 = aug(0, q_vals)
    ones_rows = jnp.where(lax.broadcasted_iota(jnp.int32, (VT_ROWS - LANES, T), 0) == 0, 1.0, 0.0).astype(BF16)

    def prep(j, carry):
        r0 = pl.multiple_of(j * T, T)
        k = k_ref[pl.ds(r0, T), :]
        k1_scr[pl.ds(r0, T), :] = jnp.where(first_half, k, k_aug1)
        k2_scr[pl.ds(r0, T), :] = jnp.where(first_half, k_aug2, k)
        vt_scr[:LANES, pl.ds(r0, T)] = v_ref[pl.ds(r0, T), :].astype(F32).T.astype(BF16)
        vt_scr[LANES:, pl.ds(r0, T)] = ones_rows
        return carry

    lax.fori_loop(0, S // T, prep, 0)

    kj = lax.broadcasted_iota(jnp.int32, (T, T), 0)
    qj = lax.broadcasted_iota(jnp.int32, (T, T), 1)
    visible = jnp.right_shift(kj, 6) <= jnp.right_shift(qj, 6)
    ahead = jnp.maximum(kj - qj, 0).astype(F32)
    corr_scr[...] = jnp.where(visible, -2.0 * slope2 * ahead, NEG)

    lam = (jnp.exp(jnp.sum(lq1_ref[...] * lk1_ref[...], axis=-1, keepdims=True))
           - jnp.exp(jnp.sum(lq2_ref[...] * lk2_ref[...], axis=-1, keepdims=True)) + lambda_init)
    k_scrs = (k1_scr, k2_scr)
    s_scrs = ((s1e_scr, s2e_scr), (s1o_scr, s2o_scr))
    qt_scrs = (qt1_scr, qt2_scr)
    accs = (a1, a2)
    n_q = S // T
    TK = T // SUB
    neg = jnp.full((1, T), NEG, F32)

    def load_queries(qi):
        q = q_ref[pl.ds(pl.multiple_of(qi * T, T), T), :]
        for qt_scr, qa in zip(qt_scrs, (jnp.where(first_half, q, q_aug1), jnp.where(first_half, q_aug2, q))):
            qt_scr[...] = qa.astype(F32).T.astype(BF16)

    def block_shift(qi, j):
        return -slope2 * lax.convert_element_type((qi - j) * T, F32)

    def score_chunk(qi, par, j, part, ms, corr):
        k0 = pl.multiple_of(j * T + part * TK, TK)
        out = []
        for mp in range(2):
            s = jnp.dot(k_scrs[mp][pl.ds(k0, TK), :], qt_scrs[mp][...], preferred_element_type=F32)
            if corr is not None:
                s = s + corr[part * TK:(part + 1) * TK, :]
            s_scrs[par][mp][pl.ds(k0, TK), :] = s
            out.append(jnp.maximum(ms[mp], jnp.max(s, axis=0, keepdims=True) + block_shift(qi, j)))
        return tuple(out)

    def weight_chunk(qi, par, j, part, ms):
        k0 = pl.multiple_of(j * T + part * TK, TK)
        vt = vt_scr[:, pl.ds(k0, TK)]
        for mp in range(2):
            p = jnp.exp2(s_scrs[par][mp][pl.ds(k0, TK), :] + (block_shift(qi, j) - ms[mp]))
            accs[mp][...] += jnp.dot(vt, p.astype(BF16), preferred_element_type=F32)

    def finalize(qi):
        acc1 = a1[...]
        acc2 = a2[...]
        o = (acc1[:LANES] / acc1[LANES:LANES + 1] - lam * (acc2[:LANES] / acc2[LANES:LANES + 1])).T
        msq = jnp.mean(o * o, axis=-1, keepdims=True)
        o = o * lax.rsqrt(msq + EPS) * subg_ref[...] * (1.0 - lambda_init)
        o_ref[pl.ds(pl.multiple_of(qi * T, T), T), :] = o.astype(o_ref.dtype)
        a1[...] = jnp.zeros_like(a1)
        a2[...] = jnp.zeros_like(a2)

    a1[...] = jnp.zeros_like(a1)
    a2[...] = jnp.zeros_like(a2)
    load_queries(0)
    ms0 = (neg, neg)
    for part in range(SUB):
        ms0 = score_chunk(0, 0, 0, part, ms0, corr_scr)

    def q_block(qi, par, ms):
        load_queries(qi + 1)

        def chunk(j, ms_next):
            for part in range(SUB):
                ms_next = score_chunk(qi + 1, 1 - par, j, part, ms_next, None)
                weight_chunk(qi, par, j, part, ms)
            return ms_next

        ms_next = lax.fori_loop(0, qi + 1, chunk, (neg, neg))
        for part in range(SUB):
            ms_next = score_chunk(qi + 1, 1 - par, qi + 1, part, ms_next, corr_scr)
        finalize(qi)
        return ms_next

    def q_pair(qq, ms):
        return q_block(2 * qq + 1, 1, q_block(2 * qq, 0, ms))

    ms_even = lax.fori_loop(0, n_q // 2 - 1, q_pair, ms0)
    ms_last = q_block(n_q - 2, 0, ms_even)

    def last_chunk(j, carry):
        for part in range(SUB):
            weight_chunk(n_q - 1, 1, j, part, ms_last)
        return carry

    lax.fori_loop(0, n_q, last_chunk, 0)
    finalize(n_q - 1)


def _da_attention(qkv, slopes, lq1, lk1, lq2, lk2, subg, lambda_init, T=512, sub=2):
    B, S, _ = qkv.shape
    H = DA_HEADS
    vec64 = pl.BlockSpec((1, DA_HEAD_DIM), lambda b, h: (0, 0))
    return pl.pallas_call(
        functools.partial(_da_attn_kernel, T=T, SUB=sub, S=S, lambda_init=lambda_init),
        grid=(B, H),
        in_specs=[pl.BlockSpec(memory_space=pltpu.SMEM),
                  pl.BlockSpec((None, S, LANES), lambda b, h: (b, 0, h)),
                  pl.BlockSpec((None, S, LANES), lambda b, h: (b, 0, H + h)),
                  pl.BlockSpec((None, S, LANES), lambda b, h: (b, 0, 2 * H + h)),
                  vec64, vec64, vec64, vec64,
                  pl.BlockSpec((1, LANES), lambda b, h: (0, 0))],
        out_specs=pl.BlockSpec((None, S, LANES), lambda b, h: (b, 0, h)),
        out_shape=jax.ShapeDtypeStruct((B, S, H * LANES), BF16),
        scratch_shapes=[pltpu.VMEM((S, LANES), BF16), pltpu.VMEM((S, LANES), BF16), pltpu.VMEM((VT_ROWS, S), BF16),
                        pltpu.VMEM((T, T), F32)] + [pltpu.VMEM((S, T), F32) for _ in range(4)] + [
                        pltpu.VMEM((LANES, T), BF16), pltpu.VMEM((LANES, T), BF16),
                        pltpu.VMEM((VT_ROWS, T), F32), pltpu.VMEM((VT_ROWS, T), F32)],
        compiler_params=_cp("arbitrary", "arbitrary"),
        name="da_attention",
    )(slopes, qkv, qkv, qkv, lq1, lk1, lq2, lk2, subg)


def _post0_kernel(x_ref, o_ref, wo_ref, g1_ref, ng_ref, sc_ref, sh_ref, g2_ref, w1_ref, w3_ref, w2_ref,
                  out_ref, x1_scr, h_scr, acc_scr, *, parts):
    j = pl.program_id(1)
    tm = out_ref.shape[0]

    def out_proj(rows):
        return jnp.dot(o_ref[rows, :], wo_ref[...], preferred_element_type=F32)

    def residual_norm(rows, y):
        x1 = x_ref[rows, :] + g1_ref[...] * y
        x1_scr[rows, :] = x1
        h_scr[rows, :] = _modnorm(x1, ng_ref[...], sc_ref[...], sh_ref[...]).astype(BF16)
        acc_scr[rows, :] = jnp.zeros((y.shape[0], D), F32)

    @pl.when(j == 0)
    def _():
        _row_pipeline(parts, tm, out_proj, residual_norm)

    def up(rows):
        h = h_scr[rows, :]
        return (jnp.dot(h, w1_ref[...], preferred_element_type=F32),
                jnp.dot(h, w3_ref[...], preferred_element_type=F32))

    def down(rows, ab):
        a, b = ab
        gact = (a * _sigmoid(a) * b).astype(BF16)
        acc_scr[rows, :] += jnp.dot(gact, w2_ref[...], preferred_element_type=F32)

    _row_pipeline(parts, tm, up, down)

    @pl.when(j == pl.num_programs(1) - 1)
    def _():
        out_ref[...] = x1_scr[...] + g2_ref[...] * acc_scr[...]


def _post0(x2d, o2d, S, wo, g1, ng, sc, sh, g2, w1, w3, w2, tm=512, tf=1408, parts=2):
    N = x2d.shape[0]
    F = w1.shape[1]
    vec = pl.BlockSpec((None, 1, D), lambda i, j: ((i * tm) // S, 0, 0))
    row = pl.BlockSpec((tm, D), lambda i, j: (i, 0))
    return pl.pallas_call(
        functools.partial(_post0_kernel, parts=parts),
        grid=(N // tm, F // tf),
        in_specs=[row, row,
                  pl.BlockSpec((D, D), lambda i, j: (0, 0)),
                  vec,
                  pl.BlockSpec((1, D), lambda i, j: (0, 0)),
                  vec, vec, vec,
                  pl.BlockSpec((D, tf), lambda i, j: (0, j)),
                  pl.BlockSpec((D, tf), lambda i, j: (0, j)),
                  pl.BlockSpec((tf, D), lambda i, j: (j, 0))],
        out_specs=row,
        out_shape=jax.ShapeDtypeStruct((N, D), F32),
        scratch_shapes=[pltpu.VMEM((tm, D), F32), pltpu.VMEM((tm, D), BF16), pltpu.VMEM((tm, D), F32)],
        compiler_params=_cp("arbitrary", "arbitrary"),
        name="post0_ffn",
    )(x2d, o2d, wo, g1, ng, sc, sh, g2, w1, w3, w2)


def _gla_inproj_kernel(x_ref, g_ref, sc_ref, sh_ref, w_ref, mult_ref, wa1_ref, wa2_ref, ba_ref,
                       o_ref, la_ref, h_scr, *, n_plain_blocks, parts):
    j = pl.program_id(1)
    tm = o_ref.shape[0]

    def project_new(rows):
        h = _modnorm(x_ref[rows, :], g_ref[...], sc_ref[...], sh_ref[...]).astype(BF16)
        h_scr[rows, :] = h
        low = jnp.dot(h, wa1_ref[...], preferred_element_type=F32).astype(BF16)
        z = jnp.dot(low, wa2_ref[...], preferred_element_type=F32) + ba_ref[...]
        log_sig = jnp.minimum(z, 0.0) - jnp.log(1.0 + jnp.exp(-jnp.abs(z)))
        la_ref[rows, :] = log_sig * (1.0 / GLA_TAU)
        return jnp.dot(h, w_ref[...], preferred_element_type=F32)

    def project(rows):
        return jnp.dot(h_scr[rows, :], w_ref[...], preferred_element_type=F32)

    def finish_plain(rows, acc):
        o_ref[rows, :] = (acc * mult_ref[...]).astype(BF16)

    def finish_gate(rows, acc):
        o_ref[rows, :] = (acc * _sigmoid(acc)).astype(BF16)

    @pl.when(j == 0)
    def _():
        _row_pipeline(parts, tm, project_new, finish_plain)

    @pl.when((j > 0) & (j < n_plain_blocks))
    def _():
        _row_pipeline(parts, tm, project, finish_plain)

    @pl.when(j >= n_plain_blocks)
    def _():
        _row_pipeline(parts, tm, project, finish_gate)


def _gla_inproj(x2d, S, g, sc, sh, w_bf, mult_row, wa1, wa2, ba, tm=1024, tn=1024, parts=4):
    N = x2d.shape[0]
    n_out = w_bf.shape[1]
    kw = wa2.shape[1]
    vec = pl.BlockSpec((None, 1, D), lambda i, j: ((i * tm) // S, 0, 0))
    return pl.pallas_call(
        functools.partial(_gla_inproj_kernel, n_plain_blocks=(2 * D) // tn, parts=parts),
        grid=(N // tm, n_out // tn),
        in_specs=[pl.BlockSpec((tm, D), lambda i, j: (i, 0)),
                  pl.BlockSpec((1, D), lambda i, j: (0, 0)),
                  vec, vec,
                  pl.BlockSpec((D, tn), lambda i, j: (0, j)),
                  pl.BlockSpec((1, tn), lambda i, j: (0, j)),
                  pl.BlockSpec((D, LANES), lambda i, j: (0, 0)),
                  pl.BlockSpec((LANES, kw), lambda i, j: (0, 0)),
                  pl.BlockSpec((1, kw), lambda i, j: (0, 0))],
        out_specs=[pl.BlockSpec((tm, tn), lambda i, j: (i, j)),
                   pl.BlockSpec((tm, kw), lambda i, j: (i, 0))],
        out_shape=[jax.ShapeDtypeStruct((N, n_out), BF16), jax.ShapeDtypeStruct((N, kw), F32)],
        scratch_shapes=[pltpu.VMEM((tm, D), BF16)],
        compiler_params=_cp("arbitrary", "arbitrary"),
        name="gla_inproj",
    )(x2d, g, sc, sh, w_bf, mult_row, wa1, wa2, ba)


def _gla_kernel(q_ref, k_ref, v_ref, r_ref, la_ref, tri_ref, og_ref, o_ref, state_scr, *, NB, SB):
    @pl.when(pl.program_id(1) == 0)
    def _():
        state_scr[...] = jnp.zeros_like(state_scr)

    tri = tri_ref[...]
    ti = lax.broadcasted_iota(jnp.int32, (CHUNK, CHUNK), 0)
    si = lax.broadcasted_iota(jnp.int32, (CHUNK, CHUNK), 1)
    causal = si <= ti
    og = og_ref[...]
    chains = [(nb, h) for nb in range(NB) for h in range(GLA_HEADS)]
    nt = (((1,), (1,)), ((), ()))
    tn = (((0,), (0,)), ((), ()))

    def chunk(c, carry):
        r0 = pl.multiple_of(c * CHUNK, CHUNK)
        rows = pl.ds(r0, CHUNK)
        kcol = [slice(h * GLA_DK, (h + 1) * GLA_DK) for _, h in chains]
        vcol = [slice(h * GLA_DV, (h + 1) * GLA_DV) for _, h in chains]
        las = [la_ref[nb, rows, kc] for (nb, _), kc in zip(chains, kcol)]
        his = [la.astype(BF16) for la in las]
        los = [(la - hi.astype(F32)).astype(BF16) for la, hi in zip(las, his)]
        bs = [jnp.dot(tri, hi, preferred_element_type=F32) + jnp.dot(tri, lo, preferred_element_type=F32)
              for hi, lo in zip(his, los)]
        b_lasts = [b[CHUNK - 1:CHUNK, :] for b in bs]
        qs = [q_ref[nb, rows, kc].astype(F32) for (nb, _), kc in zip(chains, kcol)]
        ks = [k_ref[nb, rows, kc].astype(F32) for (nb, _), kc in zip(chains, kcol)]
        vs = [v_ref[nb, rows, vc] for (nb, _), vc in zip(chains, vcol)]
        q_decs = [(q * jnp.exp(b)).astype(BF16) for q, b in zip(qs, bs)]
        k_invs = [(k * jnp.exp(-b)).astype(BF16) for k, b in zip(ks, bs)]
        k_decs = [(k * jnp.exp(bl - b)).astype(BF16) for k, b, bl in zip(ks, bs, b_lasts)]
        states = [state_scr[nb, h] for nb, h in chains]
        attns = [lax.dot_general(qd, ki, nt, preferred_element_type=F32) for qd, ki in zip(q_decs, k_invs)]
        o_inters = [lax.dot_general(qd, st.astype(BF16), nt, preferred_element_type=F32)
                    for qd, st in zip(q_decs, states)]
        kv_ts = [lax.dot_general(v, kd, tn, preferred_element_type=F32) for v, kd in zip(vs, k_decs)]
        attns = [jnp.where(causal, a, 0.0).astype(BF16) for a in attns]
        o_intras = [jnp.dot(a, v, preferred_element_type=F32) for a, v in zip(attns, vs)]
        for (nb, h), st, bl, kv_t, oi, oe, vc in zip(chains, states, b_lasts, kv_ts, o_intras, o_inters, vcol):
            state_scr[nb, h] = st * jnp.exp(bl) + kv_t
            o = oi + oe
            ms = jnp.mean(o * o, axis=-1, keepdims=True)
            o = o * lax.rsqrt(ms + EPS) * og * r_ref[nb, rows, vc].astype(F32)
            o_ref[nb, rows, vc] = o.astype(o_ref.dtype)
        return carry

    lax.fori_loop(0, SB // CHUNK, chunk, 0)


def _gla(qkvr, la, out_g, nb=2, sb=512):
    B, S, _ = qkvr.shape
    H = GLA_HEADS
    kw = H * GLA_DK
    vw = H * GLA_DV
    tri = (jnp.arange(CHUNK)[None, :] <= jnp.arange(CHUNK)[:, None]).astype(BF16)
    return pl.pallas_call(
        functools.partial(_gla_kernel, NB=nb, SB=sb),
        grid=(B // nb, S // sb),
        in_specs=[pl.BlockSpec((nb, sb, kw), lambda b, s: (b, s, 0)),
                  pl.BlockSpec((nb, sb, kw), lambda b, s: (b, s, 1)),
                  pl.BlockSpec((nb, sb, vw), lambda b, s: (b, s, (2 * kw) // vw)),
                  pl.BlockSpec((nb, sb, vw), lambda b, s: (b, s, (2 * kw) // vw + 1)),
                  pl.BlockSpec((nb, sb, kw), lambda b, s: (b, s, 0)),
                  pl.BlockSpec((CHUNK, CHUNK), lambda b, s: (0, 0)),
                  pl.BlockSpec((1, GLA_DV), lambda b, s: (0, 0))],
        out_specs=pl.BlockSpec((nb, sb, vw), lambda b, s: (b, s, 0)),
        out_shape=jax.ShapeDtypeStruct((B, S, vw), BF16),
        scratch_shapes=[pltpu.VMEM((nb, H, GLA_DV, GLA_DK), F32)],
        compiler_params=_cp("arbitrary", "arbitrary"),
        name="gla",
    )(qkvr, qkvr, qkvr, qkvr, la, tri, out_g)


def _post1_kernel(x_ref, o_ref, wo_ref, g1_ref, ng_ref, sc_ref, sh_ref, rw_ref, ltri_ref,
                  x3_ref, hp_ref, route_ref, cnt_ref, carry_scr, *, sub):
    i = pl.program_id(0)

    @pl.when(i == 0)
    def _():
        carry_scr[...] = jnp.zeros_like(carry_scr)

    y = jnp.dot(o_ref[...], wo_ref[...], preferred_element_type=F32)
    x3 = x_ref[...] + g1_ref[...] * y
    x3_ref[...] = x3
    hb = _modnorm(x3, ng_ref[...], sc_ref[...], sh_ref[...]).astype(BF16)

    half = D // 2
    lo = pltpu.bitcast(hb[:, :half].astype(F32), jnp.uint32)
    hi = pltpu.bitcast(hb[:, half:].astype(F32), jnp.uint32)
    hp_ref[...] = jnp.right_shift(lo, jnp.uint32(16)) | (hi & jnp.uint32(0xFFFF0000))

    tm = hb.shape[0]
    lane = lax.broadcasted_iota(jnp.int32, (tm, LANES), 1)
    lanef = lane.astype(F32)
    logits = jnp.dot(hb, rw_ref[...], preferred_element_type=F32)
    logits = jnp.where(lane < N_EXPERTS, logits, NEG)
    m1 = jnp.max(logits, axis=-1, keepdims=True)
    i1 = jnp.min(jnp.where(logits == m1, lanef, float(LANES)), axis=-1, keepdims=True)
    oh1 = lanef == i1
    rest = jnp.where(oh1, NEG, logits)
    m2 = jnp.max(rest, axis=-1, keepdims=True)
    i2 = jnp.min(jnp.where(rest == m2, lanef, float(LANES)), axis=-1, keepdims=True)
    oh2 = lanef == i2
    e = jnp.exp(m2 - m1)
    gate1 = 1.0 / (1.0 + e)
    gate2 = e / (1.0 + e)

    cnt = jnp.where(oh1, 1.0, 0.0) + jnp.where(oh2, 1.0, 0.0)
    ltri = ltri_ref[...]
    carry = carry_scr[...]
    pres = []
    for s in range(tm // sub):
        c_sub = cnt[s * sub:(s + 1) * sub, :]
        pres.append(jnp.dot(ltri, c_sub.astype(BF16), preferred_element_type=F32) + carry)
        carry = carry + jnp.sum(c_sub, axis=0, keepdims=True)
    pre = jnp.concatenate(pres, axis=0)
    carry_scr[...] = carry
    cnt_ref[...] = carry
    r1 = jnp.sum(jnp.where(oh1, pre, 0.0), axis=-1, keepdims=True)
    r2 = jnp.sum(jnp.where(oh2, pre, 0.0), axis=-1, keepdims=True)
    route = jnp.where(lane == 0, i1, 0.0)
    for idx, val in ((1, i2), (2, r1), (3, r2), (4, gate1), (5, gate2)):
        route = jnp.where(lane == idx, val, route)
    route_ref[...] = route


def _post1(x2d, o2d, S, wo, g1, ng, sc, sh, router_pad, tm=512, sub=256):
    N = x2d.shape[0]
    ltri = (jnp.arange(sub)[None, :] < jnp.arange(sub)[:, None]).astype(BF16)
    vec = pl.BlockSpec((None, 1, D), lambda i: ((i * tm) // S, 0, 0))
    row = pl.BlockSpec((tm, D), lambda i: (i, 0))
    return pl.pallas_call(
        functools.partial(_post1_kernel, sub=sub),
        grid=(N // tm,),
        in_specs=[row, row,
                  pl.BlockSpec((D, D), lambda i: (0, 0)),
                  vec,
                  pl.BlockSpec((1, D), lambda i: (0, 0)),
                  vec, vec,
                  pl.BlockSpec((D, LANES), lambda i: (0, 0)),
                  pl.BlockSpec((sub, sub), lambda i: (0, 0))],
        out_specs=[row,
                   pl.BlockSpec((tm, D // 2), lambda i: (i, 0)),
                   pl.BlockSpec((tm, LANES), lambda i: (i, 0)),
                   pl.BlockSpec((1, LANES), lambda i: (0, 0))],
        out_shape=[jax.ShapeDtypeStruct((N, D), F32),
                   jax.ShapeDtypeStruct((N, D // 2), jnp.uint32),
                   jax.ShapeDtypeStruct((N, LANES), F32),
                   jax.ShapeDtypeStruct((1, LANES), F32)],
        scratch_shapes=[pltpu.VMEM((1, LANES), F32)],
        compiler_params=_cp("arbitrary"),
        name="post1_router",
    )(x2d, o2d, wo, g1, ng, sc, sh, router_pad, ltri)


def _dispatch_kernel(pos_ref, h_ref, xs_in_ref, xs_ref, sem, *, td):
    del xs_in_ref

    def row(r, carry):
        for kk in range(2):
            p = pos_ref[0, 2 * r + kk]
            pltpu.make_async_copy(h_ref.at[pl.ds(r, 1)], xs_ref.at[pl.ds(p, 1)], sem).start()
        return carry

    lax.fori_loop(0, td, row, 0, unroll=8)
    for _ in range(2):
        pltpu.make_async_copy(h_ref, xs_ref.at[pl.ds(0, td)], sem).wait()


def _dispatch(hp, pos, P, td=512):
    N, W = hp.shape
    xs0 = jnp.zeros((P, W), jnp.uint32)
    return pl.pallas_call(
        functools.partial(_dispatch_kernel, td=td),
        grid=(N // td,),
        in_specs=[pl.BlockSpec((None, 1, 2 * td), lambda i: (i, 0, 0), memory_space=pltpu.SMEM),
                  pl.BlockSpec((td, W), lambda i: (i, 0)),
                  pl.BlockSpec(memory_space=pl.ANY)],
        out_specs=pl.BlockSpec(memory_space=pl.ANY),
        out_shape=jax.ShapeDtypeStruct((P, W), jnp.uint32),
        scratch_shapes=[pltpu.SemaphoreType.DMA(())],
        input_output_aliases={2: 0},
        compiler_params=_cp("arbitrary"),
        name="moe_dispatch",
    )(pos.reshape(N // td, 1, 2 * td), hp, xs0)


def _experts_kernel(be_ref, nv_ref, xs_ref, w1_ref, w3_ref, w2_ref, y_ref, xb_scr, *, parts):
    i = pl.program_id(0)
    j = pl.program_id(1)
    valid = i < nv_ref[0]
    tm = y_ref.shape[0]

    @pl.when(valid & (j == 0))
    def _():
        w = xs_ref[...]
        half = D // 2
        xb_scr[:, :half] = pltpu.bitcast(jnp.left_shift(w, jnp.uint32(16)), F32).astype(BF16)
        xb_scr[:, half:] = pltpu.bitcast(w & jnp.uint32(0xFFFF0000), F32).astype(BF16)
        y_ref[...] = jnp.zeros_like(y_ref)

    @pl.when(jnp.logical_not(valid) & (j == 0))
    def _():
        y_ref[...] = jnp.zeros_like(y_ref)

    def up(rows):
        xb = xb_scr[rows, :]
        return (jnp.dot(xb, w1_ref[...], preferred_element_type=F32),
                jnp.dot(xb, w3_ref[...], preferred_element_type=F32))

    def down(rows, ab):
        a, b = ab
        gact = (a * _sigmoid(a) * b).astype(BF16)
        y_ref[rows, :] += jnp.dot(gact, w2_ref[...], preferred_element_type=F32)

    @pl.when(valid)
    def _():
        _row_pipeline(parts, tm, up, down)


def _experts(xs, blk_expert, n_valid, w1, w3, w2, tm, tf=1792, parts=4):
    P, W = xs.shape
    E, _, F = w1.shape
    nblk = P // tm
    nj = F // tf

    def _i(i, nv):
        return jnp.minimum(i, nv[0] - 1)

    def _j(i, j, nv):
        return jnp.where(i < nv[0], j, nj - 1)

    grid_spec = pltpu.PrefetchScalarGridSpec(
        num_scalar_prefetch=2,
        grid=(nblk, nj),
        in_specs=[pl.BlockSpec((tm, W), lambda i, j, be, nv: (_i(i, nv), 0)),
                  pl.BlockSpec((None, D, tf), lambda i, j, be, nv: (be[_i(i, nv)], 0, _j(i, j, nv))),
                  pl.BlockSpec((None, D, tf), lambda i, j, be, nv: (be[_i(i, nv)], 0, _j(i, j, nv))),
                  pl.BlockSpec((None, tf, D), lambda i, j, be, nv: (be[_i(i, nv)], _j(i, j, nv), 0))],
        out_specs=pl.BlockSpec((tm, D), lambda i, j, be, nv: (i, 0)),
        scratch_shapes=[pltpu.VMEM((tm, D), BF16)],
    )
    return pl.pallas_call(
        functools.partial(_experts_kernel, parts=parts),
        grid_spec=grid_spec,
        out_shape=jax.ShapeDtypeStruct((P, D), F32),
        compiler_params=_cp("arbitrary", "arbitrary"),
        name="moe_experts",
    )(blk_expert, n_valid, xs, w1, w3, w2)


def _combine_kernel(pos_ref, pos_next_ref, route_ref, x_ref, g2_ref, y_ref, out_ref, ybuf, sems, *, tc):
    i = pl.program_id(0)
    slot = jnp.bitwise_and(i, 1)

    def gather_copy(p, s, kk, r):
        return pltpu.make_async_copy(y_ref.at[pl.ds(p, 1)], ybuf.at[s, kk, pl.ds(r, 1)], sems.at[s])

    def start_gathers(table_ref, s):
        def row(r, carry):
            for kk in range(2):
                gather_copy(table_ref[0, 2 * r + kk], s, kk, r).start()
            return carry

        lax.fori_loop(0, tc, row, 0, unroll=8)

    @pl.when(i == 0)
    def _():
        start_gathers(pos_ref, 0)

    @pl.when(i + 1 < pl.num_programs(0))
    def _():
        start_gathers(pos_next_ref, 1 - slot)

    for kk in range(2):
        pltpu.make_async_copy(y_ref.at[pl.ds(0, tc)], ybuf.at[slot, kk], sems.at[slot]).wait()
    route = route_ref[...]
    moe = route[:, 4:5] * ybuf[slot, 0] + route[:, 5:6] * ybuf[slot, 1]
    out_ref[...] = x_ref[...] + g2_ref[...] * moe


def _combine(y, pos, route, x3, S, g2, tc=256):
    N = x3.shape[0]
    n_steps = N // tc
    pos_blocks = pos.reshape(n_steps, 1, 2 * tc)
    return pl.pallas_call(
        functools.partial(_combine_kernel, tc=tc),
        grid=(n_steps,),
        in_specs=[pl.BlockSpec((None, 1, 2 * tc), lambda i: (i, 0, 0), memory_space=pltpu.SMEM),
                  pl.BlockSpec((None, 1, 2 * tc), lambda i: (jnp.minimum(i + 1, n_steps - 1), 0, 0),
                               memory_space=pltpu.SMEM),
                  pl.BlockSpec((tc, LANES), lambda i: (i, 0)),
                  pl.BlockSpec((tc, D), lambda i: (i, 0)),
                  pl.BlockSpec((None, 1, D), lambda i: ((i * tc) // S, 0, 0)),
                  pl.BlockSpec(memory_space=pl.ANY)],
        out_specs=pl.BlockSpec((tc, D), lambda i: (i, 0)),
        out_shape=jax.ShapeDtypeStruct((N, D), F32),
        scratch_shapes=[pltpu.VMEM((2, 2, tc, D), F32), pltpu.SemaphoreType.DMA((2,))],
        compiler_params=_cp("arbitrary"),
        name="moe_combine",
    )(pos_blocks, pos_blocks, route, x3, g2, y)


def _moe(hp, route, counts, x3, S, g2, w1, w3, w2, tm=1024):
    N = x3.shape[0]
    cnt = counts[0, :N_EXPERTS].astype(jnp.int32)
    nblk_e = (cnt + tm - 1) // tm
    blk_end = jnp.cumsum(nblk_e)
    row_start = (blk_end - nblk_e) * tm
    e_idx = route[:, 0:2].astype(jnp.int32)
    rank = route[:, 2:4].astype(jnp.int32)
    pos = row_start[e_idx] + rank
    P = 2 * N + N_EXPERTS * tm
    nblk = P // tm
    blk_expert = jnp.clip(jnp.searchsorted(blk_end, jnp.arange(nblk, dtype=jnp.int32), side="right"),
                          0, N_EXPERTS - 1).astype(jnp.int32)
    n_valid = blk_end[-1:].astype(jnp.int32)
    xs = _dispatch(hp, pos, P)
    y = _experts(xs, blk_expert, n_valid, w1, w3, w2, tm)
    return _combine(y, pos, route, x3, S, g2)


def kernel(x, c, ada_w, ada_b, norm1_g, norm2_g, da_w_in, da_q_gain, da_k_gain, da_lam_q1, da_lam_k1, da_lam_q2,
           da_lam_k2, da_subln_g, da_w_out, gla_w_in, gla_w_a1, gla_w_a2, gla_b_a, gla_out_g, gla_w_out,
           ffn_w1, ffn_w3, ffn_w2, moe_router, moe_w1, moe_w3, moe_w2):
    B, S, _ = x.shape
    N = B * S
    mod = _adaln(c, ada_w, ada_b)
    mods = [[mod[l, :, k * D:(k + 1) * D].reshape(B, 1, D) for k in range(6)] for l in range(2)]
    x2d = x.reshape(N, D)

    sh1, sc1, gt1, sh2, sc2, gt2 = mods[0]
    lambda_init = 0.8 - 0.6 * math.exp(-0.3 * 0)
    qk_scale = DA_HEAD_DIM ** -0.5 * LOG2E
    gain_row = jnp.concatenate([jnp.tile(da_q_gain[0].reshape(-1) * qk_scale, DA_HEADS),
                                jnp.tile(da_k_gain[0].reshape(-1), DA_HEADS),
                                jnp.ones((D,), F32)]).reshape(1, 3 * D)
    qkv = _da_inproj(x2d, S, norm1_g[0].reshape(1, D), sc1, sh1, da_w_in[0].astype(BF16), gain_row)
    slopes = 2.0 ** (-8.0 * jnp.arange(1, DA_HEADS + 1, dtype=F32) / DA_HEADS)
    o = _da_attention(qkv.reshape(B, S, 3 * D), slopes,
                      da_lam_q1[0].reshape(1, -1), da_lam_k1[0].reshape(1, -1),
                      da_lam_q2[0].reshape(1, -1), da_lam_k2[0].reshape(1, -1),
                      da_subln_g[0].reshape(1, -1), lambda_init)
    x2d = _post0(x2d, o.reshape(N, D), S, da_w_out[0].astype(BF16), gt1, norm2_g[0].reshape(1, D), sc2, sh2, gt2,
                 ffn_w1[0].astype(BF16), ffn_w3[0].astype(BF16), ffn_w2[0].astype(BF16))

    sh1, sc1, gt1, sh2, sc2, gt2 = mods[1]
    kw = GLA_HEADS * GLA_DK
    mult_row = jnp.concatenate([jnp.full((kw,), GLA_DK ** -0.5, F32), jnp.ones((3 * D - kw,), F32)]).reshape(1, -1)
    rank = gla_w_a1.shape[-1]
    wa1 = jnp.zeros((D, LANES), BF16).at[:, :rank].set(gla_w_a1[0].astype(BF16))
    wa2 = jnp.zeros((LANES, kw), BF16).at[:rank, :].set(gla_w_a2[0].astype(BF16))
    qkvr, la = _gla_inproj(x2d, S, norm1_g[1].reshape(1, D), sc1, sh1, gla_w_in[0].astype(BF16), mult_row,
                           wa1, wa2, gla_b_a[0].reshape(1, kw))
    o = _gla(qkvr.reshape(B, S, 3 * D), la.reshape(B, S, kw), gla_out_g[0].reshape(1, -1))
    router_pad = jnp.zeros((D, LANES), BF16).at[:, :N_EXPERTS].set(moe_router[0].astype(BF16))
    x3, hp, route, counts = _post1(x2d, o.reshape(N, D), S, gla_w_out[0].astype(BF16), gt1,
                                   norm2_g[1].reshape(1, D), sc2, sh2, router_pad)
    out = _moe(hp, route, counts, x3, S, gt2,
               moe_w1[0].astype(BF16), moe_w3[0].astype(BF16), moe_w2[0].astype(BF16))
    return out.reshape(B, S, D)
```

```python
import functools
import math

import jax
import jax.numpy as jnp
from jax import lax
from jax.experimental import pallas as pl
from jax.experimental.pallas import tpu as pltpu

F32 = jnp.float32
BF16 = jnp.bfloat16
FP8 = jnp.float8_e4m3fn
FP8_TOP = 256.0
FP8_TINY = 1e-30

D = 1024
EPS = 1e-6
NEG = -1e30
CHUNK = 64
DA_HEADS = 8
DA_HEAD_DIM = 64
GLA_HEADS = 4
GLA_DK = 128
GLA_DV = 256
GLA_TAU = 16.0
N_EXPERTS = 8
LANES = 128
LOG2E = 1.4426950408889634
VT_ROWS = LANES + 16

VMEM_LIMIT = 56 * 1024 * 1024


def _cp(*sem):
    return pltpu.CompilerParams(dimension_semantics=sem, vmem_limit_bytes=VMEM_LIMIT)


def _sigmoid(x):
    return 1.0 / (1.0 + jnp.exp(-x))


def _modnorm(x, g, sc, sh):
    ms = jnp.mean(x * x, axis=-1, keepdims=True)
    return (x * lax.rsqrt(ms + EPS) * g) * (1.0 + sc) + sh


def _adaln_kernel(c_ref, w_ref, b_ref, o_ref):
    c = c_ref[...]
    ca = (c * _sigmoid(c)).astype(BF16)
    o_ref[...] = jnp.dot(ca, w_ref[...].astype(BF16), preferred_element_type=F32) + b_ref[...]


def _adaln(c, ada_w, ada_b):
    L, _, n6 = ada_w.shape
    B = c.shape[0]
    tn = 1536
    return pl.pallas_call(
        _adaln_kernel,
        grid=(L, n6 // tn),
        in_specs=[pl.BlockSpec((B, D), lambda l, n: (0, 0)),
                  pl.BlockSpec((None, D, tn), lambda l, n: (l, 0, n)),
                  pl.BlockSpec((None, 1, tn), lambda l, n: (l, 0, n))],
        out_specs=pl.BlockSpec((None, B, tn), lambda l, n: (l, 0, n)),
        out_shape=jax.ShapeDtypeStruct((L, B, n6), F32),
        compiler_params=_cp("arbitrary", "arbitrary"),
        name="adaln",
    )(c, ada_w, ada_b.reshape(L, 1, n6))


def _row_pipeline(parts, rows_total, matmul, finish):
    rp = rows_total // parts
    prev = None
    for p in range(parts):
        rows = slice(p * rp, (p + 1) * rp)
        acc = matmul(rows)
        if prev is not None:
            finish(*prev)
        prev = (rows, acc)
    finish(*prev)


def _da_inproj_kernel(x_ref, g_ref, sc_ref, sh_ref, w_ref, gain_ref, gsum_ref, o_ref, h_scr, *, n_qk_blocks, parts):
    j = pl.program_id(1)
    tm, tn = o_ref.shape

    def project_new(rows):
        h = _modnorm(x_ref[rows, :], g_ref[...], sc_ref[...], sh_ref[...]).astype(BF16)
        h_scr[rows, :] = h
        return jnp.dot(h, w_ref[...], preferred_element_type=F32)

    def project(rows):
        return jnp.dot(h_scr[rows, :], w_ref[...], preferred_element_type=F32)

    def finish_qk(rows, acc):
        y2 = (acc * acc).astype(BF16)
        gsum = gsum_ref[...]
        ms = jnp.concatenate(
            [jnp.dot(y2[:, c * 256:(c + 1) * 256], gsum, preferred_element_type=F32) for c in range(tn // 256)],
            axis=1)
        o_ref[rows, :] = (acc * lax.rsqrt(ms + EPS) * gain_ref[...]).astype(BF16)

    def finish_v(rows, acc):
        o_ref[rows, :] = (acc * gain_ref[...]).astype(BF16)

    @pl.when(j == 0)
    def _():
        _row_pipeline(parts, tm, project_new, finish_qk)

    @pl.when((j > 0) & (j < n_qk_blocks))
    def _():
        _row_pipeline(parts, tm, project, finish_qk)

    @pl.when(j >= n_qk_blocks)
    def _():
        _row_pipeline(parts, tm, project, finish_v)


def _da_inproj(x2d, S, g, sc, sh, w_bf, gain_row, tm=1024, tn=1024, parts=4):
    N = x2d.shape[0]
    n_out = w_bf.shape[1]
    r = jnp.arange(256) // DA_HEAD_DIM
    gsum = jnp.where(r[:, None] == r[None, :], 1.0 / DA_HEAD_DIM, 0.0).astype(BF16)
    vec = pl.BlockSpec((None, 1, D), lambda i, j: ((i * tm) // S, 0, 0))
    return pl.pallas_call(
        functools.partial(_da_inproj_kernel, n_qk_blocks=(2 * D) // tn, parts=parts),
        grid=(N // tm, n_out // tn),
        in_specs=[pl.BlockSpec((tm, D), lambda i, j: (i, 0)),
                  pl.BlockSpec((1, D), lambda i, j: (0, 0)),
                  vec, vec,
                  pl.BlockSpec((D, tn), lambda i, j: (0, j)),
                  pl.BlockSpec((1, tn), lambda i, j: (0, j)),
                  pl.BlockSpec((256, 256), lambda i, j: (0, 0))],
        out_specs=pl.BlockSpec((tm, tn), lambda i, j: (i, j)),
        out_shape=jax.ShapeDtypeStruct((N, n_out), BF16),
        scratch_shapes=[pltpu.VMEM((tm, D), BF16)],
        compiler_params=_cp("arbitrary", "arbitrary"),
        name="da_inproj",
    )(x2d, g, sc, sh, w_bf, gain_row, gsum)


def _da_attn_kernel(slope_ref, q_ref, k_ref, v_ref, lq1_ref, lk1_ref, lq2_ref, lk2_ref, subg_ref, o_ref,
                    k1_scr, k2_scr, vt_scr, corr_scr, s1e_scr, s2e_scr, s1o_scr, s2o_scr, qt1_scr, qt2_scr, a1, a2,
                    *, T, SUB, S, lambda_init):
    h = pl.program_id(1)
    slope2 = slope_ref[h] * LOG2E
    lane = lax.broadcasted_iota(jnp.int32, (T, LANES), 1)
    local = lax.broadcasted_iota(jnp.int32, (T, LANES), 0)
    lo = jnp.bitwise_and(local, 255).astype(F32)
    hi = (local - jnp.bitwise_and(local, 255)).astype(F32)
    first_half = lane < DA_HEAD_DIM

    def aug(base, vals):
        out = jnp.zeros((T, LANES), F32)
        for off, val in enumerate(vals):
            out = jnp.where(lane == base + off, val, out)
        return out.astype(BF16)

    c_hi = (slope2 + jnp.zeros((T, LANES), F32)).astype(BF16).astype(F32)
    c_mid = (slope2 - c_hi).astype(BF16).astype(F32)
    c_lo = slope2 - c_hi - c_mid
    k_vals = (lo, lo, lo, hi, hi, hi, -c_hi, -c_mid, -c_lo, -c_hi, -c_mid, -c_lo)
    q_vals = (c_hi, c_mid, c_lo, c_hi, c_mid, c_lo, lo, lo, lo, hi, hi, hi)
    k_aug1 = aug(DA_HEAD_DIM, k_vals)
    k_aug2 = aug(0, k_vals)
    q_aug1 = aug(DA_HEAD_DIM, q_vals)
    q_aug2 = aug(0, q_vals)
    ones_rows = jnp.where(lax.broadcasted_iota(jnp.int32, (VT_ROWS - LANES, T), 0) == 0, 1.0, 0.0).astype(BF16)

    def prep(j, carry):
        r0 = pl.multiple_of(j * T, T)
        k = k_ref[pl.ds(r0, T), :]
        k1_scr[pl.ds(r0, T), :] = jnp.where(first_half, k, k_aug1)
        k2_scr[pl.ds(r0, T), :] = jnp.where(first_half, k_aug2, k)
        vt_scr[:LANES, pl.ds(r0, T)] = v_ref[pl.ds(r0, T), :].astype(F32).T.astype(BF16)
        vt_scr[LANES:, pl.ds(r0, T)] = ones_rows
        return carry

    lax.fori_loop(0, S // T, prep, 0)

    kj = lax.broadcasted_iota(jnp.int32, (T, T), 0)
    qj = lax.broadcasted_iota(jnp.int32, (T, T), 1)
    visible = jnp.right_shift(kj, 6) <= jnp.right_shift(qj, 6)
    ahead = jnp.maximum(kj - qj, 0).astype(F32)
    corr_scr[...] = jnp.where(visible, -2.0 * slope2 * ahead, NEG)

    lam = (jnp.exp(jnp.sum(lq1_ref[...] * lk1_ref[...], axis=-1, keepdims=True))
           - jnp.exp(jnp.sum(lq2_ref[...] * lk2_ref[...], axis=-1, keepdims=True)) + lambda_init)
    k_scrs = (k1_scr, k2_scr)
    s_scrs = ((s1e_scr, s2e_scr), (s1o_scr, s2o_scr))
    qt_scrs = (qt1_scr, qt2_scr)
    accs = (a1, a2)
    n_q = S // T
    TK = T // SUB
    neg = jnp.full((1, T), NEG, F32)

    def load_queries(qi):
        q = q_ref[pl.ds(pl.multiple_of(qi * T, T), T), :]
        for qt_scr, qa in zip(qt_scrs, (jnp.where(first_half, q, q_aug1), jnp.where(first_half, q_aug2, q))):
            qt_scr[...] = qa.astype(F32).T.astype(BF16)

    def block_shift(qi, j):
        return -slope2 * lax.convert_element_type((qi - j) * T, F32)

    def score_chunk(qi, par, j, part, ms, corr):
        k0 = pl.multiple_of(j * T + part * TK, TK)
        out = []
        for mp in range(2):
            s = jnp.dot(k_scrs[mp][pl.ds(k0, TK), :], qt_scrs[mp][...], preferred_element_type=F32)
            if corr is not None:
                s = s + corr[part * TK:(part + 1) * TK, :]
            s_scrs[par][mp][pl.ds(k0, TK), :] = s
            out.append(jnp.maximum(ms[mp], jnp.max(s, axis=0, keepdims=True) + block_shift(qi, j)))
        return tuple(out)

    def weight_chunk(qi, par, j, part, ms):
        k0 = pl.multiple_of(j * T + part * TK, TK)
        vt = vt_scr[:, pl.ds(k0, TK)]
        for mp in range(2):
            p = jnp.exp2(s_scrs[par][mp][pl.ds(k0, TK), :] + (block_shift(qi, j) - ms[mp]))
            accs[mp][...] += jnp.dot(vt, p.astype(BF16), preferred_element_type=F32)

    def finalize(qi):
        acc1 = a1[...]
        acc2 = a2[...]
        o = (acc1[:LANES] / acc1[LANES:LANES + 1] - lam * (acc2[:LANES] / acc2[LANES:LANES + 1])).T
        msq = jnp.mean(o * o, axis=-1, keepdims=True)
        o = o * lax.rsqrt(msq + EPS) * subg_ref[...] * (1.0 - lambda_init)
        o_ref[pl.ds(pl.multiple_of(qi * T, T), T), :] = o.astype(o_ref.dtype)
        a1[...] = jnp.zeros_like(a1)
        a2[...] = jnp.zeros_like(a2)

    a1[...] = jnp.zeros_like(a1)
    a2[...] = jnp.zeros_like(a2)
    load_queries(0)
    ms0 = (neg, neg)
    for part in range(SUB):
        ms0 = score_chunk(0, 0, 0, part, ms0, corr_scr)

    def q_block(qi, par, ms):
        load_queries(qi + 1)

        def chunk(j, ms_next):
            for part in range(SUB):
                ms_next = score_chunk(qi + 1, 1 - par, j, part, ms_next, None)
                weight_chunk(qi, par, j, part, ms)
            return ms_next

        ms_next = lax.fori_loop(0, qi + 1, chunk, (neg, neg))
        for part in range(SUB):
            ms_next = score_chunk(qi + 1, 1 - par, qi + 1, part, ms_next, corr_scr)
        finalize(qi)
        return ms_next

    def q_pair(qq, ms):
        return q_block(2 * qq + 1, 1, q_block(2 * qq, 0, ms))

    ms_even = lax.fori_loop(0, n_q // 2 - 1, q_pair, ms0)
    ms_last = q_block(n_q - 2, 0, ms_even)

    def last_chunk(j, carry):
        for part in range(SUB):
            weight_chunk(n_q - 1, 1, j, part, ms_last)
        return carry

    lax.fori_loop(0, n_q, last_chunk, 0)
    finalize(n_q - 1)


def _da_attention(qkv, slopes, lq1, lk1, lq2, lk2, subg, lambda_init, T=512, sub=2):
    B, S, _ = qkv.shape
    H = DA_HEADS
    vec64 = pl.BlockSpec((1, DA_HEAD_DIM), lambda b, h: (0, 0))
    return pl.pallas_call(
        functools.partial(_da_attn_kernel, T=T, SUB=sub, S=S, lambda_init=lambda_init),
        grid=(B, H),
        in_specs=[pl.BlockSpec(memory_space=pltpu.SMEM),
                  pl.BlockSpec((None, S, LANES), lambda b, h: (b, 0, h)),
                  pl.BlockSpec((None, S, LANES), lambda b, h: (b, 0, H + h)),
                  pl.BlockSpec((None, S, LANES), lambda b, h: (b, 0, 2 * H + h)),
                  vec64, vec64, vec64, vec64,
                  pl.BlockSpec((1, LANES), lambda b, h: (0, 0))],
        out_specs=pl.BlockSpec((None, S, LANES), lambda b, h: (b, 0, h)),
        out_shape=jax.ShapeDtypeStruct((B, S, H * LANES), BF16),
        scratch_shapes=[pltpu.VMEM((S, LANES), BF16), pltpu.VMEM((S, LANES), BF16), pltpu.VMEM((VT_ROWS, S), BF16),
                        pltpu.VMEM((T, T), F32)] + [pltpu.VMEM((S, T), F32) for _ in range(4)] + [
                        pltpu.VMEM((LANES, T), BF16), pltpu.VMEM((LANES, T), BF16),
                        pltpu.VMEM((VT_ROWS, T), F32), pltpu.VMEM((VT_ROWS, T), F32)],
        compiler_params=_cp("arbitrary", "arbitrary"),
        name="da_attention",
    )(slopes, qkv, qkv, qkv, lq1, lk1, lq2, lk2, subg)


def _post0_kernel(x_ref, o_ref, wo_ref, g1_ref, ng_ref, sc_ref, sh_ref, g2_ref, w1_ref, w3_ref, w2_ref,
                  out_ref, x1_scr, h_scr, acc_scr, *, parts):
    j = pl.program_id(1)
    tm = out_ref.shape[0]

    def out_proj(rows):
        return jnp.dot(o_ref[rows, :], wo_ref[...], preferred_element_type=F32)

    def residual_norm(rows, y):
        x1 = x_ref[rows, :] + g1_ref[...] * y
        x1_scr[rows, :] = x1
        h_scr[rows, :] = _modnorm(x1, ng_ref[...], sc_ref[...], sh_ref[...]).astype(BF16)
        acc_scr[rows, :] = jnp.zeros((y.shape[0], D), F32)

    @pl.when(j == 0)
    def _():
        _row_pipeline(parts, tm, out_proj, residual_norm)

    def up(rows):
        h = h_scr[rows, :]
        return (jnp.dot(h, w1_ref[...], preferred_element_type=F32),
                jnp.dot(h, w3_ref[...], preferred_element_type=F32))

    def down(rows, ab):
        a, b = ab
        gact = (a * _sigmoid(a) * b).astype(BF16)
        acc_scr[rows, :] += jnp.dot(gact, w2_ref[...], preferred_element_type=F32)

    _row_pipeline(parts, tm, up, down)

    @pl.when(j == pl.num_programs(1) - 1)
    def _():
        out_ref[...] = x1_scr[...] + g2_ref[...] * acc_scr[...]


def _post0(x2d, o2d, S, wo, g1, ng, sc, sh, g2, w1, w3, w2, tm=512, tf=1408, parts=2):
    N = x2d.shape[0]
    F = w1.shape[1]
    vec = pl.BlockSpec((None, 1, D), lambda i, j: ((i * tm) // S, 0, 0))
    row = pl.BlockSpec((tm, D), lambda i, j: (i, 0))
    return pl.pallas_call(
        functools.partial(_post0_kernel, parts=parts),
        grid=(N // tm, F // tf),
        in_specs=[row, row,
                  pl.BlockSpec((D, D), lambda i, j: (0, 0)),
                  vec,
                  pl.BlockSpec((1, D), lambda i, j: (0, 0)),
                  vec, vec, vec,
                  pl.BlockSpec((D, tf), lambda i, j: (0, j)),
                  pl.BlockSpec((D, tf), lambda i, j: (0, j)),
                  pl.BlockSpec((tf, D), lambda i, j: (j, 0))],
        out_specs=row,
        out_shape=jax.ShapeDtypeStruct((N, D), F32),
        scratch_shapes=[pltpu.VMEM((tm, D), F32), pltpu.VMEM((tm, D), BF16), pltpu.VMEM((tm, D), F32)],
        compiler_params=_cp("arbitrary", "arbitrary"),
        name="post0_ffn",
    )(x2d, o2d, wo, g1, ng, sc, sh, g2, w1, w3, w2)


def _gla_inproj_kernel(x_ref, g_ref, sc_ref, sh_ref, w_ref, mult_ref, wa1_ref, wa2_ref, ba_ref,
                       o_ref, la_ref, h_scr, *, n_plain_blocks, parts):
    j = pl.program_id(1)
    tm = o_ref.shape[0]

    def project_new(rows):
        h = _modnorm(x_ref[rows, :], g_ref[...], sc_ref[...], sh_ref[...]).astype(BF16)
        h_scr[rows, :] = h
        low = jnp.dot(h, wa1_ref[...], preferred_element_type=F32).astype(BF16)
        z = jnp.dot(low, wa2_ref[...], preferred_element_type=F32) + ba_ref[...]
        log_sig = jnp.minimum(z, 0.0) - jnp.log(1.0 + jnp.exp(-jnp.abs(z)))
        la_ref[rows, :] = log_sig * (1.0 / GLA_TAU)
        return jnp.dot(h, w_ref[...], preferred_element_type=F32)

    def project(rows):
        return jnp.dot(h_scr[rows, :], w_ref[...], preferred_element_type=F32)

    def finish_plain(rows, acc):
        o_ref[rows, :] = (acc * mult_ref[...]).astype(BF16)

    def finish_gate(rows, acc):
        o_ref[rows, :] = (acc * _sigmoid(acc)).astype(BF16)

    @pl.when(j == 0)
    def _():
        _row_pipeline(parts, tm, project_new, finish_plain)

    @pl.when((j > 0) & (j < n_plain_blocks))
    def _():
        _row_pipeline(parts, tm, project, finish_plain)

    @pl.when(j >= n_plain_blocks)
    def _():
        _row_pipeline(parts, tm, project, finish_gate)


def _gla_inproj(x2d, S, g, sc, sh, w_bf, mult_row, wa1, wa2, ba, tm=1024, tn=1024, parts=4):
    N = x2d.shape[0]
    n_out = w_bf.shape[1]
    kw = wa2.shape[1]
    vec = pl.BlockSpec((None, 1, D), lambda i, j: ((i * tm) // S, 0, 0))
    return pl.pallas_call(
        functools.partial(_gla_inproj_kernel, n_plain_blocks=(2 * D) // tn, parts=parts),
        grid=(N // tm, n_out // tn),
        in_specs=[pl.BlockSpec((tm, D), lambda i, j: (i, 0)),
                  pl.BlockSpec((1, D), lambda i, j: (0, 0)),
                  vec, vec,
                  pl.BlockSpec((D, tn), lambda i, j: (0, j)),
                  pl.BlockSpec((1, tn), lambda i, j: (0, j)),
                  pl.BlockSpec((D, LANES), lambda i, j: (0, 0)),
                  pl.BlockSpec((LANES, kw), lambda i, j: (0, 0)),
                  pl.BlockSpec((1, kw), lambda i, j: (0, 0))],
        out_specs=[pl.BlockSpec((tm, tn), lambda i, j: (i, j)),
                   pl.BlockSpec((tm, kw), lambda i, j: (i, 0))],
        out_shape=[jax.ShapeDtypeStruct((N, n_out), BF16), jax.ShapeDtypeStruct((N, kw), F32)],
        scratch_shapes=[pltpu.VMEM((tm, D), BF16)],
        compiler_params=_cp("arbitrary", "arbitrary"),
        name="gla_inproj",
    )(x2d, g, sc, sh, w_bf, mult_row, wa1, wa2, ba)


def _gla_kernel(q_ref, k_ref, v_ref, r_ref, la_ref, tri_ref, og_ref, o_ref, state_scr, *, NB, SB):
    @pl.when(pl.program_id(1) == 0)
    def _():
        state_scr[...] = jnp.zeros_like(state_scr)

    tri = tri_ref[...]
    ti = lax.broadcasted_iota(jnp.int32, (CHUNK, CHUNK), 0)
    si = lax.broadcasted_iota(jnp.int32, (CHUNK, CHUNK), 1)
    causal = si <= ti
    og = og_ref[...]
    chains = [(nb, h) for nb in range(NB) for h in range(GLA_HEADS)]
    nt = (((1,), (1,)), ((), ()))
    tn = (((0,), (0,)), ((), ()))

    def chunk(c, carry):
        r0 = pl.multiple_of(c * CHUNK, CHUNK)
        rows = pl.ds(r0, CHUNK)
        kcol = [slice(h * GLA_DK, (h + 1) * GLA_DK) for _, h in chains]
        vcol = [slice(h * GLA_DV, (h + 1) * GLA_DV) for _, h in chains]
        las = [la_ref[nb, rows, kc] for (nb, _), kc in zip(chains, kcol)]
        his = [la.astype(BF16) for la in las]
        los = [(la - hi.astype(F32)).astype(BF16) for la, hi in zip(las, his)]
        bs = [jnp.dot(tri, hi, preferred_element_type=F32) + jnp.dot(tri, lo, preferred_element_type=F32)
              for hi, lo in zip(his, los)]
        b_lasts = [b[CHUNK - 1:CHUNK, :] for b in bs]
        qs = [q_ref[nb, rows, kc].astype(F32) for (nb, _), kc in zip(chains, kcol)]
        ks = [k_ref[nb, rows, kc].astype(F32) for (nb, _), kc in zip(chains, kcol)]
        vs = [v_ref[nb, rows, vc] for (nb, _), vc in zip(chains, vcol)]
        q_decs = [(q * jnp.exp(b)).astype(BF16) for q, b in zip(qs, bs)]
        k_invs = [(k * jnp.exp(-b)).astype(BF16) for k, b in zip(ks, bs)]
        k_decs = [(k * jnp.exp(bl - b)).astype(BF16) for k, b, bl in zip(ks, bs, b_lasts)]
        states = [state_scr[nb, h] for nb, h in chains]
        attns = [lax.dot_general(qd, ki, nt, preferred_element_type=F32) for qd, ki in zip(q_decs, k_invs)]
        o_inters = [lax.dot_general(qd, st.astype(BF16), nt, preferred_element_type=F32)
                    for qd, st in zip(q_decs, states)]
        kv_ts = [lax.dot_general(v, kd, tn, preferred_element_type=F32) for v, kd in zip(vs, k_decs)]
        attns = [jnp.where(causal, a, 0.0).astype(BF16) for a in attns]
        o_intras = [jnp.dot(a, v, preferred_element_type=F32) for a, v in zip(attns, vs)]
        for (nb, h), st, bl, kv_t, oi, oe, vc in zip(chains, states, b_lasts, kv_ts, o_intras, o_inters, vcol):
            state_scr[nb, h] = st * jnp.exp(bl) + kv_t
            o = oi + oe
            ms = jnp.mean(o * o, axis=-1, keepdims=True)
            o = o * lax.rsqrt(ms + EPS) * og * r_ref[nb, rows, vc].astype(F32)
            o_ref[nb, rows, vc] = o.astype(o_ref.dtype)
        return carry

    lax.fori_loop(0, SB // CHUNK, chunk, 0)


def _gla(qkvr, la, out_g, nb=2, sb=512):
    B, S, _ = qkvr.shape
    H = GLA_HEADS
    kw = H * GLA_DK
    vw = H * GLA_DV
    tri = (jnp.arange(CHUNK)[None, :] <= jnp.arange(CHUNK)[:, None]).astype(BF16)
    return pl.pallas_call(
        functools.partial(_gla_kernel, NB=nb, SB=sb),
        grid=(B // nb, S // sb),
        in_specs=[pl.BlockSpec((nb, sb, kw), lambda b, s: (b, s, 0)),
                  pl.BlockSpec((nb, sb, kw), lambda b, s: (b, s, 1)),
                  pl.BlockSpec((nb, sb, vw), lambda b, s: (b, s, (2 * kw) // vw)),
                  pl.BlockSpec((nb, sb, vw), lambda b, s: (b, s, (2 * kw) // vw + 1)),
                  pl.BlockSpec((nb, sb, kw), lambda b, s: (b, s, 0)),
                  pl.BlockSpec((CHUNK, CHUNK), lambda b, s: (0, 0)),
                  pl.BlockSpec((1, GLA_DV), lambda b, s: (0, 0))],
        out_specs=pl.BlockSpec((nb, sb, vw), lambda b, s: (b, s, 0)),
        out_shape=jax.ShapeDtypeStruct((B, S, vw), BF16),
        scratch_shapes=[pltpu.VMEM((nb, H, GLA_DV, GLA_DK), F32)],
        compiler_params=_cp("arbitrary", "arbitrary"),
        name="gla",
    )(qkvr, qkvr, qkvr, qkvr, la, tri, out_g)


def _post1_kernel(x_ref, o_ref, wo_ref, g1_ref, ng_ref, sc_ref, sh_ref, rw_ref, ltri_ref,
                  x3_ref, hp_ref, route_ref, cnt_ref, carry_scr, *, sub):
    i = pl.program_id(0)

    @pl.when(i == 0)
    def _():
        carry_scr[...] = jnp.zeros_like(carry_scr)

    y = jnp.dot(o_ref[...], wo_ref[...], preferred_element_type=F32)
    x3 = x_ref[...] + g1_ref[...] * y
    x3_ref[...] = x3
    hb = _modnorm(x3, ng_ref[...], sc_ref[...], sh_ref[...]).astype(BF16)

    half = D // 2
    lo = pltpu.bitcast(hb[:, :half].astype(F32), jnp.uint32)
    hi = pltpu.bitcast(hb[:, half:].astype(F32), jnp.uint32)
    hp_ref[...] = jnp.right_shift(lo, jnp.uint32(16)) | (hi & jnp.uint32(0xFFFF0000))

    tm = hb.shape[0]
    lane = lax.broadcasted_iota(jnp.int32, (tm, LANES), 1)
    lanef = lane.astype(F32)
    logits = jnp.dot(hb, rw_ref[...], preferred_element_type=F32)
    logits = jnp.where(lane < N_EXPERTS, logits, NEG)
    m1 = jnp.max(logits, axis=-1, keepdims=True)
    i1 = jnp.min(jnp.where(logits == m1, lanef, float(LANES)), axis=-1, keepdims=True)
    oh1 = lanef == i1
    rest = jnp.where(oh1, NEG, logits)
    m2 = jnp.max(rest, axis=-1, keepdims=True)
    i2 = jnp.min(jnp.where(rest == m2, lanef, float(LANES)), axis=-1, keepdims=True)
    oh2 = lanef == i2
    e = jnp.exp(m2 - m1)
    gate1 = 1.0 / (1.0 + e)
    gate2 = e / (1.0 + e)

    cnt = jnp.where(oh1, 1.0, 0.0) + jnp.where(oh2, 1.0, 0.0)
    ltri = ltri_ref[...]
    carry = carry_scr[...]
    pres = []
    for s in range(tm // sub):
        c_sub = cnt[s * sub:(s + 1) * sub, :]
        pres.append(jnp.dot(ltri, c_sub.astype(BF16), preferred_element_type=F32) + carry)
        carry = carry + jnp.sum(c_sub, axis=0, keepdims=True)
    pre = jnp.concatenate(pres, axis=0)
    carry_scr[...] = carry
    cnt_ref[...] = carry
    r1 = jnp.sum(jnp.where(oh1, pre, 0.0), axis=-1, keepdims=True)
    r2 = jnp.sum(jnp.where(oh2, pre, 0.0), axis=-1, keepdims=True)
    route = jnp.where(lane == 0, i1, 0.0)
    for idx, val in ((1, i2), (2, r1), (3, r2), (4, gate1), (5, gate2)):
        route = jnp.where(lane == idx, val, route)
    route_ref[...] = route


def _post1(x2d, o2d, S, wo, g1, ng, sc, sh, router_pad, tm=512, sub=256):
    N = x2d.shape[0]
    ltri = (jnp.arange(sub)[None, :] < jnp.arange(sub)[:, None]).astype(BF16)
    vec = pl.BlockSpec((None, 1, D), lambda i: ((i * tm) // S, 0, 0))
    row = pl.BlockSpec((tm, D), lambda i: (i, 0))
    return pl.pallas_call(
        functools.partial(_post1_kernel, sub=sub),
        grid=(N // tm,),
        in_specs=[row, row,
                  pl.BlockSpec((D, D), lambda i: (0, 0)),
                  vec,
                  pl.BlockSpec((1, D), lambda i: (0, 0)),
                  vec, vec,
                  pl.BlockSpec((D, LANES), lambda i: (0, 0)),
                  pl.BlockSpec((sub, sub), lambda i: (0, 0))],
        out_specs=[row,
                   pl.BlockSpec((tm, D // 2), lambda i: (i, 0)),
                   pl.BlockSpec((tm, LANES), lambda i: (i, 0)),
                   pl.BlockSpec((1, LANES), lambda i: (0, 0))],
        out_shape=[jax.ShapeDtypeStruct((N, D), F32),
                   jax.ShapeDtypeStruct((N, D // 2), jnp.uint32),
                   jax.ShapeDtypeStruct((N, LANES), F32),
                   jax.ShapeDtypeStruct((1, LANES), F32)],
        scratch_shapes=[pltpu.VMEM((1, LANES), F32)],
        compiler_params=_cp("arbitrary"),
        name="post1_router",
    )(x2d, o2d, wo, g1, ng, sc, sh, router_pad, ltri)


def _dispatch_kernel(pos_ref, h_ref, xs_in_ref, xs_ref, sem, *, td):
    del xs_in_ref

    def row(r, carry):
        for kk in range(2):
            p = pos_ref[0, 2 * r + kk]
            pltpu.make_async_copy(h_ref.at[pl.ds(r, 1)], xs_ref.at[pl.ds(p, 1)], sem).start()
        return carry

    lax.fori_loop(0, td, row, 0, unroll=8)
    for _ in range(2):
        pltpu.make_async_copy(h_ref, xs_ref.at[pl.ds(0, td)], sem).wait()


def _dispatch(hp, pos, P, td=512):
    N, W = hp.shape
    xs0 = jnp.zeros((P, W), jnp.uint32)
    return pl.pallas_call(
        functools.partial(_dispatch_kernel, td=td),
        grid=(N // td,),
        in_specs=[pl.BlockSpec((None, 1, 2 * td), lambda i: (i, 0, 0), memory_space=pltpu.SMEM),
                  pl.BlockSpec((td, W), lambda i: (i, 0)),
                  pl.BlockSpec(memory_space=pl.ANY)],
        out_specs=pl.BlockSpec(memory_space=pl.ANY),
        out_shape=jax.ShapeDtypeStruct((P, W), jnp.uint32),
        scratch_shapes=[pltpu.SemaphoreType.DMA(())],
        input_output_aliases={2: 0},
        compiler_params=_cp("arbitrary"),
        name="moe_dispatch",
    )(pos.reshape(N // td, 1, 2 * td), hp, xs0)


def _row_scale(v):
    amax = jnp.maximum(jnp.max(jnp.abs(v), axis=-1, keepdims=True), FP8_TINY)
    return FP8_TOP / amax, amax * (1.0 / FP8_TOP)


def _experts_kernel(be_ref, nv_ref, xs_ref, w1_ref, w3_ref, w2_ref, c1_ref, c3_ref, c2_ref, y_ref,
                    xq_scr, xr_scr, *, parts):
    i = pl.program_id(0)
    j = pl.program_id(1)
    valid = i < nv_ref[0]
    tm = y_ref.shape[0]

    @pl.when(valid & (j == 0))
    def _():
        w = xs_ref[...]
        half = D // 2
        lo = pltpu.bitcast(jnp.left_shift(w, jnp.uint32(16)), F32)
        hi = pltpu.bitcast(w & jnp.uint32(0xFFFF0000), F32)
        amax = jnp.maximum(jnp.maximum(jnp.max(jnp.abs(lo), axis=-1, keepdims=True),
                                       jnp.max(jnp.abs(hi), axis=-1, keepdims=True)), FP8_TINY)
        scale = FP8_TOP / amax
        xq_scr[:, :half] = (lo * scale).astype(FP8)
        xq_scr[:, half:] = (hi * scale).astype(FP8)
        xr_scr[...] = jnp.broadcast_to(amax * (1.0 / FP8_TOP), xr_scr.shape)
        y_ref[...] = jnp.zeros_like(y_ref)

    @pl.when(jnp.logical_not(valid) & (j == 0))
    def _():
        y_ref[...] = jnp.zeros_like(y_ref)

    def up(rows):
        xq = xq_scr[rows, :]
        return (jnp.dot(xq, w1_ref[...], preferred_element_type=F32),
                jnp.dot(xq, w3_ref[...], preferred_element_type=F32))

    def down(rows, ab):
        x_unscale = xr_scr[rows, :1]
        a = ab[0] * x_unscale * c1_ref[...]
        b = ab[1] * x_unscale * c3_ref[...]
        g = a * _sigmoid(a) * b
        g_scale, g_unscale = _row_scale(g)
        acc = jnp.dot((g * g_scale).astype(FP8), w2_ref[...], preferred_element_type=F32)
        y_ref[rows, :] += acc * g_unscale * c2_ref[...]

    @pl.when(valid)
    def _():
        _row_pipeline(parts, tm, up, down)


def _quantize_columns(w):
    amax = jnp.maximum(jnp.max(jnp.abs(w), axis=1, keepdims=True), FP8_TINY)
    return (w * (FP8_TOP / amax)).astype(FP8), amax * (1.0 / FP8_TOP)


def _experts(xs, blk_expert, n_valid, w1, w3, w2, tm, tf=1792, parts=4):
    P, W = xs.shape
    E, _, F = w1.shape
    nblk = P // tm
    nj = F // tf
    w1, c1 = _quantize_columns(w1)
    w3, c3 = _quantize_columns(w3)
    w2, c2 = _quantize_columns(w2)

    def _i(i, nv):
        return jnp.minimum(i, nv[0] - 1)

    def _j(i, j, nv):
        return jnp.where(i < nv[0], j, nj - 1)

    grid_spec = pltpu.PrefetchScalarGridSpec(
        num_scalar_prefetch=2,
        grid=(nblk, nj),
        in_specs=[pl.BlockSpec((tm, W), lambda i, j, be, nv: (_i(i, nv), 0)),
                  pl.BlockSpec((None, D, tf), lambda i, j, be, nv: (be[_i(i, nv)], 0, _j(i, j, nv))),
                  pl.BlockSpec((None, D, tf), lambda i, j, be, nv: (be[_i(i, nv)], 0, _j(i, j, nv))),
                  pl.BlockSpec((None, tf, D), lambda i, j, be, nv: (be[_i(i, nv)], _j(i, j, nv), 0)),
                  pl.BlockSpec((None, 1, tf), lambda i, j, be, nv: (be[_i(i, nv)], 0, _j(i, j, nv))),
                  pl.BlockSpec((None, 1, tf), lambda i, j, be, nv: (be[_i(i, nv)], 0, _j(i, j, nv))),
                  pl.BlockSpec((None, 1, D), lambda i, j, be, nv: (be[_i(i, nv)], 0, 0))],
        out_specs=pl.BlockSpec((tm, D), lambda i, j, be, nv: (i, 0)),
        scratch_shapes=[pltpu.VMEM((tm, D), FP8), pltpu.VMEM((tm, LANES), F32)],
    )
    return pl.pallas_call(
        functools.partial(_experts_kernel, parts=parts),
        grid_spec=grid_spec,
        out_shape=jax.ShapeDtypeStruct((P, D), F32),
        compiler_params=_cp("arbitrary", "arbitrary"),
        name="moe_experts",
    )(blk_expert, n_valid, xs, w1, w3, w2, c1, c3, c2)


def _combine_kernel(pos_ref, pos_next_ref, route_ref, x_ref, g2_ref, y_ref, out_ref, ybuf, sems, *, tc):
    i = pl.program_id(0)
    slot = jnp.bitwise_and(i, 1)

    def gather_copy(p, s, kk, r):
        return pltpu.make_async_copy(y_ref.at[pl.ds(p, 1)], ybuf.at[s, kk, pl.ds(r, 1)], sems.at[s])

    def start_gathers(table_ref, s):
        def row(r, carry):
            for kk in range(2):
                gather_copy(table_ref[0, 2 * r + kk], s, kk, r).start()
            return carry

        lax.fori_loop(0, tc, row, 0, unroll=8)

    @pl.when(i == 0)
    def _():
        start_gathers(pos_ref, 0)

    @pl.when(i + 1 < pl.num_programs(0))
    def _():
        start_gathers(pos_next_ref, 1 - slot)

    for kk in range(2):
        pltpu.make_async_copy(y_ref.at[pl.ds(0, tc)], ybuf.at[slot, kk], sems.at[slot]).wait()
    route = route_ref[...]
    moe = route[:, 4:5] * ybuf[slot, 0] + route[:, 5:6] * ybuf[slot, 1]
    out_ref[...] = x_ref[...] + g2_ref[...] * moe


def _combine(y, pos, route, x3, S, g2, tc=256):
    N = x3.shape[0]
    n_steps = N // tc
    pos_blocks = pos.reshape(n_steps, 1, 2 * tc)
    return pl.pallas_call(
        functools.partial(_combine_kernel, tc=tc),
        grid=(n_steps,),
        in_specs=[pl.BlockSpec((None, 1, 2 * tc), lambda i: (i, 0, 0), memory_space=pltpu.SMEM),
                  pl.BlockSpec((None, 1, 2 * tc), lambda i: (jnp.minimum(i + 1, n_steps - 1), 0, 0),
                               memory_space=pltpu.SMEM),
                  pl.BlockSpec((tc, LANES), lambda i: (i, 0)),
                  pl.BlockSpec((tc, D), lambda i: (i, 0)),
                  pl.BlockSpec((None, 1, D), lambda i: ((i * tc) // S, 0, 0)),
                  pl.BlockSpec(memory_space=pl.ANY)],
        out_specs=pl.BlockSpec((tc, D), lambda i: (i, 0)),
        out_shape=jax.ShapeDtypeStruct((N, D), F32),
        scratch_shapes=[pltpu.VMEM((2, 2, tc, D), F32), pltpu.SemaphoreType.DMA((2,))],
        compiler_params=_cp("arbitrary"),
        name="moe_combine",
    )(pos_blocks, pos_blocks, route, x3, g2, y)


def _moe(hp, route, counts, x3, S, g2, w1, w3, w2, tm=1024):
    N = x3.shape[0]
    cnt = counts[0, :N_EXPERTS].astype(jnp.int32)
    nblk_e = (cnt + tm - 1) // tm
    blk_end = jnp.cumsum(nblk_e)
    row_start = (blk_end - nblk_e) * tm
    e_idx = route[:, 0:2].astype(jnp.int32)
    rank = route[:, 2:4].astype(jnp.int32)
    pos = row_start[e_idx] + rank
    P = 2 * N + N_EXPERTS * tm
    nblk = P // tm
    blk_expert = jnp.clip(jnp.searchsorted(blk_end, jnp.arange(nblk, dtype=jnp.int32), side="right"),
                          0, N_EXPERTS - 1).astype(jnp.int32)
    n_valid = blk_end[-1:].astype(jnp.int32)
    xs = _dispatch(hp, pos, P)
    y = _experts(xs, blk_expert, n_valid, w1, w3, w2, tm)
    return _combine(y, pos, route, x3, S, g2)


def kernel(x, c, ada_w, ada_b, norm1_g, norm2_g, da_w_in, da_q_gain, da_k_gain, da_lam_q1, da_lam_k1, da_lam_q2,
           da_lam_k2, da_subln_g, da_w_out, gla_w_in, gla_w_a1, gla_w_a2, gla_b_a, gla_out_g, gla_w_out,
           ffn_w1, ffn_w3, ffn_w2, moe_router, moe_w1, moe_w3, moe_w2):
    B, S, _ = x.shape
    N = B * S
    mod = _adaln(c, ada_w, ada_b)
    mods = [[mod[l, :, k * D:(k + 1) * D].reshape(B, 1, D) for k in range(6)] for l in range(2)]
    x2d = x.reshape(N, D)

    sh1, sc1, gt1, sh2, sc2, gt2 = mods[0]
    lambda_init = 0.8 - 0.6 * math.exp(-0.3 * 0)
    qk_scale = DA_HEAD_DIM ** -0.5 * LOG2E
    gain_row = jnp.concatenate([jnp.tile(da_q_gain[0].reshape(-1) * qk_scale, DA_HEADS),
                                jnp.tile(da_k_gain[0].reshape(-1), DA_HEADS),
                                jnp.ones((D,), F32)]).reshape(1, 3 * D)
    qkv = _da_inproj(x2d, S, norm1_g[0].reshape(1, D), sc1, sh1, da_w_in[0].astype(BF16), gain_row)
    slopes = 2.0 ** (-8.0 * jnp.arange(1, DA_HEADS + 1, dtype=F32) / DA_HEADS)
    o = _da_attention(qkv.reshape(B, S, 3 * D), slopes,
                      da_lam_q1[0].reshape(1, -1), da_lam_k1[0].reshape(1, -1),
                      da_lam_q2[0].reshape(1, -1), da_lam_k2[0].reshape(1, -1),
                      da_subln_g[0].reshape(1, -1), lambda_init)
    x2d = _post0(x2d, o.reshape(N, D), S, da_w_out[0].astype(BF16), gt1, norm2_g[0].reshape(1, D), sc2, sh2, gt2,
                 ffn_w1[0].astype(BF16), ffn_w3[0].astype(BF16), ffn_w2[0].astype(BF16))

    sh1, sc1, gt1, sh2, sc2, gt2 = mods[1]
    kw = GLA_HEADS * GLA_DK
    mult_row = jnp.concatenate([jnp.full((kw,), GLA_DK ** -0.5, F32), jnp.ones((3 * D - kw,), F32)]).reshape(1, -1)
    rank = gla_w_a1.shape[-1]
    wa1 = jnp.zeros((D, LANES), BF16).at[:, :rank].set(gla_w_a1[0].astype(BF16))
    wa2 = jnp.zeros((LANES, kw), BF16).at[:rank, :].set(gla_w_a2[0].astype(BF16))
    qkvr, la = _gla_inproj(x2d, S, norm1_g[1].reshape(1, D), sc1, sh1, gla_w_in[0].astype(BF16), mult_row,
                           wa1, wa2, gla_b_a[0].reshape(1, kw))
    o = _gla(qkvr.reshape(B, S, 3 * D), la.reshape(B, S, kw), gla_out_g[0].reshape(1, -1))
    router_pad = jnp.zeros((D, LANES), BF16).at[:, :N_EXPERTS].set(moe_router[0].astype(BF16))
    x3, hp, route, counts = _post1(x2d, o.reshape(N, D), S, gla_w_out[0].astype(BF16), gt1,
                                   norm2_g[1].reshape(1, D), sc2, sh2, router_pad)
    out = _moe(hp, route, counts, x3, S, gt2,
               moe_w1[0], moe_w3[0], moe_w2[0])
    return out.reshape(B, S, D)
```

```python
import functools
import math

import jax
import jax.numpy as jnp
from jax import lax
from jax.experimental import pallas as pl
from jax.experimental.pallas import tpu as pltpu

F32 = jnp.float32
BF16 = jnp.bfloat16
FP8 = jnp.float8_e4m3fn
FP8_TOP = 256.0
FP8_TINY = 1e-30

D = 1024
EPS = 1e-6
NEG = -1e30
CHUNK = 64
DA_HEADS = 8
DA_HEAD_DIM = 64
GLA_HEADS = 4
GLA_DK = 128
GLA_DV = 256
GLA_TAU = 16.0
N_EXPERTS = 8
LANES = 128
LOG2E = 1.4426950408889634
VT_ROWS = LANES + 16

VMEM_LIMIT = 56 * 1024 * 1024


def _cp(*sem):
    return pltpu.CompilerParams(dimension_semantics=sem, vmem_limit_bytes=VMEM_LIMIT)


def _sigmoid(x):
    return 1.0 / (1.0 + jnp.exp(-x))


def _modnorm(x, g, sc, sh):
    ms = jnp.mean(x * x, axis=-1, keepdims=True)
    return (x * lax.rsqrt(ms + EPS) * g) * (1.0 + sc) + sh


def _row_scale(v):
    amax = jnp.maximum(jnp.max(jnp.abs(v), axis=-1, keepdims=True), FP8_TINY)
    return FP8_TOP / amax, amax * (1.0 / FP8_TOP)


def _quantize_columns(w):
    amax = jnp.maximum(jnp.max(jnp.abs(w), axis=1, keepdims=True), FP8_TINY)
    return (w * (FP8_TOP / amax)).astype(FP8), amax * (1.0 / FP8_TOP)


def _adaln_kernel(c_ref, w_ref, b_ref, o_ref):
    c = c_ref[...]
    ca = (c * _sigmoid(c)).astype(BF16)
    o_ref[...] = jnp.dot(ca, w_ref[...].astype(BF16), preferred_element_type=F32) + b_ref[...]


def _adaln(c, ada_w, ada_b):
    L, _, n6 = ada_w.shape
    B = c.shape[0]
    tn = 1536
    return pl.pallas_call(
        _adaln_kernel,
        grid=(L, n6 // tn),
        in_specs=[pl.BlockSpec((B, D), lambda l, n: (0, 0)),
                  pl.BlockSpec((None, D, tn), lambda l, n: (l, 0, n)),
                  pl.BlockSpec((None, 1, tn), lambda l, n: (l, 0, n))],
        out_specs=pl.BlockSpec((None, B, tn), lambda l, n: (l, 0, n)),
        out_shape=jax.ShapeDtypeStruct((L, B, n6), F32),
        compiler_params=_cp("arbitrary", "arbitrary"),
        name="adaln",
    )(c, ada_w, ada_b.reshape(L, 1, n6))


def _row_pipeline(parts, rows_total, matmul, finish):
    rp = rows_total // parts
    prev = None
    for p in range(parts):
        rows = slice(p * rp, (p + 1) * rp)
        acc = matmul(rows)
        if prev is not None:
            finish(*prev)
        prev = (rows, acc)
    finish(*prev)


def _da_inproj_kernel(x_ref, g_ref, sc_ref, sh_ref, w_ref, gain_ref, gsum_ref, o_ref, h_scr, *, n_qk_blocks, parts):
    j = pl.program_id(1)
    tm, tn = o_ref.shape

    def project_new(rows):
        h = _modnorm(x_ref[rows, :], g_ref[...], sc_ref[...], sh_ref[...]).astype(BF16)
        h_scr[rows, :] = h
        return jnp.dot(h, w_ref[...], preferred_element_type=F32)

    def project(rows):
        return jnp.dot(h_scr[rows, :], w_ref[...], preferred_element_type=F32)

    def finish_qk(rows, acc):
        y2 = (acc * acc).astype(BF16)
        gsum = gsum_ref[...]
        ms = jnp.concatenate(
            [jnp.dot(y2[:, c * 256:(c + 1) * 256], gsum, preferred_element_type=F32) for c in range(tn // 256)],
            axis=1)
        o_ref[rows, :] = (acc * lax.rsqrt(ms + EPS) * gain_ref[...]).astype(BF16)

    def finish_v(rows, acc):
        o_ref[rows, :] = (acc * gain_ref[...]).astype(BF16)

    @pl.when(j == 0)
    def _():
        _row_pipeline(parts, tm, project_new, finish_qk)

    @pl.when((j > 0) & (j < n_qk_blocks))
    def _():
        _row_pipeline(parts, tm, project, finish_qk)

    @pl.when(j >= n_qk_blocks)
    def _():
        _row_pipeline(parts, tm, project, finish_v)


def _da_inproj(x2d, S, g, sc, sh, w_bf, gain_row, tm=1024, tn=1024, parts=4):
    N = x2d.shape[0]
    n_out = w_bf.shape[1]
    r = jnp.arange(256) // DA_HEAD_DIM
    gsum = jnp.where(r[:, None] == r[None, :], 1.0 / DA_HEAD_DIM, 0.0).astype(BF16)
    vec = pl.BlockSpec((None, 1, D), lambda i, j: ((i * tm) // S, 0, 0))
    return pl.pallas_call(
        functools.partial(_da_inproj_kernel, n_qk_blocks=(2 * D) // tn, parts=parts),
        grid=(N // tm, n_out // tn),
        in_specs=[pl.BlockSpec((tm, D), lambda i, j: (i, 0)),
                  pl.BlockSpec((1, D), lambda i, j: (0, 0)),
                  vec, vec,
                  pl.BlockSpec((D, tn), lambda i, j: (0, j)),
                  pl.BlockSpec((1, tn), lambda i, j: (0, j)),
                  pl.BlockSpec((256, 256), lambda i, j: (0, 0))],
        out_specs=pl.BlockSpec((tm, tn), lambda i, j: (i, j)),
        out_shape=jax.ShapeDtypeStruct((N, n_out), BF16),
        scratch_shapes=[pltpu.VMEM((tm, D), BF16)],
        compiler_params=_cp("arbitrary", "arbitrary"),
        name="da_inproj",
    )(x2d, g, sc, sh, w_bf, gain_row, gsum)


def _da_attn_kernel(slope_ref, q_ref, k_ref, v_ref, lq1_ref, lk1_ref, lq2_ref, lk2_ref, subg_ref, o_ref,
                    k1_scr, k2_scr, vt_scr, corr_scr, aug_scr, s1e_scr, s2e_scr, s1o_scr, s2o_scr, qt1_scr, qt2_scr, a1, a2,
                    *, T, SUB, S, lambda_init):
    h = pl.program_id(0)
    slope2 = slope_ref[h] * LOG2E
    lane = lax.broadcasted_iota(jnp.int32, (T, LANES), 1)
    first_half = lane < DA_HEAD_DIM

    @pl.when(pl.program_id(1) == 0)
    def _():
        local = lax.broadcasted_iota(jnp.int32, (T, LANES), 0)
        lo = jnp.bitwise_and(local, 255).astype(F32)
        hi = (local - jnp.bitwise_and(local, 255)).astype(F32)

        def aug(base, vals):
            out = jnp.zeros((T, LANES), F32)
            for off, val in enumerate(vals):
                out = jnp.where(lane == base + off, val, out)
            return out.astype(BF16)

        c_hi = (slope2 + jnp.zeros((T, LANES), F32)).astype(BF16).astype(F32)
        c_mid = (slope2 - c_hi).astype(BF16).astype(F32)
        c_lo = slope2 - c_hi - c_mid
        k_vals = (lo, lo, lo, hi, hi, hi, -c_hi, -c_mid, -c_lo, -c_hi, -c_mid, -c_lo)
        q_vals = (c_hi, c_mid, c_lo, c_hi, c_mid, c_lo, lo, lo, lo, hi, hi, hi)
        aug_scr[0] = aug(DA_HEAD_DIM, k_vals)
        aug_scr[1] = aug(0, k_vals)
        aug_scr[2] = aug(DA_HEAD_DIM, q_vals)
        aug_scr[3] = aug(0, q_vals)
        kj = lax.broadcasted_iota(jnp.int32, (T, T), 0)
        qj = lax.broadcasted_iota(jnp.int32, (T, T), 1)
        visible = jnp.right_shift(kj, 6) <= jnp.right_shift(qj, 6)
        ahead = jnp.maximum(kj - qj, 0).astype(F32)
        corr_scr[...] = jnp.where(visible, -2.0 * slope2 * ahead, NEG)

    k_aug1, k_aug2, q_aug1, q_aug2 = aug_scr[0], aug_scr[1], aug_scr[2], aug_scr[3]
    ones_rows = jnp.where(lax.broadcasted_iota(jnp.int32, (VT_ROWS - LANES, T), 0) == 0, 1.0, 0.0).astype(BF16)

    def prep(j, carry):
        r0 = pl.multiple_of(j * T, T)
        k = k_ref[pl.ds(r0, T), :]
        k1_scr[pl.ds(r0, T), :] = jnp.where(first_half, k, k_aug1)
        k2_scr[pl.ds(r0, T), :] = jnp.where(first_half, k_aug2, k)
        vt_scr[:LANES, pl.ds(r0, T)] = v_ref[pl.ds(r0, T), :].astype(F32).T.astype(BF16)
        vt_scr[LANES:, pl.ds(r0, T)] = ones_rows
        return carry

    lax.fori_loop(0, S // T, prep, 0)

    lam =(jnp.exp(jnp.sum(lq1_ref[...] * lk1_ref[...], axis=-1, keepdims=True))
           - jnp.exp(jnp.sum(lq2_ref[...] * lk2_ref[...], axis=-1, keepdims=True)) + lambda_init)
    k_scrs = (k1_scr, k2_scr)
    s_scrs = ((s1e_scr, s2e_scr), (s1o_scr, s2o_scr))
    qt_scrs = (qt1_scr, qt2_scr)
    accs = (a1, a2)
    n_q = S // T
    TK = T // SUB
    neg = jnp.full((1, T), NEG, F32)

    def load_queries(qi):
        q = q_ref[pl.ds(pl.multiple_of(qi * T, T), T), :]
        for qt_scr, qa in zip(qt_scrs, (jnp.where(first_half, q, q_aug1), jnp.where(first_half, q_aug2, q))):
            qt_scr[...] = qa.astype(F32).T.astype(BF16)

    def block_shift(qi, j):
        return -slope2 * lax.convert_element_type((qi - j) * T, F32)

    def score_chunk(qi, par, j, part, ms, corr):
        k0 = pl.multiple_of(j * T + part * TK, TK)
        out = []
        for mp in range(2):
            s = jnp.dot(k_scrs[mp][pl.ds(k0, TK), :], qt_scrs[mp][...], preferred_element_type=F32)
            if corr is not None:
                s = s + corr[part * TK:(part + 1) * TK, :]
            s_scrs[par][mp][pl.ds(k0, TK), :] = s
            out.append(jnp.maximum(ms[mp], jnp.max(s, axis=0, keepdims=True) + block_shift(qi, j)))
        return tuple(out)

    def weight_chunk(qi, par, j, part, ms):
        k0 = pl.multiple_of(j * T + part * TK, TK)
        vt = vt_scr[:, pl.ds(k0, TK)]
        for mp in range(2):
            p = jnp.exp2(s_scrs[par][mp][pl.ds(k0, TK), :] + (block_shift(qi, j) - ms[mp]))
            accs[mp][...] += jnp.dot(vt, p.astype(BF16), preferred_element_type=F32)

    def finalize(qi):
        acc1 = a1[...]
        acc2 = a2[...]
        o = (acc1[:LANES] / acc1[LANES:LANES + 1] - lam * (acc2[:LANES] / acc2[LANES:LANES + 1])).T
        msq = jnp.mean(o * o, axis=-1, keepdims=True)
        o = o * lax.rsqrt(msq + EPS) * subg_ref[...] * (1.0 - lambda_init)
        o_ref[pl.ds(pl.multiple_of(qi * T, T), T), :] = o.astype(o_ref.dtype)
        a1[...] = jnp.zeros_like(a1)
        a2[...] = jnp.zeros_like(a2)

    a1[...] = jnp.zeros_like(a1)
    a2[...] = jnp.zeros_like(a2)
    load_queries(0)
    ms0 = (neg, neg)
    for part in range(SUB):
        ms0 = score_chunk(0, 0, 0, part, ms0, corr_scr)

    def q_block(qi, par, ms):
        load_queries(qi + 1)

        def chunk(j, ms_next):
            for part in range(SUB):
                ms_next = score_chunk(qi + 1, 1 - par, j, part, ms_next, None)
                weight_chunk(qi, par, j, part, ms)
            return ms_next

        ms_next = lax.fori_loop(0, qi + 1, chunk, (neg, neg))
        for part in range(SUB):
            ms_next = score_chunk(qi + 1, 1 - par, qi + 1, part, ms_next, corr_scr)
        finalize(qi)
        return ms_next

    def q_pair(qq, ms):
        return q_block(2 * qq + 1, 1, q_block(2 * qq, 0, ms))

    ms_even = lax.fori_loop(0, n_q // 2 - 1, q_pair, ms0)
    ms_last = q_block(n_q - 2, 0, ms_even)

    def last_chunk(j, carry):
        for part in range(SUB):
            weight_chunk(n_q - 1, 1, j, part, ms_last)
        return carry

    lax.fori_loop(0, n_q, last_chunk, 0)
    finalize(n_q - 1)


def _da_attention(qkv, slopes, lq1, lk1, lq2, lk2, subg, lambda_init, T=512, sub=2):
    B, S, _ = qkv.shape
    H = DA_HEADS
    vec64 = pl.BlockSpec((1, DA_HEAD_DIM), lambda h, b: (0, 0))
    return pl.pallas_call(
        functools.partial(_da_attn_kernel, T=T, SUB=sub, S=S, lambda_init=lambda_init),
        grid=(H, B),
        in_specs=[pl.BlockSpec(memory_space=pltpu.SMEM),
                  pl.BlockSpec((None, S, LANES), lambda h, b: (b, 0, h)),
                  pl.BlockSpec((None, S, LANES), lambda h, b: (b, 0, H + h)),
                  pl.BlockSpec((None, S, LANES), lambda h, b: (b, 0, 2 * H + h)),
                  vec64, vec64, vec64, vec64,
                  pl.BlockSpec((1, LANES), lambda h, b: (0, 0))],
        out_specs=pl.BlockSpec((None, S, LANES), lambda h, b: (b, 0, h)),
        out_shape=jax.ShapeDtypeStruct((B, S, H * LANES), BF16),
        scratch_shapes=[pltpu.VMEM((S, LANES), BF16), pltpu.VMEM((S, LANES), BF16), pltpu.VMEM((VT_ROWS, S), BF16),
                        pltpu.VMEM((T, T), F32), pltpu.VMEM((4, T, LANES), BF16)]
        + [pltpu.VMEM((S, T), F32) for _ in range(4)] + [
                        pltpu.VMEM((LANES, T), BF16), pltpu.VMEM((LANES, T), BF16),
                        pltpu.VMEM((VT_ROWS, T), F32), pltpu.VMEM((VT_ROWS, T), F32)],
        compiler_params=_cp("arbitrary", "arbitrary"),
        name="da_attention",
    )(slopes, qkv, qkv, qkv, lq1, lk1, lq2, lk2, subg)


def _post0_kernel(x_ref, o_ref, wo_ref, g1_ref, ng_ref, sc_ref, sh_ref, g2_ref, w1_ref, w3_ref, w2_ref,
                  out_ref, h_scr, acc_scr, *, parts):
    j = pl.program_id(1)
    tm = out_ref.shape[0]

    def out_proj(rows):
        return jnp.dot(o_ref[rows, :], wo_ref[...], preferred_element_type=F32)

    def residual_norm(rows, y):
        x1 = x_ref[rows, :] + g1_ref[...] * y
        out_ref[rows, :] = x1
        h_scr[rows, :] = _modnorm(x1, ng_ref[...], sc_ref[...], sh_ref[...]).astype(BF16)
        acc_scr[rows, :] = jnp.zeros((y.shape[0], D), F32)

    @pl.when(j == 0)
    def _():
        _row_pipeline(parts, tm, out_proj, residual_norm)

    def up(rows):
        h = h_scr[rows, :]
        return (jnp.dot(h, w1_ref[...], preferred_element_type=F32),
                jnp.dot(h, w3_ref[...], preferred_element_type=F32))

    def down(rows, ab):
        a, b = ab
        gact = (a * _sigmoid(a) * b).astype(BF16)
        acc_scr[rows, :] += jnp.dot(gact, w2_ref[...], preferred_element_type=F32)

    _row_pipeline(parts, tm, up, down)

    @pl.when(j == pl.num_programs(1) - 1)
    def _():
        out_ref[...] = out_ref[...] + g2_ref[...] * acc_scr[...]


def _post0(x2d, o2d, S, wo, g1, ng, sc, sh, g2, w1, w3, w2, tm=1024, tf=1408, parts=4):
    N = x2d.shape[0]
    F = w1.shape[1]
    vec = pl.BlockSpec((None, 1, D), lambda i, j: ((i * tm) // S, 0, 0))
    row = pl.BlockSpec((tm, D), lambda i, j: (i, 0))
    return pl.pallas_call(
        functools.partial(_post0_kernel, parts=parts),
        grid=(N // tm, F // tf),
        in_specs=[row, row,
                  pl.BlockSpec((D, D), lambda i, j: (0, 0)),
                  vec,
                  pl.BlockSpec((1, D), lambda i, j: (0, 0)),
                  vec, vec, vec,
                  pl.BlockSpec((D, tf), lambda i, j: (0, j)),
                  pl.BlockSpec((D, tf), lambda i, j: (0, j)),
                  pl.BlockSpec((tf, D), lambda i, j: (j, 0))],
        out_specs=row,
        out_shape=jax.ShapeDtypeStruct((N, D), F32),
        scratch_shapes=[pltpu.VMEM((tm, D), BF16), pltpu.VMEM((tm, D), F32)],
        compiler_params=_cp("arbitrary", "arbitrary"),
        name="post0_ffn",
    )(x2d, o2d, wo, g1, ng, sc, sh, g2, w1, w3, w2)


def _gla_inproj_kernel(x_ref, g_ref, sc_ref, sh_ref, w_ref, mult_ref, wa1_ref, wa2_ref, ba_ref,
                       o_ref, la_ref, h_scr, *, n_plain_blocks, parts):
    j = pl.program_id(1)
    tm = o_ref.shape[0]

    def project_new(rows):
        h = _modnorm(x_ref[rows, :], g_ref[...], sc_ref[...], sh_ref[...]).astype(BF16)
        h_scr[rows, :] = h
        low = jnp.dot(h, wa1_ref[...], preferred_element_type=F32).astype(BF16)
        z = jnp.dot(low, wa2_ref[...], preferred_element_type=F32) + ba_ref[...]
        log_sig = jnp.minimum(z, 0.0) - jnp.log(1.0 + jnp.exp(-jnp.abs(z)))
        la_ref[rows, :] = log_sig * (1.0 / GLA_TAU)
        return jnp.dot(h, w_ref[...], preferred_element_type=F32)

    def project(rows):
        return jnp.dot(h_scr[rows, :], w_ref[...], preferred_element_type=F32)

    def finish_plain(rows, acc):
        o_ref[rows, :] = (acc * mult_ref[...]).astype(BF16)

    def finish_gate(rows, acc):
        o_ref[rows, :] = (acc * _sigmoid(acc)).astype(BF16)

    @pl.when(j == 0)
    def _():
        _row_pipeline(parts, tm, project_new, finish_plain)

    @pl.when((j > 0) & (j < n_plain_blocks))
    def _():
        _row_pipeline(parts, tm, project, finish_plain)

    @pl.when(j >= n_plain_blocks)
    def _():
        _row_pipeline(parts, tm, project, finish_gate)


def _gla_inproj(x2d, S, g, sc, sh, w_bf, mult_row, wa1, wa2, ba, tm=1024, tn=1024, parts=4):
    N = x2d.shape[0]
    n_out = w_bf.shape[1]
    kw = wa2.shape[1]
    vec = pl.BlockSpec((None, 1, D), lambda i, j: ((i * tm) // S, 0, 0))
    return pl.pallas_call(
        functools.partial(_gla_inproj_kernel, n_plain_blocks=(2 * D) // tn, parts=parts),
        grid=(N // tm, n_out // tn),
        in_specs=[pl.BlockSpec((tm, D), lambda i, j: (i, 0)),
                  pl.BlockSpec((1, D), lambda i, j: (0, 0)),
                  vec, vec,
                  pl.BlockSpec((D, tn), lambda i, j: (0, j)),
                  pl.BlockSpec((1, tn), lambda i, j: (0, j)),
                  pl.BlockSpec((D, LANES), lambda i, j: (0, 0)),
                  pl.BlockSpec((LANES, kw), lambda i, j: (0, 0)),
                  pl.BlockSpec((1, kw), lambda i, j: (0, 0))],
        out_specs=[pl.BlockSpec((tm, tn), lambda i, j: (i, j)),
                   pl.BlockSpec((tm, kw), lambda i, j: (i, 0))],
        out_shape=[jax.ShapeDtypeStruct((N, n_out), BF16), jax.ShapeDtypeStruct((N, kw), F32)],
        scratch_shapes=[pltpu.VMEM((tm, D), BF16)],
        compiler_params=_cp("arbitrary", "arbitrary"),
        name="gla_inproj",
    )(x2d, g, sc, sh, w_bf, mult_row, wa1, wa2, ba)


def _gla_kernel(q_ref, k_ref, v_ref, r_ref, la_ref, tri_ref, og_ref, o_ref, state_scr, *, NB, SB):
    @pl.when(pl.program_id(1) == 0)
    def _():
        state_scr[...] = jnp.zeros_like(state_scr)

    tri = tri_ref[...]
    ti = lax.broadcasted_iota(jnp.int32, (CHUNK, CHUNK), 0)
    si = lax.broadcasted_iota(jnp.int32, (CHUNK, CHUNK), 1)
    causal = si <= ti
    og = og_ref[...]
    chains = [(nb, h) for nb in range(NB) for h in range(GLA_HEADS)]
    nt = (((1,), (1,)), ((), ()))
    tn = (((0,), (0,)), ((), ()))

    def chunk(c, carry):
        r0 = pl.multiple_of(c * CHUNK, CHUNK)
        rows = pl.ds(r0, CHUNK)
        kcol = [slice(h * GLA_DK, (h + 1) * GLA_DK) for _, h in chains]
        vcol = [slice(h * GLA_DV, (h + 1) * GLA_DV) for _, h in chains]
        las = [la_ref[nb, rows, kc] for (nb, _), kc in zip(chains, kcol)]
        his = [la.astype(BF16) for la in las]
        los = [(la - hi.astype(F32)).astype(BF16) for la, hi in zip(las, his)]
        bs = [jnp.dot(tri, hi, preferred_element_type=F32) + jnp.dot(tri, lo, preferred_element_type=F32)
              for hi, lo in zip(his, los)]
        b_lasts = [b[CHUNK - 1:CHUNK, :] for b in bs]
        qs = [q_ref[nb, rows, kc].astype(F32) for (nb, _), kc in zip(chains, kcol)]
        ks = [k_ref[nb, rows, kc].astype(F32) for (nb, _), kc in zip(chains, kcol)]
        vs = [v_ref[nb, rows, vc] for (nb, _), vc in zip(chains, vcol)]
        q_decs = [(q * jnp.exp(b)).astype(BF16) for q, b in zip(qs, bs)]
        k_invs = [(k * jnp.exp(-b)).astype(BF16) for k, b in zip(ks, bs)]
        k_decs = [(k * jnp.exp(bl - b)).astype(BF16) for k, b, bl in zip(ks, bs, b_lasts)]
        states = [state_scr[nb, h] for nb, h in chains]
        attns = [lax.dot_general(qd, ki, nt, preferred_element_type=F32) for qd, ki in zip(q_decs, k_invs)]
        o_inters = [lax.dot_general(qd, st.astype(BF16), nt, preferred_element_type=F32)
                    for qd, st in zip(q_decs, states)]
        kv_ts = [lax.dot_general(v, kd, tn, preferred_element_type=F32) for v, kd in zip(vs, k_decs)]
        attns = [jnp.where(causal, a, 0.0).astype(BF16) for a in attns]
        o_intras = [jnp.dot(a, v, preferred_element_type=F32) for a, v in zip(attns, vs)]
        for (nb, h), st, bl, kv_t, oi, oe, vc in zip(chains, states, b_lasts, kv_ts, o_intras, o_inters, vcol):
            state_scr[nb, h] = st * jnp.exp(bl) + kv_t
            o = oi + oe
            ms = jnp.mean(o * o, axis=-1, keepdims=True)
            o = o * lax.rsqrt(ms + EPS) * og * r_ref[nb, rows, vc].astype(F32)
            o_ref[nb, rows, vc] = o.astype(o_ref.dtype)
        return carry

    lax.fori_loop(0, SB // CHUNK, chunk, 0)


def _gla(qkvr, la, out_g, nb=4, sb=256):
    B, S, _ = qkvr.shape
    H = GLA_HEADS
    kw = H * GLA_DK
    vw = H * GLA_DV
    tri = (jnp.arange(CHUNK)[None, :] <= jnp.arange(CHUNK)[:, None]).astype(BF16)
    return pl.pallas_call(
        functools.partial(_gla_kernel, NB=nb, SB=sb),
        grid=(B // nb, S // sb),
        in_specs=[pl.BlockSpec((nb, sb, kw), lambda b, s: (b, s, 0)),
                  pl.BlockSpec((nb, sb, kw), lambda b, s: (b, s, 1)),
                  pl.BlockSpec((nb, sb, vw), lambda b, s: (b, s, (2 * kw) // vw)),
                  pl.BlockSpec((nb, sb, vw), lambda b, s: (b, s, (2 * kw) // vw + 1)),
                  pl.BlockSpec((nb, sb, kw), lambda b, s: (b, s, 0)),
                  pl.BlockSpec((CHUNK, CHUNK), lambda b, s: (0, 0)),
                  pl.BlockSpec((1, GLA_DV), lambda b, s: (0, 0))],
        out_specs=pl.BlockSpec((nb, sb, vw), lambda b, s: (b, s, 0)),
        out_shape=jax.ShapeDtypeStruct((B, S, vw), BF16),
        scratch_shapes=[pltpu.VMEM((nb, H, GLA_DV, GLA_DK), F32)],
        compiler_params=_cp("arbitrary", "arbitrary"),
        name="gla",
    )(qkvr, qkvr, qkvr, qkvr, la, tri, out_g)


def _post1_kernel(x_ref, o_ref, wo_ref, g1_ref, ng_ref, sc_ref, sh_ref, rw_ref, ltri_ref,
                  x3_ref, hp_ref, route_ref, cnt_ref, carry_scr, *, sub):
    i = pl.program_id(0)

    @pl.when(i == 0)
    def _():
        carry_scr[...] = jnp.zeros_like(carry_scr)

    y = jnp.dot(o_ref[...], wo_ref[...], preferred_element_type=F32)
    x3 = x_ref[...] + g1_ref[...] * y
    x3_ref[...] = x3
    hb = _modnorm(x3, ng_ref[...], sc_ref[...], sh_ref[...]).astype(BF16)

    half = D // 2
    lo = pltpu.bitcast(hb[:, :half].astype(F32), jnp.uint32)
    hi = pltpu.bitcast(hb[:, half:].astype(F32), jnp.uint32)
    hp_ref[...] = jnp.right_shift(lo, jnp.uint32(16)) | (hi & jnp.uint32(0xFFFF0000))

    tm = hb.shape[0]
    lane = lax.broadcasted_iota(jnp.int32, (tm, LANES), 1)
    lanef = lane.astype(F32)
    logits = jnp.dot(hb, rw_ref[...], preferred_element_type=F32)
    logits = jnp.where(lane < N_EXPERTS, logits, NEG)
    m1 = jnp.max(logits, axis=-1, keepdims=True)
    i1 = jnp.min(jnp.where(logits == m1, lanef, float(LANES)), axis=-1, keepdims=True)
    oh1 = lanef == i1
    rest = jnp.where(oh1, NEG, logits)
    m2 = jnp.max(rest, axis=-1, keepdims=True)
    i2 = jnp.min(jnp.where(rest == m2, lanef, float(LANES)), axis=-1, keepdims=True)
    oh2 = lanef == i2
    e = jnp.exp(m2 - m1)
    gate1 = 1.0 / (1.0 + e)
    gate2 = e / (1.0 + e)

    cnt = jnp.where(oh1, 1.0, 0.0) + jnp.where(oh2, 1.0, 0.0)
    ltri = ltri_ref[...]
    carry = carry_scr[...]
    pres = []
    for s in range(tm // sub):
        c_sub = cnt[s * sub:(s + 1) * sub, :]
        pres.append(jnp.dot(ltri, c_sub.astype(BF16), preferred_element_type=F32) + carry)
        carry = carry + jnp.sum(c_sub, axis=0, keepdims=True)
    pre = jnp.concatenate(pres, axis=0)
    carry_scr[...] = carry
    cnt_ref[...] = carry
    r1 = jnp.sum(jnp.where(oh1, pre, 0.0), axis=-1, keepdims=True)
    r2 = jnp.sum(jnp.where(oh2, pre, 0.0), axis=-1, keepdims=True)
    route = jnp.where(lane == 0, i1, 0.0)
    for idx, val in ((1, i2), (2, r1), (3, r2), (4, gate1), (5, gate2)):
        route = jnp.where(lane == idx, val, route)
    route_ref[...] = route


def _post1(x2d, o2d, S, wo, g1, ng, sc, sh, router_pad, tm=512, sub=256):
    N = x2d.shape[0]
    ltri = (jnp.arange(sub)[None, :] < jnp.arange(sub)[:, None]).astype(BF16)
    vec = pl.BlockSpec((None, 1, D), lambda i: ((i * tm) // S, 0, 0))
    row = pl.BlockSpec((tm, D), lambda i: (i, 0))
    return pl.pallas_call(
        functools.partial(_post1_kernel, sub=sub),
        grid=(N // tm,),
        in_specs=[row, row,
                  pl.BlockSpec((D, D), lambda i: (0, 0)),
                  vec,
                  pl.BlockSpec((1, D), lambda i: (0, 0)),
                  vec, vec,
                  pl.BlockSpec((D, LANES), lambda i: (0, 0)),
                  pl.BlockSpec((sub, sub), lambda i: (0, 0))],
        out_specs=[row,
                   pl.BlockSpec((tm, D // 2), lambda i: (i, 0)),
                   pl.BlockSpec((tm, LANES), lambda i: (i, 0)),
                   pl.BlockSpec((1, LANES), lambda i: (0, 0))],
        out_shape=[jax.ShapeDtypeStruct((N, D), F32),
                   jax.ShapeDtypeStruct((N, D // 2), jnp.uint32),
                   jax.ShapeDtypeStruct((N, LANES), F32),
                   jax.ShapeDtypeStruct((1, LANES), F32)],
        scratch_shapes=[pltpu.VMEM((1, LANES), F32)],
        compiler_params=_cp("arbitrary"),
        name="post1_router",
    )(x2d, o2d, wo, g1, ng, sc, sh, router_pad, ltri)


def _dispatch_kernel(pos_ref, h_ref, xs_in_ref, xs_ref, sem, *, td):
    del xs_in_ref

    def row(r, carry):
        for kk in range(2):
            p = pos_ref[0, 2 * r + kk]
            pltpu.make_async_copy(h_ref.at[pl.ds(r, 1)], xs_ref.at[pl.ds(p, 1)], sem).start()
        return carry

    lax.fori_loop(0, td, row, 0, unroll=8)
    for _ in range(2):
        pltpu.make_async_copy(h_ref, xs_ref.at[pl.ds(0, td)], sem).wait()


def _dispatch(hp, pos, P, td=512):
    N, W = hp.shape
    xs0 = jnp.zeros((P, W), jnp.uint32)
    return pl.pallas_call(
        functools.partial(_dispatch_kernel, td=td),
        grid=(N // td,),
        in_specs=[pl.BlockSpec((None, 1, 2 * td), lambda i: (i, 0, 0), memory_space=pltpu.SMEM),
                  pl.BlockSpec((td, W), lambda i: (i, 0)),
                  pl.BlockSpec(memory_space=pl.ANY)],
        out_specs=pl.BlockSpec(memory_space=pl.ANY),
        out_shape=jax.ShapeDtypeStruct((P, W), jnp.uint32),
        scratch_shapes=[pltpu.SemaphoreType.DMA(())],
        input_output_aliases={2: 0},
        compiler_params=_cp("arbitrary"),
        name="moe_dispatch",
    )(pos.reshape(N // td, 1, 2 * td), hp, xs0)


def _experts_kernel(be_ref, nv_ref, xs_ref, w1_ref, w3_ref, w2_ref, c1_ref, c3_ref, c2_ref, y_ref,
                    xq_scr, xr_scr, *, parts):
    i = pl.program_id(0)
    j = pl.program_id(1)
    valid = i < nv_ref[0]
    tm = y_ref.shape[0]

    @pl.when(valid & (j == 0))
    def _():
        w = xs_ref[...]
        half = D // 2
        lo = pltpu.bitcast(jnp.left_shift(w, jnp.uint32(16)), F32)
        hi = pltpu.bitcast(w & jnp.uint32(0xFFFF0000), F32)
        amax = jnp.maximum(jnp.maximum(jnp.max(jnp.abs(lo), axis=-1, keepdims=True),
                                       jnp.max(jnp.abs(hi), axis=-1, keepdims=True)), FP8_TINY)
        scale = FP8_TOP / amax
        xq_scr[:, :half] = (lo * scale).astype(FP8)
        xq_scr[:, half:] = (hi * scale).astype(FP8)
        xr_scr[...] = jnp.broadcast_to(amax * (1.0 / FP8_TOP), xr_scr.shape)
        y_ref[...] = jnp.zeros_like(y_ref)

    @pl.when(jnp.logical_not(valid) & (j == 0))
    def _():
        y_ref[...] = jnp.zeros_like(y_ref)

    def up(rows):
        xq = xq_scr[rows, :]
        return (jnp.dot(xq, w1_ref[...], preferred_element_type=F32),
                jnp.dot(xq, w3_ref[...], preferred_element_type=F32))

    def down(rows, ab):
        x_unscale = xr_scr[rows, :1]
        a = ab[0] * x_unscale * c1_ref[...]
        b = ab[1] * x_unscale * c3_ref[...]
        g = a * _sigmoid(a) * b
        g_scale, g_unscale = _row_scale(g)
        acc = jnp.dot((g * g_scale).astype(FP8), w2_ref[...], preferred_element_type=F32)
        y_ref[rows, :] += acc * g_unscale * c2_ref[...]

    @pl.when(valid)
    def _():
        _row_pipeline(parts, tm, up, down)


def _experts(xs, blk_expert, n_valid, w1, w3, w2, tm, tf=1792, parts=4):
    P, W = xs.shape
    E, _, F = w1.shape
    nblk = P // tm
    nj = F // tf
    w1, c1 = _quantize_columns(w1)
    w3, c3 = _quantize_columns(w3)
    w2, c2 = _quantize_columns(w2)

    def _i(i, nv):
        return jnp.minimum(i, nv[0] - 1)

    def _j(i, j, nv):
        return jnp.where(i < nv[0], j, nj - 1)

    grid_spec = pltpu.PrefetchScalarGridSpec(
        num_scalar_prefetch=2,
        grid=(nblk, nj),
        in_specs=[pl.BlockSpec((tm, W), lambda i, j, be, nv: (_i(i, nv), 0)),
                  pl.BlockSpec((None, D, tf), lambda i, j, be, nv: (be[_i(i, nv)], 0, _j(i, j, nv))),
                  pl.BlockSpec((None, D, tf), lambda i, j, be, nv: (be[_i(i, nv)], 0, _j(i, j, nv))),
                  pl.BlockSpec((None, tf, D), lambda i, j, be, nv: (be[_i(i, nv)], _j(i, j, nv), 0)),
                  pl.BlockSpec((None, 1, tf), lambda i, j, be, nv: (be[_i(i, nv)], 0, _j(i, j, nv))),
                  pl.BlockSpec((None, 1, tf), lambda i, j, be, nv: (be[_i(i, nv)], 0, _j(i, j, nv))),
                  pl.BlockSpec((None, 1, D), lambda i, j, be, nv: (be[_i(i, nv)], 0, 0))],
        out_specs=pl.BlockSpec((tm, D), lambda i, j, be, nv: (i, 0)),
        scratch_shapes=[pltpu.VMEM((tm, D), FP8), pltpu.VMEM((tm, LANES), F32)],
    )
    return pl.pallas_call(
        functools.partial(_experts_kernel, parts=parts),
        grid_spec=grid_spec,
        out_shape=jax.ShapeDtypeStruct((P, D), F32),
        compiler_params=_cp("arbitrary", "arbitrary"),
        name="moe_experts",
    )(blk_expert, n_valid, xs, w1, w3, w2, c1, c3, c2)


def _combine_kernel(pos_ref, pos_next_ref, route_ref, x_ref, g2_ref, y_ref, out_ref, ybuf, sems, *, tc):
    i = pl.program_id(0)
    slot = jnp.bitwise_and(i, 1)

    def gather_copy(p, s, kk, r):
        return pltpu.make_async_copy(y_ref.at[pl.ds(p, 1)], ybuf.at[s, kk, pl.ds(r, 1)], sems.at[s])

    def start_gathers(table_ref, s):
        def row(r, carry):
            for kk in range(2):
                gather_copy(table_ref[0, 2 * r + kk], s, kk, r).start()
            return carry

        lax.fori_loop(0, tc, row, 0, unroll=8)

    @pl.when(i == 0)
    def _():
        start_gathers(pos_ref, 0)

    @pl.when(i + 1 < pl.num_programs(0))
    def _():
        start_gathers(pos_next_ref, 1 - slot)

    for kk in range(2):
        pltpu.make_async_copy(y_ref.at[pl.ds(0, tc)], ybuf.at[slot, kk], sems.at[slot]).wait()
    route = route_ref[...]
    moe = route[:, 4:5] * ybuf[slot, 0] + route[:, 5:6] * ybuf[slot, 1]
    out_ref[...] = x_ref[...] + g2_ref[...] * moe


def _combine(y, pos, route, x3, S, g2, tc=256):
    N = x3.shape[0]
    n_steps = N // tc
    pos_blocks = pos.reshape(n_steps, 1, 2 * tc)
    return pl.pallas_call(
        functools.partial(_combine_kernel, tc=tc),
        grid=(n_steps,),
        in_specs=[pl.BlockSpec((None, 1, 2 * tc), lambda i: (i, 0, 0), memory_space=pltpu.SMEM),
                  pl.BlockSpec((None, 1, 2 * tc), lambda i: (jnp.minimum(i + 1, n_steps - 1), 0, 0),
                               memory_space=pltpu.SMEM),
                  pl.BlockSpec((tc, LANES), lambda i: (i, 0)),
                  pl.BlockSpec((tc, D), lambda i: (i, 0)),
                  pl.BlockSpec((None, 1, D), lambda i: ((i * tc) // S, 0, 0)),
                  pl.BlockSpec(memory_space=pl.ANY)],
        out_specs=pl.BlockSpec((tc, D), lambda i: (i, 0)),
        out_shape=jax.ShapeDtypeStruct((N, D), F32),
        scratch_shapes=[pltpu.VMEM((2, 2, tc, D), F32), pltpu.SemaphoreType.DMA((2,))],
        compiler_params=_cp("arbitrary"),
        name="moe_combine",
    )(pos_blocks, pos_blocks, route, x3, g2, y)


def _moe(hp, route, counts, x3, S, g2, w1, w3, w2, tm=1024):
    N = x3.shape[0]
    cnt = counts[0, :N_EXPERTS].astype(jnp.int32)
    nblk_e = (cnt + tm - 1) // tm
    blk_end = jnp.cumsum(nblk_e)
    row_start = (blk_end - nblk_e) * tm
    e_idx = route[:, 0:2].astype(jnp.int32)
    rank = route[:, 2:4].astype(jnp.int32)
    pos = row_start[e_idx] + rank
    P = 2 * N + N_EXPERTS * tm
    nblk = P // tm
    blk_expert = jnp.clip(jnp.searchsorted(blk_end, jnp.arange(nblk, dtype=jnp.int32), side="right"),
                          0, N_EXPERTS - 1).astype(jnp.int32)
    n_valid = blk_end[-1:].astype(jnp.int32)
    xs = _dispatch(hp, pos, P)
    y = _experts(xs, blk_expert, n_valid, w1, w3, w2, tm)
    return _combine(y, pos, route, x3, S, g2)


def kernel(x, c, ada_w, ada_b, norm1_g, norm2_g, da_w_in, da_q_gain, da_k_gain, da_lam_q1, da_lam_k1, da_lam_q2,
           da_lam_k2, da_subln_g, da_w_out, gla_w_in, gla_w_a1, gla_w_a2, gla_b_a, gla_out_g, gla_w_out,
           ffn_w1, ffn_w3, ffn_w2, moe_router, moe_w1, moe_w3, moe_w2):
    B, S, _ = x.shape
    N = B * S
    mod = _adaln(c, ada_w, ada_b)
    mods = [[mod[l, :, k * D:(k + 1) * D].reshape(B, 1, D) for k in range(6)] for l in range(2)]
    x2d = x.reshape(N, D)

    sh1, sc1, gt1, sh2, sc2, gt2 = mods[0]
    lambda_init = 0.8 - 0.6 * math.exp(-0.3 * 0)
    qk_scale = DA_HEAD_DIM ** -0.5 * LOG2E
    gain_row = jnp.concatenate([jnp.tile(da_q_gain[0].reshape(-1) * qk_scale, DA_HEADS),
                                jnp.tile(da_k_gain[0].reshape(-1), DA_HEADS),
                                jnp.ones((D,), F32)]).reshape(1, 3 * D)
    qkv = _da_inproj(x2d, S, norm1_g[0].reshape(1, D), sc1, sh1, da_w_in[0].astype(BF16), gain_row)
    slopes = 2.0 ** (-8.0 * jnp.arange(1, DA_HEADS + 1, dtype=F32) / DA_HEADS)
    o = _da_attention(qkv.reshape(B, S, 3 * D), slopes,
                      da_lam_q1[0].reshape(1, -1), da_lam_k1[0].reshape(1, -1),
                      da_lam_q2[0].reshape(1, -1), da_lam_k2[0].reshape(1, -1),
                      da_subln_g[0].reshape(1, -1), lambda_init)
    x2d = _post0(x2d, o.reshape(N, D), S, da_w_out[0].astype(BF16), gt1, norm2_g[0].reshape(1, D), sc2, sh2, gt2,
                 ffn_w1[0].astype(BF16), ffn_w3[0].astype(BF16), ffn_w2[0].astype(BF16))

    sh1, sc1, gt1, sh2, sc2, gt2 = mods[1]
    kw = GLA_HEADS * GLA_DK
    mult_row = jnp.concatenate([jnp.full((kw,), GLA_DK ** -0.5, F32), jnp.ones((3 * D - kw,), F32)]).reshape(1, -1)
    rank = gla_w_a1.shape[-1]
    wa1 = jnp.zeros((D, LANES), BF16).at[:, :rank].set(gla_w_a1[0].astype(BF16))
    wa2 = jnp.zeros((LANES, kw), BF16).at[:rank, :].set(gla_w_a2[0].astype(BF16))
    qkvr, la = _gla_inproj(x2d, S, norm1_g[1].reshape(1, D), sc1, sh1, gla_w_in[0].astype(BF16), mult_row,
                           wa1, wa2, gla_b_a[0].reshape(1, kw))
    o = _gla(qkvr.reshape(B, S, 3 * D), la.reshape(B, S, kw), gla_out_g[0].reshape(1, -1))
    router_pad = jnp.zeros((D, LANES), BF16).at[:, :N_EXPERTS].set(moe_router[0].astype(BF16))
    x3, hp, route, counts = _post1(x2d, o.reshape(N, D), S, gla_w_out[0].astype(BF16), gt1,
                                   norm2_g[1].reshape(1, D), sc2, sh2, router_pad)
    out = _moe(hp, route, counts, x3, S, gt2,
               moe_w1[0], moe_w3[0], moe_w2[0])
    return out.reshape(B, S, D)
```

```python
import functools
import math

import jax
import jax.numpy as jnp
from jax import lax
from jax.experimental import pallas as pl
from jax.experimental.pallas import tpu as pltpu

F32 = jnp.float32
BF16 = jnp.bfloat16
FP8 = jnp.float8_e4m3fn
FP8_TOP = 256.0
FP8_TINY = 1e-30

D = 1024
EPS = 1e-6
NEG = -1e30
CHUNK = 64
DA_HEADS = 8
DA_HEAD_DIM = 64
GLA_HEADS = 4
GLA_DK = 128
GLA_DV = 256
GLA_TAU = 16.0
N_EXPERTS = 8
LANES = 128
LOG2E = 1.4426950408889634
VT_ROWS = LANES + 16

VMEM_LIMIT = 56 * 1024 * 1024


def _cp(*sem):
    return pltpu.CompilerParams(dimension_semantics=sem, vmem_limit_bytes=VMEM_LIMIT)


def _sigmoid(x):
    return 1.0 / (1.0 + jnp.exp(-x))


def _modnorm(x, g, sc, sh):
    ms = jnp.mean(x * x, axis=-1, keepdims=True)
    return (x * lax.rsqrt(ms + EPS) * g) * (1.0 + sc) + sh


def _row_scale(v):
    amax = jnp.maximum(jnp.max(jnp.abs(v), axis=-1, keepdims=True), FP8_TINY)
    return FP8_TOP / amax, amax * (1.0 / FP8_TOP)


def _quantize_columns(w):
    amax = jnp.maximum(jnp.max(jnp.abs(w), axis=1, keepdims=True), FP8_TINY)
    return (w * (FP8_TOP / amax)).astype(FP8), amax * (1.0 / FP8_TOP)


def _adaln_kernel(c_ref, w_ref, b_ref, o_ref):
    c = c_ref[...]
    ca = (c * _sigmoid(c)).astype(BF16)
    o_ref[...] = jnp.dot(ca, w_ref[...].astype(BF16), preferred_element_type=F32) + b_ref[...]


def _adaln(c, ada_w, ada_b):
    L, _, n6 = ada_w.shape
    B = c.shape[0]
    tn = 1536
    return pl.pallas_call(
        _adaln_kernel,
        grid=(L, n6 // tn),
        in_specs=[pl.BlockSpec((B, D), lambda l, n: (0, 0)),
                  pl.BlockSpec((None, D, tn), lambda l, n: (l, 0, n)),
                  pl.BlockSpec((None, 1, tn), lambda l, n: (l, 0, n))],
        out_specs=pl.BlockSpec((None, B, tn), lambda l, n: (l, 0, n)),
        out_shape=jax.ShapeDtypeStruct((L, B, n6), F32),
        compiler_params=_cp("arbitrary", "arbitrary"),
        name="adaln",
    )(c, ada_w, ada_b.reshape(L, 1, n6))


def _pipeline(items, matmul, finish):
    prev = None
    for item in items:
        acc = matmul(item)
        if prev is not None:
            finish(*prev)
        prev = (item, acc)
    finish(*prev)


def _row_pipeline(parts, rows_total, matmul, finish):
    rp = rows_total // parts
    _pipeline([slice(p * rp, (p + 1) * rp) for p in range(parts)], matmul, finish)


def _da_inproj_kernel(x_ref, g_ref, sc_ref, sh_ref, w_ref, gain_ref, gsum_ref, o_ref, h_scr,
                      *, tn, n_qk_groups, parts):
    tm, n_out = o_ref.shape
    rp = tm // parts
    items = [(slice(p * rp, (p + 1) * rp), cg) for p in range(parts) for cg in range(n_out // tn)]

    def project(item):
        rows, cg = item
        if cg == 0:
            h = _modnorm(x_ref[rows, :], g_ref[...], sc_ref[...], sh_ref[...]).astype(BF16)
            h_scr[rows, :] = h
        else:
            h = h_scr[rows, :]
        return jnp.dot(h, w_ref[:, cg * tn:(cg + 1) * tn], preferred_element_type=F32)

    def finish(item, acc):
        rows, cg = item
        cols = slice(cg * tn, (cg + 1) * tn)
        if cg < n_qk_groups:
            y2 = (acc * acc).astype(BF16)
            gsum = gsum_ref[...]
            ms = jnp.concatenate(
                [jnp.dot(y2[:, c * 256:(c + 1) * 256], gsum, preferred_element_type=F32) for c in range(tn // 256)],
                axis=1)
            acc = acc * lax.rsqrt(ms + EPS)
        o_ref[rows, cols] = (acc * gain_ref[:, cols]).astype(BF16)

    _pipeline(items, project, finish)


def _da_inproj(x2d, S, g, sc, sh, w_bf, gain_row, tm=1024, tn=1024, parts=4):
    N = x2d.shape[0]
    n_out = w_bf.shape[1]
    r = jnp.arange(256) // DA_HEAD_DIM
    gsum = jnp.where(r[:, None] == r[None, :], 1.0 / DA_HEAD_DIM, 0.0).astype(BF16)
    vec = pl.BlockSpec((None, 1, D), lambda i: ((i * tm) // S, 0, 0))
    return pl.pallas_call(
        functools.partial(_da_inproj_kernel, tn=tn, n_qk_groups=(2 * D) // tn, parts=parts),
        grid=(N // tm,),
        in_specs=[pl.BlockSpec((tm, D), lambda i: (i, 0)),
                  pl.BlockSpec((1, D), lambda i: (0, 0)),
                  vec, vec,
                  pl.BlockSpec((D, n_out), lambda i: (0, 0)),
                  pl.BlockSpec((1, n_out), lambda i: (0, 0)),
                  pl.BlockSpec((256, 256), lambda i: (0, 0))],
        out_specs=pl.BlockSpec((tm, n_out), lambda i: (i, 0)),
        out_shape=jax.ShapeDtypeStruct((N, n_out), BF16),
        scratch_shapes=[pltpu.VMEM((tm, D), BF16)],
        compiler_params=_cp("arbitrary"),
        name="da_inproj",
    )(x2d, g, sc, sh, w_bf, gain_row, gsum)


def _da_attn_kernel(slope_ref, q_ref, k_ref, v_ref, lq1_ref, lk1_ref, lq2_ref, lk2_ref, subg_ref, o_ref,
                    k1_scr, k2_scr, vt_scr, corr_scr, aug_scr, s1e_scr, s2e_scr, s1o_scr, s2o_scr, qt1_scr, qt2_scr, a1, a2,
                    *, T, SUB, S, lambda_init):
    h = pl.program_id(0)
    slope2 = slope_ref[h] * LOG2E
    lane = lax.broadcasted_iota(jnp.int32, (T, LANES), 1)
    first_half = lane < DA_HEAD_DIM

    @pl.when(pl.program_id(1) == 0)
    def _():
        local = lax.broadcasted_iota(jnp.int32, (T, LANES), 0)
        lo = jnp.bitwise_and(local, 255).astype(F32)
        hi = (local - jnp.bitwise_and(local, 255)).astype(F32)

        def aug(base, vals):
            out = jnp.zeros((T, LANES), F32)
            for off, val in enumerate(vals):
                out = jnp.where(lane == base + off, val, out)
            return out.astype(BF16)

        c_hi = (slope2 + jnp.zeros((T, LANES), F32)).astype(BF16).astype(F32)
        c_mid = (slope2 - c_hi).astype(BF16).astype(F32)
        c_lo = slope2 - c_hi - c_mid
        k_vals = (lo, lo, lo, hi, hi, hi, -c_hi, -c_mid, -c_lo, -c_hi, -c_mid, -c_lo)
        q_vals = (c_hi, c_mid, c_lo, c_hi, c_mid, c_lo, lo, lo, lo, hi, hi, hi)
        aug_scr[0] = aug(DA_HEAD_DIM, k_vals)
        aug_scr[1] = aug(0, k_vals)
        aug_scr[2] = aug(DA_HEAD_DIM, q_vals)
        aug_scr[3] = aug(0, q_vals)
        kj = lax.broadcasted_iota(jnp.int32, (T, T), 0)
        qj = lax.broadcasted_iota(jnp.int32, (T, T), 1)
        visible = jnp.right_shift(kj, 6) <= jnp.right_shift(qj, 6)
        ahead = jnp.maximum(kj - qj, 0).astype(F32)
        corr_scr[...] = jnp.where(visible, -2.0 * slope2 * ahead, NEG)

    k_aug1, k_aug2, q_aug1, q_aug2 = aug_scr[0], aug_scr[1], aug_scr[2], aug_scr[3]
    ones_rows = jnp.where(lax.broadcasted_iota(jnp.int32, (VT_ROWS - LANES, T), 0) == 0, 1.0, 0.0).astype(BF16)

    def prep(j, carry):
        r0 = pl.multiple_of(j * T, T)
        k = k_ref[pl.ds(r0, T), :]
        k1_scr[pl.ds(r0, T), :] = jnp.where(first_half, k, k_aug1)
        k2_scr[pl.ds(r0, T), :] = jnp.where(first_half, k_aug2, k)
        vt_scr[:LANES, pl.ds(r0, T)] = v_ref[pl.ds(r0, T), :].astype(F32).T.astype(BF16)
        vt_scr[LANES:, pl.ds(r0, T)] = ones_rows
        return carry

    lax.fori_loop(0, S // T, prep, 0)

    lam =(jnp.exp(jnp.sum(lq1_ref[...] * lk1_ref[...], axis=-1, keepdims=True))
           - jnp.exp(jnp.sum(lq2_ref[...] * lk2_ref[...], axis=-1, keepdims=True)) + lambda_init)
    k_scrs = (k1_scr, k2_scr)
    s_scrs = ((s1e_scr, s2e_scr), (s1o_scr, s2o_scr))
    qt_scrs = (qt1_scr, qt2_scr)
    accs = (a1, a2)
    n_q = S // T
    TK = T // SUB
    neg = jnp.full((1, T), NEG, F32)

    def load_queries(qi):
        q = q_ref[pl.ds(pl.multiple_of(qi * T, T), T), :]
        for qt_scr, qa in zip(qt_scrs, (jnp.where(first_half, q, q_aug1), jnp.where(first_half, q_aug2, q))):
            qt_scr[...] = qa.astype(F32).T.astype(BF16)

    def block_shift(qi, j):
        return -slope2 * lax.convert_element_type((qi - j) * T, F32)

    def score_chunk(qi, par, j, part, ms, corr):
        k0 = pl.multiple_of(j * T + part * TK, TK)
        out = []
        for mp in range(2):
            s = jnp.dot(k_scrs[mp][pl.ds(k0, TK), :], qt_scrs[mp][...], preferred_element_type=F32)
            if corr is not None:
                s = s + corr[part * TK:(part + 1) * TK, :]
            s_scrs[par][mp][pl.ds(k0, TK), :] = s
            out.append(jnp.maximum(ms[mp], jnp.max(s, axis=0, keepdims=True) + block_shift(qi, j)))
        return tuple(out)

    def weight_chunk(qi, par, j, part, ms):
        k0 = pl.multiple_of(j * T + part * TK, TK)
        vt = vt_scr[:, pl.ds(k0, TK)]
        for mp in range(2):
            p = jnp.exp2(s_scrs[par][mp][pl.ds(k0, TK), :] + (block_shift(qi, j) - ms[mp]))
            accs[mp][...] += jnp.dot(vt, p.astype(BF16), preferred_element_type=F32)

    def finalize(qi):
        acc1 = a1[...]
        acc2 = a2[...]
        o = (acc1[:LANES] / acc1[LANES:LANES + 1] - lam * (acc2[:LANES] / acc2[LANES:LANES + 1])).T
        msq = jnp.mean(o * o, axis=-1, keepdims=True)
        o = o * lax.rsqrt(msq + EPS) * subg_ref[...] * (1.0 - lambda_init)
        o_ref[pl.ds(pl.multiple_of(qi * T, T), T), :] = o.astype(o_ref.dtype)
        a1[...] = jnp.zeros_like(a1)
        a2[...] = jnp.zeros_like(a2)

    a1[...] = jnp.zeros_like(a1)
    a2[...] = jnp.zeros_like(a2)
    load_queries(0)
    ms0 = (neg, neg)
    for part in range(SUB):
        ms0 = score_chunk(0, 0, 0, part, ms0, corr_scr)

    def q_block(qi, par, ms):
        load_queries(qi + 1)

        def chunk(j, ms_next):
            for part in range(SUB):
                ms_next = score_chunk(qi + 1, 1 - par, j, part, ms_next, None)
                weight_chunk(qi, par, j, part, ms)
            return ms_next

        ms_next = lax.fori_loop(0, qi + 1, chunk, (neg, neg))
        for part in range(SUB):
            ms_next = score_chunk(qi + 1, 1 - par, qi + 1, part, ms_next, corr_scr)
        finalize(qi)
        return ms_next

    def q_pair(qq, ms):
        return q_block(2 * qq + 1, 1, q_block(2 * qq, 0, ms))

    ms_even = lax.fori_loop(0, n_q // 2 - 1, q_pair, ms0)
    ms_last = q_block(n_q - 2, 0, ms_even)

    def last_chunk(j, carry):
        for part in range(SUB):
            weight_chunk(n_q - 1, 1, j, part, ms_last)
        return carry

    lax.fori_loop(0, n_q, last_chunk, 0)
    finalize(n_q - 1)


def _da_attention(qkv, slopes, lq1, lk1, lq2, lk2, subg, lambda_init, T=512, sub=2):
    B, S, _ = qkv.shape
    H = DA_HEADS
    vec64 = pl.BlockSpec((1, DA_HEAD_DIM), lambda h, b: (0, 0))
    return pl.pallas_call(
        functools.partial(_da_attn_kernel, T=T, SUB=sub, S=S, lambda_init=lambda_init),
        grid=(H, B),
        in_specs=[pl.BlockSpec(memory_space=pltpu.SMEM),
                  pl.BlockSpec((None, S, LANES), lambda h, b: (b, 0, h)),
                  pl.BlockSpec((None, S, LANES), lambda h, b: (b, 0, H + h)),
                  pl.BlockSpec((None, S, LANES), lambda h, b: (b, 0, 2 * H + h)),
                  vec64, vec64, vec64, vec64,
                  pl.BlockSpec((1, LANES), lambda h, b: (0, 0))],
        out_specs=pl.BlockSpec((None, S, LANES), lambda h, b: (b, 0, h)),
        out_shape=jax.ShapeDtypeStruct((B, S, H * LANES), BF16),
        scratch_shapes=[pltpu.VMEM((S, LANES), BF16), pltpu.VMEM((S, LANES), BF16), pltpu.VMEM((VT_ROWS, S), BF16),
                        pltpu.VMEM((T, T), F32), pltpu.VMEM((4, T, LANES), BF16)]
        + [pltpu.VMEM((S, T), F32) for _ in range(4)] + [
                        pltpu.VMEM((LANES, T), BF16), pltpu.VMEM((LANES, T), BF16),
                        pltpu.VMEM((VT_ROWS, T), F32), pltpu.VMEM((VT_ROWS, T), F32)],
        compiler_params=_cp("arbitrary", "arbitrary"),
        name="da_attention",
    )(slopes, qkv, qkv, qkv, lq1, lk1, lq2, lk2, subg)


def _post0_kernel(x_ref, o_ref, wo_ref, g1_ref, ng_ref, sc_ref, sh_ref, g2_ref, w1_ref, w3_ref, w2_ref,
                  out_ref, h_scr, acc_scr, *, parts):
    j = pl.program_id(1)
    tm = out_ref.shape[0]

    def out_proj(rows):
        return jnp.dot(o_ref[rows, :], wo_ref[...], preferred_element_type=F32)

    def residual_norm(rows, y):
        x1 = x_ref[rows, :] + g1_ref[...] * y
        out_ref[rows, :] = x1
        h_scr[rows, :] = _modnorm(x1, ng_ref[...], sc_ref[...], sh_ref[...]).astype(BF16)
        acc_scr[rows, :] = jnp.zeros((y.shape[0], D), F32)

    @pl.when(j == 0)
    def _():
        _row_pipeline(parts, tm, out_proj, residual_norm)

    def up(rows):
        h = h_scr[rows, :]
        return (jnp.dot(h, w1_ref[...], preferred_element_type=F32),
                jnp.dot(h, w3_ref[...], preferred_element_type=F32))

    def down(rows, ab):
        a, b = ab
        gact = (a * _sigmoid(a) * b).astype(BF16)
        acc_scr[rows, :] += jnp.dot(gact, w2_ref[...], preferred_element_type=F32)

    _row_pipeline(parts, tm, up, down)

    @pl.when(j == pl.num_programs(1) - 1)
    def _():
        out_ref[...] = out_ref[...] + g2_ref[...] * acc_scr[...]


def _post0(x2d, o2d, S, wo, g1, ng, sc, sh, g2, w1, w3, w2, tm=1024, tf=1408, parts=4):
    N = x2d.shape[0]
    F = w1.shape[1]
    vec = pl.BlockSpec((None, 1, D), lambda i, j: ((i * tm) // S, 0, 0))
    row = pl.BlockSpec((tm, D), lambda i, j: (i, 0))
    return pl.pallas_call(
        functools.partial(_post0_kernel, parts=parts),
        grid=(N // tm, F // tf),
        in_specs=[row, row,
                  pl.BlockSpec((D, D), lambda i, j: (0, 0)),
                  vec,
                  pl.BlockSpec((1, D), lambda i, j: (0, 0)),
                  vec, vec, vec,
                  pl.BlockSpec((D, tf), lambda i, j: (0, j)),
                  pl.BlockSpec((D, tf), lambda i, j: (0, j)),
                  pl.BlockSpec((tf, D), lambda i, j: (j, 0))],
        out_specs=row,
        out_shape=jax.ShapeDtypeStruct((N, D), F32),
        scratch_shapes=[pltpu.VMEM((tm, D), BF16), pltpu.VMEM((tm, D), F32)],
        compiler_params=_cp("arbitrary", "arbitrary"),
        name="post0_ffn",
    )(x2d, o2d, wo, g1, ng, sc, sh, g2, w1, w3, w2)


def _gla_inproj_kernel(x_ref, g_ref, sc_ref, sh_ref, w_ref, mult_ref, wa1_ref, wa2_ref, ba_ref,
                       o_ref, la_ref, h_scr, *, tn, n_plain_groups, parts):
    tm, n_out = o_ref.shape
    rp = tm // parts
    items = [(slice(p * rp, (p + 1) * rp), cg) for p in range(parts) for cg in range(n_out // tn)]

    def project(item):
        rows, cg = item
        if cg == 0:
            h = _modnorm(x_ref[rows, :], g_ref[...], sc_ref[...], sh_ref[...]).astype(BF16)
            h_scr[rows, :] = h
            low = jnp.dot(h, wa1_ref[...], preferred_element_type=F32).astype(BF16)
            z = jnp.dot(low, wa2_ref[...], preferred_element_type=F32) + ba_ref[...]
            log_sig = jnp.minimum(z, 0.0) - jnp.log(1.0 + jnp.exp(-jnp.abs(z)))
            la_ref[rows, :] = log_sig * (1.0 / GLA_TAU)
        else:
            h = h_scr[rows, :]
        return jnp.dot(h, w_ref[:, cg * tn:(cg + 1) * tn], preferred_element_type=F32)

    def finish(item, acc):
        rows, cg = item
        cols = slice(cg * tn, (cg + 1) * tn)
        if cg < n_plain_groups:
            o_ref[rows, cols] = (acc * mult_ref[:, cols]).astype(BF16)
        else:
            o_ref[rows, cols] = (acc * _sigmoid(acc)).astype(BF16)

    _pipeline(items, project, finish)


def _gla_inproj(x2d, S, g, sc, sh, w_bf, mult_row, wa1, wa2, ba, tm=1024, tn=1024, parts=4):
    N = x2d.shape[0]
    n_out = w_bf.shape[1]
    kw = wa2.shape[1]
    vec = pl.BlockSpec((None, 1, D), lambda i: ((i * tm) // S, 0, 0))
    return pl.pallas_call(
        functools.partial(_gla_inproj_kernel, tn=tn, n_plain_groups=(2 * D) // tn, parts=parts),
        grid=(N // tm,),
        in_specs=[pl.BlockSpec((tm, D), lambda i: (i, 0)),
                  pl.BlockSpec((1, D), lambda i: (0, 0)),
                  vec, vec,
                  pl.BlockSpec((D, n_out), lambda i: (0, 0)),
                  pl.BlockSpec((1, n_out), lambda i: (0, 0)),
                  pl.BlockSpec((D, LANES), lambda i: (0, 0)),
                  pl.BlockSpec((LANES, kw), lambda i: (0, 0)),
                  pl.BlockSpec((1, kw), lambda i: (0, 0))],
        out_specs=[pl.BlockSpec((tm, n_out), lambda i: (i, 0)),
                   pl.BlockSpec((tm, kw), lambda i: (i, 0))],
        out_shape=[jax.ShapeDtypeStruct((N, n_out), BF16), jax.ShapeDtypeStruct((N, kw), F32)],
        scratch_shapes=[pltpu.VMEM((tm, D), BF16)],
        compiler_params=_cp("arbitrary"),
        name="gla_inproj",
    )(x2d, g, sc, sh, w_bf, mult_row, wa1, wa2, ba)


def _gla_kernel(q_ref, k_ref, v_ref, r_ref, la_ref, tri_ref, og_ref, o_ref, state_scr, *, NB, SB):
    @pl.when(pl.program_id(1) == 0)
    def _():
        state_scr[...] = jnp.zeros_like(state_scr)

    tri = tri_ref[...]
    ti = lax.broadcasted_iota(jnp.int32, (CHUNK, CHUNK), 0)
    si = lax.broadcasted_iota(jnp.int32, (CHUNK, CHUNK), 1)
    causal = si <= ti
    og = og_ref[...]
    chains = [(nb, h) for nb in range(NB) for h in range(GLA_HEADS)]
    nt = (((1,), (1,)), ((), ()))
    tn = (((0,), (0,)), ((), ()))

    def chunk(c, carry):
        r0 = pl.multiple_of(c * CHUNK, CHUNK)
        rows = pl.ds(r0, CHUNK)
        kcol = [slice(h * GLA_DK, (h + 1) * GLA_DK) for _, h in chains]
        vcol = [slice(h * GLA_DV, (h + 1) * GLA_DV) for _, h in chains]
        las = [la_ref[nb, rows, kc] for (nb, _), kc in zip(chains, kcol)]
        his = [la.astype(BF16) for la in las]
        los = [(la - hi.astype(F32)).astype(BF16) for la, hi in zip(las, his)]
        bs = [jnp.dot(tri, hi, preferred_element_type=F32) + jnp.dot(tri, lo, preferred_element_type=F32)
              for hi, lo in zip(his, los)]
        b_lasts = [b[CHUNK - 1:CHUNK, :] for b in bs]
        qs = [q_ref[nb, rows, kc].astype(F32) for (nb, _), kc in zip(chains, kcol)]
        ks = [k_ref[nb, rows, kc].astype(F32) for (nb, _), kc in zip(chains, kcol)]
        vs = [v_ref[nb, rows, vc] for (nb, _), vc in zip(chains, vcol)]
        q_decs = [(q * jnp.exp(b)).astype(BF16) for q, b in zip(qs, bs)]
        k_invs = [(k * jnp.exp(-b)).astype(BF16) for k, b in zip(ks, bs)]
        k_decs = [(k * jnp.exp(bl - b)).astype(BF16) for k, b, bl in zip(ks, bs, b_lasts)]
        states = [state_scr[nb, h] for nb, h in chains]
        attns = [lax.dot_general(qd, ki, nt, preferred_element_type=F32) for qd, ki in zip(q_decs, k_invs)]
        o_inters = [lax.dot_general(qd, st.astype(BF16), nt, preferred_element_type=F32)
                    for qd, st in zip(q_decs, states)]
        kv_ts = [lax.dot_general(v, kd, tn, preferred_element_type=F32) for v, kd in zip(vs, k_decs)]
        attns = [jnp.where(causal, a, 0.0).astype(BF16) for a in attns]
        o_intras = [jnp.dot(a, v, preferred_element_type=F32) for a, v in zip(attns, vs)]
        for (nb, h), st, bl, kv_t, oi, oe, vc in zip(chains, states, b_lasts, kv_ts, o_intras, o_inters, vcol):
            state_scr[nb, h] = st * jnp.exp(bl) + kv_t
            o = oi + oe
            ms = jnp.mean(o * o, axis=-1, keepdims=True)
            o = o * lax.rsqrt(ms + EPS) * og * r_ref[nb, rows, vc].astype(F32)
            o_ref[nb, rows, vc] = o.astype(o_ref.dtype)
        return carry

    lax.fori_loop(0, SB // CHUNK, chunk, 0)


def _gla(qkvr, la, out_g, nb=4, sb=256):
    B, S, _ = qkvr.shape
    H = GLA_HEADS
    kw = H * GLA_DK
    vw = H * GLA_DV
    tri = (jnp.arange(CHUNK)[None, :] <= jnp.arange(CHUNK)[:, None]).astype(BF16)
    return pl.pallas_call(
        functools.partial(_gla_kernel, NB=nb, SB=sb),
        grid=(B // nb, S // sb),
        in_specs=[pl.BlockSpec((nb, sb, kw), lambda b, s: (b, s, 0)),
                  pl.BlockSpec((nb, sb, kw), lambda b, s: (b, s, 1)),
                  pl.BlockSpec((nb, sb, vw), lambda b, s: (b, s, (2 * kw) // vw)),
                  pl.BlockSpec((nb, sb, vw), lambda b, s: (b, s, (2 * kw) // vw + 1)),
                  pl.BlockSpec((nb, sb, kw), lambda b, s: (b, s, 0)),
                  pl.BlockSpec((CHUNK, CHUNK), lambda b, s: (0, 0)),
                  pl.BlockSpec((1, GLA_DV), lambda b, s: (0, 0))],
        out_specs=pl.BlockSpec((nb, sb, vw), lambda b, s: (b, s, 0)),
        out_shape=jax.ShapeDtypeStruct((B, S, vw), BF16),
        scratch_shapes=[pltpu.VMEM((nb, H, GLA_DV, GLA_DK), F32)],
        compiler_params=_cp("arbitrary", "arbitrary"),
        name="gla",
    )(qkvr, qkvr, qkvr, qkvr, la, tri, out_g)


def _post1_kernel(x_ref, o_ref, wo_ref, g1_ref, ng_ref, sc_ref, sh_ref, rw_ref, ltri_ref,
                  x3_ref, hp_ref, route_ref, cnt_ref, carry_scr, *, sub):
    i = pl.program_id(0)

    @pl.when(i == 0)
    def _():
        carry_scr[...] = jnp.zeros_like(carry_scr)

    tm = x3_ref.shape[0]
    lane = lax.broadcasted_iota(jnp.int32, (sub, LANES), 1)
    lanef = lane.astype(F32)

    def out_proj(rows):
        return jnp.dot(o_ref[rows, :], wo_ref[...], preferred_element_type=F32)

    def route_rows(rows, y):
        x3 = x_ref[rows, :] + g1_ref[...] * y
        x3_ref[rows, :] = x3
        hb = _modnorm(x3, ng_ref[...], sc_ref[...], sh_ref[...]).astype(BF16)

        half = D // 2
        lo = pltpu.bitcast(hb[:, :half].astype(F32), jnp.uint32)
        hi = pltpu.bitcast(hb[:, half:].astype(F32), jnp.uint32)
        hp_ref[rows, :] = jnp.right_shift(lo, jnp.uint32(16)) | (hi & jnp.uint32(0xFFFF0000))

        logits = jnp.dot(hb, rw_ref[...], preferred_element_type=F32)
        logits = jnp.where(lane < N_EXPERTS, logits, NEG)
        m1 = jnp.max(logits, axis=-1, keepdims=True)
        i1 = jnp.min(jnp.where(logits == m1, lanef, float(LANES)), axis=-1, keepdims=True)
        oh1 = lanef == i1
        rest = jnp.where(oh1, NEG, logits)
        m2 = jnp.max(rest, axis=-1, keepdims=True)
        i2 = jnp.min(jnp.where(rest == m2, lanef, float(LANES)), axis=-1, keepdims=True)
        oh2 = lanef == i2
        e = jnp.exp(m2 - m1)
        gate1 = 1.0 / (1.0 + e)
        gate2 = e / (1.0 + e)

        cnt = jnp.where(oh1, 1.0, 0.0) + jnp.where(oh2, 1.0, 0.0)
        carry = carry_scr[...]
        pre = jnp.dot(ltri_ref[...], cnt.astype(BF16), preferred_element_type=F32) + carry
        carry_scr[...] = carry + jnp.sum(cnt, axis=0, keepdims=True)
        r1 = jnp.sum(jnp.where(oh1, pre, 0.0), axis=-1, keepdims=True)
        r2 = jnp.sum(jnp.where(oh2, pre, 0.0), axis=-1, keepdims=True)
        route = jnp.where(lane == 0, i1, 0.0)
        for idx, val in ((1, i2), (2, r1), (3, r2), (4, gate1), (5, gate2)):
            route = jnp.where(lane == idx, val, route)
        route_ref[rows, :] = route

    _row_pipeline(tm // sub, tm, out_proj, route_rows)
    cnt_ref[...] = carry_scr[...]


def _post1(x2d, o2d, S, wo, g1, ng, sc, sh, router_pad, tm=1024, sub=256):
    N = x2d.shape[0]
    ltri = (jnp.arange(sub)[None, :] < jnp.arange(sub)[:, None]).astype(BF16)
    vec = pl.BlockSpec((None, 1, D), lambda i: ((i * tm) // S, 0, 0))
    row = pl.BlockSpec((tm, D), lambda i: (i, 0))
    return pl.pallas_call(
        functools.partial(_post1_kernel, sub=sub),
        grid=(N // tm,),
        in_specs=[row, row,
                  pl.BlockSpec((D, D), lambda i: (0, 0)),
                  vec,
                  pl.BlockSpec((1, D), lambda i: (0, 0)),
                  vec, vec,
                  pl.BlockSpec((D, LANES), lambda i: (0, 0)),
                  pl.BlockSpec((sub, sub), lambda i: (0, 0))],
        out_specs=[row,
                   pl.BlockSpec((tm, D // 2), lambda i: (i, 0)),
                   pl.BlockSpec((tm, LANES), lambda i: (i, 0)),
                   pl.BlockSpec((1, LANES), lambda i: (0, 0))],
        out_shape=[jax.ShapeDtypeStruct((N, D), F32),
                   jax.ShapeDtypeStruct((N, D // 2), jnp.uint32),
                   jax.ShapeDtypeStruct((N, LANES), F32),
                   jax.ShapeDtypeStruct((1, LANES), F32)],
        scratch_shapes=[pltpu.VMEM((1, LANES), F32)],
        compiler_params=_cp("arbitrary"),
        name="post1_router",
    )(x2d, o2d, wo, g1, ng, sc, sh, router_pad, ltri)


def _dispatch_kernel(pos_ref, h_ref, xs_in_ref, xs_ref, sem, *, td):
    del xs_in_ref

    def row(r, carry):
        for kk in range(2):
            p = pos_ref[0, 2 * r + kk]
            pltpu.make_async_copy(h_ref.at[pl.ds(r, 1)], xs_ref.at[pl.ds(p, 1)], sem).start()
        return carry

    lax.fori_loop(0, td, row, 0, unroll=8)
    for _ in range(2):
        pltpu.make_async_copy(h_ref, xs_ref.at[pl.ds(0, td)], sem).wait()


def _dispatch(hp, pos, P, td=512):
    N, W = hp.shape
    xs0 = jnp.zeros((P, W), jnp.uint32)
    return pl.pallas_call(
        functools.partial(_dispatch_kernel, td=td),
        grid=(N // td,),
        in_specs=[pl.BlockSpec((None, 1, 2 * td), lambda i: (i, 0, 0), memory_space=pltpu.SMEM),
                  pl.BlockSpec((td, W), lambda i: (i, 0)),
                  pl.BlockSpec(memory_space=pl.ANY)],
        out_specs=pl.BlockSpec(memory_space=pl.ANY),
        out_shape=jax.ShapeDtypeStruct((P, W), jnp.uint32),
        scratch_shapes=[pltpu.SemaphoreType.DMA(())],
        input_output_aliases={2: 0},
        compiler_params=_cp("arbitrary"),
        name="moe_dispatch",
    )(pos.reshape(N // td, 1, 2 * td), hp, xs0)


def _experts_kernel(be_ref, nv_ref, xs_ref, w1_ref, w3_ref, w2_ref, c1_ref, c3_ref, c2_ref, y_ref,
                    xq_scr, xr_scr, *, parts):
    i = pl.program_id(0)
    j = pl.program_id(1)
    valid = i < nv_ref[0]
    tm = y_ref.shape[0]

    @pl.when(valid & (j == 0))
    def _():
        w = xs_ref[...]
        half = D // 2
        lo = pltpu.bitcast(jnp.left_shift(w, jnp.uint32(16)), F32)
        hi = pltpu.bitcast(w & jnp.uint32(0xFFFF0000), F32)
        amax = jnp.maximum(jnp.maximum(jnp.max(jnp.abs(lo), axis=-1, keepdims=True),
                                       jnp.max(jnp.abs(hi), axis=-1, keepdims=True)), FP8_TINY)
        scale = FP8_TOP / amax
        xq_scr[:, :half] = (lo * scale).astype(FP8)
        xq_scr[:, half:] = (hi * scale).astype(FP8)
        xr_scr[...] = jnp.broadcast_to(amax * (1.0 / FP8_TOP), xr_scr.shape)
        y_ref[...] = jnp.zeros_like(y_ref)

    @pl.when(jnp.logical_not(valid) & (j == 0))
    def _():
        y_ref[...] = jnp.zeros_like(y_ref)

    def up(rows):
        xq = xq_scr[rows, :]
        return (jnp.dot(xq, w1_ref[...], preferred_element_type=F32),
                jnp.dot(xq, w3_ref[...], preferred_element_type=F32))

    def down(rows, ab):
        x_unscale = xr_scr[rows, :1]
        a = ab[0] * x_unscale * c1_ref[...]
        b = ab[1] * x_unscale * c3_ref[...]
        g = a * _sigmoid(a) * b
        g_scale, g_unscale = _row_scale(g)
        acc = jnp.dot((g * g_scale).astype(FP8), w2_ref[...], preferred_element_type=F32)
        y_ref[rows, :] += acc * g_unscale * c2_ref[...]

    @pl.when(valid)
    def _():
        _row_pipeline(parts, tm, up, down)


def _experts(xs, blk_expert, n_valid, w1, w3, w2, tm, tf=1792, parts=4):
    P, W = xs.shape
    E, _, F = w1.shape
    nblk = P // tm
    nj = F // tf
    w1, c1 = _quantize_columns(w1)
    w3, c3 = _quantize_columns(w3)
    w2, c2 = _quantize_columns(w2)

    def _i(i, nv):
        return jnp.minimum(i, nv[0] - 1)

    def _j(i, j, nv):
        return jnp.where(i < nv[0], j, nj - 1)

    grid_spec = pltpu.PrefetchScalarGridSpec(
        num_scalar_prefetch=2,
        grid=(nblk, nj),
        in_specs=[pl.BlockSpec((tm, W), lambda i, j, be, nv: (_i(i, nv), 0)),
                  pl.BlockSpec((None, D, tf), lambda i, j, be, nv: (be[_i(i, nv)], 0, _j(i, j, nv))),
                  pl.BlockSpec((None, D, tf), lambda i, j, be, nv: (be[_i(i, nv)], 0, _j(i, j, nv))),
                  pl.BlockSpec((None, tf, D), lambda i, j, be, nv: (be[_i(i, nv)], _j(i, j, nv), 0)),
                  pl.BlockSpec((None, 1, tf), lambda i, j, be, nv: (be[_i(i, nv)], 0, _j(i, j, nv))),
                  pl.BlockSpec((None, 1, tf), lambda i, j, be, nv: (be[_i(i, nv)], 0, _j(i, j, nv))),
                  pl.BlockSpec((None, 1, D), lambda i, j, be, nv: (be[_i(i, nv)], 0, 0))],
        out_specs=pl.BlockSpec((tm, D), lambda i, j, be, nv: (i, 0)),
        scratch_shapes=[pltpu.VMEM((tm, D), FP8), pltpu.VMEM((tm, LANES), F32)],
    )
    return pl.pallas_call(
        functools.partial(_experts_kernel, parts=parts),
        grid_spec=grid_spec,
        out_shape=jax.ShapeDtypeStruct((P, D), F32),
        compiler_params=_cp("arbitrary", "arbitrary"),
        name="moe_experts",
    )(blk_expert, n_valid, xs, w1, w3, w2, c1, c3, c2)


def _combine_kernel(pos_ref, pos_next_ref, route_ref, x_ref, g2_ref, y_ref, out_ref, ybuf, sems, *, tc):
    i = pl.program_id(0)
    slot = jnp.bitwise_and(i, 1)

    def gather_copy(p, s, kk, r):
        return pltpu.make_async_copy(y_ref.at[pl.ds(p, 1)], ybuf.at[s, kk, pl.ds(r, 1)], sems.at[s])

    def start_gathers(table_ref, s):
        def row(r, carry):
            for kk in range(2):
                gather_copy(table_ref[0, 2 * r + kk], s, kk, r).start()
            return carry

        lax.fori_loop(0, tc, row, 0, unroll=8)

    @pl.when(i == 0)
    def _():
        start_gathers(pos_ref, 0)

    @pl.when(i + 1 < pl.num_programs(0))
    def _():
        start_gathers(pos_next_ref, 1 - slot)

    for kk in range(2):
        pltpu.make_async_copy(y_ref.at[pl.ds(0, tc)], ybuf.at[slot, kk], sems.at[slot]).wait()
    route = route_ref[...]
    moe = route[:, 4:5] * ybuf[slot, 0] + route[:, 5:6] * ybuf[slot, 1]
    out_ref[...] = x_ref[...] + g2_ref[...] * moe


def _combine(y, pos, route, x3, S, g2, tc=512):
    N = x3.shape[0]
    n_steps = N // tc
    pos_blocks = pos.reshape(n_steps, 1, 2 * tc)
    return pl.pallas_call(
        functools.partial(_combine_kernel, tc=tc),
        grid=(n_steps,),
        in_specs=[pl.BlockSpec((None, 1, 2 * tc), lambda i: (i, 0, 0), memory_space=pltpu.SMEM),
                  pl.BlockSpec((None, 1, 2 * tc), lambda i: (jnp.minimum(i + 1, n_steps - 1), 0, 0),
                               memory_space=pltpu.SMEM),
                  pl.BlockSpec((tc, LANES), lambda i: (i, 0)),
                  pl.BlockSpec((tc, D), lambda i: (i, 0)),
                  pl.BlockSpec((None, 1, D), lambda i: ((i * tc) // S, 0, 0)),
                  pl.BlockSpec(memory_space=pl.ANY)],
        out_specs=pl.BlockSpec((tc, D), lambda i: (i, 0)),
        out_shape=jax.ShapeDtypeStruct((N, D), F32),
        scratch_shapes=[pltpu.VMEM((2, 2, tc, D), F32), pltpu.SemaphoreType.DMA((2,))],
        compiler_params=_cp("arbitrary"),
        name="moe_combine",
    )(pos_blocks, pos_blocks, route, x3, g2, y)


def _moe(hp, route, counts, x3, S, g2, w1, w3, w2, tm=1024):
    N = x3.shape[0]
    cnt = counts[0, :N_EXPERTS].astype(jnp.int32)
    nblk_e = (cnt + tm - 1) // tm
    blk_end = jnp.cumsum(nblk_e)
    row_start = (blk_end - nblk_e) * tm
    e_idx = route[:, 0:2].astype(jnp.int32)
    rank = route[:, 2:4].astype(jnp.int32)
    pos = row_start[e_idx] + rank
    P = 2 * N + N_EXPERTS * tm
    nblk = P // tm
    blk_expert = jnp.minimum(
        jnp.sum((jnp.arange(nblk, dtype=jnp.int32)[:, None] >= blk_end[None, :]).astype(jnp.int32), axis=1),
        N_EXPERTS - 1)
    n_valid = blk_end[-1:].astype(jnp.int32)
    xs = _dispatch(hp, pos, P)
    y = _experts(xs, blk_expert, n_valid, w1, w3, w2, tm)
    return _combine(y, pos, route, x3, S, g2)


def kernel(x, c, ada_w, ada_b, norm1_g, norm2_g, da_w_in, da_q_gain, da_k_gain, da_lam_q1, da_lam_k1, da_lam_q2,
           da_lam_k2, da_subln_g, da_w_out, gla_w_in, gla_w_a1, gla_w_a2, gla_b_a, gla_out_g, gla_w_out,
           ffn_w1, ffn_w3, ffn_w2, moe_router, moe_w1, moe_w3, moe_w2):
    B, S, _ = x.shape
    N = B * S
    mod = _adaln(c, ada_w, ada_b)
    mods = [[mod[l, :, k * D:(k + 1) * D].reshape(B, 1, D) for k in range(6)] for l in range(2)]
    x2d = x.reshape(N, D)

    sh1, sc1, gt1, sh2, sc2, gt2 = mods[0]
    lambda_init = 0.8 - 0.6 * math.exp(-0.3 * 0)
    qk_scale = DA_HEAD_DIM ** -0.5 * LOG2E
    gain_row = jnp.concatenate([jnp.tile(da_q_gain[0].reshape(-1) * qk_scale, DA_HEADS),
                                jnp.tile(da_k_gain[0].reshape(-1), DA_HEADS),
                                jnp.ones((D,), F32)]).reshape(1, 3 * D)
    qkv = _da_inproj(x2d, S, norm1_g[0].reshape(1, D), sc1, sh1, da_w_in[0].astype(BF16), gain_row)
    slopes = 2.0 ** (-8.0 * jnp.arange(1, DA_HEADS + 1, dtype=F32) / DA_HEADS)
    o = _da_attention(qkv.reshape(B, S, 3 * D), slopes,
                      da_lam_q1[0].reshape(1, -1), da_lam_k1[0].reshape(1, -1),
                      da_lam_q2[0].reshape(1, -1), da_lam_k2[0].reshape(1, -1),
                      da_subln_g[0].reshape(1, -1), lambda_init)
    x2d = _post0(x2d, o.reshape(N, D), S, da_w_out[0].astype(BF16), gt1, norm2_g[0].reshape(1, D), sc2, sh2, gt2,
                 ffn_w1[0].astype(BF16), ffn_w3[0].astype(BF16), ffn_w2[0].astype(BF16))

    sh1, sc1, gt1, sh2, sc2, gt2 = mods[1]
    kw = GLA_HEADS * GLA_DK
    mult_row = jnp.concatenate([jnp.full((kw,), GLA_DK ** -0.5, F32), jnp.ones((3 * D - kw,), F32)]).reshape(1, -1)
    rank = gla_w_a1.shape[-1]
    wa1 = jnp.zeros((D, LANES), BF16).at[:, :rank].set(gla_w_a1[0].astype(BF16))
    wa2 = jnp.zeros((LANES, kw), BF16).at[:rank, :].set(gla_w_a2[0].astype(BF16))
    qkvr, la = _gla_inproj(x2d, S, norm1_g[1].reshape(1, D), sc1, sh1, gla_w_in[0].astype(BF16), mult_row,
                           wa1, wa2, gla_b_a[0].reshape(1, kw))
    o = _gla(qkvr.reshape(B, S, 3 * D), la.reshape(B, S, kw), gla_out_g[0].reshape(1, -1))
    router_pad = jnp.zeros((D, LANES), BF16).at[:, :N_EXPERTS].set(moe_router[0].astype(BF16))
    x3, hp, route, counts = _post1(x2d, o.reshape(N, D), S, gla_w_out[0].astype(BF16), gt1,
                                   norm2_g[1].reshape(1, D), sc2, sh2, router_pad)
    out = _moe(hp, route, counts, x3, S, gt2,
               moe_w1[0], moe_w3[0], moe_w2[0])
    return out.reshape(B, S, D)
```

```python
import functools
import math

import jax
import jax.numpy as jnp
from jax import lax
from jax.experimental import pallas as pl
from jax.experimental.pallas import tpu as pltpu

F32 = jnp.float32
BF16 = jnp.bfloat16
FP8 = jnp.float8_e4m3fn
FP8_TOP = 256.0
FP8_TINY = 1e-30

D = 1024
EPS = 1e-6
NEG = -1e30
CHUNK = 64
DA_HEADS = 8
DA_HEAD_DIM = 64
GLA_HEADS = 4
GLA_DK = 128
GLA_DV = 256
GLA_TAU = 16.0
N_EXPERTS = 8
LANES = 128
LOG2E = 1.4426950408889634
VT_ROWS = LANES + 16

VMEM_LIMIT = 56 * 1024 * 1024


def _cp(*sem):
    return pltpu.CompilerParams(dimension_semantics=sem, vmem_limit_bytes=VMEM_LIMIT)


def _sigmoid(x):
    return 1.0 / (1.0 + jnp.exp(-x))


def _modnorm(x, g, sc, sh):
    ms = jnp.mean(x * x, axis=-1, keepdims=True)
    return (x * lax.rsqrt(ms + EPS) * g) * (1.0 + sc) + sh


def _row_scale(v):
    amax = jnp.maximum(jnp.max(jnp.abs(v), axis=-1, keepdims=True), FP8_TINY)
    return FP8_TOP / amax, amax * (1.0 / FP8_TOP)


def _quantize_columns(w):
    amax = jnp.maximum(jnp.max(jnp.abs(w), axis=1, keepdims=True), FP8_TINY)
    return (w * (FP8_TOP / amax)).astype(FP8), amax * (1.0 / FP8_TOP)


def _adaln_kernel(c_ref, w_ref, b_ref, o_ref):
    c = c_ref[...]
    ca = (c * _sigmoid(c)).astype(BF16)
    o_ref[...] = jnp.dot(ca, w_ref[...].astype(BF16), preferred_element_type=F32) + b_ref[...]


def _adaln(c, ada_w, ada_b):
    L, _, n6 = ada_w.shape
    B = c.shape[0]
    tn = 1536
    return pl.pallas_call(
        _adaln_kernel,
        grid=(L, n6 // tn),
        in_specs=[pl.BlockSpec((B, D), lambda l, n: (0, 0)),
                  pl.BlockSpec((None, D, tn), lambda l, n: (l, 0, n)),
                  pl.BlockSpec((None, 1, tn), lambda l, n: (l, 0, n))],
        out_specs=pl.BlockSpec((None, B, tn), lambda l, n: (l, 0, n)),
        out_shape=jax.ShapeDtypeStruct((L, B, n6), F32),
        compiler_params=_cp("arbitrary", "arbitrary"),
        name="adaln",
    )(c, ada_w, ada_b.reshape(L, 1, n6))


def _pipeline(items, matmul, finish):
    prev = None
    for item in items:
        acc = matmul(item)
        if prev is not None:
            finish(*prev)
        prev = (item, acc)
    finish(*prev)


def _row_pipeline(parts, rows_total, matmul, finish):
    rp = rows_total // parts
    _pipeline([slice(p * rp, (p + 1) * rp) for p in range(parts)], matmul, finish)


def _da_inproj_kernel(x_ref, g_ref, sc_ref, sh_ref, w_ref, gain_ref, gsum_ref, o_ref, h_scr,
                      *, tn, n_qk_groups, parts):
    tm, n_out = o_ref.shape
    rp = tm // parts
    items = [(slice(p * rp, (p + 1) * rp), cg) for p in range(parts) for cg in range(n_out // tn)]

    def project(item):
        rows, cg = item
        if cg == 0:
            h = _modnorm(x_ref[rows, :], g_ref[...], sc_ref[...], sh_ref[...]).astype(BF16)
            h_scr[rows, :] = h
        else:
            h = h_scr[rows, :]
        return jnp.dot(h, w_ref[:, cg * tn:(cg + 1) * tn], preferred_element_type=F32)

    def finish(item, acc):
        rows, cg = item
        cols = slice(cg * tn, (cg + 1) * tn)
        if cg < n_qk_groups:
            y2 = (acc * acc).astype(BF16)
            gsum = gsum_ref[...]
            ms = jnp.concatenate(
                [jnp.dot(y2[:, c * 256:(c + 1) * 256], gsum, preferred_element_type=F32) for c in range(tn // 256)],
                axis=1)
            acc = acc * lax.rsqrt(ms + EPS)
        o_ref[rows, cols] = (acc * gain_ref[:, cols]).astype(BF16)

    _pipeline(items, project, finish)


def _da_inproj(x2d, S, g, sc, sh, w_bf, gain_row, tm=1024, tn=1024, parts=4):
    N = x2d.shape[0]
    n_out = w_bf.shape[1]
    r = jnp.arange(256) // DA_HEAD_DIM
    gsum = jnp.where(r[:, None] == r[None, :], 1.0 / DA_HEAD_DIM, 0.0).astype(BF16)
    vec = pl.BlockSpec((None, 1, D), lambda i: ((i * tm) // S, 0, 0))
    return pl.pallas_call(
        functools.partial(_da_inproj_kernel, tn=tn, n_qk_groups=(2 * D) // tn, parts=parts),
        grid=(N // tm,),
        in_specs=[pl.BlockSpec((tm, D), lambda i: (i, 0)),
                  pl.BlockSpec((1, D), lambda i: (0, 0)),
                  vec, vec,
                  pl.BlockSpec((D, n_out), lambda i: (0, 0)),
                  pl.BlockSpec((1, n_out), lambda i: (0, 0)),
                  pl.BlockSpec((256, 256), lambda i: (0, 0))],
        out_specs=pl.BlockSpec((tm, n_out), lambda i: (i, 0)),
        out_shape=jax.ShapeDtypeStruct((N, n_out), BF16),
        scratch_shapes=[pltpu.VMEM((tm, D), BF16)],
        compiler_params=_cp("arbitrary"),
        name="da_inproj",
    )(x2d, g, sc, sh, w_bf, gain_row, gsum)


def _da_attn_kernel(slope_ref, q_ref, k_ref, v_ref, lq1_ref, lk1_ref, lq2_ref, lk2_ref, subg_ref, o_ref,
                    k1_scr, k2_scr, vt_scr, corr_scr, aug_scr, s1e_scr, s2e_scr, s1o_scr, s2o_scr, qt1_scr, qt2_scr, a1, a2,
                    *, T, SUB, S, lambda_init):
    h = pl.program_id(0)
    slope2 = slope_ref[h] * LOG2E
    lane = lax.broadcasted_iota(jnp.int32, (T, LANES), 1)
    first_half = lane < DA_HEAD_DIM

    @pl.when(pl.program_id(1) == 0)
    def _():
        local = lax.broadcasted_iota(jnp.int32, (T, LANES), 0)
        lo = jnp.bitwise_and(local, 255).astype(F32)
        hi = (local - jnp.bitwise_and(local, 255)).astype(F32)

        def aug(base, vals):
            out = jnp.zeros((T, LANES), F32)
            for off, val in enumerate(vals):
                out = jnp.where(lane == base + off, val, out)
            return out.astype(BF16)

        c_hi = (slope2 + jnp.zeros((T, LANES), F32)).astype(BF16).astype(F32)
        c_mid = (slope2 - c_hi).astype(BF16).astype(F32)
        c_lo = slope2 - c_hi - c_mid
        k_vals = (lo, lo, lo, hi, hi, hi, -c_hi, -c_mid, -c_lo, -c_hi, -c_mid, -c_lo)
        q_vals = (c_hi, c_mid, c_lo, c_hi, c_mid, c_lo, lo, lo, lo, hi, hi, hi)
        aug_scr[0] = aug(DA_HEAD_DIM, k_vals)
        aug_scr[1] = aug(0, k_vals)
        aug_scr[2] = aug(DA_HEAD_DIM, q_vals)
        aug_scr[3] = aug(0, q_vals)
        kj = lax.broadcasted_iota(jnp.int32, (T, T), 0)
        qj = lax.broadcasted_iota(jnp.int32, (T, T), 1)
        visible = jnp.right_shift(kj, 6) <= jnp.right_shift(qj, 6)
        ahead = jnp.maximum(kj - qj, 0).astype(F32)
        corr_scr[...] = jnp.where(visible, -2.0 * slope2 * ahead, NEG)

    k_aug1, k_aug2, q_aug1, q_aug2 = aug_scr[0], aug_scr[1], aug_scr[2], aug_scr[3]
    ones_rows = jnp.where(lax.broadcasted_iota(jnp.int32, (VT_ROWS - LANES, T), 0) == 0, 1.0, 0.0).astype(BF16)

    def prep(j, carry):
        r0 = pl.multiple_of(j * T, T)
        k = k_ref[pl.ds(r0, T), :]
        k1_scr[pl.ds(r0, T), :] = jnp.where(first_half, k, k_aug1)
        k2_scr[pl.ds(r0, T), :] = jnp.where(first_half, k_aug2, k)
        vt_scr[:LANES, pl.ds(r0, T)] = v_ref[pl.ds(r0, T), :].astype(F32).T.astype(BF16)
        vt_scr[LANES:, pl.ds(r0, T)] = ones_rows
        return carry

    lax.fori_loop(0, S // T, prep, 0)

    lam =(jnp.exp(jnp.sum(lq1_ref[...] * lk1_ref[...], axis=-1, keepdims=True))
           - jnp.exp(jnp.sum(lq2_ref[...] * lk2_ref[...], axis=-1, keepdims=True)) + lambda_init)
    k_scrs = (k1_scr, k2_scr)
    s_scrs = ((s1e_scr, s2e_scr), (s1o_scr, s2o_scr))
    qt_scrs = (qt1_scr, qt2_scr)
    accs = (a1, a2)
    n_q = S // T
    TK = T // SUB
    neg = jnp.full((1, T), NEG, F32)

    def load_queries(qi):
        q = q_ref[pl.ds(pl.multiple_of(qi * T, T), T), :]
        for qt_scr, qa in zip(qt_scrs, (jnp.where(first_half, q, q_aug1), jnp.where(first_half, q_aug2, q))):
            qt_scr[...] = qa.astype(F32).T.astype(BF16)

    def block_shift(qi, j):
        return -slope2 * lax.convert_element_type((qi - j) * T, F32)

    def score_chunk(qi, par, j, part, ms, corr):
        k0 = pl.multiple_of(j * T + part * TK, TK)
        out = []
        for mp in range(2):
            s = jnp.dot(k_scrs[mp][pl.ds(k0, TK), :], qt_scrs[mp][...], preferred_element_type=F32)
            if corr is not None:
                s = s + corr[part * TK:(part + 1) * TK, :]
            s_scrs[par][mp][pl.ds(k0, TK), :] = s
            out.append(jnp.maximum(ms[mp], jnp.max(s, axis=0, keepdims=True) + block_shift(qi, j)))
        return tuple(out)

    def weight_chunk(qi, par, j, part, ms):
        k0 = pl.multiple_of(j * T + part * TK, TK)
        vt = vt_scr[:, pl.ds(k0, TK)]
        for mp in range(2):
            p = jnp.exp2(s_scrs[par][mp][pl.ds(k0, TK), :] + (block_shift(qi, j) - ms[mp]))
            accs[mp][...] += jnp.dot(vt, p.astype(BF16), preferred_element_type=F32)

    def finalize(qi):
        acc1 = a1[...]
        acc2 = a2[...]
        o = (acc1[:LANES] / acc1[LANES:LANES + 1] - lam * (acc2[:LANES] / acc2[LANES:LANES + 1])).T
        msq = jnp.mean(o * o, axis=-1, keepdims=True)
        o = o * lax.rsqrt(msq + EPS) * subg_ref[...] * (1.0 - lambda_init)
        o_ref[pl.ds(pl.multiple_of(qi * T, T), T), :] = o.astype(o_ref.dtype)
        a1[...] = jnp.zeros_like(a1)
        a2[...] = jnp.zeros_like(a2)

    a1[...] = jnp.zeros_like(a1)
    a2[...] = jnp.zeros_like(a2)
    load_queries(0)
    ms0 = (neg, neg)
    for part in range(SUB):
        ms0 = score_chunk(0, 0, 0, part, ms0, corr_scr)

    def q_block(qi, par, ms):
        load_queries(qi + 1)

        def chunk(j, ms_next):
            for part in range(SUB):
                ms_next = score_chunk(qi + 1, 1 - par, j, part, ms_next, None)
                weight_chunk(qi, par, j, part, ms)
            return ms_next

        def chunk_pair(jj, ms_next):
            return chunk(2 * jj + 1, chunk(2 * jj, ms_next))

        ms_next = lax.fori_loop(0, jnp.right_shift(qi + 1, 1), chunk_pair, (neg, neg))
        if par == 0:
            ms_next = chunk(qi, ms_next)
        for part in range(SUB):
            ms_next = score_chunk(qi + 1, 1 - par, qi + 1, part, ms_next, corr_scr)
        finalize(qi)
        return ms_next

    def q_pair(qq, ms):
        return q_block(2 * qq + 1, 1, q_block(2 * qq, 0, ms))

    ms_even = lax.fori_loop(0, n_q // 2 - 1, q_pair, ms0)
    ms_last = q_block(n_q - 2, 0, ms_even)

    def last_chunk_pair(jj, carry):
        for j in (2 * jj, 2 * jj + 1):
            for part in range(SUB):
                weight_chunk(n_q - 1, 1, j, part, ms_last)
        return carry

    lax.fori_loop(0, n_q // 2, last_chunk_pair, 0)
    finalize(n_q - 1)


def _da_attention(qkv, slopes, lq1, lk1, lq2, lk2, subg, lambda_init, T=512, sub=2):
    B, S, _ = qkv.shape
    H = DA_HEADS
    vec64 = pl.BlockSpec((1, DA_HEAD_DIM), lambda h, b: (0, 0))
    return pl.pallas_call(
        functools.partial(_da_attn_kernel, T=T, SUB=sub, S=S, lambda_init=lambda_init),
        grid=(H, B),
        in_specs=[pl.BlockSpec(memory_space=pltpu.SMEM),
                  pl.BlockSpec((None, S, LANES), lambda h, b: (b, 0, h)),
                  pl.BlockSpec((None, S, LANES), lambda h, b: (b, 0, H + h)),
                  pl.BlockSpec((None, S, LANES), lambda h, b: (b, 0, 2 * H + h)),
                  vec64, vec64, vec64, vec64,
                  pl.BlockSpec((1, LANES), lambda h, b: (0, 0))],
        out_specs=pl.BlockSpec((None, S, LANES), lambda h, b: (b, 0, h)),
        out_shape=jax.ShapeDtypeStruct((B, S, H * LANES), BF16),
        scratch_shapes=[pltpu.VMEM((S, LANES), BF16), pltpu.VMEM((S, LANES), BF16), pltpu.VMEM((VT_ROWS, S), BF16),
                        pltpu.VMEM((T, T), F32), pltpu.VMEM((4, T, LANES), BF16)]
        + [pltpu.VMEM((S, T), F32) for _ in range(4)] + [
                        pltpu.VMEM((LANES, T), BF16), pltpu.VMEM((LANES, T), BF16),
                        pltpu.VMEM((VT_ROWS, T), F32), pltpu.VMEM((VT_ROWS, T), F32)],
        compiler_params=_cp("arbitrary", "arbitrary"),
        name="da_attention",
    )(slopes, qkv, qkv, qkv, lq1, lk1, lq2, lk2, subg)


def _post0_kernel(x_ref, o_ref, wo_ref, g1_ref, ng_ref, sc_ref, sh_ref, g2_ref, w1_ref, w3_ref, w2_ref,
                  out_ref, h_scr, acc_scr, *, parts):
    j = pl.program_id(1)
    tm = out_ref.shape[0]

    def out_proj(rows):
        return jnp.dot(o_ref[rows, :], wo_ref[...], preferred_element_type=F32)

    def residual_norm(rows, y):
        x1 = x_ref[rows, :] + g1_ref[...] * y
        out_ref[rows, :] = x1
        h_scr[rows, :] = _modnorm(x1, ng_ref[...], sc_ref[...], sh_ref[...]).astype(BF16)
        acc_scr[rows, :] = jnp.zeros((y.shape[0], D), F32)

    @pl.when(j == 0)
    def _():
        _row_pipeline(parts, tm, out_proj, residual_norm)

    def up(rows):
        h = h_scr[rows, :]
        return (jnp.dot(h, w1_ref[...], preferred_element_type=F32),
                jnp.dot(h, w3_ref[...], preferred_element_type=F32))

    def down(rows, ab):
        a, b = ab
        gact = (a * _sigmoid(a) * b).astype(BF16)
        acc_scr[rows, :] += jnp.dot(gact, w2_ref[...], preferred_element_type=F32)

    _row_pipeline(parts, tm, up, down)

    @pl.when(j == pl.num_programs(1) - 1)
    def _():
        out_ref[...] = out_ref[...] + g2_ref[...] * acc_scr[...]


def _post0(x2d, o2d, S, wo, g1, ng, sc, sh, g2, w1, w3, w2, tm=1024, tf=1408, parts=4):
    N = x2d.shape[0]
    F = w1.shape[1]
    vec = pl.BlockSpec((None, 1, D), lambda i, j: ((i * tm) // S, 0, 0))
    row = pl.BlockSpec((tm, D), lambda i, j: (i, 0))
    return pl.pallas_call(
        functools.partial(_post0_kernel, parts=parts),
        grid=(N // tm, F // tf),
        in_specs=[row, row,
                  pl.BlockSpec((D, D), lambda i, j: (0, 0)),
                  vec,
                  pl.BlockSpec((1, D), lambda i, j: (0, 0)),
                  vec, vec, vec,
                  pl.BlockSpec((D, tf), lambda i, j: (0, j)),
                  pl.BlockSpec((D, tf), lambda i, j: (0, j)),
                  pl.BlockSpec((tf, D), lambda i, j: (j, 0))],
        out_specs=row,
        out_shape=jax.ShapeDtypeStruct((N, D), F32),
        scratch_shapes=[pltpu.VMEM((tm, D), BF16), pltpu.VMEM((tm, D), F32)],
        compiler_params=_cp("arbitrary", "arbitrary"),
        name="post0_ffn",
    )(x2d, o2d, wo, g1, ng, sc, sh, g2, w1, w3, w2)


def _gla_inproj_kernel(x_ref, g_ref, sc_ref, sh_ref, w_ref, mult_ref, wa1_ref, wa2_ref, ba_ref,
                       o_ref, la_ref, h_scr, *, tn, n_plain_groups, parts):
    tm, n_out = o_ref.shape
    rp = tm // parts
    items = [(slice(p * rp, (p + 1) * rp), cg) for p in range(parts) for cg in range(n_out // tn)]

    def project(item):
        rows, cg = item
        if cg == 0:
            h = _modnorm(x_ref[rows, :], g_ref[...], sc_ref[...], sh_ref[...]).astype(BF16)
            h_scr[rows, :] = h
            low = jnp.dot(h, wa1_ref[...], preferred_element_type=F32).astype(BF16)
            z = jnp.dot(low, wa2_ref[...], preferred_element_type=F32) + ba_ref[...]
            log_sig = jnp.minimum(z, 0.0) - jnp.log(1.0 + jnp.exp(-jnp.abs(z)))
            la_ref[rows, :] = log_sig * (1.0 / GLA_TAU)
        else:
            h = h_scr[rows, :]
        return jnp.dot(h, w_ref[:, cg * tn:(cg + 1) * tn], preferred_element_type=F32)

    def finish(item, acc):
        rows, cg = item
        cols = slice(cg * tn, (cg + 1) * tn)
        if cg < n_plain_groups:
            o_ref[rows, cols] = (acc * mult_ref[:, cols]).astype(BF16)
        else:
            o_ref[rows, cols] = (acc * _sigmoid(acc)).astype(BF16)

    _pipeline(items, project, finish)


def _gla_inproj(x2d, S, g, sc, sh, w_bf, mult_row, wa1, wa2, ba, tm=1024, tn=1024, parts=4):
    N = x2d.shape[0]
    n_out = w_bf.shape[1]
    kw = wa2.shape[1]
    vec = pl.BlockSpec((None, 1, D), lambda i: ((i * tm) // S, 0, 0))
    return pl.pallas_call(
        functools.partial(_gla_inproj_kernel, tn=tn, n_plain_groups=(2 * D) // tn, parts=parts),
        grid=(N // tm,),
        in_specs=[pl.BlockSpec((tm, D), lambda i: (i, 0)),
                  pl.BlockSpec((1, D), lambda i: (0, 0)),
                  vec, vec,
                  pl.BlockSpec((D, n_out), lambda i: (0, 0)),
                  pl.BlockSpec((1, n_out), lambda i: (0, 0)),
                  pl.BlockSpec((D, LANES), lambda i: (0, 0)),
                  pl.BlockSpec((LANES, kw), lambda i: (0, 0)),
                  pl.BlockSpec((1, kw), lambda i: (0, 0))],
        out_specs=[pl.BlockSpec((tm, n_out), lambda i: (i, 0)),
                   pl.BlockSpec((tm, kw), lambda i: (i, 0))],
        out_shape=[jax.ShapeDtypeStruct((N, n_out), BF16), jax.ShapeDtypeStruct((N, kw), F32)],
        scratch_shapes=[pltpu.VMEM((tm, D), BF16)],
        compiler_params=_cp("arbitrary"),
        name="gla_inproj",
    )(x2d, g, sc, sh, w_bf, mult_row, wa1, wa2, ba)


def _gla_kernel(q_ref, k_ref, v_ref, r_ref, la_ref, tri_ref, og_ref, o_ref, state_scr, *, NB, SB):
    @pl.when(pl.program_id(1) == 0)
    def _():
        state_scr[...] = jnp.zeros_like(state_scr)

    tri = tri_ref[...]
    ti = lax.broadcasted_iota(jnp.int32, (CHUNK, CHUNK), 0)
    si = lax.broadcasted_iota(jnp.int32, (CHUNK, CHUNK), 1)
    causal = si <= ti
    og = og_ref[...]
    chains = [(nb, h) for nb in range(NB) for h in range(GLA_HEADS)]
    nt = (((1,), (1,)), ((), ()))
    tn = (((0,), (0,)), ((), ()))

    def chunk(c, carry):
        r0 = pl.multiple_of(c * CHUNK, CHUNK)
        rows = pl.ds(r0, CHUNK)
        kcol = [slice(h * GLA_DK, (h + 1) * GLA_DK) for _, h in chains]
        vcol = [slice(h * GLA_DV, (h + 1) * GLA_DV) for _, h in chains]
        las = [la_ref[nb, rows, kc] for (nb, _), kc in zip(chains, kcol)]
        his = [la.astype(BF16) for la in las]
        los = [(la - hi.astype(F32)).astype(BF16) for la, hi in zip(las, his)]
        bs = [jnp.dot(tri, hi, preferred_element_type=F32) + jnp.dot(tri, lo, preferred_element_type=F32)
              for hi, lo in zip(his, los)]
        b_lasts = [b[CHUNK - 1:CHUNK, :] for b in bs]
        qs = [q_ref[nb, rows, kc].astype(F32) for (nb, _), kc in zip(chains, kcol)]
        ks = [k_ref[nb, rows, kc].astype(F32) for (nb, _), kc in zip(chains, kcol)]
        vs = [v_ref[nb, rows, vc] for (nb, _), vc in zip(chains, vcol)]
        q_decs = [(q * jnp.exp(b)).astype(BF16) for q, b in zip(qs, bs)]
        k_invs = [(k * jnp.exp(-b)).astype(BF16) for k, b in zip(ks, bs)]
        k_decs = [(k * jnp.exp(bl - b)).astype(BF16) for k, b, bl in zip(ks, bs, b_lasts)]
        states = [state_scr[nb, h] for nb, h in chains]
        attns = [lax.dot_general(qd, ki, nt, preferred_element_type=F32) for qd, ki in zip(q_decs, k_invs)]
        o_inters = [lax.dot_general(qd, st.astype(BF16), nt, preferred_element_type=F32)
                    for qd, st in zip(q_decs, states)]
        kv_ts = [lax.dot_general(v, kd, tn, preferred_element_type=F32) for v, kd in zip(vs, k_decs)]
        attns = [jnp.where(causal, a, 0.0).astype(BF16) for a in attns]
        o_intras = [jnp.dot(a, v, preferred_element_type=F32) for a, v in zip(attns, vs)]
        for (nb, h), st, bl, kv_t, oi, oe, vc in zip(chains, states, b_lasts, kv_ts, o_intras, o_inters, vcol):
            state_scr[nb, h] = st * jnp.exp(bl) + kv_t
            o = oi + oe
            ms = jnp.mean(o * o, axis=-1, keepdims=True)
            o = o * lax.rsqrt(ms + EPS) * og * r_ref[nb, rows, vc].astype(F32)
            o_ref[nb, rows, vc] = o.astype(o_ref.dtype)
        return carry

    lax.fori_loop(0, SB // CHUNK, chunk, 0)


def _gla(qkvr, la, out_g, nb=4, sb=256):
    B, S, _ = qkvr.shape
    H = GLA_HEADS
    kw = H * GLA_DK
    vw = H * GLA_DV
    tri = (jnp.arange(CHUNK)[None, :] <= jnp.arange(CHUNK)[:, None]).astype(BF16)
    return pl.pallas_call(
        functools.partial(_gla_kernel, NB=nb, SB=sb),
        grid=(B // nb, S // sb),
        in_specs=[pl.BlockSpec((nb, sb, kw), lambda b, s: (b, s, 0)),
                  pl.BlockSpec((nb, sb, kw), lambda b, s: (b, s, 1)),
                  pl.BlockSpec((nb, sb, vw), lambda b, s: (b, s, (2 * kw) // vw)),
                  pl.BlockSpec((nb, sb, vw), lambda b, s: (b, s, (2 * kw) // vw + 1)),
                  pl.BlockSpec((nb, sb, kw), lambda b, s: (b, s, 0)),
                  pl.BlockSpec((CHUNK, CHUNK), lambda b, s: (0, 0)),
                  pl.BlockSpec((1, GLA_DV), lambda b, s: (0, 0))],
        out_specs=pl.BlockSpec((nb, sb, vw), lambda b, s: (b, s, 0)),
        out_shape=jax.ShapeDtypeStruct((B, S, vw), BF16),
        scratch_shapes=[pltpu.VMEM((nb, H, GLA_DV, GLA_DK), F32)],
        compiler_params=_cp("arbitrary", "arbitrary"),
        name="gla",
    )(qkvr, qkvr, qkvr, qkvr, la, tri, out_g)


def _post1_kernel(x_ref, o_ref, wo_ref, g1_ref, ng_ref, sc_ref, sh_ref, rw_ref, ltri_ref,
                  x3_ref, hp_ref, route_ref, cnt_ref, carry_scr, *, sub):
    i = pl.program_id(0)

    @pl.when(i == 0)
    def _():
        carry_scr[...] = jnp.zeros_like(carry_scr)

    tm = x3_ref.shape[0]
    lane = lax.broadcasted_iota(jnp.int32, (sub, LANES), 1)
    lanef = lane.astype(F32)

    def out_proj(rows):
        return jnp.dot(o_ref[rows, :], wo_ref[...], preferred_element_type=F32)

    def route_rows(rows, y):
        x3 = x_ref[rows, :] + g1_ref[...] * y
        x3_ref[rows, :] = x3
        hb = _modnorm(x3, ng_ref[...], sc_ref[...], sh_ref[...]).astype(BF16)

        half = D // 2
        lo = pltpu.bitcast(hb[:, :half].astype(F32), jnp.uint32)
        hi = pltpu.bitcast(hb[:, half:].astype(F32), jnp.uint32)
        hp_ref[rows, :] = jnp.right_shift(lo, jnp.uint32(16)) | (hi & jnp.uint32(0xFFFF0000))

        logits = jnp.dot(hb, rw_ref[...], preferred_element_type=F32)
        logits = jnp.where(lane < N_EXPERTS, logits, NEG)
        m1 = jnp.max(logits, axis=-1, keepdims=True)
        i1 = jnp.min(jnp.where(logits == m1, lanef, float(LANES)), axis=-1, keepdims=True)
        oh1 = lanef == i1
        rest = jnp.where(oh1, NEG, logits)
        m2 = jnp.max(rest, axis=-1, keepdims=True)
        i2 = jnp.min(jnp.where(rest == m2, lanef, float(LANES)), axis=-1, keepdims=True)
        oh2 = lanef == i2
        e = jnp.exp(m2 - m1)
        gate1 = 1.0 / (1.0 + e)
        gate2 = e / (1.0 + e)

        cnt = jnp.where(oh1, 1.0, 0.0) + jnp.where(oh2, 1.0, 0.0)
        carry = carry_scr[...]
        pre = jnp.dot(ltri_ref[...], cnt.astype(BF16), preferred_element_type=F32) + carry
        carry_scr[...] = carry + jnp.sum(cnt, axis=0, keepdims=True)
        r1 = jnp.sum(jnp.where(oh1, pre, 0.0), axis=-1, keepdims=True)
        r2 = jnp.sum(jnp.where(oh2, pre, 0.0), axis=-1, keepdims=True)
        route = jnp.where(lane == 0, i1, 0.0)
        for idx, val in ((1, i2), (2, r1), (3, r2), (4, gate1), (5, gate2)):
            route = jnp.where(lane == idx, val, route)
        route_ref[rows, :] = route

    _row_pipeline(tm // sub, tm, out_proj, route_rows)
    cnt_ref[...] = carry_scr[...]


def _post1(x2d, o2d, S, wo, g1, ng, sc, sh, router_pad, tm=1024, sub=256):
    N = x2d.shape[0]
    ltri = (jnp.arange(sub)[None, :] < jnp.arange(sub)[:, None]).astype(BF16)
    vec = pl.BlockSpec((None, 1, D), lambda i: ((i * tm) // S, 0, 0))
    row = pl.BlockSpec((tm, D), lambda i: (i, 0))
    return pl.pallas_call(
        functools.partial(_post1_kernel, sub=sub),
        grid=(N // tm,),
        in_specs=[row, row,
                  pl.BlockSpec((D, D), lambda i: (0, 0)),
                  vec,
                  pl.BlockSpec((1, D), lambda i: (0, 0)),
                  vec, vec,
                  pl.BlockSpec((D, LANES), lambda i: (0, 0)),
                  pl.BlockSpec((sub, sub), lambda i: (0, 0))],
        out_specs=[row,
                   pl.BlockSpec((tm, D // 2), lambda i: (i, 0)),
                   pl.BlockSpec((tm, LANES), lambda i: (i, 0)),
                   pl.BlockSpec((1, LANES), lambda i: (0, 0))],
        out_shape=[jax.ShapeDtypeStruct((N, D), F32),
                   jax.ShapeDtypeStruct((N, D // 2), jnp.uint32),
                   jax.ShapeDtypeStruct((N, LANES), F32),
                   jax.ShapeDtypeStruct((1, LANES), F32)],
        scratch_shapes=[pltpu.VMEM((1, LANES), F32)],
        compiler_params=_cp("arbitrary"),
        name="post1_router",
    )(x2d, o2d, wo, g1, ng, sc, sh, router_pad, ltri)


def _dispatch_kernel(pos_ref, h_ref, xs_in_ref, xs_ref, sem, *, td):
    del xs_in_ref

    def row(r, carry):
        for kk in range(2):
            p = pos_ref[0, 2 * r + kk]
            pltpu.make_async_copy(h_ref.at[pl.ds(r, 1)], xs_ref.at[pl.ds(p, 1)], sem).start()
        return carry

    lax.fori_loop(0, td, row, 0, unroll=8)
    for _ in range(2):
        pltpu.make_async_copy(h_ref, xs_ref.at[pl.ds(0, td)], sem).wait()


def _dispatch(hp, pos, P, td=512):
    N, W = hp.shape
    xs0 = jnp.zeros((P, W), jnp.uint32)
    return pl.pallas_call(
        functools.partial(_dispatch_kernel, td=td),
        grid=(N // td,),
        in_specs=[pl.BlockSpec((None, 1, 2 * td), lambda i: (i, 0, 0), memory_space=pltpu.SMEM),
                  pl.BlockSpec((td, W), lambda i: (i, 0)),
                  pl.BlockSpec(memory_space=pl.ANY)],
        out_specs=pl.BlockSpec(memory_space=pl.ANY),
        out_shape=jax.ShapeDtypeStruct((P, W), jnp.uint32),
        scratch_shapes=[pltpu.SemaphoreType.DMA(())],
        input_output_aliases={2: 0},
        compiler_params=_cp("arbitrary"),
        name="moe_dispatch",
    )(pos.reshape(N // td, 1, 2 * td), hp, xs0)


def _experts_kernel(be_ref, nv_ref, xs_ref, w1_ref, w3_ref, w2_ref, c1_ref, c3_ref, c2_ref, y_ref,
                    xq_scr, xr_scr, *, parts):
    i = pl.program_id(0)
    j = pl.program_id(1)
    valid = i < nv_ref[0]
    tm = y_ref.shape[0]

    @pl.when(valid & (j == 0))
    def _():
        w = xs_ref[...]
        half = D // 2
        lo = pltpu.bitcast(jnp.left_shift(w, jnp.uint32(16)), F32)
        hi = pltpu.bitcast(w & jnp.uint32(0xFFFF0000), F32)
        amax = jnp.maximum(jnp.maximum(jnp.max(jnp.abs(lo), axis=-1, keepdims=True),
                                       jnp.max(jnp.abs(hi), axis=-1, keepdims=True)), FP8_TINY)
        scale = FP8_TOP / amax
        xq_scr[:, :half] = (lo * scale).astype(FP8)
        xq_scr[:, half:] = (hi * scale).astype(FP8)
        xr_scr[...] = jnp.broadcast_to(amax * (1.0 / FP8_TOP), xr_scr.shape)
        y_ref[...] = jnp.zeros_like(y_ref)

    @pl.when(jnp.logical_not(valid) & (j == 0))
    def _():
        y_ref[...] = jnp.zeros_like(y_ref)

    def up(rows):
        xq = xq_scr[rows, :]
        return (jnp.dot(xq, w1_ref[...], preferred_element_type=F32),
                jnp.dot(xq, w3_ref[...], preferred_element_type=F32))

    def down(rows, ab):
        x_unscale = xr_scr[rows, :1]
        a = ab[0] * x_unscale * c1_ref[...]
        b = ab[1] * x_unscale * c3_ref[...]
        g = a * _sigmoid(a) * b
        g_scale, g_unscale = _row_scale(g)
        acc = jnp.dot((g * g_scale).astype(FP8), w2_ref[...], preferred_element_type=F32)
        y_ref[rows, :] += acc * g_unscale * c2_ref[...]

    @pl.when(valid)
    def _():
        _row_pipeline(parts, tm, up, down)


def _experts(xs, blk_expert, n_valid, w1, w3, w2, tm, tf=1792, parts=4):
    P, W = xs.shape
    E, _, F = w1.shape
    nblk = P // tm
    nj = F // tf
    w1, c1 = _quantize_columns(w1)
    w3, c3 = _quantize_columns(w3)
    w2, c2 = _quantize_columns(w2)

    def _i(i, nv):
        return jnp.minimum(i, nv[0] - 1)

    def _j(i, j, nv):
        return jnp.where(i < nv[0], j, nj - 1)

    grid_spec = pltpu.PrefetchScalarGridSpec(
        num_scalar_prefetch=2,
        grid=(nblk, nj),
        in_specs=[pl.BlockSpec((tm, W), lambda i, j, be, nv: (_i(i, nv), 0)),
                  pl.BlockSpec((None, D, tf), lambda i, j, be, nv: (be[_i(i, nv)], 0, _j(i, j, nv))),
                  pl.BlockSpec((None, D, tf), lambda i, j, be, nv: (be[_i(i, nv)], 0, _j(i, j, nv))),
                  pl.BlockSpec((None, tf, D), lambda i, j, be, nv: (be[_i(i, nv)], _j(i, j, nv), 0)),
                  pl.BlockSpec((None, 1, tf), lambda i, j, be, nv: (be[_i(i, nv)], 0, _j(i, j, nv))),
                  pl.BlockSpec((None, 1, tf), lambda i, j, be, nv: (be[_i(i, nv)], 0, _j(i, j, nv))),
                  pl.BlockSpec((None, 1, D), lambda i, j, be, nv: (be[_i(i, nv)], 0, 0))],
        out_specs=pl.BlockSpec((tm, D), lambda i, j, be, nv: (i, 0)),
        scratch_shapes=[pltpu.VMEM((tm, D), FP8), pltpu.VMEM((tm, LANES), F32)],
    )
    return pl.pallas_call(
        functools.partial(_experts_kernel, parts=parts),
        grid_spec=grid_spec,
        out_shape=jax.ShapeDtypeStruct((P, D), F32),
        compiler_params=_cp("arbitrary", "arbitrary"),
        name="moe_experts",
    )(blk_expert, n_valid, xs, w1, w3, w2, c1, c3, c2)


def _combine_kernel(pos_ref, pos_next_ref, route_ref, x_ref, g2_ref, y_ref, out_ref, ybuf, sems, *, tc):
    i = pl.program_id(0)
    slot = jnp.bitwise_and(i, 1)

    def gather_copy(p, s, kk, r):
        return pltpu.make_async_copy(y_ref.at[pl.ds(p, 1)], ybuf.at[s, kk, pl.ds(r, 1)], sems.at[s])

    def start_gathers(table_ref, s):
        def row(r, carry):
            for kk in range(2):
                gather_copy(table_ref[0, 2 * r + kk], s, kk, r).start()
            return carry

        lax.fori_loop(0, tc, row, 0, unroll=8)

    @pl.when(i == 0)
    def _():
        start_gathers(pos_ref, 0)

    @pl.when(i + 1 < pl.num_programs(0))
    def _():
        start_gathers(pos_next_ref, 1 - slot)

    for kk in range(2):
        pltpu.make_async_copy(y_ref.at[pl.ds(0, tc)], ybuf.at[slot, kk], sems.at[slot]).wait()
    route = route_ref[...]
    moe = route[:, 4:5] * ybuf[slot, 0] + route[:, 5:6] * ybuf[slot, 1]
    out_ref[...] = x_ref[...] + g2_ref[...] * moe


def _combine(y, pos, route, x3, S, g2, tc=512):
    N = x3.shape[0]
    n_steps = N // tc
    pos_blocks = pos.reshape(n_steps, 1, 2 * tc)
    return pl.pallas_call(
        functools.partial(_combine_kernel, tc=tc),
        grid=(n_steps,),
        in_specs=[pl.BlockSpec((None, 1, 2 * tc), lambda i: (i, 0, 0), memory_space=pltpu.SMEM),
                  pl.BlockSpec((None, 1, 2 * tc), lambda i: (jnp.minimum(i + 1, n_steps - 1), 0, 0),
                               memory_space=pltpu.SMEM),
                  pl.BlockSpec((tc, LANES), lambda i: (i, 0)),
                  pl.BlockSpec((tc, D), lambda i: (i, 0)),
                  pl.BlockSpec((None, 1, D), lambda i: ((i * tc) // S, 0, 0)),
                  pl.BlockSpec(memory_space=pl.ANY)],
        out_specs=pl.BlockSpec((tc, D), lambda i: (i, 0)),
        out_shape=jax.ShapeDtypeStruct((N, D), F32),
        scratch_shapes=[pltpu.VMEM((2, 2, tc, D), F32), pltpu.SemaphoreType.DMA((2,))],
        compiler_params=_cp("arbitrary"),
        name="moe_combine",
    )(pos_blocks, pos_blocks, route, x3, g2, y)


def _moe(hp, route, counts, x3, S, g2, w1, w3, w2, tm=1024):
    N = x3.shape[0]
    cnt = counts[0, :N_EXPERTS].astype(jnp.int32)
    nblk_e = (cnt + tm - 1) // tm
    blk_end = jnp.cumsum(nblk_e)
    row_start = (blk_end - nblk_e) * tm
    e_idx = route[:, 0:2].astype(jnp.int32)
    rank = route[:, 2:4].astype(jnp.int32)
    pos = row_start[e_idx] + rank
    P = 2 * N + N_EXPERTS * tm
    nblk = P // tm
    blk_expert = jnp.minimum(
        jnp.sum((jnp.arange(nblk, dtype=jnp.int32)[:, None] >= blk_end[None, :]).astype(jnp.int32), axis=1),
        N_EXPERTS - 1)
    n_valid = blk_end[-1:].astype(jnp.int32)
    xs = _dispatch(hp, pos, P)
    y = _experts(xs, blk_expert, n_valid, w1, w3, w2, tm)
    return _combine(y, pos, route, x3, S, g2)


def kernel(x, c, ada_w, ada_b, norm1_g, norm2_g, da_w_in, da_q_gain, da_k_gain, da_lam_q1, da_lam_k1, da_lam_q2,
           da_lam_k2, da_subln_g, da_w_out, gla_w_in, gla_w_a1, gla_w_a2, gla_b_a, gla_out_g, gla_w_out,
           ffn_w1, ffn_w3, ffn_w2, moe_router, moe_w1, moe_w3, moe_w2):
    B, S, _ = x.shape
    N = B * S
    mod = _adaln(c, ada_w, ada_b)
    mods = [[mod[l, :, k * D:(k + 1) * D].reshape(B, 1, D) for k in range(6)] for l in range(2)]
    x2d = x.reshape(N, D)

    sh1, sc1, gt1, sh2, sc2, gt2 = mods[0]
    lambda_init = 0.8 - 0.6 * math.exp(-0.3 * 0)
    qk_scale = DA_HEAD_DIM ** -0.5 * LOG2E
    gain_row = jnp.concatenate([jnp.tile(da_q_gain[0].reshape(-1) * qk_scale, DA_HEADS),
                                jnp.tile(da_k_gain[0].reshape(-1), DA_HEADS),
                                jnp.ones((D,), F32)]).reshape(1, 3 * D)
    qkv = _da_inproj(x2d, S, norm1_g[0].reshape(1, D), sc1, sh1, da_w_in[0].astype(BF16), gain_row)
    slopes = 2.0 ** (-8.0 * jnp.arange(1, DA_HEADS + 1, dtype=F32) / DA_HEADS)
    o = _da_attention(qkv.reshape(B, S, 3 * D), slopes,
                      da_lam_q1[0].reshape(1, -1), da_lam_k1[0].reshape(1, -1),
                      da_lam_q2[0].reshape(1, -1), da_lam_k2[0].reshape(1, -1),
                      da_subln_g[0].reshape(1, -1), lambda_init)
    x2d = _post0(x2d, o.reshape(N, D), S, da_w_out[0].astype(BF16), gt1, norm2_g[0].reshape(1, D), sc2, sh2, gt2,
                 ffn_w1[0].astype(BF16), ffn_w3[0].astype(BF16), ffn_w2[0].astype(BF16))

    sh1, sc1, gt1, sh2, sc2, gt2 = mods[1]
    kw = GLA_HEADS * GLA_DK
    mult_row = jnp.concatenate([jnp.full((kw,), GLA_DK ** -0.5, F32), jnp.ones((3 * D - kw,), F32)]).reshape(1, -1)
    rank = gla_w_a1.shape[-1]
    wa1 = jnp.zeros((D, LANES), BF16).at[:, :rank].set(gla_w_a1[0].astype(BF16))
    wa2 = jnp.zeros((LANES, kw), BF16).at[:rank, :].set(gla_w_a2[0].astype(BF16))
    qkvr, la = _gla_inproj(x2d, S, norm1_g[1].reshape(1, D), sc1, sh1, gla_w_in[0].astype(BF16), mult_row,
                           wa1, wa2, gla_b_a[0].reshape(1, kw))
    o = _gla(qkvr.reshape(B, S, 3 * D), la.reshape(B, S, kw), gla_out_g[0].reshape(1, -1))
    router_pad = jnp.zeros((D, LANES), BF16).at[:, :N_EXPERTS].set(moe_router[0].astype(BF16))
    x3, hp, route, counts = _post1(x2d, o.reshape(N, D), S, gla_w_out[0].astype(BF16), gt1,
                                   norm2_g[1].reshape(1, D), sc2, sh2, router_pad)
    out = _moe(hp, route, counts, x3, S, gt2,
               moe_w1[0], moe_w3[0], moe_w2[0])
    return out.reshape(B, S, D)
```

```python
import functools
import math

import jax
import jax.numpy as jnp
from jax import lax
from jax.experimental import pallas as pl
from jax.experimental.pallas import tpu as pltpu

F32 = jnp.float32
BF16 = jnp.bfloat16
FP8 = jnp.float8_e4m3fn
FP8_TOP = 256.0
FP8_TINY = 1e-30

D = 1024
EPS = 1e-6
NEG = -1e30
CHUNK = 64
DA_HEADS = 8
DA_HEAD_DIM = 64
GLA_HEADS = 4
GLA_DK = 128
GLA_DV = 256
GLA_TAU = 16.0
N_EXPERTS = 8
LANES = 128
LOG2E = 1.4426950408889634
VT_ROWS = LANES + 16

VMEM_LIMIT = 56 * 1024 * 1024


def _cp(*sem):
    return pltpu.CompilerParams(dimension_semantics=sem, vmem_limit_bytes=VMEM_LIMIT)


def _sigmoid(x):
    return 1.0 / (1.0 + jnp.exp(-x))


def _modnorm(x, g, sc, sh):
    ms = jnp.mean(x * x, axis=-1, keepdims=True)
    return (x * lax.rsqrt(ms + EPS) * g) * (1.0 + sc) + sh


def _row_scale(v):
    amax = jnp.maximum(jnp.max(jnp.abs(v), axis=-1, keepdims=True), FP8_TINY)
    return FP8_TOP / amax, amax * (1.0 / FP8_TOP)


def _quantize_columns(w):
    amax = jnp.maximum(jnp.max(jnp.abs(w), axis=1, keepdims=True), FP8_TINY)
    return (w * (FP8_TOP / amax)).astype(FP8), amax * (1.0 / FP8_TOP)


def _adaln_kernel(c_ref, w_ref, b_ref, o_ref):
    c = c_ref[...]
    ca = (c * _sigmoid(c)).astype(BF16)
    o_ref[...] = jnp.dot(ca, w_ref[...].astype(BF16), preferred_element_type=F32) + b_ref[...]


def _adaln(c, ada_w, ada_b):
    L, _, n6 = ada_w.shape
    B = c.shape[0]
    tn = 1536
    return pl.pallas_call(
        _adaln_kernel,
        grid=(L, n6 // tn),
        in_specs=[pl.BlockSpec((B, D), lambda l, n: (0, 0)),
                  pl.BlockSpec((None, D, tn), lambda l, n: (l, 0, n)),
                  pl.BlockSpec((None, 1, tn), lambda l, n: (l, 0, n))],
        out_specs=pl.BlockSpec((None, B, tn), lambda l, n: (l, 0, n)),
        out_shape=jax.ShapeDtypeStruct((L, B, n6), F32),
        compiler_params=_cp("arbitrary", "arbitrary"),
        name="adaln",
    )(c, ada_w, ada_b.reshape(L, 1, n6))


def _pipeline(items, matmul, finish):
    prev = None
    for item in items:
        acc = matmul(item)
        if prev is not None:
            finish(*prev)
        prev = (item, acc)
    finish(*prev)


def _row_pipeline(parts, rows_total, matmul, finish):
    rp = rows_total // parts
    _pipeline([slice(p * rp, (p + 1) * rp) for p in range(parts)], matmul, finish)


def _da_inproj_kernel(x_ref, g_ref, sc_ref, sh_ref, w_ref, gain_ref, gsum_ref, o_ref, h_scr,
                      *, tn, n_qk_groups, parts):
    tm, n_out = o_ref.shape
    rp = tm // parts
    items = [(slice(p * rp, (p + 1) * rp), cg) for p in range(parts) for cg in range(n_out // tn)]

    def project(item):
        rows, cg = item
        if cg == 0:
            h = _modnorm(x_ref[rows, :], g_ref[...], sc_ref[...], sh_ref[...]).astype(BF16)
            h_scr[rows, :] = h
        else:
            h = h_scr[rows, :]
        return jnp.dot(h, w_ref[:, cg * tn:(cg + 1) * tn], preferred_element_type=F32)

    def finish(item, acc):
        rows, cg = item
        cols = slice(cg * tn, (cg + 1) * tn)
        if cg < n_qk_groups:
            y2 = (acc * acc).astype(BF16)
            gsum = gsum_ref[...]
            ms = jnp.concatenate(
                [jnp.dot(y2[:, c * 256:(c + 1) * 256], gsum, preferred_element_type=F32) for c in range(tn // 256)],
                axis=1)
            acc = acc * lax.rsqrt(ms + EPS)
        o_ref[rows, cols] = (acc * gain_ref[:, cols]).astype(BF16)

    _pipeline(items, project, finish)


def _da_inproj(x2d, S, g, sc, sh, w_bf, gain_row, tm=1024, tn=1024, parts=4):
    N = x2d.shape[0]
    n_out = w_bf.shape[1]
    r = jnp.arange(256) // DA_HEAD_DIM
    gsum = jnp.where(r[:, None] == r[None, :], 1.0 / DA_HEAD_DIM, 0.0).astype(BF16)
    vec = pl.BlockSpec((None, 1, D), lambda i: ((i * tm) // S, 0, 0))
    return pl.pallas_call(
        functools.partial(_da_inproj_kernel, tn=tn, n_qk_groups=(2 * D) // tn, parts=parts),
        grid=(N // tm,),
        in_specs=[pl.BlockSpec((tm, D), lambda i: (i, 0)),
                  pl.BlockSpec((1, D), lambda i: (0, 0)),
                  vec, vec,
                  pl.BlockSpec((D, n_out), lambda i: (0, 0)),
                  pl.BlockSpec((1, n_out), lambda i: (0, 0)),
                  pl.BlockSpec((256, 256), lambda i: (0, 0))],
        out_specs=pl.BlockSpec((tm, n_out), lambda i: (i, 0)),
        out_shape=jax.ShapeDtypeStruct((N, n_out), BF16),
        scratch_shapes=[pltpu.VMEM((tm, D), BF16)],
        compiler_params=_cp("arbitrary"),
        name="da_inproj",
    )(x2d, g, sc, sh, w_bf, gain_row, gsum)


def _da_attn_kernel(slope_ref, q_ref, k_ref, v_ref, lq1_ref, lk1_ref, lq2_ref, lk2_ref, subg_ref, o_ref,
                    k1_scr, k2_scr, vt_scr, corr_scr, aug_scr, s1e_scr, s2e_scr, s1o_scr, s2o_scr, qt1_scr, qt2_scr, a1, a2,
                    *, T, SUB, S, lambda_init):
    h = pl.program_id(0)
    slope2 = slope_ref[h] * LOG2E
    lane = lax.broadcasted_iota(jnp.int32, (T, LANES), 1)
    first_half = lane < DA_HEAD_DIM

    @pl.when(pl.program_id(1) == 0)
    def _():
        local = lax.broadcasted_iota(jnp.int32, (T, LANES), 0)
        lo = jnp.bitwise_and(local, 255).astype(F32)
        hi = (local - jnp.bitwise_and(local, 255)).astype(F32)

        def aug(base, vals):
            out = jnp.zeros((T, LANES), F32)
            for off, val in enumerate(vals):
                out = jnp.where(lane == base + off, val, out)
            return out.astype(BF16)

        c_hi = (slope2 + jnp.zeros((T, LANES), F32)).astype(BF16).astype(F32)
        c_mid = (slope2 - c_hi).astype(BF16).astype(F32)
        c_lo = slope2 - c_hi - c_mid
        k_vals = (lo, lo, lo, hi, hi, hi, -c_hi, -c_mid, -c_lo, -c_hi, -c_mid, -c_lo)
        q_vals = (c_hi, c_mid, c_lo, c_hi, c_mid, c_lo, lo, lo, lo, hi, hi, hi)
        aug_scr[0] = aug(DA_HEAD_DIM, k_vals)
        aug_scr[1] = aug(0, k_vals)
        aug_scr[2] = aug(DA_HEAD_DIM, q_vals)
        aug_scr[3] = aug(0, q_vals)
        kj = lax.broadcasted_iota(jnp.int32, (T, T), 0)
        qj = lax.broadcasted_iota(jnp.int32, (T, T), 1)
        visible = jnp.right_shift(kj, 6) <= jnp.right_shift(qj, 6)
        ahead = jnp.maximum(kj - qj, 0).astype(F32)
        corr_scr[...] = jnp.where(visible, -2.0 * slope2 * ahead, NEG)

    k_aug1, k_aug2, q_aug1, q_aug2 = aug_scr[0], aug_scr[1], aug_scr[2], aug_scr[3]
    ones_rows = jnp.where(lax.broadcasted_iota(jnp.int32, (VT_ROWS - LANES, T), 0) == 0, 1.0, 0.0).astype(BF16)

    def prep(j, carry):
        r0 = pl.multiple_of(j * T, T)
        k = k_ref[pl.ds(r0, T), :]
        k1_scr[pl.ds(r0, T), :] = jnp.where(first_half, k, k_aug1)
        k2_scr[pl.ds(r0, T), :] = jnp.where(first_half, k_aug2, k)
        vt_scr[:LANES, pl.ds(r0, T)] = v_ref[pl.ds(r0, T), :].astype(F32).T.astype(BF16)
        vt_scr[LANES:, pl.ds(r0, T)] = ones_rows
        return carry

    lax.fori_loop(0, S // T, prep, 0)

    lam =(jnp.exp(jnp.sum(lq1_ref[...] * lk1_ref[...], axis=-1, keepdims=True))
           - jnp.exp(jnp.sum(lq2_ref[...] * lk2_ref[...], axis=-1, keepdims=True)) + lambda_init)
    k_scrs = (k1_scr, k2_scr)
    s_scrs = ((s1e_scr, s2e_scr), (s1o_scr, s2o_scr))
    qt_scrs = (qt1_scr, qt2_scr)
    accs = (a1, a2)
    n_q = S // T
    TK = T // SUB
    neg = jnp.full((1, T), NEG, F32)

    def load_queries(qi):
        q = q_ref[pl.ds(pl.multiple_of(qi * T, T), T), :]
        for qt_scr, qa in zip(qt_scrs, (jnp.where(first_half, q, q_aug1), jnp.where(first_half, q_aug2, q))):
            qt_scr[...] = qa.astype(F32).T.astype(BF16)

    def block_shift(qi, j):
        return -slope2 * lax.convert_element_type((qi - j) * T, F32)

    def score_chunk(qi, par, j, part, ms, corr):
        k0 = pl.multiple_of(j * T + part * TK, TK)
        out = []
        for mp in range(2):
            s = jnp.dot(k_scrs[mp][pl.ds(k0, TK), :], qt_scrs[mp][...], preferred_element_type=F32)
            if corr is not None:
                s = s + corr[part * TK:(part + 1) * TK, :]
            s_scrs[par][mp][pl.ds(k0, TK), :] = s
            out.append(jnp.maximum(ms[mp], jnp.max(s, axis=0, keepdims=True) + block_shift(qi, j)))
        return tuple(out)

    def weight_chunk(qi, par, j, part, ms):
        k0 = pl.multiple_of(j * T + part * TK, TK)
        vt = vt_scr[:, pl.ds(k0, TK)]
        for mp in range(2):
            p = jnp.exp2(s_scrs[par][mp][pl.ds(k0, TK), :] + (block_shift(qi, j) - ms[mp]))
            accs[mp][...] += jnp.dot(vt, p.astype(BF16), preferred_element_type=F32)

    def finalize(qi):
        acc1 = a1[...]
        acc2 = a2[...]
        o = (acc1[:LANES] / acc1[LANES:LANES + 1] - lam * (acc2[:LANES] / acc2[LANES:LANES + 1])).T
        msq = jnp.mean(o * o, axis=-1, keepdims=True)
        o = o * lax.rsqrt(msq + EPS) * subg_ref[...] * (1.0 - lambda_init)
        o_ref[pl.ds(pl.multiple_of(qi * T, T), T), :] = o.astype(o_ref.dtype)
        a1[...] = jnp.zeros_like(a1)
        a2[...] = jnp.zeros_like(a2)

    a1[...] = jnp.zeros_like(a1)
    a2[...] = jnp.zeros_like(a2)
    load_queries(0)
    ms0 = (neg, neg)
    for part in range(SUB):
        ms0 = score_chunk(0, 0, 0, part, ms0, corr_scr)

    def q_block(qi, par, ms):
        load_queries(qi + 1)

        def chunk(j, ms_next):
            for part in range(SUB):
                ms_next = score_chunk(qi + 1, 1 - par, j, part, ms_next, None)
                weight_chunk(qi, par, j, part, ms)
            return ms_next

        def chunk_pair(jj, ms_next):
            return chunk(2 * jj + 1, chunk(2 * jj, ms_next))

        ms_next = lax.fori_loop(0, jnp.right_shift(qi + 1, 1), chunk_pair, (neg, neg))
        if par == 0:
            ms_next = chunk(qi, ms_next)
        for part in range(SUB):
            ms_next = score_chunk(qi + 1, 1 - par, qi + 1, part, ms_next, corr_scr)
        finalize(qi)
        return ms_next

    def q_pair(qq, ms):
        return q_block(2 * qq + 1, 1, q_block(2 * qq, 0, ms))

    ms_even = lax.fori_loop(0, n_q // 2 - 1, q_pair, ms0)
    ms_last = q_block(n_q - 2, 0, ms_even)

    def last_chunk_pair(jj, carry):
        for j in (2 * jj, 2 * jj + 1):
            for part in range(SUB):
                weight_chunk(n_q - 1, 1, j, part, ms_last)
        return carry

    lax.fori_loop(0, n_q // 2, last_chunk_pair, 0)
    finalize(n_q - 1)


def _da_attention(qkv, slopes, lq1, lk1, lq2, lk2, subg, lambda_init, T=512, sub=2):
    B, S, _ = qkv.shape
    H = DA_HEADS
    vec64 = pl.BlockSpec((1, DA_HEAD_DIM), lambda h, b: (0, 0))
    return pl.pallas_call(
        functools.partial(_da_attn_kernel, T=T, SUB=sub, S=S, lambda_init=lambda_init),
        grid=(H, B),
        in_specs=[pl.BlockSpec(memory_space=pltpu.SMEM),
                  pl.BlockSpec((None, S, LANES), lambda h, b: (b, 0, h)),
                  pl.BlockSpec((None, S, LANES), lambda h, b: (b, 0, H + h)),
                  pl.BlockSpec((None, S, LANES), lambda h, b: (b, 0, 2 * H + h)),
                  vec64, vec64, vec64, vec64,
                  pl.BlockSpec((1, LANES), lambda h, b: (0, 0))],
        out_specs=pl.BlockSpec((None, S, LANES), lambda h, b: (b, 0, h)),
        out_shape=jax.ShapeDtypeStruct((B, S, H * LANES), BF16),
        scratch_shapes=[pltpu.VMEM((S, LANES), BF16), pltpu.VMEM((S, LANES), BF16), pltpu.VMEM((VT_ROWS, S), BF16),
                        pltpu.VMEM((T, T), F32), pltpu.VMEM((4, T, LANES), BF16)]
        + [pltpu.VMEM((S, T), F32) for _ in range(4)] + [
                        pltpu.VMEM((LANES, T), BF16), pltpu.VMEM((LANES, T), BF16),
                        pltpu.VMEM((VT_ROWS, T), F32), pltpu.VMEM((VT_ROWS, T), F32)],
        compiler_params=_cp("arbitrary", "arbitrary"),
        name="da_attention",
    )(slopes, qkv, qkv, qkv, lq1, lk1, lq2, lk2, subg)


def _post0_kernel(x_ref, o_ref, wo_ref, g1_ref, ng_ref, sc_ref, sh_ref, g2_ref, w1_ref, w3_ref, w2_ref,
                  out_ref, h_scr, acc_scr, *, parts, tf):
    tm = out_ref.shape[0]
    n_ft = w1_ref.shape[1] // tf
    rp = tm // parts

    def out_proj(rows):
        return jnp.dot(o_ref[rows, :], wo_ref[...], preferred_element_type=F32)

    def residual_norm(rows, y):
        x1 = x_ref[rows, :] + g1_ref[...] * y
        out_ref[rows, :] = x1
        h_scr[rows, :] = _modnorm(x1, ng_ref[...], sc_ref[...], sh_ref[...]).astype(BF16)

    _row_pipeline(parts, tm, out_proj, residual_norm)

    def up(item):
        rows, ft = item
        h = h_scr[rows, :]
        cols = slice(ft * tf, (ft + 1) * tf)
        return (jnp.dot(h, w1_ref[:, cols], preferred_element_type=F32),
                jnp.dot(h, w3_ref[:, cols], preferred_element_type=F32))

    def down(item, ab):
        rows, ft = item
        a, b = ab
        gact = (a * _sigmoid(a) * b).astype(BF16)
        part = jnp.dot(gact, w2_ref[ft * tf:(ft + 1) * tf, :], preferred_element_type=F32)
        total = part if ft == 0 else acc_scr[rows, :] + part
        if ft == n_ft - 1:
            out_ref[rows, :] = out_ref[rows, :] + g2_ref[...] * total
        else:
            acc_scr[rows, :] = total

    _pipeline([(slice(p * rp, (p + 1) * rp), ft) for p in range(parts) for ft in range(n_ft)], up, down)


def _post0(x2d, o2d, S, wo, g1, ng, sc, sh, g2, w1, w3, w2, tm=1024, tf=1408, parts=4):
    N = x2d.shape[0]
    F = w1.shape[1]
    vec = pl.BlockSpec((None, 1, D), lambda i: ((i * tm) // S, 0, 0))
    row = pl.BlockSpec((tm, D), lambda i: (i, 0))

    def resident(shape):
        return pl.BlockSpec(shape, lambda i: (0, 0), pipeline_mode=pl.Buffered(1))

    return pl.pallas_call(
        functools.partial(_post0_kernel, parts=parts, tf=tf),
        grid=(N // tm,),
        in_specs=[row, row,
                  resident((D, D)),
                  vec,
                  pl.BlockSpec((1, D), lambda i: (0, 0)),
                  vec, vec, vec,
                  resident((D, F)), resident((D, F)), resident((F, D))],
        out_specs=row,
        out_shape=jax.ShapeDtypeStruct((N, D), F32),
        scratch_shapes=[pltpu.VMEM((tm, D), BF16), pltpu.VMEM((tm, D), F32)],
        compiler_params=_cp("arbitrary"),
        name="post0_ffn",
    )(x2d, o2d, wo, g1, ng, sc, sh, g2, w1, w3, w2)


def _gla_inproj_kernel(x_ref, g_ref, sc_ref, sh_ref, w_ref, mult_ref, wa1_ref, wa2_ref, ba_ref,
                       o_ref, la_ref, h_scr, *, tn, n_plain_groups, parts):
    tm, n_out = o_ref.shape
    rp = tm // parts
    items = [(slice(p * rp, (p + 1) * rp), cg) for p in range(parts) for cg in range(n_out // tn)]

    def project(item):
        rows, cg = item
        if cg == 0:
            h = _modnorm(x_ref[rows, :], g_ref[...], sc_ref[...], sh_ref[...]).astype(BF16)
            h_scr[rows, :] = h
            low = jnp.dot(h, wa1_ref[...], preferred_element_type=F32).astype(BF16)
            z = jnp.dot(low, wa2_ref[...], preferred_element_type=F32) + ba_ref[...]
            log_sig = jnp.minimum(z, 0.0) - jnp.log(1.0 + jnp.exp(-jnp.abs(z)))
            la_ref[rows, :] = log_sig * (1.0 / GLA_TAU)
        else:
            h = h_scr[rows, :]
        return jnp.dot(h, w_ref[:, cg * tn:(cg + 1) * tn], preferred_element_type=F32)

    def finish(item, acc):
        rows, cg = item
        cols = slice(cg * tn, (cg + 1) * tn)
        if cg < n_plain_groups:
            o_ref[rows, cols] = (acc * mult_ref[:, cols]).astype(BF16)
        else:
            o_ref[rows, cols] = (acc * _sigmoid(acc)).astype(BF16)

    _pipeline(items, project, finish)


def _gla_inproj(x2d, S, g, sc, sh, w_bf, mult_row, wa1, wa2, ba, tm=1024, tn=1024, parts=4):
    N = x2d.shape[0]
    n_out = w_bf.shape[1]
    kw = wa2.shape[1]
    vec = pl.BlockSpec((None, 1, D), lambda i: ((i * tm) // S, 0, 0))
    return pl.pallas_call(
        functools.partial(_gla_inproj_kernel, tn=tn, n_plain_groups=(2 * D) // tn, parts=parts),
        grid=(N // tm,),
        in_specs=[pl.BlockSpec((tm, D), lambda i: (i, 0)),
                  pl.BlockSpec((1, D), lambda i: (0, 0)),
                  vec, vec,
                  pl.BlockSpec((D, n_out), lambda i: (0, 0)),
                  pl.BlockSpec((1, n_out), lambda i: (0, 0)),
                  pl.BlockSpec((D, LANES), lambda i: (0, 0)),
                  pl.BlockSpec((LANES, kw), lambda i: (0, 0)),
                  pl.BlockSpec((1, kw), lambda i: (0, 0))],
        out_specs=[pl.BlockSpec((tm, n_out), lambda i: (i, 0)),
                   pl.BlockSpec((tm, kw), lambda i: (i, 0))],
        out_shape=[jax.ShapeDtypeStruct((N, n_out), BF16), jax.ShapeDtypeStruct((N, kw), F32)],
        scratch_shapes=[pltpu.VMEM((tm, D), BF16)],
        compiler_params=_cp("arbitrary"),
        name="gla_inproj",
    )(x2d, g, sc, sh, w_bf, mult_row, wa1, wa2, ba)


def _gla_kernel(q_ref, k_ref, v_ref, r_ref, la_ref, tri_ref, og_ref, o_ref, state_scr, *, NB, SB):
    @pl.when(pl.program_id(1) == 0)
    def _():
        state_scr[...] = jnp.zeros_like(state_scr)

    tri = tri_ref[...]
    ti = lax.broadcasted_iota(jnp.int32, (CHUNK, CHUNK), 0)
    si = lax.broadcasted_iota(jnp.int32, (CHUNK, CHUNK), 1)
    causal = si <= ti
    og = og_ref[...]
    chains = [(nb, h) for nb in range(NB) for h in range(GLA_HEADS)]
    nt = (((1,), (1,)), ((), ()))
    tn = (((0,), (0,)), ((), ()))

    def chunk(c, carry):
        r0 = pl.multiple_of(c * CHUNK, CHUNK)
        rows = pl.ds(r0, CHUNK)
        kcol = [slice(h * GLA_DK, (h + 1) * GLA_DK) for _, h in chains]
        vcol = [slice(h * GLA_DV, (h + 1) * GLA_DV) for _, h in chains]
        las = [la_ref[nb, rows, kc] for (nb, _), kc in zip(chains, kcol)]
        his = [la.astype(BF16) for la in las]
        los = [(la - hi.astype(F32)).astype(BF16) for la, hi in zip(las, his)]
        bs = [jnp.dot(tri, hi, preferred_element_type=F32) + jnp.dot(tri, lo, preferred_element_type=F32)
              for hi, lo in zip(his, los)]
        b_lasts = [b[CHUNK - 1:CHUNK, :] for b in bs]
        qs = [q_ref[nb, rows, kc].astype(F32) for (nb, _), kc in zip(chains, kcol)]
        ks = [k_ref[nb, rows, kc].astype(F32) for (nb, _), kc in zip(chains, kcol)]
        vs = [v_ref[nb, rows, vc] for (nb, _), vc in zip(chains, vcol)]
        q_decs = [(q * jnp.exp(b)).astype(BF16) for q, b in zip(qs, bs)]
        k_invs = [(k * jnp.exp(-b)).astype(BF16) for k, b in zip(ks, bs)]
        k_decs = [(k * jnp.exp(bl - b)).astype(BF16) for k, b, bl in zip(ks, bs, b_lasts)]
        states = [state_scr[nb, h] for nb, h in chains]
        attns = [lax.dot_general(qd, ki, nt, preferred_element_type=F32) for qd, ki in zip(q_decs, k_invs)]
        o_inters = [lax.dot_general(qd, st.astype(BF16), nt, preferred_element_type=F32)
                    for qd, st in zip(q_decs, states)]
        kv_ts = [lax.dot_general(v, kd, tn, preferred_element_type=F32) for v, kd in zip(vs, k_decs)]
        attns = [jnp.where(causal, a, 0.0).astype(BF16) for a in attns]
        o_intras = [jnp.dot(a, v, preferred_element_type=F32) for a, v in zip(attns, vs)]
        for (nb, h), st, bl, kv_t, oi, oe, vc in zip(chains, states, b_lasts, kv_ts, o_intras, o_inters, vcol):
            state_scr[nb, h] = st * jnp.exp(bl) + kv_t
            o = oi + oe
            ms = jnp.mean(o * o, axis=-1, keepdims=True)
            o = o * lax.rsqrt(ms + EPS) * og * r_ref[nb, rows, vc].astype(F32)
            o_ref[nb, rows, vc] = o.astype(o_ref.dtype)
        return carry

    lax.fori_loop(0, SB // CHUNK, chunk, 0)


def _gla(qkvr, la, out_g, nb=4, sb=256):
    B, S, _ = qkvr.shape
    H = GLA_HEADS
    kw = H * GLA_DK
    vw = H * GLA_DV
    tri = (jnp.arange(CHUNK)[None, :] <= jnp.arange(CHUNK)[:, None]).astype(BF16)
    return pl.pallas_call(
        functools.partial(_gla_kernel, NB=nb, SB=sb),
        grid=(B // nb, S // sb),
        in_specs=[pl.BlockSpec((nb, sb, kw), lambda b, s: (b, s, 0)),
                  pl.BlockSpec((nb, sb, kw), lambda b, s: (b, s, 1)),
                  pl.BlockSpec((nb, sb, vw), lambda b, s: (b, s, (2 * kw) // vw)),
                  pl.BlockSpec((nb, sb, vw), lambda b, s: (b, s, (2 * kw) // vw + 1)),
                  pl.BlockSpec((nb, sb, kw), lambda b, s: (b, s, 0)),
                  pl.BlockSpec((CHUNK, CHUNK), lambda b, s: (0, 0)),
                  pl.BlockSpec((1, GLA_DV), lambda b, s: (0, 0))],
        out_specs=pl.BlockSpec((nb, sb, vw), lambda b, s: (b, s, 0)),
        out_shape=jax.ShapeDtypeStruct((B, S, vw), BF16),
        scratch_shapes=[pltpu.VMEM((nb, H, GLA_DV, GLA_DK), F32)],
        compiler_params=_cp("arbitrary", "arbitrary"),
        name="gla",
    )(qkvr, qkvr, qkvr, qkvr, la, tri, out_g)


def _post1_kernel(x_ref, o_ref, wo_ref, g1_ref, ng_ref, sc_ref, sh_ref, rw_ref, ltri_ref,
                  x3_ref, hp_ref, route_ref, cnt_ref, carry_scr, *, sub):
    i = pl.program_id(0)

    @pl.when(i == 0)
    def _():
        carry_scr[...] = jnp.zeros_like(carry_scr)

    tm = x3_ref.shape[0]
    lane = lax.broadcasted_iota(jnp.int32, (sub, LANES), 1)
    lanef = lane.astype(F32)

    def out_proj(rows):
        return jnp.dot(o_ref[rows, :], wo_ref[...], preferred_element_type=F32)

    def route_rows(rows, y):
        x3 = x_ref[rows, :] + g1_ref[...] * y
        x3_ref[rows, :] = x3
        hb = _modnorm(x3, ng_ref[...], sc_ref[...], sh_ref[...]).astype(BF16)

        half = D // 2
        lo = pltpu.bitcast(hb[:, :half].astype(F32), jnp.uint32)
        hi = pltpu.bitcast(hb[:, half:].astype(F32), jnp.uint32)
        hp_ref[rows, :] = jnp.right_shift(lo, jnp.uint32(16)) | (hi & jnp.uint32(0xFFFF0000))

        logits = jnp.dot(hb, rw_ref[...], preferred_element_type=F32)
        logits = jnp.where(lane < N_EXPERTS, logits, NEG)
        m1 = jnp.max(logits, axis=-1, keepdims=True)
        i1 = jnp.min(jnp.where(logits == m1, lanef, float(LANES)), axis=-1, keepdims=True)
        oh1 = lanef == i1
        rest = jnp.where(oh1, NEG, logits)
        m2 = jnp.max(rest, axis=-1, keepdims=True)
        i2 = jnp.min(jnp.where(rest == m2, lanef, float(LANES)), axis=-1, keepdims=True)
        oh2 = lanef == i2
        e = jnp.exp(m2 - m1)
        gate1 = 1.0 / (1.0 + e)
        gate2 = e / (1.0 + e)

        cnt = jnp.where(oh1, 1.0, 0.0) + jnp.where(oh2, 1.0, 0.0)
        carry = carry_scr[...]
        pre = jnp.dot(ltri_ref[...], cnt.astype(BF16), preferred_element_type=F32) + carry
        carry_scr[...] = carry + jnp.sum(cnt, axis=0, keepdims=True)
        r1 = jnp.sum(jnp.where(oh1, pre, 0.0), axis=-1, keepdims=True)
        r2 = jnp.sum(jnp.where(oh2, pre, 0.0), axis=-1, keepdims=True)
        route = jnp.where(lane == 0, i1, 0.0)
        for idx, val in ((1, i2), (2, r1), (3, r2), (4, gate1), (5, gate2)):
            route = jnp.where(lane == idx, val, route)
        route_ref[rows, :] = route

    _row_pipeline(tm // sub, tm, out_proj, route_rows)
    cnt_ref[...] = carry_scr[...]


def _post1(x2d, o2d, S, wo, g1, ng, sc, sh, router_pad, tm=1024, sub=256):
    N = x2d.shape[0]
    ltri = (jnp.arange(sub)[None, :] < jnp.arange(sub)[:, None]).astype(BF16)
    vec = pl.BlockSpec((None, 1, D), lambda i: ((i * tm) // S, 0, 0))
    row = pl.BlockSpec((tm, D), lambda i: (i, 0))
    return pl.pallas_call(
        functools.partial(_post1_kernel, sub=sub),
        grid=(N // tm,),
        in_specs=[row, row,
                  pl.BlockSpec((D, D), lambda i: (0, 0)),
                  vec,
                  pl.BlockSpec((1, D), lambda i: (0, 0)),
                  vec, vec,
                  pl.BlockSpec((D, LANES), lambda i: (0, 0)),
                  pl.BlockSpec((sub, sub), lambda i: (0, 0))],
        out_specs=[row,
                   pl.BlockSpec((tm, D // 2), lambda i: (i, 0)),
                   pl.BlockSpec((tm, LANES), lambda i: (i, 0)),
                   pl.BlockSpec((1, LANES), lambda i: (0, 0))],
        out_shape=[jax.ShapeDtypeStruct((N, D), F32),
                   jax.ShapeDtypeStruct((N, D // 2), jnp.uint32),
                   jax.ShapeDtypeStruct((N, LANES), F32),
                   jax.ShapeDtypeStruct((1, LANES), F32)],
        scratch_shapes=[pltpu.VMEM((1, LANES), F32)],
        compiler_params=_cp("arbitrary"),
        name="post1_router",
    )(x2d, o2d, wo, g1, ng, sc, sh, router_pad, ltri)


def _dispatch_kernel(pos_ref, h_ref, xs_in_ref, xs_ref, sem, *, td):
    del xs_in_ref

    def row(r, carry):
        for kk in range(2):
            p = pos_ref[0, 2 * r + kk]
            pltpu.make_async_copy(h_ref.at[pl.ds(r, 1)], xs_ref.at[pl.ds(p, 1)], sem).start()
        return carry

    lax.fori_loop(0, td, row, 0, unroll=8)
    for _ in range(2):
        pltpu.make_async_copy(h_ref, xs_ref.at[pl.ds(0, td)], sem).wait()


def _dispatch(hp, pos, P, td=512):
    N, W = hp.shape
    xs0 = jnp.zeros((P, W), jnp.uint32)
    return pl.pallas_call(
        functools.partial(_dispatch_kernel, td=td),
        grid=(N // td,),
        in_specs=[pl.BlockSpec((None, 1, 2 * td), lambda i: (i, 0, 0), memory_space=pltpu.SMEM),
                  pl.BlockSpec((td, W), lambda i: (i, 0)),
                  pl.BlockSpec(memory_space=pl.ANY)],
        out_specs=pl.BlockSpec(memory_space=pl.ANY),
        out_shape=jax.ShapeDtypeStruct((P, W), jnp.uint32),
        scratch_shapes=[pltpu.SemaphoreType.DMA(())],
        input_output_aliases={2: 0},
        compiler_params=_cp("arbitrary"),
        name="moe_dispatch",
    )(pos.reshape(N // td, 1, 2 * td), hp, xs0)


def _experts_kernel(be_ref, nv_ref, xs_ref, w1_ref, w3_ref, w2_ref, c1_ref, c3_ref, c2_ref, y_ref,
                    xq_scr, xr_scr, *, parts, tf):
    i = pl.program_id(0)
    valid = i < nv_ref[0]
    tm = y_ref.shape[0]
    n_ft = w1_ref.shape[1] // tf
    rp = tm // parts
    half = D // 2

    @pl.when(jnp.logical_not(valid))
    def _():
        y_ref[...] = jnp.zeros_like(y_ref)

    def up(item):
        rows, ft = item
        if ft == 0:
            w = xs_ref[rows, :]
            lo = pltpu.bitcast(jnp.left_shift(w, jnp.uint32(16)), F32)
            hi = pltpu.bitcast(w & jnp.uint32(0xFFFF0000), F32)
            amax = jnp.maximum(jnp.maximum(jnp.max(jnp.abs(lo), axis=-1, keepdims=True),
                                           jnp.max(jnp.abs(hi), axis=-1, keepdims=True)), FP8_TINY)
            scale = FP8_TOP / amax
            xq_scr[rows, :half] = (lo * scale).astype(FP8)
            xq_scr[rows, half:] = (hi * scale).astype(FP8)
            xr_scr[rows, :] = jnp.broadcast_to(amax * (1.0 / FP8_TOP), (rp, LANES))
        xq = xq_scr[rows, :]
        cols = slice(ft * tf, (ft + 1) * tf)
        return (jnp.dot(xq, w1_ref[:, cols], preferred_element_type=F32),
                jnp.dot(xq, w3_ref[:, cols], preferred_element_type=F32))

    def down(item, ab):
        rows, ft = item
        cols = slice(ft * tf, (ft + 1) * tf)
        x_unscale = xr_scr[rows, :1]
        a = ab[0] * x_unscale * c1_ref[:, cols]
        b = ab[1] * x_unscale * c3_ref[:, cols]
        g = a * _sigmoid(a) * b
        g_scale, g_unscale = _row_scale(g)
        acc = jnp.dot((g * g_scale).astype(FP8), w2_ref[ft * tf:(ft + 1) * tf, :], preferred_element_type=F32)
        part = acc * g_unscale * c2_ref[...]
        y_ref[rows, :] = part if ft == 0 else y_ref[rows, :] + part

    @pl.when(valid)
    def _():
        _pipeline([(slice(p * rp, (p + 1) * rp), ft) for p in range(parts) for ft in range(n_ft)], up, down)


def _experts(xs, blk_expert, n_valid, w1, w3, w2, tm, tf=1792, parts=4):
    P, W = xs.shape
    E, _, F = w1.shape
    nblk = P // tm
    w1, c1 = _quantize_columns(w1)
    w3, c3 = _quantize_columns(w3)
    w2, c2 = _quantize_columns(w2)

    def _i(i, nv):
        return jnp.minimum(i, nv[0] - 1)

    grid_spec = pltpu.PrefetchScalarGridSpec(
        num_scalar_prefetch=2,
        grid=(nblk,),
        in_specs=[pl.BlockSpec((tm, W), lambda i, be, nv: (_i(i, nv), 0)),
                  pl.BlockSpec((None, D, F), lambda i, be, nv: (be[_i(i, nv)], 0, 0)),
                  pl.BlockSpec((None, D, F), lambda i, be, nv: (be[_i(i, nv)], 0, 0)),
                  pl.BlockSpec((None, F, D), lambda i, be, nv: (be[_i(i, nv)], 0, 0)),
                  pl.BlockSpec((None, 1, F), lambda i, be, nv: (be[_i(i, nv)], 0, 0)),
                  pl.BlockSpec((None, 1, F), lambda i, be, nv: (be[_i(i, nv)], 0, 0)),
                  pl.BlockSpec((None, 1, D), lambda i, be, nv: (be[_i(i, nv)], 0, 0))],
        out_specs=pl.BlockSpec((tm, D), lambda i, be, nv: (i, 0)),
        scratch_shapes=[pltpu.VMEM((tm, D), FP8), pltpu.VMEM((tm, LANES), F32)],
    )
    return pl.pallas_call(
        functools.partial(_experts_kernel, parts=parts, tf=tf),
        grid_spec=grid_spec,
        out_shape=jax.ShapeDtypeStruct((P, D), F32),
        compiler_params=_cp("arbitrary"),
        name="moe_experts",
    )(blk_expert, n_valid, xs, w1, w3, w2, c1, c3, c2)


def _combine_kernel(pos_ref, pos_next_ref, route_ref, x_ref, g2_ref, y_ref, out_ref, ybuf, sems, *, tc):
    i = pl.program_id(0)
    slot = jnp.bitwise_and(i, 1)

    def gather_copy(p, s, kk, r):
        return pltpu.make_async_copy(y_ref.at[pl.ds(p, 1)], ybuf.at[s, kk, pl.ds(r, 1)], sems.at[s])

    def start_gathers(table_ref, s):
        def row(r, carry):
            for kk in range(2):
                gather_copy(table_ref[0, 2 * r + kk], s, kk, r).start()
            return carry

        lax.fori_loop(0, tc, row, 0, unroll=8)

    @pl.when(i == 0)
    def _():
        start_gathers(pos_ref, 0)

    @pl.when(i + 1 < pl.num_programs(0))
    def _():
        start_gathers(pos_next_ref, 1 - slot)

    for kk in range(2):
        pltpu.make_async_copy(y_ref.at[pl.ds(0, tc)], ybuf.at[slot, kk], sems.at[slot]).wait()
    route = route_ref[...]
    moe = route[:, 4:5] * ybuf[slot, 0] + route[:, 5:6] * ybuf[slot, 1]
    out_ref[...] = x_ref[...] + g2_ref[...] * moe


def _combine(y, pos, route, x3, S, g2, tc=512):
    N = x3.shape[0]
    n_steps = N // tc
    pos_blocks = pos.reshape(n_steps, 1, 2 * tc)
    return pl.pallas_call(
        functools.partial(_combine_kernel, tc=tc),
        grid=(n_steps,),
        in_specs=[pl.BlockSpec((None, 1, 2 * tc), lambda i: (i, 0, 0), memory_space=pltpu.SMEM),
                  pl.BlockSpec((None, 1, 2 * tc), lambda i: (jnp.minimum(i + 1, n_steps - 1), 0, 0),
                               memory_space=pltpu.SMEM),
                  pl.BlockSpec((tc, LANES), lambda i: (i, 0)),
                  pl.BlockSpec((tc, D), lambda i: (i, 0)),
                  pl.BlockSpec((None, 1, D), lambda i: ((i * tc) // S, 0, 0)),
                  pl.BlockSpec(memory_space=pl.ANY)],
        out_specs=pl.BlockSpec((tc, D), lambda i: (i, 0)),
        out_shape=jax.ShapeDtypeStruct((N, D), F32),
        scratch_shapes=[pltpu.VMEM((2, 2, tc, D), F32), pltpu.SemaphoreType.DMA((2,))],
        compiler_params=_cp("arbitrary"),
        name="moe_combine",
    )(pos_blocks, pos_blocks, route, x3, g2, y)


def _moe(hp, route, counts, x3, S, g2, w1, w3, w2, tm=1024):
    N = x3.shape[0]
    cnt = counts[0, :N_EXPERTS].astype(jnp.int32)
    nblk_e = (cnt + tm - 1) // tm
    blk_end = jnp.cumsum(nblk_e)
    row_start = (blk_end - nblk_e) * tm
    e_idx = route[:, 0:2].astype(jnp.int32)
    rank = route[:, 2:4].astype(jnp.int32)
    pos = row_start[e_idx] + rank
    P = 2 * N + N_EXPERTS * tm
    nblk = P // tm
    blk_expert = jnp.minimum(
        jnp.sum((jnp.arange(nblk, dtype=jnp.int32)[:, None] >= blk_end[None, :]).astype(jnp.int32), axis=1),
        N_EXPERTS - 1)
    n_valid = blk_end[-1:].astype(jnp.int32)
    xs = _dispatch(hp, pos, P)
    y = _experts(xs, blk_expert, n_valid, w1, w3, w2, tm)
    return _combine(y, pos, route, x3, S, g2)


def kernel(x, c, ada_w, ada_b, norm1_g, norm2_g, da_w_in, da_q_gain, da_k_gain, da_lam_q1, da_lam_k1, da_lam_q2,
           da_lam_k2, da_subln_g, da_w_out, gla_w_in, gla_w_a1, gla_w_a2, gla_b_a, gla_out_g, gla_w_out,
           ffn_w1, ffn_w3, ffn_w2, moe_router, moe_w1, moe_w3, moe_w2):
    B, S, _ = x.shape
    N = B * S
    mod = _adaln(c, ada_w, ada_b)
    mods = [[mod[l, :, k * D:(k + 1) * D].reshape(B, 1, D) for k in range(6)] for l in range(2)]
    x2d = x.reshape(N, D)

    sh1, sc1, gt1, sh2, sc2, gt2 = mods[0]
    lambda_init = 0.8 - 0.6 * math.exp(-0.3 * 0)
    qk_scale = DA_HEAD_DIM ** -0.5 * LOG2E
    gain_row = jnp.concatenate([jnp.tile(da_q_gain[0].reshape(-1) * qk_scale, DA_HEADS),
                                jnp.tile(da_k_gain[0].reshape(-1), DA_HEADS),
                                jnp.ones((D,), F32)]).reshape(1, 3 * D)
    qkv = _da_inproj(x2d, S, norm1_g[0].reshape(1, D), sc1, sh1, da_w_in[0].astype(BF16), gain_row)
    slopes = 2.0 ** (-8.0 * jnp.arange(1, DA_HEADS + 1, dtype=F32) / DA_HEADS)
    o = _da_attention(qkv.reshape(B, S, 3 * D), slopes,
                      da_lam_q1[0].reshape(1, -1), da_lam_k1[0].reshape(1, -1),
                      da_lam_q2[0].reshape(1, -1), da_lam_k2[0].reshape(1, -1),
                      da_subln_g[0].reshape(1, -1), lambda_init)
    x2d = _post0(x2d, o.reshape(N, D), S, da_w_out[0].astype(BF16), gt1, norm2_g[0].reshape(1, D), sc2, sh2, gt2,
                 ffn_w1[0].astype(BF16), ffn_w3[0].astype(BF16), ffn_w2[0].astype(BF16))

    sh1, sc1, gt1, sh2, sc2, gt2 = mods[1]
    kw = GLA_HEADS * GLA_DK
    mult_row = jnp.concatenate([jnp.full((kw,), GLA_DK ** -0.5, F32), jnp.ones((3 * D - kw,), F32)]).reshape(1, -1)
    rank = gla_w_a1.shape[-1]
    wa1 = jnp.zeros((D, LANES), BF16).at[:, :rank].set(gla_w_a1[0].astype(BF16))
    wa2 = jnp.zeros((LANES, kw), BF16).at[:rank, :].set(gla_w_a2[0].astype(BF16))
    qkvr, la = _gla_inproj(x2d, S, norm1_g[1].reshape(1, D), sc1, sh1, gla_w_in[0].astype(BF16), mult_row,
                           wa1, wa2, gla_b_a[0].reshape(1, kw))
    o = _gla(qkvr.reshape(B, S, 3 * D), la.reshape(B, S, kw), gla_out_g[0].reshape(1, -1))
    router_pad = jnp.zeros((D, LANES), BF16).at[:, :N_EXPERTS].set(moe_router[0].astype(BF16))
    x3, hp, route, counts = _post1(x2d, o.reshape(N, D), S, gla_w_out[0].astype(BF16), gt1,
                                   norm2_g[1].reshape(1, D), sc2, sh2, router_pad)
    out = _moe(hp, route, counts, x3, S, gt2,
               moe_w1[0], moe_w3[0], moe_w2[0])
    return out.reshape(B, S, D)
```

```python
import functools
import math

import jax
import jax.numpy as jnp
from jax import lax
from jax.experimental import pallas as pl
from jax.experimental.pallas import tpu as pltpu

F32 = jnp.float32
BF16 = jnp.bfloat16
FP8 = jnp.float8_e4m3fn
FP8_TOP = 256.0
FP8_TINY = 1e-30

D = 1024
EPS = 1e-6
NEG = -1e30
CHUNK = 64
DA_HEADS = 8
DA_HEAD_DIM = 64
GLA_HEADS = 4
GLA_DK = 128
GLA_DV = 256
GLA_TAU = 16.0
N_EXPERTS = 8
LANES = 128
LOG2E = 1.4426950408889634
VT_ROWS = LANES + 16

VMEM_LIMIT = 56 * 1024 * 1024


def _cp(*sem):
    return pltpu.CompilerParams(dimension_semantics=sem, vmem_limit_bytes=VMEM_LIMIT)


def _sigmoid(x):
    return 1.0 / (1.0 + jnp.exp(-x))


def _modnorm(x, g, sc, sh):
    ms = jnp.mean(x * x, axis=-1, keepdims=True)
    return (x * lax.rsqrt(ms + EPS) * g) * (1.0 + sc) + sh


def _row_scale(v):
    amax = jnp.maximum(jnp.max(jnp.abs(v), axis=-1, keepdims=True), FP8_TINY)
    return FP8_TOP / amax, amax * (1.0 / FP8_TOP)


def _quantize_kernel(w_ref, q_ref, inv_ref):
    w = w_ref[...]
    amax = jnp.maximum(jnp.max(jnp.abs(w), axis=0, keepdims=True), FP8_TINY)
    q_ref[...] = (w * (FP8_TOP / amax)).astype(FP8)
    inv_ref[...] = amax * (1.0 / FP8_TOP)


def _quantize_columns(w, tn=512):
    E, K, n_cols = w.shape
    return pl.pallas_call(
        _quantize_kernel,
        grid=(E, n_cols // tn),
        in_specs=[pl.BlockSpec((None, K, tn), lambda e, n: (e, 0, n))],
        out_specs=[pl.BlockSpec((None, K, tn), lambda e, n: (e, 0, n)),
                   pl.BlockSpec((None, 1, tn), lambda e, n: (e, 0, n))],
        out_shape=[jax.ShapeDtypeStruct((E, K, n_cols), FP8), jax.ShapeDtypeStruct((E, 1, n_cols), F32)],
        compiler_params=_cp("arbitrary", "arbitrary"),
        name="quantize_columns",
    )(w)


def _adaln_kernel(c_ref, w_ref, b_ref, o_ref):
    c = c_ref[...]
    ca = (c * _sigmoid(c)).astype(BF16)
    o_ref[...] = jnp.dot(ca, w_ref[...].astype(BF16), preferred_element_type=F32) + b_ref[...]


def _adaln(c, ada_w, ada_b):
    L, _, n6 = ada_w.shape
    B = c.shape[0]
    tn = 1536
    return pl.pallas_call(
        _adaln_kernel,
        grid=(L, n6 // tn),
        in_specs=[pl.BlockSpec((B, D), lambda l, n: (0, 0)),
                  pl.BlockSpec((None, D, tn), lambda l, n: (l, 0, n)),
                  pl.BlockSpec((None, 1, tn), lambda l, n: (l, 0, n))],
        out_specs=pl.BlockSpec((None, B, tn), lambda l, n: (l, 0, n)),
        out_shape=jax.ShapeDtypeStruct((L, B, n6), F32),
        compiler_params=_cp("arbitrary", "arbitrary"),
        name="adaln",
    )(c, ada_w, ada_b.reshape(L, 1, n6))


def _pipeline(items, matmul, finish):
    prev = None
    for item in items:
        acc = matmul(item)
        if prev is not None:
            finish(*prev)
        prev = (item, acc)
    finish(*prev)


def _row_pipeline(parts, rows_total, matmul, finish):
    rp = rows_total // parts
    _pipeline([slice(p * rp, (p + 1) * rp) for p in range(parts)], matmul, finish)


def _da_inproj_kernel(x_ref, g_ref, sc_ref, sh_ref, w_ref, gain_ref, gsum_ref, o_ref, h_scr,
                      *, tn, n_qk_groups, parts):
    tm, n_out = o_ref.shape
    rp = tm // parts
    items = [(slice(p * rp, (p + 1) * rp), cg) for p in range(parts) for cg in range(n_out // tn)]

    def project(item):
        rows, cg = item
        if cg == 0:
            h = _modnorm(x_ref[rows, :], g_ref[...], sc_ref[...], sh_ref[...]).astype(BF16)
            h_scr[rows, :] = h
        else:
            h = h_scr[rows, :]
        return jnp.dot(h, w_ref[:, cg * tn:(cg + 1) * tn], preferred_element_type=F32)

    def finish(item, acc):
        rows, cg = item
        cols = slice(cg * tn, (cg + 1) * tn)
        if cg < n_qk_groups:
            y2 = (acc * acc).astype(BF16)
            gsum = gsum_ref[...]
            ms = jnp.concatenate(
                [jnp.dot(y2[:, c * 256:(c + 1) * 256], gsum, preferred_element_type=F32) for c in range(tn // 256)],
                axis=1)
            acc = acc * lax.rsqrt(ms + EPS)
        o_ref[rows, cols] = (acc * gain_ref[:, cols]).astype(BF16)

    _pipeline(items, project, finish)


def _da_inproj(x2d, S, g, sc, sh, w_bf, gain_row, tm=1024, tn=1024, parts=4):
    N = x2d.shape[0]
    n_out = w_bf.shape[1]
    r = jnp.arange(256) // DA_HEAD_DIM
    gsum = jnp.where(r[:, None] == r[None, :], 1.0 / DA_HEAD_DIM, 0.0).astype(BF16)
    vec = pl.BlockSpec((None, 1, D), lambda i: ((i * tm) // S, 0, 0))
    return pl.pallas_call(
        functools.partial(_da_inproj_kernel, tn=tn, n_qk_groups=(2 * D) // tn, parts=parts),
        grid=(N // tm,),
        in_specs=[pl.BlockSpec((tm, D), lambda i: (i, 0)),
                  pl.BlockSpec((1, D), lambda i: (0, 0)),
                  vec, vec,
                  pl.BlockSpec((D, n_out), lambda i: (0, 0)),
                  pl.BlockSpec((1, n_out), lambda i: (0, 0)),
                  pl.BlockSpec((256, 256), lambda i: (0, 0))],
        out_specs=pl.BlockSpec((tm, n_out), lambda i: (i, 0)),
        out_shape=jax.ShapeDtypeStruct((N, n_out), BF16),
        scratch_shapes=[pltpu.VMEM((tm, D), BF16)],
        compiler_params=_cp("arbitrary"),
        name="da_inproj",
    )(x2d, g, sc, sh, w_bf, gain_row, gsum)


def _da_attn_kernel(slope_ref, q_ref, k_ref, v_ref, lq1_ref, lk1_ref, lq2_ref, lk2_ref, subg_ref, o_ref,
                    k1_scr, k2_scr, vt_scr, corr_scr, aug_scr, s1e_scr, s2e_scr, s1o_scr, s2o_scr, qt1_scr, qt2_scr, a1, a2,
                    *, T, SUB, S, lambda_init):
    h = pl.program_id(0)
    slope2 = slope_ref[h] * LOG2E
    lane = lax.broadcasted_iota(jnp.int32, (T, LANES), 1)
    first_half = lane < DA_HEAD_DIM

    @pl.when(pl.program_id(1) == 0)
    def _():
        local = lax.broadcasted_iota(jnp.int32, (T, LANES), 0)
        lo = jnp.bitwise_and(local, 255).astype(F32)
        hi = (local - jnp.bitwise_and(local, 255)).astype(F32)

        def aug(base, vals):
            out = jnp.zeros((T, LANES), F32)
            for off, val in enumerate(vals):
                out = jnp.where(lane == base + off, val, out)
            return out.astype(BF16)

        c_hi = (slope2 + jnp.zeros((T, LANES), F32)).astype(BF16).astype(F32)
        c_mid = (slope2 - c_hi).astype(BF16).astype(F32)
        c_lo = slope2 - c_hi - c_mid
        k_vals = (lo, lo, lo, hi, hi, hi, -c_hi, -c_mid, -c_lo, -c_hi, -c_mid, -c_lo)
        q_vals = (c_hi, c_mid, c_lo, c_hi, c_mid, c_lo, lo, lo, lo, hi, hi, hi)
        aug_scr[0] = aug(DA_HEAD_DIM, k_vals)
        aug_scr[1] = aug(0, k_vals)
        aug_scr[2] = aug(DA_HEAD_DIM, q_vals)
        aug_scr[3] = aug(0, q_vals)
        kj = lax.broadcasted_iota(jnp.int32, (T, T), 0)
        qj = lax.broadcasted_iota(jnp.int32, (T, T), 1)
        visible = jnp.right_shift(kj, 6) <= jnp.right_shift(qj, 6)
        ahead = jnp.maximum(kj - qj, 0).astype(F32)
        corr_scr[...] = jnp.where(visible, -2.0 * slope2 * ahead, NEG)

    k_aug1, k_aug2, q_aug1, q_aug2 = aug_scr[0], aug_scr[1], aug_scr[2], aug_scr[3]
    ones_rows = jnp.where(lax.broadcasted_iota(jnp.int32, (VT_ROWS - LANES, T), 0) == 0, 1.0, 0.0).astype(BF16)

    def prep(j, carry):
        r0 = pl.multiple_of(j * T, T)
        k = k_ref[pl.ds(r0, T), :]
        k1_scr[pl.ds(r0, T), :] = jnp.where(first_half, k, k_aug1)
        k2_scr[pl.ds(r0, T), :] = jnp.where(first_half, k_aug2, k)
        vt_scr[:LANES, pl.ds(r0, T)] = v_ref[pl.ds(r0, T), :].astype(F32).T.astype(BF16)
        vt_scr[LANES:, pl.ds(r0, T)] = ones_rows
        return carry

    lax.fori_loop(0, S // T, prep, 0)

    lam =(jnp.exp(jnp.sum(lq1_ref[...] * lk1_ref[...], axis=-1, keepdims=True))
           - jnp.exp(jnp.sum(lq2_ref[...] * lk2_ref[...], axis=-1, keepdims=True)) + lambda_init)
    k_scrs = (k1_scr, k2_scr)
    s_scrs = ((s1e_scr, s2e_scr), (s1o_scr, s2o_scr))
    qt_scrs = (qt1_scr, qt2_scr)
    accs = (a1, a2)
    n_q = S // T
    TK = T // SUB
    neg = jnp.full((1, T), NEG, F32)

    def load_queries(qi):
        q = q_ref[pl.ds(pl.multiple_of(qi * T, T), T), :]
        for qt_scr, qa in zip(qt_scrs, (jnp.where(first_half, q, q_aug1), jnp.where(first_half, q_aug2, q))):
            qt_scr[...] = qa.astype(F32).T.astype(BF16)

    def block_shift(qi, j):
        return -slope2 * lax.convert_element_type((qi - j) * T, F32)

    def score_chunk(qi, par, j, part, ms, corr):
        k0 = pl.multiple_of(j * T + part * TK, TK)
        out = []
        for mp in range(2):
            s = jnp.dot(k_scrs[mp][pl.ds(k0, TK), :], qt_scrs[mp][...], preferred_element_type=F32)
            if corr is not None:
                s = s + corr[part * TK:(part + 1) * TK, :]
            s_scrs[par][mp][pl.ds(k0, TK), :] = s
            out.append(jnp.maximum(ms[mp], jnp.max(s, axis=0, keepdims=True) + block_shift(qi, j)))
        return tuple(out)

    def weight_chunk(qi, par, j, part, ms):
        k0 = pl.multiple_of(j * T + part * TK, TK)
        vt = vt_scr[:, pl.ds(k0, TK)]
        for mp in range(2):
            p = jnp.exp2(s_scrs[par][mp][pl.ds(k0, TK), :] + (block_shift(qi, j) - ms[mp]))
            accs[mp][...] += jnp.dot(vt, p.astype(BF16), preferred_element_type=F32)

    def finalize(qi):
        acc1 = a1[...]
        acc2 = a2[...]
        o = (acc1[:LANES] / acc1[LANES:LANES + 1] - lam * (acc2[:LANES] / acc2[LANES:LANES + 1])).T
        msq = jnp.mean(o * o, axis=-1, keepdims=True)
        o = o * lax.rsqrt(msq + EPS) * subg_ref[...] * (1.0 - lambda_init)
        o_ref[pl.ds(pl.multiple_of(qi * T, T), T), :] = o.astype(o_ref.dtype)
        a1[...] = jnp.zeros_like(a1)
        a2[...] = jnp.zeros_like(a2)

    a1[...] = jnp.zeros_like(a1)
    a2[...] = jnp.zeros_like(a2)
    load_queries(0)
    ms0 = (neg, neg)
    for part in range(SUB):
        ms0 = score_chunk(0, 0, 0, part, ms0, corr_scr)

    def q_block(qi, par, ms):
        load_queries(qi + 1)

        def chunk(j, ms_next):
            for part in range(SUB):
                ms_next = score_chunk(qi + 1, 1 - par, j, part, ms_next, None)
                weight_chunk(qi, par, j, part, ms)
            return ms_next

        def chunk_pair(jj, ms_next):
            return chunk(2 * jj + 1, chunk(2 * jj, ms_next))

        ms_next = lax.fori_loop(0, jnp.right_shift(qi + 1, 1), chunk_pair, (neg, neg))
        if par == 0:
            ms_next = chunk(qi, ms_next)
        for part in range(SUB):
            ms_next = score_chunk(qi + 1, 1 - par, qi + 1, part, ms_next, corr_scr)
        finalize(qi)
        return ms_next

    def q_pair(qq, ms):
        return q_block(2 * qq + 1, 1, q_block(2 * qq, 0, ms))

    ms_even = lax.fori_loop(0, n_q // 2 - 1, q_pair, ms0)
    ms_last = q_block(n_q - 2, 0, ms_even)

    def last_chunk_pair(jj, carry):
        for j in (2 * jj, 2 * jj + 1):
            for part in range(SUB):
                weight_chunk(n_q - 1, 1, j, part, ms_last)
        return carry

    lax.fori_loop(0, n_q // 2, last_chunk_pair, 0)
    finalize(n_q - 1)


def _da_attention(qkv, slopes, lq1, lk1, lq2, lk2, subg, lambda_init, T=512, sub=2):
    B, S, _ = qkv.shape
    H = DA_HEADS
    vec64 = pl.BlockSpec((1, DA_HEAD_DIM), lambda h, b: (0, 0))
    return pl.pallas_call(
        functools.partial(_da_attn_kernel, T=T, SUB=sub, S=S, lambda_init=lambda_init),
        grid=(H, B),
        in_specs=[pl.BlockSpec(memory_space=pltpu.SMEM),
                  pl.BlockSpec((None, S, LANES), lambda h, b: (b, 0, h)),
                  pl.BlockSpec((None, S, LANES), lambda h, b: (b, 0, H + h)),
                  pl.BlockSpec((None, S, LANES), lambda h, b: (b, 0, 2 * H + h)),
                  vec64, vec64, vec64, vec64,
                  pl.BlockSpec((1, LANES), lambda h, b: (0, 0))],
        out_specs=pl.BlockSpec((None, S, LANES), lambda h, b: (b, 0, h)),
        out_shape=jax.ShapeDtypeStruct((B, S, H * LANES), BF16),
        scratch_shapes=[pltpu.VMEM((S, LANES), BF16), pltpu.VMEM((S, LANES), BF16), pltpu.VMEM((VT_ROWS, S), BF16),
                        pltpu.VMEM((T, T), F32), pltpu.VMEM((4, T, LANES), BF16)]
        + [pltpu.VMEM((S, T), F32) for _ in range(4)] + [
                        pltpu.VMEM((LANES, T), BF16), pltpu.VMEM((LANES, T), BF16),
                        pltpu.VMEM((VT_ROWS, T), F32), pltpu.VMEM((VT_ROWS, T), F32)],
        compiler_params=_cp("arbitrary", "arbitrary"),
        name="da_attention",
    )(slopes, qkv, qkv, qkv, lq1, lk1, lq2, lk2, subg)


def _post0_kernel(x_ref, o_ref, wo_ref, g1_ref, ng_ref, sc_ref, sh_ref, g2_ref, w1_ref, w3_ref, w2_ref,
                  out_ref, h_scr, acc_scr, *, parts, tf):
    tm = out_ref.shape[0]
    n_ft = w1_ref.shape[1] // tf
    rp = tm // parts

    def out_proj(rows):
        return jnp.dot(o_ref[rows, :], wo_ref[...], preferred_element_type=F32)

    def residual_norm(rows, y):
        x1 = x_ref[rows, :] + g1_ref[...] * y
        out_ref[rows, :] = x1
        h_scr[rows, :] = _modnorm(x1, ng_ref[...], sc_ref[...], sh_ref[...]).astype(BF16)

    _row_pipeline(parts, tm, out_proj, residual_norm)

    def up(item):
        rows, ft = item
        h = h_scr[rows, :]
        cols = slice(ft * tf, (ft + 1) * tf)
        return (jnp.dot(h, w1_ref[:, cols], preferred_element_type=F32),
                jnp.dot(h, w3_ref[:, cols], preferred_element_type=F32))

    def down(item, ab):
        rows, ft = item
        a, b = ab
        gact = (a * _sigmoid(a) * b).astype(BF16)
        part = jnp.dot(gact, w2_ref[ft * tf:(ft + 1) * tf, :], preferred_element_type=F32)
        total = part if ft == 0 else acc_scr[rows, :] + part
        if ft == n_ft - 1:
            out_ref[rows, :] = out_ref[rows, :] + g2_ref[...] * total
        else:
            acc_scr[rows, :] = total

    _pipeline([(slice(p * rp, (p + 1) * rp), ft) for p in range(parts) for ft in range(n_ft)], up, down)


def _post0(x2d, o2d, S, wo, g1, ng, sc, sh, g2, w1, w3, w2, tm=1024, tf=1408, parts=4):
    N = x2d.shape[0]
    F = w1.shape[1]
    vec = pl.BlockSpec((None, 1, D), lambda i: ((i * tm) // S, 0, 0))
    row = pl.BlockSpec((tm, D), lambda i: (i, 0))

    def resident(shape):
        return pl.BlockSpec(shape, lambda i: (0, 0), pipeline_mode=pl.Buffered(1))

    return pl.pallas_call(
        functools.partial(_post0_kernel, parts=parts, tf=tf),
        grid=(N // tm,),
        in_specs=[row, row,
                  resident((D, D)),
                  vec,
                  pl.BlockSpec((1, D), lambda i: (0, 0)),
                  vec, vec, vec,
                  resident((D, F)), resident((D, F)), resident((F, D))],
        out_specs=row,
        out_shape=jax.ShapeDtypeStruct((N, D), F32),
        scratch_shapes=[pltpu.VMEM((tm, D), BF16), pltpu.VMEM((tm, D), F32)],
        compiler_params=_cp("arbitrary"),
        name="post0_ffn",
    )(x2d, o2d, wo, g1, ng, sc, sh, g2, w1, w3, w2)


def _gla_inproj_kernel(x_ref, g_ref, sc_ref, sh_ref, w_ref, mult_ref, wa1_ref, wa2_ref, ba_ref,
                       o_ref, la_ref, h_scr, *, tn, n_plain_groups, parts):
    tm, n_out = o_ref.shape
    rp = tm // parts
    items = [(slice(p * rp, (p + 1) * rp), cg) for p in range(parts) for cg in range(n_out // tn)]

    def project(item):
        rows, cg = item
        if cg == 0:
            h = _modnorm(x_ref[rows, :], g_ref[...], sc_ref[...], sh_ref[...]).astype(BF16)
            h_scr[rows, :] = h
            low = jnp.dot(h, wa1_ref[...], preferred_element_type=F32).astype(BF16)
            z = jnp.dot(low, wa2_ref[...], preferred_element_type=F32) + ba_ref[...]
            log_sig = jnp.minimum(z, 0.0) - jnp.log(1.0 + jnp.exp(-jnp.abs(z)))
            la_ref[rows, :] = log_sig * (1.0 / GLA_TAU)
        else:
            h = h_scr[rows, :]
        return jnp.dot(h, w_ref[:, cg * tn:(cg + 1) * tn], preferred_element_type=F32)

    def finish(item, acc):
        rows, cg = item
        cols = slice(cg * tn, (cg + 1) * tn)
        if cg < n_plain_groups:
            o_ref[rows, cols] = (acc * mult_ref[:, cols]).astype(BF16)
        else:
            o_ref[rows, cols] = (acc * _sigmoid(acc)).astype(BF16)

    _pipeline(items, project, finish)


def _gla_inproj(x2d, S, g, sc, sh, w_bf, mult_row, wa1, wa2, ba, tm=1024, tn=1024, parts=4):
    N = x2d.shape[0]
    n_out = w_bf.shape[1]
    kw = wa2.shape[1]
    vec = pl.BlockSpec((None, 1, D), lambda i: ((i * tm) // S, 0, 0))
    return pl.pallas_call(
        functools.partial(_gla_inproj_kernel, tn=tn, n_plain_groups=(2 * D) // tn, parts=parts),
        grid=(N // tm,),
        in_specs=[pl.BlockSpec((tm, D), lambda i: (i, 0)),
                  pl.BlockSpec((1, D), lambda i: (0, 0)),
                  vec, vec,
                  pl.BlockSpec((D, n_out), lambda i: (0, 0)),
                  pl.BlockSpec((1, n_out), lambda i: (0, 0)),
                  pl.BlockSpec((D, LANES), lambda i: (0, 0)),
                  pl.BlockSpec((LANES, kw), lambda i: (0, 0)),
                  pl.BlockSpec((1, kw), lambda i: (0, 0))],
        out_specs=[pl.BlockSpec((tm, n_out), lambda i: (i, 0)),
                   pl.BlockSpec((tm, kw), lambda i: (i, 0))],
        out_shape=[jax.ShapeDtypeStruct((N, n_out), BF16), jax.ShapeDtypeStruct((N, kw), F32)],
        scratch_shapes=[pltpu.VMEM((tm, D), BF16)],
        compiler_params=_cp("arbitrary"),
        name="gla_inproj",
    )(x2d, g, sc, sh, w_bf, mult_row, wa1, wa2, ba)


def _gla_kernel(q_ref, k_ref, v_ref, r_ref, la_ref, tri_ref, og_ref, o_ref, state_scr, *, NB, SB):
    @pl.when(pl.program_id(1) == 0)
    def _():
        state_scr[...] = jnp.zeros_like(state_scr)

    tri = tri_ref[...]
    ti = lax.broadcasted_iota(jnp.int32, (CHUNK, CHUNK), 0)
    si = lax.broadcasted_iota(jnp.int32, (CHUNK, CHUNK), 1)
    causal = si <= ti
    og = og_ref[...]
    chains = [(nb, h) for nb in range(NB) for h in range(GLA_HEADS)]
    nt = (((1,), (1,)), ((), ()))
    tn = (((0,), (0,)), ((), ()))

    def chunk(c, carry):
        r0 = pl.multiple_of(c * CHUNK, CHUNK)
        rows = pl.ds(r0, CHUNK)
        kcol = [slice(h * GLA_DK, (h + 1) * GLA_DK) for _, h in chains]
        vcol = [slice(h * GLA_DV, (h + 1) * GLA_DV) for _, h in chains]
        las = [la_ref[nb, rows, kc] for (nb, _), kc in zip(chains, kcol)]
        his = [la.astype(BF16) for la in las]
        los = [(la - hi.astype(F32)).astype(BF16) for la, hi in zip(las, his)]
        bs = [jnp.dot(tri, hi, preferred_element_type=F32) + jnp.dot(tri, lo, preferred_element_type=F32)
              for hi, lo in zip(his, los)]
        b_lasts = [b[CHUNK - 1:CHUNK, :] for b in bs]
        qs = [q_ref[nb, rows, kc].astype(F32) for (nb, _), kc in zip(chains, kcol)]
        ks = [k_ref[nb, rows, kc].astype(F32) for (nb, _), kc in zip(chains, kcol)]
        vs = [v_ref[nb, rows, vc] for (nb, _), vc in zip(chains, vcol)]
        q_decs = [(q * jnp.exp(b)).astype(BF16) for q, b in zip(qs, bs)]
        k_invs = [(k * jnp.exp(-b)).astype(BF16) for k, b in zip(ks, bs)]
        k_decs = [(k * jnp.exp(bl - b)).astype(BF16) for k, b, bl in zip(ks, bs, b_lasts)]
        states = [state_scr[nb, h] for nb, h in chains]
        attns = [lax.dot_general(qd, ki, nt, preferred_element_type=F32) for qd, ki in zip(q_decs, k_invs)]
        o_inters = [lax.dot_general(qd, st.astype(BF16), nt, preferred_element_type=F32)
                    for qd, st in zip(q_decs, states)]
        kv_ts = [lax.dot_general(v, kd, tn, preferred_element_type=F32) for v, kd in zip(vs, k_decs)]
        attns = [jnp.where(causal, a, 0.0).astype(BF16) for a in attns]
        o_intras = [jnp.dot(a, v, preferred_element_type=F32) for a, v in zip(attns, vs)]
        for (nb, h), st, bl, kv_t, oi, oe, vc in zip(chains, states, b_lasts, kv_ts, o_intras, o_inters, vcol):
            state_scr[nb, h] = st * jnp.exp(bl) + kv_t
            o = oi + oe
            ms = jnp.mean(o * o, axis=-1, keepdims=True)
            o = o * lax.rsqrt(ms + EPS) * og * r_ref[nb, rows, vc].astype(F32)
            o_ref[nb, rows, vc] = o.astype(o_ref.dtype)
        return carry

    lax.fori_loop(0, SB // CHUNK, chunk, 0)


def _gla(qkvr, la, out_g, nb=4, sb=256):
    B, S, _ = qkvr.shape
    H = GLA_HEADS
    kw = H * GLA_DK
    vw = H * GLA_DV
    tri = (jnp.arange(CHUNK)[None, :] <= jnp.arange(CHUNK)[:, None]).astype(BF16)
    return pl.pallas_call(
        functools.partial(_gla_kernel, NB=nb, SB=sb),
        grid=(B // nb, S // sb),
        in_specs=[pl.BlockSpec((nb, sb, kw), lambda b, s: (b, s, 0)),
                  pl.BlockSpec((nb, sb, kw), lambda b, s: (b, s, 1)),
                  pl.BlockSpec((nb, sb, vw), lambda b, s: (b, s, (2 * kw) // vw)),
                  pl.BlockSpec((nb, sb, vw), lambda b, s: (b, s, (2 * kw) // vw + 1)),
                  pl.BlockSpec((nb, sb, kw), lambda b, s: (b, s, 0)),
                  pl.BlockSpec((CHUNK, CHUNK), lambda b, s: (0, 0)),
                  pl.BlockSpec((1, GLA_DV), lambda b, s: (0, 0))],
        out_specs=pl.BlockSpec((nb, sb, vw), lambda b, s: (b, s, 0)),
        out_shape=jax.ShapeDtypeStruct((B, S, vw), BF16),
        scratch_shapes=[pltpu.VMEM((nb, H, GLA_DV, GLA_DK), F32)],
        compiler_params=_cp("arbitrary", "arbitrary"),
        name="gla",
    )(qkvr, qkvr, qkvr, qkvr, la, tri, out_g)


def _post1_kernel(x_ref, o_ref, wo_ref, g1_ref, ng_ref, sc_ref, sh_ref, rw_ref, ltri_ref,
                  x3_ref, hp_ref, route_ref, cnt_ref, carry_scr, *, sub):
    i = pl.program_id(0)

    @pl.when(i == 0)
    def _():
        carry_scr[...] = jnp.zeros_like(carry_scr)

    tm = x3_ref.shape[0]
    lane = lax.broadcasted_iota(jnp.int32, (sub, LANES), 1)
    lanef = lane.astype(F32)

    def out_proj(rows):
        return jnp.dot(o_ref[rows, :], wo_ref[...], preferred_element_type=F32)

    def route_rows(rows, y):
        x3 = x_ref[rows, :] + g1_ref[...] * y
        x3_ref[rows, :] = x3
        hb = _modnorm(x3, ng_ref[...], sc_ref[...], sh_ref[...]).astype(BF16)

        half = D // 2
        lo = pltpu.bitcast(hb[:, :half].astype(F32), jnp.uint32)
        hi = pltpu.bitcast(hb[:, half:].astype(F32), jnp.uint32)
        hp_ref[rows, :] = jnp.right_shift(lo, jnp.uint32(16)) | (hi & jnp.uint32(0xFFFF0000))

        logits = jnp.dot(hb, rw_ref[...], preferred_element_type=F32)
        logits = jnp.where(lane < N_EXPERTS, logits, NEG)
        m1 = jnp.max(logits, axis=-1, keepdims=True)
        i1 = jnp.min(jnp.where(logits == m1, lanef, float(LANES)), axis=-1, keepdims=True)
        oh1 = lanef == i1
        rest = jnp.where(oh1, NEG, logits)
        m2 = jnp.max(rest, axis=-1, keepdims=True)
        i2 = jnp.min(jnp.where(rest == m2, lanef, float(LANES)), axis=-1, keepdims=True)
        oh2 = lanef == i2
        e = jnp.exp(m2 - m1)
        gate1 = 1.0 / (1.0 + e)
        gate2 = e / (1.0 + e)

        cnt = jnp.where(oh1, 1.0, 0.0) + jnp.where(oh2, 1.0, 0.0)
        carry = carry_scr[...]
        pre = jnp.dot(ltri_ref[...], cnt.astype(BF16), preferred_element_type=F32) + carry
        carry_scr[...] = carry + jnp.sum(cnt, axis=0, keepdims=True)
        r1 = jnp.sum(jnp.where(oh1, pre, 0.0), axis=-1, keepdims=True)
        r2 = jnp.sum(jnp.where(oh2, pre, 0.0), axis=-1, keepdims=True)
        route = jnp.where(lane == 0, i1, 0.0)
        for idx, val in ((1, i2), (2, r1), (3, r2), (4, gate1), (5, gate2)):
            route = jnp.where(lane == idx, val, route)
        route_ref[rows, :] = route

    _row_pipeline(tm // sub, tm, out_proj, route_rows)
    cnt_ref[...] = carry_scr[...]


def _post1(x2d, o2d, S, wo, g1, ng, sc, sh, router_pad, tm=1024, sub=256):
    N = x2d.shape[0]
    ltri = (jnp.arange(sub)[None, :] < jnp.arange(sub)[:, None]).astype(BF16)
    vec = pl.BlockSpec((None, 1, D), lambda i: ((i * tm) // S, 0, 0))
    row = pl.BlockSpec((tm, D), lambda i: (i, 0))
    return pl.pallas_call(
        functools.partial(_post1_kernel, sub=sub),
        grid=(N // tm,),
        in_specs=[row, row,
                  pl.BlockSpec((D, D), lambda i: (0, 0)),
                  vec,
                  pl.BlockSpec((1, D), lambda i: (0, 0)),
                  vec, vec,
                  pl.BlockSpec((D, LANES), lambda i: (0, 0)),
                  pl.BlockSpec((sub, sub), lambda i: (0, 0))],
        out_specs=[row,
                   pl.BlockSpec((tm, D // 2), lambda i: (i, 0)),
                   pl.BlockSpec((tm, LANES), lambda i: (i, 0)),
                   pl.BlockSpec((1, LANES), lambda i: (0, 0))],
        out_shape=[jax.ShapeDtypeStruct((N, D), F32),
                   jax.ShapeDtypeStruct((N, D // 2), jnp.uint32),
                   jax.ShapeDtypeStruct((N, LANES), F32),
                   jax.ShapeDtypeStruct((1, LANES), F32)],
        scratch_shapes=[pltpu.VMEM((1, LANES), F32)],
        compiler_params=_cp("arbitrary"),
        name="post1_router",
    )(x2d, o2d, wo, g1, ng, sc, sh, router_pad, ltri)


def _dispatch_kernel(pos_ref, h_ref, xs_in_ref, xs_ref, sem, *, td):
    del xs_in_ref

    def row(r, carry):
        for kk in range(2):
            p = pos_ref[0, 2 * r + kk]
            pltpu.make_async_copy(h_ref.at[pl.ds(r, 1)], xs_ref.at[pl.ds(p, 1)], sem).start()
        return carry

    lax.fori_loop(0, td, row, 0, unroll=8)
    for _ in range(2):
        pltpu.make_async_copy(h_ref, xs_ref.at[pl.ds(0, td)], sem).wait()


def _dispatch(hp, pos, P, td=512):
    N, W = hp.shape
    xs0 = jnp.zeros((P, W), jnp.uint32)
    return pl.pallas_call(
        functools.partial(_dispatch_kernel, td=td),
        grid=(N // td,),
        in_specs=[pl.BlockSpec((None, 1, 2 * td), lambda i: (i, 0, 0), memory_space=pltpu.SMEM),
                  pl.BlockSpec((td, W), lambda i: (i, 0)),
                  pl.BlockSpec(memory_space=pl.ANY)],
        out_specs=pl.BlockSpec(memory_space=pl.ANY),
        out_shape=jax.ShapeDtypeStruct((P, W), jnp.uint32),
        scratch_shapes=[pltpu.SemaphoreType.DMA(())],
        input_output_aliases={2: 0},
        compiler_params=_cp("arbitrary"),
        name="moe_dispatch",
    )(pos.reshape(N // td, 1, 2 * td), hp, xs0)


def _experts_kernel(be_ref, nv_ref, xs_ref, w1_ref, w3_ref, w2_ref, c1_ref, c3_ref, c2_ref, y_ref,
                    xq_scr, xr_scr, *, parts, tf):
    i = pl.program_id(0)
    valid = i < nv_ref[0]
    tm = y_ref.shape[0]
    n_ft = w1_ref.shape[1] // tf
    rp = tm // parts
    half = D // 2

    @pl.when(jnp.logical_not(valid))
    def _():
        y_ref[...] = jnp.zeros_like(y_ref)

    def up(item):
        rows, ft = item
        if ft == 0:
            w = xs_ref[rows, :]
            lo = pltpu.bitcast(jnp.left_shift(w, jnp.uint32(16)), F32)
            hi = pltpu.bitcast(w & jnp.uint32(0xFFFF0000), F32)
            amax = jnp.maximum(jnp.maximum(jnp.max(jnp.abs(lo), axis=-1, keepdims=True),
                                           jnp.max(jnp.abs(hi), axis=-1, keepdims=True)), FP8_TINY)
            scale = FP8_TOP / amax
            xq_scr[rows, :half] = (lo * scale).astype(FP8)
            xq_scr[rows, half:] = (hi * scale).astype(FP8)
            xr_scr[rows, :] = jnp.broadcast_to(amax * (1.0 / FP8_TOP), (rp, LANES))
        xq = xq_scr[rows, :]
        cols = slice(ft * tf, (ft + 1) * tf)
        return (jnp.dot(xq, w1_ref[:, cols], preferred_element_type=F32),
                jnp.dot(xq, w3_ref[:, cols], preferred_element_type=F32))

    def down(item, ab):
        rows, ft = item
        cols = slice(ft * tf, (ft + 1) * tf)
        x_unscale = xr_scr[rows, :1]
        a = ab[0] * x_unscale * c1_ref[:, cols]
        b = ab[1] * x_unscale * c3_ref[:, cols]
        g = a * _sigmoid(a) * b
        g_scale, g_unscale = _row_scale(g)
        acc = jnp.dot((g * g_scale).astype(FP8), w2_ref[ft * tf:(ft + 1) * tf, :], preferred_element_type=F32)
        part = acc * g_unscale * c2_ref[...]
        y_ref[rows, :] = part if ft == 0 else y_ref[rows, :] + part

    @pl.when(valid)
    def _():
        _pipeline([(slice(p * rp, (p + 1) * rp), ft) for p in range(parts) for ft in range(n_ft)], up, down)


def _experts(xs, blk_expert, n_valid, w1, w3, w2, tm, tf=1792, parts=4):
    P, W = xs.shape
    E, _, F = w1.shape
    nblk = P // tm
    w1, c1 = _quantize_columns(w1)
    w3, c3 = _quantize_columns(w3)
    w2, c2 = _quantize_columns(w2)

    def _i(i, nv):
        return jnp.minimum(i, nv[0] - 1)

    grid_spec = pltpu.PrefetchScalarGridSpec(
        num_scalar_prefetch=2,
        grid=(nblk,),
        in_specs=[pl.BlockSpec((tm, W), lambda i, be, nv: (_i(i, nv), 0)),
                  pl.BlockSpec((None, D, F), lambda i, be, nv: (be[_i(i, nv)], 0, 0)),
                  pl.BlockSpec((None, D, F), lambda i, be, nv: (be[_i(i, nv)], 0, 0)),
                  pl.BlockSpec((None, F, D), lambda i, be, nv: (be[_i(i, nv)], 0, 0)),
                  pl.BlockSpec((None, 1, F), lambda i, be, nv: (be[_i(i, nv)], 0, 0)),
                  pl.BlockSpec((None, 1, F), lambda i, be, nv: (be[_i(i, nv)], 0, 0)),
                  pl.BlockSpec((None, 1, D), lambda i, be, nv: (be[_i(i, nv)], 0, 0))],
        out_specs=pl.BlockSpec((tm, D), lambda i, be, nv: (i, 0)),
        scratch_shapes=[pltpu.VMEM((tm, D), FP8), pltpu.VMEM((tm, LANES), F32)],
    )
    return pl.pallas_call(
        functools.partial(_experts_kernel, parts=parts, tf=tf),
        grid_spec=grid_spec,
        out_shape=jax.ShapeDtypeStruct((P, D), F32),
        compiler_params=_cp("arbitrary"),
        name="moe_experts",
    )(blk_expert, n_valid, xs, w1, w3, w2, c1, c3, c2)


def _combine_kernel(pos_ref, pos_next_ref, route_ref, x_ref, g2_ref, y_ref, out_ref, ybuf, sems, *, tc):
    i = pl.program_id(0)
    slot = jnp.bitwise_and(i, 1)

    def gather_copy(p, s, kk, r):
        return pltpu.make_async_copy(y_ref.at[pl.ds(p, 1)], ybuf.at[s, kk, pl.ds(r, 1)], sems.at[s])

    def start_gathers(table_ref, s):
        def row(r, carry):
            for kk in range(2):
                gather_copy(table_ref[0, 2 * r + kk], s, kk, r).start()
            return carry

        lax.fori_loop(0, tc, row, 0, unroll=8)

    @pl.when(i == 0)
    def _():
        start_gathers(pos_ref, 0)

    @pl.when(i + 1 < pl.num_programs(0))
    def _():
        start_gathers(pos_next_ref, 1 - slot)

    for kk in range(2):
        pltpu.make_async_copy(y_ref.at[pl.ds(0, tc)], ybuf.at[slot, kk], sems.at[slot]).wait()
    route = route_ref[...]
    moe = route[:, 4:5] * ybuf[slot, 0] + route[:, 5:6] * ybuf[slot, 1]
    out_ref[...] = x_ref[...] + g2_ref[...] * moe


def _combine(y, pos, route, x3, S, g2, tc=512):
    N = x3.shape[0]
    n_steps = N // tc
    pos_blocks = pos.reshape(n_steps, 1, 2 * tc)
    return pl.pallas_call(
        functools.partial(_combine_kernel, tc=tc),
        grid=(n_steps,),
        in_specs=[pl.BlockSpec((None, 1, 2 * tc), lambda i: (i, 0, 0), memory_space=pltpu.SMEM),
                  pl.BlockSpec((None, 1, 2 * tc), lambda i: (jnp.minimum(i + 1, n_steps - 1), 0, 0),
                               memory_space=pltpu.SMEM),
                  pl.BlockSpec((tc, LANES), lambda i: (i, 0)),
                  pl.BlockSpec((tc, D), lambda i: (i, 0)),
                  pl.BlockSpec((None, 1, D), lambda i: ((i * tc) // S, 0, 0)),
                  pl.BlockSpec(memory_space=pl.ANY)],
        out_specs=pl.BlockSpec((tc, D), lambda i: (i, 0)),
        out_shape=jax.ShapeDtypeStruct((N, D), F32),
        scratch_shapes=[pltpu.VMEM((2, 2, tc, D), F32), pltpu.SemaphoreType.DMA((2,))],
        compiler_params=_cp("arbitrary"),
        name="moe_combine",
    )(pos_blocks, pos_blocks, route, x3, g2, y)


def _moe(hp, route, counts, x3, S, g2, w1, w3, w2, tm=1024):
    N = x3.shape[0]
    cnt = counts[0, :N_EXPERTS].astype(jnp.int32)
    nblk_e = (cnt + tm - 1) // tm
    blk_end = jnp.cumsum(nblk_e)
    row_start = (blk_end - nblk_e) * tm
    e_idx = route[:, 0:2].astype(jnp.int32)
    rank = route[:, 2:4].astype(jnp.int32)
    pos = row_start[e_idx] + rank
    P = 2 * N + N_EXPERTS * tm
    nblk = P // tm
    blk_expert = jnp.minimum(
        jnp.sum((jnp.arange(nblk, dtype=jnp.int32)[:, None] >= blk_end[None, :]).astype(jnp.int32), axis=1),
        N_EXPERTS - 1)
    n_valid = blk_end[-1:].astype(jnp.int32)
    xs = _dispatch(hp, pos, P)
    y = _experts(xs, blk_expert, n_valid, w1, w3, w2, tm)
    return _combine(y, pos, route, x3, S, g2)


def kernel(x, c, ada_w, ada_b, norm1_g, norm2_g, da_w_in, da_q_gain, da_k_gain, da_lam_q1, da_lam_k1, da_lam_q2,
           da_lam_k2, da_subln_g, da_w_out, gla_w_in, gla_w_a1, gla_w_a2, gla_b_a, gla_out_g, gla_w_out,
           ffn_w1, ffn_w3, ffn_w2, moe_router, moe_w1, moe_w3, moe_w2):
    B, S, _ = x.shape
    N = B * S
    mod = _adaln(c, ada_w, ada_b)
    mods = [[mod[l, :, k * D:(k + 1) * D].reshape(B, 1, D) for k in range(6)] for l in range(2)]
    x2d = x.reshape(N, D)

    sh1, sc1, gt1, sh2, sc2, gt2 = mods[0]
    lambda_init = 0.8 - 0.6 * math.exp(-0.3 * 0)
    qk_scale = DA_HEAD_DIM ** -0.5 * LOG2E
    gain_row = jnp.concatenate([jnp.tile(da_q_gain[0].reshape(-1) * qk_scale, DA_HEADS),
                                jnp.tile(da_k_gain[0].reshape(-1), DA_HEADS),
                                jnp.ones((D,), F32)]).reshape(1, 3 * D)
    qkv = _da_inproj(x2d, S, norm1_g[0].reshape(1, D), sc1, sh1, da_w_in[0].astype(BF16), gain_row)
    slopes = 2.0 ** (-8.0 * jnp.arange(1, DA_HEADS + 1, dtype=F32) / DA_HEADS)
    o = _da_attention(qkv.reshape(B, S, 3 * D), slopes,
                      da_lam_q1[0].reshape(1, -1), da_lam_k1[0].reshape(1, -1),
                      da_lam_q2[0].reshape(1, -1), da_lam_k2[0].reshape(1, -1),
                      da_subln_g[0].reshape(1, -1), lambda_init)
    x2d = _post0(x2d, o.reshape(N, D), S, da_w_out[0].astype(BF16), gt1, norm2_g[0].reshape(1, D), sc2, sh2, gt2,
                 ffn_w1[0].astype(BF16), ffn_w3[0].astype(BF16), ffn_w2[0].astype(BF16))

    sh1, sc1, gt1, sh2, sc2, gt2 = mods[1]
    kw = GLA_HEADS * GLA_DK
    mult_row = jnp.concatenate([jnp.full((kw,), GLA_DK ** -0.5, F32), jnp.ones((3 * D - kw,), F32)]).reshape(1, -1)
    rank = gla_w_a1.shape[-1]
    wa1 = jnp.zeros((D, LANES), BF16).at[:, :rank].set(gla_w_a1[0].astype(BF16))
    wa2 = jnp.zeros((LANES, kw), BF16).at[:rank, :].set(gla_w_a2[0].astype(BF16))
    qkvr, la = _gla_inproj(x2d, S, norm1_g[1].reshape(1, D), sc1, sh1, gla_w_in[0].astype(BF16), mult_row,
                           wa1, wa2, gla_b_a[0].reshape(1, kw))
    o = _gla(qkvr.reshape(B, S, 3 * D), la.reshape(B, S, kw), gla_out_g[0].reshape(1, -1))
    router_pad = jnp.zeros((D, LANES), BF16).at[:, :N_EXPERTS].set(moe_router[0].astype(BF16))
    x3, hp, route, counts = _post1(x2d, o.reshape(N, D), S, gla_w_out[0].astype(BF16), gt1,
                                   norm2_g[1].reshape(1, D), sc2, sh2, router_pad)
    out = _moe(hp, route, counts, x3, S, gt2,
               moe_w1[0], moe_w3[0], moe_w2[0])
    return out.reshape(B, S, D)
```

```python
import functools
import math

import jax
import jax.numpy as jnp
from jax import lax
from jax.experimental import pallas as pl
from jax.experimental.pallas import tpu as pltpu

F32 = jnp.float32
BF16 = jnp.bfloat16
FP8 = jnp.float8_e4m3fn
FP8_TOP = 256.0
FP8_TINY = 1e-30

D = 1024
EPS = 1e-6
NEG = -1e30
CHUNK = 64
DA_HEADS = 8
DA_HEAD_DIM = 64
GLA_HEADS = 4
GLA_DK = 128
GLA_DV = 256
GLA_TAU = 16.0
N_EXPERTS = 8
LANES = 128
LOG2E = 1.4426950408889634
VT_ROWS = LANES + 16

VMEM_LIMIT = 56 * 1024 * 1024


def _cp(*sem):
    return pltpu.CompilerParams(dimension_semantics=sem, vmem_limit_bytes=VMEM_LIMIT)


def _sigmoid(x):
    return 1.0 / (1.0 + jnp.exp(-x))


def _modnorm(x, g, sc, sh):
    ms = jnp.mean(x * x, axis=-1, keepdims=True)
    return (x * lax.rsqrt(ms + EPS) * g) * (1.0 + sc) + sh


def _row_scale(v):
    amax = jnp.maximum(jnp.max(jnp.abs(v), axis=-1, keepdims=True), FP8_TINY)
    return FP8_TOP / amax, amax * (1.0 / FP8_TOP)


def _quantize_kernel(w_ref, q_ref, inv_ref):
    w = w_ref[...]
    amax = jnp.maximum(jnp.max(jnp.abs(w), axis=0, keepdims=True), FP8_TINY)
    q_ref[...] = (w * (FP8_TOP / amax)).astype(FP8)
    inv_ref[...] = amax * (1.0 / FP8_TOP)


def _quantize_columns(w, tn=512):
    E, K, n_cols = w.shape
    return pl.pallas_call(
        _quantize_kernel,
        grid=(E, n_cols // tn),
        in_specs=[pl.BlockSpec((None, K, tn), lambda e, n: (e, 0, n))],
        out_specs=[pl.BlockSpec((None, K, tn), lambda e, n: (e, 0, n)),
                   pl.BlockSpec((None, 1, tn), lambda e, n: (e, 0, n))],
        out_shape=[jax.ShapeDtypeStruct((E, K, n_cols), FP8), jax.ShapeDtypeStruct((E, 1, n_cols), F32)],
        compiler_params=_cp("arbitrary", "arbitrary"),
        name="quantize_columns",
    )(w)


def _adaln_kernel(c_ref, w_ref, b_ref, o_ref):
    c = c_ref[...]
    ca = (c * _sigmoid(c)).astype(BF16)
    o_ref[...] = jnp.dot(ca, w_ref[...].astype(BF16), preferred_element_type=F32) + b_ref[...]


def _adaln(c, ada_w, ada_b):
    L, _, n6 = ada_w.shape
    B = c.shape[0]
    tn = 1536
    return pl.pallas_call(
        _adaln_kernel,
        grid=(L, n6 // tn),
        in_specs=[pl.BlockSpec((B, D), lambda l, n: (0, 0)),
                  pl.BlockSpec((None, D, tn), lambda l, n: (l, 0, n)),
                  pl.BlockSpec((None, 1, tn), lambda l, n: (l, 0, n))],
        out_specs=pl.BlockSpec((None, B, tn), lambda l, n: (l, 0, n)),
        out_shape=jax.ShapeDtypeStruct((L, B, n6), F32),
        compiler_params=_cp("arbitrary", "arbitrary"),
        name="adaln",
    )(c, ada_w, ada_b.reshape(L, 1, n6))


def _pipeline(items, matmul, finish):
    prev = None
    for item in items:
        acc = matmul(item)
        if prev is not None:
            finish(*prev)
        prev = (item, acc)
    finish(*prev)


def _row_pipeline(parts, rows_total, matmul, finish):
    rp = rows_total // parts
    _pipeline([slice(p * rp, (p + 1) * rp) for p in range(parts)], matmul, finish)


def _da_inproj_kernel(x_ref, g_ref, sc_ref, sh_ref, w_ref, gain_ref, gsum_ref, o_ref, h_scr,
                      *, tn, n_qk_groups, parts):
    tm, n_out = o_ref.shape
    rp = tm // parts
    items = [(slice(p * rp, (p + 1) * rp), cg) for p in range(parts) for cg in range(n_out // tn)]

    def project(item):
        rows, cg = item
        if cg == 0:
            h = _modnorm(x_ref[rows, :], g_ref[...], sc_ref[...], sh_ref[...]).astype(BF16)
            h_scr[rows, :] = h
        else:
            h = h_scr[rows, :]
        return jnp.dot(h, w_ref[:, cg * tn:(cg + 1) * tn], preferred_element_type=F32)

    def finish(item, acc):
        rows, cg = item
        cols = slice(cg * tn, (cg + 1) * tn)
        if cg < n_qk_groups:
            y2 = (acc * acc).astype(BF16)
            gsum = gsum_ref[...]
            ms = jnp.concatenate(
                [jnp.dot(y2[:, c * 256:(c + 1) * 256], gsum, preferred_element_type=F32) for c in range(tn // 256)],
                axis=1)
            acc = acc * lax.rsqrt(ms + EPS)
        o_ref[rows, cols] = (acc * gain_ref[:, cols]).astype(BF16)

    _pipeline(items, project, finish)


def _da_inproj(x2d, S, g, sc, sh, w_bf, gain_row, tm=1024, tn=1024, parts=4):
    N = x2d.shape[0]
    n_out = w_bf.shape[1]
    r = jnp.arange(256) // DA_HEAD_DIM
    gsum = jnp.where(r[:, None] == r[None, :], 1.0 / DA_HEAD_DIM, 0.0).astype(BF16)
    vec = pl.BlockSpec((None, 1, D), lambda i: ((i * tm) // S, 0, 0))
    return pl.pallas_call(
        functools.partial(_da_inproj_kernel, tn=tn, n_qk_groups=(2 * D) // tn, parts=parts),
        grid=(N // tm,),
        in_specs=[pl.BlockSpec((tm, D), lambda i: (i, 0)),
                  pl.BlockSpec((1, D), lambda i: (0, 0)),
                  vec, vec,
                  pl.BlockSpec((D, n_out), lambda i: (0, 0)),
                  pl.BlockSpec((1, n_out), lambda i: (0, 0)),
                  pl.BlockSpec((256, 256), lambda i: (0, 0))],
        out_specs=pl.BlockSpec((tm, n_out), lambda i: (i, 0)),
        out_shape=jax.ShapeDtypeStruct((N, n_out), BF16),
        scratch_shapes=[pltpu.VMEM((tm, D), BF16)],
        compiler_params=_cp("arbitrary"),
        name="da_inproj",
    )(x2d, g, sc, sh, w_bf, gain_row, gsum)


def _da_attn_kernel(slope_ref, q_ref, k_ref, v_ref, lq1_ref, lk1_ref, lq2_ref, lk2_ref, subg_ref, o_ref,
                    k1_scr, k2_scr, vt_scr, corr_scr, aug_scr, s1e_scr, s2e_scr, s1o_scr, s2o_scr, qt1_scr, qt2_scr, a1, a2,
                    *, T, SUB, S, lambda_init):
    h = pl.program_id(0)
    slope2 = slope_ref[h] * LOG2E
    lane = lax.broadcasted_iota(jnp.int32, (T, LANES), 1)
    first_half = lane < DA_HEAD_DIM

    @pl.when(pl.program_id(1) == 0)
    def _():
        local = lax.broadcasted_iota(jnp.int32, (T, LANES), 0)
        lo = jnp.bitwise_and(local, 255).astype(F32)
        hi = (local - jnp.bitwise_and(local, 255)).astype(F32)

        def aug(base, vals):
            out = jnp.zeros((T, LANES), F32)
            for off, val in enumerate(vals):
                out = jnp.where(lane == base + off, val, out)
            return out.astype(BF16)

        c_hi = (slope2 + jnp.zeros((T, LANES), F32)).astype(BF16).astype(F32)
        c_mid = (slope2 - c_hi).astype(BF16).astype(F32)
        c_lo = slope2 - c_hi - c_mid
        k_vals = (lo, lo, lo, hi, hi, hi, -c_hi, -c_mid, -c_lo, -c_hi, -c_mid, -c_lo)
        q_vals = (c_hi, c_mid, c_lo, c_hi, c_mid, c_lo, lo, lo, lo, hi, hi, hi)
        aug_scr[0] = aug(DA_HEAD_DIM, k_vals)
        aug_scr[1] = aug(0, k_vals)
        aug_scr[2] = aug(DA_HEAD_DIM, q_vals)
        aug_scr[3] = aug(0, q_vals)
        kj = lax.broadcasted_iota(jnp.int32, (T, T), 0)
        qj = lax.broadcasted_iota(jnp.int32, (T, T), 1)
        visible = jnp.right_shift(kj, 6) <= jnp.right_shift(qj, 6)
        ahead = jnp.maximum(kj - qj, 0).astype(F32)
        corr_scr[...] = jnp.where(visible, -2.0 * slope2 * ahead, NEG)

    k_aug1, k_aug2, q_aug1, q_aug2 = aug_scr[0], aug_scr[1], aug_scr[2], aug_scr[3]
    ones_rows = jnp.where(lax.broadcasted_iota(jnp.int32, (VT_ROWS - LANES, T), 0) == 0, 1.0, 0.0).astype(BF16)

    def prep(j, carry):
        r0 = pl.multiple_of(j * T, T)
        k = k_ref[pl.ds(r0, T), :]
        k1_scr[pl.ds(r0, T), :] = jnp.where(first_half, k, k_aug1)
        k2_scr[pl.ds(r0, T), :] = jnp.where(first_half, k_aug2, k)
        vt_scr[:LANES, pl.ds(r0, T)] = v_ref[pl.ds(r0, T), :].astype(F32).T.astype(BF16)
        vt_scr[LANES:, pl.ds(r0, T)] = ones_rows
        return carry

    lax.fori_loop(0, S // T, prep, 0)

    lam =(jnp.exp(jnp.sum(lq1_ref[...] * lk1_ref[...], axis=-1, keepdims=True))
           - jnp.exp(jnp.sum(lq2_ref[...] * lk2_ref[...], axis=-1, keepdims=True)) + lambda_init)
    k_scrs = (k1_scr, k2_scr)
    s_scrs = ((s1e_scr, s2e_scr), (s1o_scr, s2o_scr))
    qt_scrs = (qt1_scr, qt2_scr)
    accs = (a1, a2)
    n_q = S // T
    TK = T // SUB
    neg = jnp.full((1, T), NEG, F32)

    def load_queries(qi):
        q = q_ref[pl.ds(pl.multiple_of(qi * T, T), T), :]
        for qt_scr, qa in zip(qt_scrs, (jnp.where(first_half, q, q_aug1), jnp.where(first_half, q_aug2, q))):
            qt_scr[...] = qa.astype(F32).T.astype(BF16)

    def block_shift(qi, j):
        return -slope2 * lax.convert_element_type((qi - j) * T, F32)

    def score_chunk(qi, par, j, part, ms, corr):
        k0 = pl.multiple_of(j * T + part * TK, TK)
        out = []
        for mp in range(2):
            s = jnp.dot(k_scrs[mp][pl.ds(k0, TK), :], qt_scrs[mp][...], preferred_element_type=F32)
            if corr is not None:
                s = s + corr[part * TK:(part + 1) * TK, :]
            s_scrs[par][mp][pl.ds(k0, TK), :] = s
            out.append(jnp.maximum(ms[mp], jnp.max(s, axis=0, keepdims=True) + block_shift(qi, j)))
        return tuple(out)

    def weight_chunk(qi, par, j, part, ms):
        k0 = pl.multiple_of(j * T + part * TK, TK)
        vt = vt_scr[:, pl.ds(k0, TK)]
        for mp in range(2):
            p = jnp.exp2(s_scrs[par][mp][pl.ds(k0, TK), :] + (block_shift(qi, j) - ms[mp]))
            accs[mp][...] += jnp.dot(vt, p.astype(BF16), preferred_element_type=F32)

    def finalize(qi):
        acc1 = a1[...]
        acc2 = a2[...]
        o = (acc1[:LANES] / acc1[LANES:LANES + 1] - lam * (acc2[:LANES] / acc2[LANES:LANES + 1])).T
        msq = jnp.mean(o * o, axis=-1, keepdims=True)
        o = o * lax.rsqrt(msq + EPS) * subg_ref[...] * (1.0 - lambda_init)
        o_ref[pl.ds(pl.multiple_of(qi * T, T), T), :] = o.astype(o_ref.dtype)
        a1[...] = jnp.zeros_like(a1)
        a2[...] = jnp.zeros_like(a2)

    a1[...] = jnp.zeros_like(a1)
    a2[...] = jnp.zeros_like(a2)
    load_queries(0)
    ms0 = (neg, neg)
    for part in range(SUB):
        ms0 = score_chunk(0, 0, 0, part, ms0, corr_scr)

    def q_block(qi, par, ms):
        load_queries(qi + 1)

        def chunk(j, ms_next):
            for part in range(SUB):
                ms_next = score_chunk(qi + 1, 1 - par, j, part, ms_next, None)
                weight_chunk(qi, par, j, part, ms)
            return ms_next

        def chunk_pair(jj, ms_next):
            return chunk(2 * jj + 1, chunk(2 * jj, ms_next))

        ms_next = lax.fori_loop(0, jnp.right_shift(qi + 1, 1), chunk_pair, (neg, neg))
        if par == 0:
            ms_next = chunk(qi, ms_next)
        for part in range(SUB):
            ms_next = score_chunk(qi + 1, 1 - par, qi + 1, part, ms_next, corr_scr)
        finalize(qi)
        return ms_next

    def q_pair(qq, ms):
        return q_block(2 * qq + 1, 1, q_block(2 * qq, 0, ms))

    ms_even = lax.fori_loop(0, n_q // 2 - 1, q_pair, ms0)
    ms_last = q_block(n_q - 2, 0, ms_even)

    group = 4 if n_q % 4 == 0 else 2

    def last_chunk_group(jj, carry):
        for j in range(group):
            for part in range(SUB):
                weight_chunk(n_q - 1, 1, group * jj + j, part, ms_last)
        return carry

    lax.fori_loop(0, n_q // group, last_chunk_group, 0)
    finalize(n_q - 1)


def _da_attention(qkv, slopes, lq1, lk1, lq2, lk2, subg, lambda_init, T=512, sub=2):
    B, S, _ = qkv.shape
    H = DA_HEADS
    vec64 = pl.BlockSpec((1, DA_HEAD_DIM), lambda h, b: (0, 0))
    return pl.pallas_call(
        functools.partial(_da_attn_kernel, T=T, SUB=sub, S=S, lambda_init=lambda_init),
        grid=(H, B),
        in_specs=[pl.BlockSpec(memory_space=pltpu.SMEM),
                  pl.BlockSpec((None, S, LANES), lambda h, b: (b, 0, h)),
                  pl.BlockSpec((None, S, LANES), lambda h, b: (b, 0, H + h)),
                  pl.BlockSpec((None, S, LANES), lambda h, b: (b, 0, 2 * H + h)),
                  vec64, vec64, vec64, vec64,
                  pl.BlockSpec((1, LANES), lambda h, b: (0, 0))],
        out_specs=pl.BlockSpec((None, S, LANES), lambda h, b: (b, 0, h)),
        out_shape=jax.ShapeDtypeStruct((B, S, H * LANES), BF16),
        scratch_shapes=[pltpu.VMEM((S, LANES), BF16), pltpu.VMEM((S, LANES), BF16), pltpu.VMEM((VT_ROWS, S), BF16),
                        pltpu.VMEM((T, T), F32), pltpu.VMEM((4, T, LANES), BF16)]
        + [pltpu.VMEM((S, T), F32) for _ in range(4)] + [
                        pltpu.VMEM((LANES, T), BF16), pltpu.VMEM((LANES, T), BF16),
                        pltpu.VMEM((VT_ROWS, T), F32), pltpu.VMEM((VT_ROWS, T), F32)],
        compiler_params=_cp("arbitrary", "arbitrary"),
        name="da_attention",
    )(slopes, qkv, qkv, qkv, lq1, lk1, lq2, lk2, subg)


def _post0_kernel(x_ref, o_ref, wo_ref, g1_ref, ng_ref, sc_ref, sh_ref, g2_ref, w1_ref, w3_ref, w2_ref,
                  out_ref, h_scr, acc_scr, *, parts, tf):
    tm = out_ref.shape[0]
    n_ft = w1_ref.shape[1] // tf
    rp = tm // parts

    def out_proj(rows):
        return jnp.dot(o_ref[rows, :], wo_ref[...], preferred_element_type=F32)

    def residual_norm(rows, y):
        x1 = x_ref[rows, :] + g1_ref[...] * y
        out_ref[rows, :] = x1
        h_scr[rows, :] = _modnorm(x1, ng_ref[...], sc_ref[...], sh_ref[...]).astype(BF16)

    _row_pipeline(parts, tm, out_proj, residual_norm)

    def up(item):
        rows, ft = item
        h = h_scr[rows, :]
        cols = slice(ft * tf, (ft + 1) * tf)
        return (jnp.dot(h, w1_ref[:, cols], preferred_element_type=F32),
                jnp.dot(h, w3_ref[:, cols], preferred_element_type=F32))

    def down(item, ab):
        rows, ft = item
        a, b = ab
        gact = (a * _sigmoid(a) * b).astype(BF16)
        part = jnp.dot(gact, w2_ref[ft * tf:(ft + 1) * tf, :], preferred_element_type=F32)
        total = part if ft == 0 else acc_scr[rows, :] + part
        if ft == n_ft - 1:
            out_ref[rows, :] = out_ref[rows, :] + g2_ref[...] * total
        else:
            acc_scr[rows, :] = total

    _pipeline([(slice(p * rp, (p + 1) * rp), ft) for p in range(parts) for ft in range(n_ft)], up, down)


def _post0(x2d, o2d, S, wo, g1, ng, sc, sh, g2, w1, w3, w2, tm=1024, tf=1408, parts=4):
    N = x2d.shape[0]
    F = w1.shape[1]
    vec = pl.BlockSpec((None, 1, D), lambda i: ((i * tm) // S, 0, 0))
    row = pl.BlockSpec((tm, D), lambda i: (i, 0))

    def resident(shape):
        return pl.BlockSpec(shape, lambda i: (0, 0), pipeline_mode=pl.Buffered(1))

    return pl.pallas_call(
        functools.partial(_post0_kernel, parts=parts, tf=tf),
        grid=(N // tm,),
        in_specs=[row, row,
                  resident((D, D)),
                  vec,
                  pl.BlockSpec((1, D), lambda i: (0, 0)),
                  vec, vec, vec,
                  resident((D, F)), resident((D, F)), resident((F, D))],
        out_specs=row,
        out_shape=jax.ShapeDtypeStruct((N, D), F32),
        scratch_shapes=[pltpu.VMEM((tm, D), BF16), pltpu.VMEM((tm, D), F32)],
        compiler_params=_cp("arbitrary"),
        name="post0_ffn",
    )(x2d, o2d, wo, g1, ng, sc, sh, g2, w1, w3, w2)


def _gla_inproj_kernel(x_ref, g_ref, sc_ref, sh_ref, w_ref, mult_ref, wa1_ref, wa2_ref, ba_ref,
                       o_ref, la_ref, h_scr, *, tn, n_plain_groups, parts):
    tm, n_out = o_ref.shape
    rp = tm // parts
    items = [(slice(p * rp, (p + 1) * rp), cg) for p in range(parts) for cg in range(n_out // tn)]

    def project(item):
        rows, cg = item
        if cg == 0:
            h = _modnorm(x_ref[rows, :], g_ref[...], sc_ref[...], sh_ref[...]).astype(BF16)
            h_scr[rows, :] = h
            low = jnp.dot(h, wa1_ref[...], preferred_element_type=F32).astype(BF16)
            z = jnp.dot(low, wa2_ref[...], preferred_element_type=F32) + ba_ref[...]
            log_sig = jnp.minimum(z, 0.0) - jnp.log(1.0 + jnp.exp(-jnp.abs(z)))
            la_ref[rows, :] = log_sig * (1.0 / GLA_TAU)
        else:
            h = h_scr[rows, :]
        return jnp.dot(h, w_ref[:, cg * tn:(cg + 1) * tn], preferred_element_type=F32)

    def finish(item, acc):
        rows, cg = item
        cols = slice(cg * tn, (cg + 1) * tn)
        if cg < n_plain_groups:
            o_ref[rows, cols] = (acc * mult_ref[:, cols]).astype(BF16)
        else:
            o_ref[rows, cols] = (acc * _sigmoid(acc)).astype(BF16)

    _pipeline(items, project, finish)


def _gla_inproj(x2d, S, g, sc, sh, w_bf, mult_row, wa1, wa2, ba, tm=1024, tn=1024, parts=4):
    N = x2d.shape[0]
    n_out = w_bf.shape[1]
    kw = wa2.shape[1]
    vec = pl.BlockSpec((None, 1, D), lambda i: ((i * tm) // S, 0, 0))
    return pl.pallas_call(
        functools.partial(_gla_inproj_kernel, tn=tn, n_plain_groups=(2 * D) // tn, parts=parts),
        grid=(N // tm,),
        in_specs=[pl.BlockSpec((tm, D), lambda i: (i, 0)),
                  pl.BlockSpec((1, D), lambda i: (0, 0)),
                  vec, vec,
                  pl.BlockSpec((D, n_out), lambda i: (0, 0)),
                  pl.BlockSpec((1, n_out), lambda i: (0, 0)),
                  pl.BlockSpec((D, LANES), lambda i: (0, 0)),
                  pl.BlockSpec((LANES, kw), lambda i: (0, 0)),
                  pl.BlockSpec((1, kw), lambda i: (0, 0))],
        out_specs=[pl.BlockSpec((tm, n_out), lambda i: (i, 0)),
                   pl.BlockSpec((tm, kw), lambda i: (i, 0))],
        out_shape=[jax.ShapeDtypeStruct((N, n_out), BF16), jax.ShapeDtypeStruct((N, kw), F32)],
        scratch_shapes=[pltpu.VMEM((tm, D), BF16)],
        compiler_params=_cp("arbitrary"),
        name="gla_inproj",
    )(x2d, g, sc, sh, w_bf, mult_row, wa1, wa2, ba)


def _gla_kernel(q_ref, k_ref, v_ref, r_ref, la_ref, tri_ref, og_ref, o_ref, state_scr, *, NB, SB):
    @pl.when(pl.program_id(1) == 0)
    def _():
        state_scr[...] = jnp.zeros_like(state_scr)

    tri = tri_ref[...]
    ti = lax.broadcasted_iota(jnp.int32, (CHUNK, CHUNK), 0)
    si = lax.broadcasted_iota(jnp.int32, (CHUNK, CHUNK), 1)
    causal = si <= ti
    og = og_ref[...]
    chains = [(nb, h) for nb in range(NB) for h in range(GLA_HEADS)]
    nt = (((1,), (1,)), ((), ()))
    tn = (((0,), (0,)), ((), ()))

    def chunk(c, carry):
        r0 = pl.multiple_of(c * CHUNK, CHUNK)
        rows = pl.ds(r0, CHUNK)
        kcol = [slice(h * GLA_DK, (h + 1) * GLA_DK) for _, h in chains]
        vcol = [slice(h * GLA_DV, (h + 1) * GLA_DV) for _, h in chains]
        las = [la_ref[nb, rows, kc] for (nb, _), kc in zip(chains, kcol)]
        his = [la.astype(BF16) for la in las]
        los = [(la - hi.astype(F32)).astype(BF16) for la, hi in zip(las, his)]
        bs = [jnp.dot(tri, hi, preferred_element_type=F32) + jnp.dot(tri, lo, preferred_element_type=F32)
              for hi, lo in zip(his, los)]
        b_lasts = [b[CHUNK - 1:CHUNK, :] for b in bs]
        qs = [q_ref[nb, rows, kc].astype(F32) for (nb, _), kc in zip(chains, kcol)]
        ks = [k_ref[nb, rows, kc].astype(F32) for (nb, _), kc in zip(chains, kcol)]
        vs = [v_ref[nb, rows, vc] for (nb, _), vc in zip(chains, vcol)]
        q_decs = [(q * jnp.exp(b)).astype(BF16) for q, b in zip(qs, bs)]
        k_invs = [(k * jnp.exp(-b)).astype(BF16) for k, b in zip(ks, bs)]
        k_decs = [(k * jnp.exp(bl - b)).astype(BF16) for k, b, bl in zip(ks, bs, b_lasts)]
        states = [state_scr[nb, h] for nb, h in chains]
        attns = [lax.dot_general(qd, ki, nt, preferred_element_type=F32) for qd, ki in zip(q_decs, k_invs)]
        o_inters = [lax.dot_general(qd, st.astype(BF16), nt, preferred_element_type=F32)
                    for qd, st in zip(q_decs, states)]
        kv_ts = [lax.dot_general(v, kd, tn, preferred_element_type=F32) for v, kd in zip(vs, k_decs)]
        attns = [jnp.where(causal, a, 0.0).astype(BF16) for a in attns]
        o_intras = [jnp.dot(a, v, preferred_element_type=F32) for a, v in zip(attns, vs)]
        for (nb, h), st, bl, kv_t, oi, oe, vc in zip(chains, states, b_lasts, kv_ts, o_intras, o_inters, vcol):
            state_scr[nb, h] = st * jnp.exp(bl) + kv_t
            o = oi + oe
            ms = jnp.mean(o * o, axis=-1, keepdims=True)
            o = o * lax.rsqrt(ms + EPS) * og * r_ref[nb, rows, vc].astype(F32)
            o_ref[nb, rows, vc] = o.astype(o_ref.dtype)
        return carry

    lax.fori_loop(0, SB // CHUNK, chunk, 0)


def _gla(qkvr, la, out_g, nb=4, sb=256):
    B, S, _ = qkvr.shape
    H = GLA_HEADS
    kw = H * GLA_DK
    vw = H * GLA_DV
    tri = (jnp.arange(CHUNK)[None, :] <= jnp.arange(CHUNK)[:, None]).astype(BF16)
    return pl.pallas_call(
        functools.partial(_gla_kernel, NB=nb, SB=sb),
        grid=(B // nb, S // sb),
        in_specs=[pl.BlockSpec((nb, sb, kw), lambda b, s: (b, s, 0)),
                  pl.BlockSpec((nb, sb, kw), lambda b, s: (b, s, 1)),
                  pl.BlockSpec((nb, sb, vw), lambda b, s: (b, s, (2 * kw) // vw)),
                  pl.BlockSpec((nb, sb, vw), lambda b, s: (b, s, (2 * kw) // vw + 1)),
                  pl.BlockSpec((nb, sb, kw), lambda b, s: (b, s, 0)),
                  pl.BlockSpec((CHUNK, CHUNK), lambda b, s: (0, 0)),
                  pl.BlockSpec((1, GLA_DV), lambda b, s: (0, 0))],
        out_specs=pl.BlockSpec((nb, sb, vw), lambda b, s: (b, s, 0)),
        out_shape=jax.ShapeDtypeStruct((B, S, vw), BF16),
        scratch_shapes=[pltpu.VMEM((nb, H, GLA_DV, GLA_DK), F32)],
        compiler_params=_cp("arbitrary", "arbitrary"),
        name="gla",
    )(qkvr, qkvr, qkvr, qkvr, la, tri, out_g)


def _post1_kernel(x_ref, o_ref, wo_ref, g1_ref, ng_ref, sc_ref, sh_ref, rw_ref, ltri_ref,
                  x3_ref, hp_ref, route_ref, cnt_ref, carry_scr, *, sub):
    i = pl.program_id(0)

    @pl.when(i == 0)
    def _():
        carry_scr[...] = jnp.zeros_like(carry_scr)

    tm = x3_ref.shape[0]
    lane = lax.broadcasted_iota(jnp.int32, (sub, LANES), 1)
    lanef = lane.astype(F32)

    def out_proj(rows):
        return jnp.dot(o_ref[rows, :], wo_ref[...], preferred_element_type=F32)

    def route_rows(rows, y):
        x3 = x_ref[rows, :] + g1_ref[...] * y
        x3_ref[rows, :] = x3
        hb = _modnorm(x3, ng_ref[...], sc_ref[...], sh_ref[...]).astype(BF16)

        half = D // 2
        lo = pltpu.bitcast(hb[:, :half].astype(F32), jnp.uint32)
        hi = pltpu.bitcast(hb[:, half:].astype(F32), jnp.uint32)
        hp_ref[rows, :] = jnp.right_shift(lo, jnp.uint32(16)) | (hi & jnp.uint32(0xFFFF0000))

        logits = jnp.dot(hb, rw_ref[...], preferred_element_type=F32)
        logits = jnp.where(lane < N_EXPERTS, logits, NEG)
        m1 = jnp.max(logits, axis=-1, keepdims=True)
        i1 = jnp.min(jnp.where(logits == m1, lanef, float(LANES)), axis=-1, keepdims=True)
        oh1 = lanef == i1
        rest = jnp.where(oh1, NEG, logits)
        m2 = jnp.max(rest, axis=-1, keepdims=True)
        i2 = jnp.min(jnp.where(rest == m2, lanef, float(LANES)), axis=-1, keepdims=True)
        oh2 = lanef == i2
        e = jnp.exp(m2 - m1)
        gate1 = 1.0 / (1.0 + e)
        gate2 = e / (1.0 + e)

        cnt = jnp.where(oh1, 1.0, 0.0) + jnp.where(oh2, 1.0, 0.0)
        carry = carry_scr[...]
        pre = jnp.dot(ltri_ref[...], cnt.astype(BF16), preferred_element_type=F32) + carry
        carry_scr[...] = carry + jnp.sum(cnt, axis=0, keepdims=True)
        r1 = jnp.sum(jnp.where(oh1, pre, 0.0), axis=-1, keepdims=True)
        r2 = jnp.sum(jnp.where(oh2, pre, 0.0), axis=-1, keepdims=True)
        route = jnp.where(lane == 0, i1, 0.0)
        for idx, val in ((1, i2), (2, r1), (3, r2), (4, gate1), (5, gate2)):
            route = jnp.where(lane == idx, val, route)
        route_ref[rows, :] = route

    _row_pipeline(tm // sub, tm, out_proj, route_rows)
    cnt_ref[...] = carry_scr[...]


def _post1(x2d, o2d, S, wo, g1, ng, sc, sh, router_pad, tm=1024, sub=256):
    N = x2d.shape[0]
    ltri = (jnp.arange(sub)[None, :] < jnp.arange(sub)[:, None]).astype(BF16)
    vec = pl.BlockSpec((None, 1, D), lambda i: ((i * tm) // S, 0, 0))
    row = pl.BlockSpec((tm, D), lambda i: (i, 0))
    return pl.pallas_call(
        functools.partial(_post1_kernel, sub=sub),
        grid=(N // tm,),
        in_specs=[row, row,
                  pl.BlockSpec((D, D), lambda i: (0, 0)),
                  vec,
                  pl.BlockSpec((1, D), lambda i: (0, 0)),
                  vec, vec,
                  pl.BlockSpec((D, LANES), lambda i: (0, 0)),
                  pl.BlockSpec((sub, sub), lambda i: (0, 0))],
        out_specs=[row,
                   pl.BlockSpec((tm, D // 2), lambda i: (i, 0)),
                   pl.BlockSpec((tm, LANES), lambda i: (i, 0)),
                   pl.BlockSpec((1, LANES), lambda i: (0, 0))],
        out_shape=[jax.ShapeDtypeStruct((N, D), F32),
                   jax.ShapeDtypeStruct((N, D // 2), jnp.uint32),
                   jax.ShapeDtypeStruct((N, LANES), F32),
                   jax.ShapeDtypeStruct((1, LANES), F32)],
        scratch_shapes=[pltpu.VMEM((1, LANES), F32)],
        compiler_params=_cp("arbitrary"),
        name="post1_router",
    )(x2d, o2d, wo, g1, ng, sc, sh, router_pad, ltri)


def _dispatch_kernel(pos_ref, h_ref, xs_in_ref, xs_ref, sem, *, td):
    del xs_in_ref

    def row(r, carry):
        for kk in range(2):
            p = pos_ref[0, 2 * r + kk]
            pltpu.make_async_copy(h_ref.at[pl.ds(r, 1)], xs_ref.at[pl.ds(p, 1)], sem).start()
        return carry

    lax.fori_loop(0, td, row, 0, unroll=8)
    for _ in range(2):
        pltpu.make_async_copy(h_ref, xs_ref.at[pl.ds(0, td)], sem).wait()


def _dispatch(hp, pos, P, td=1024):
    N, W = hp.shape
    xs0 = jnp.zeros((P, W), jnp.uint32)
    return pl.pallas_call(
        functools.partial(_dispatch_kernel, td=td),
        grid=(N // td,),
        in_specs=[pl.BlockSpec((None, 1, 2 * td), lambda i: (i, 0, 0), memory_space=pltpu.SMEM),
                  pl.BlockSpec((td, W), lambda i: (i, 0)),
                  pl.BlockSpec(memory_space=pl.ANY)],
        out_specs=pl.BlockSpec(memory_space=pl.ANY),
        out_shape=jax.ShapeDtypeStruct((P, W), jnp.uint32),
        scratch_shapes=[pltpu.SemaphoreType.DMA(())],
        input_output_aliases={2: 0},
        compiler_params=_cp("arbitrary"),
        name="moe_dispatch",
    )(pos.reshape(N // td, 1, 2 * td), hp, xs0)


def _experts_kernel(be_ref, nv_ref, xs_ref, w1_ref, w3_ref, w2_ref, c1_ref, c3_ref, c2_ref, y_ref,
                    xq_scr, xr_scr, *, parts, tf):
    i = pl.program_id(0)
    valid = i < nv_ref[0]
    tm = y_ref.shape[0]
    n_ft = w1_ref.shape[1] // tf
    rp = tm // parts
    half = D // 2

    @pl.when(jnp.logical_not(valid))
    def _():
        y_ref[...] = jnp.zeros_like(y_ref)

    def up(item):
        rows, ft = item
        if ft == 0:
            w = xs_ref[rows, :]
            lo = pltpu.bitcast(jnp.left_shift(w, jnp.uint32(16)), F32)
            hi = pltpu.bitcast(w & jnp.uint32(0xFFFF0000), F32)
            amax = jnp.maximum(jnp.maximum(jnp.max(jnp.abs(lo), axis=-1, keepdims=True),
                                           jnp.max(jnp.abs(hi), axis=-1, keepdims=True)), FP8_TINY)
            scale = FP8_TOP / amax
            xq_scr[rows, :half] = (lo * scale).astype(FP8)
            xq_scr[rows, half:] = (hi * scale).astype(FP8)
            xr_scr[rows, :] = jnp.broadcast_to(amax * (1.0 / FP8_TOP), (rp, LANES))
        xq = xq_scr[rows, :]
        cols = slice(ft * tf, (ft + 1) * tf)
        return (jnp.dot(xq, w1_ref[:, cols], preferred_element_type=F32),
                jnp.dot(xq, w3_ref[:, cols], preferred_element_type=F32))

    def down(item, ab):
        rows, ft = item
        cols = slice(ft * tf, (ft + 1) * tf)
        x_unscale = xr_scr[rows, :1]
        a = ab[0] * x_unscale * c1_ref[:, cols]
        b = ab[1] * x_unscale * c3_ref[:, cols]
        g = a * _sigmoid(a) * b
        g_scale, g_unscale = _row_scale(g)
        acc = jnp.dot((g * g_scale).astype(FP8), w2_ref[ft * tf:(ft + 1) * tf, :], preferred_element_type=F32)
        part = acc * g_unscale * c2_ref[...]
        y_ref[rows, :] = part if ft == 0 else y_ref[rows, :] + part

    @pl.when(valid)
    def _():
        _pipeline([(slice(p * rp, (p + 1) * rp), ft) for p in range(parts) for ft in range(n_ft)], up, down)


def _experts(xs, blk_expert, n_valid, w1, w3, w2, tm, tf=1792, parts=4):
    P, W = xs.shape
    E, _, F = w1.shape
    nblk = P // tm
    w1, c1 = _quantize_columns(w1, tf)
    w3, c3 = _quantize_columns(w3, tf)
    w2, c2 = _quantize_columns(w2)

    def _i(i, nv):
        return jnp.minimum(i, nv[0] - 1)

    grid_spec = pltpu.PrefetchScalarGridSpec(
        num_scalar_prefetch=2,
        grid=(nblk,),
        in_specs=[pl.BlockSpec((tm, W), lambda i, be, nv: (_i(i, nv), 0)),
                  pl.BlockSpec((None, D, F), lambda i, be, nv: (be[_i(i, nv)], 0, 0)),
                  pl.BlockSpec((None, D, F), lambda i, be, nv: (be[_i(i, nv)], 0, 0)),
                  pl.BlockSpec((None, F, D), lambda i, be, nv: (be[_i(i, nv)], 0, 0)),
                  pl.BlockSpec((None, 1, F), lambda i, be, nv: (be[_i(i, nv)], 0, 0)),
                  pl.BlockSpec((None, 1, F), lambda i, be, nv: (be[_i(i, nv)], 0, 0)),
                  pl.BlockSpec((None, 1, D), lambda i, be, nv: (be[_i(i, nv)], 0, 0))],
        out_specs=pl.BlockSpec((tm, D), lambda i, be, nv: (i, 0)),
        scratch_shapes=[pltpu.VMEM((tm, D), FP8), pltpu.VMEM((tm, LANES), F32)],
    )
    return pl.pallas_call(
        functools.partial(_experts_kernel, parts=parts, tf=tf),
        grid_spec=grid_spec,
        out_shape=jax.ShapeDtypeStruct((P, D), F32),
        compiler_params=_cp("arbitrary"),
        name="moe_experts",
    )(blk_expert, n_valid, xs, w1, w3, w2, c1, c3, c2)


def _combine_kernel(pos_ref, pos_next_ref, route_ref, x_ref, g2_ref, y_ref, out_ref, ybuf, sems, *, tc):
    i = pl.program_id(0)
    slot = jnp.bitwise_and(i, 1)

    def gather_copy(p, s, kk, r):
        return pltpu.make_async_copy(y_ref.at[pl.ds(p, 1)], ybuf.at[s, kk, pl.ds(r, 1)], sems.at[s])

    def start_gathers(table_ref, s):
        def row(r, carry):
            for kk in range(2):
                gather_copy(table_ref[0, 2 * r + kk], s, kk, r).start()
            return carry

        lax.fori_loop(0, tc, row, 0, unroll=8)

    @pl.when(i == 0)
    def _():
        start_gathers(pos_ref, 0)

    @pl.when(i + 1 < pl.num_programs(0))
    def _():
        start_gathers(pos_next_ref, 1 - slot)

    for kk in range(2):
        pltpu.make_async_copy(y_ref.at[pl.ds(0, tc)], ybuf.at[slot, kk], sems.at[slot]).wait()
    route = route_ref[...]
    moe = route[:, 4:5] * ybuf[slot, 0] + route[:, 5:6] * ybuf[slot, 1]
    out_ref[...] = x_ref[...] + g2_ref[...] * moe


def _combine(y, pos, route, x3, S, g2, tc=1024):
    N = x3.shape[0]
    n_steps = N // tc
    pos_blocks = pos.reshape(n_steps, 1, 2 * tc)
    return pl.pallas_call(
        functools.partial(_combine_kernel, tc=tc),
        grid=(n_steps,),
        in_specs=[pl.BlockSpec((None, 1, 2 * tc), lambda i: (i, 0, 0), memory_space=pltpu.SMEM),
                  pl.BlockSpec((None, 1, 2 * tc), lambda i: (jnp.minimum(i + 1, n_steps - 1), 0, 0),
                               memory_space=pltpu.SMEM),
                  pl.BlockSpec((tc, LANES), lambda i: (i, 0)),
                  pl.BlockSpec((tc, D), lambda i: (i, 0)),
                  pl.BlockSpec((None, 1, D), lambda i: ((i * tc) // S, 0, 0)),
                  pl.BlockSpec(memory_space=pl.ANY)],
        out_specs=pl.BlockSpec((tc, D), lambda i: (i, 0)),
        out_shape=jax.ShapeDtypeStruct((N, D), F32),
        scratch_shapes=[pltpu.VMEM((2, 2, tc, D), F32), pltpu.SemaphoreType.DMA((2,))],
        compiler_params=_cp("arbitrary"),
        name="moe_combine",
    )(pos_blocks, pos_blocks, route, x3, g2, y)


def _moe(hp, route, counts, x3, S, g2, w1, w3, w2, tm=1024):
    N = x3.shape[0]
    cnt = counts[0, :N_EXPERTS].astype(jnp.int32)
    nblk_e = (cnt + tm - 1) // tm
    blk_end = jnp.cumsum(nblk_e)
    row_start = (blk_end - nblk_e) * tm
    e_idx = route[:, 0:2].astype(jnp.int32)
    rank = route[:, 2:4].astype(jnp.int32)
    pos = row_start[e_idx] + rank
    P = 2 * N + N_EXPERTS * tm
    nblk = P // tm
    blk_expert = jnp.minimum(
        jnp.sum((jnp.arange(nblk, dtype=jnp.int32)[:, None] >= blk_end[None, :]).astype(jnp.int32), axis=1),
        N_EXPERTS - 1)
    n_valid = blk_end[-1:].astype(jnp.int32)
    xs = _dispatch(hp, pos, P)
    y = _experts(xs, blk_expert, n_valid, w1, w3, w2, tm)
    return _combine(y, pos, route, x3, S, g2)


def kernel(x, c, ada_w, ada_b, norm1_g, norm2_g, da_w_in, da_q_gain, da_k_gain, da_lam_q1, da_lam_k1, da_lam_q2,
           da_lam_k2, da_subln_g, da_w_out, gla_w_in, gla_w_a1, gla_w_a2, gla_b_a, gla_out_g, gla_w_out,
           ffn_w1, ffn_w3, ffn_w2, moe_router, moe_w1, moe_w3, moe_w2):
    B, S, _ = x.shape
    N = B * S
    mod = _adaln(c, ada_w, ada_b)
    mods = [[mod[l, :, k * D:(k + 1) * D].reshape(B, 1, D) for k in range(6)] for l in range(2)]
    x2d = x.reshape(N, D)

    sh1, sc1, gt1, sh2, sc2, gt2 = mods[0]
    lambda_init = 0.8 - 0.6 * math.exp(-0.3 * 0)
    qk_scale = DA_HEAD_DIM ** -0.5 * LOG2E
    gain_row = jnp.concatenate([jnp.tile(da_q_gain[0].reshape(-1) * qk_scale, DA_HEADS),
                                jnp.tile(da_k_gain[0].reshape(-1), DA_HEADS),
                                jnp.ones((D,), F32)]).reshape(1, 3 * D)
    qkv = _da_inproj(x2d, S, norm1_g[0].reshape(1, D), sc1, sh1, da_w_in[0].astype(BF16), gain_row)
    slopes = 2.0 ** (-8.0 * jnp.arange(1, DA_HEADS + 1, dtype=F32) / DA_HEADS)
    o = _da_attention(qkv.reshape(B, S, 3 * D), slopes,
                      da_lam_q1[0].reshape(1, -1), da_lam_k1[0].reshape(1, -1),
                      da_lam_q2[0].reshape(1, -1), da_lam_k2[0].reshape(1, -1),
                      da_subln_g[0].reshape(1, -1), lambda_init)
    x2d = _post0(x2d, o.reshape(N, D), S, da_w_out[0].astype(BF16), gt1, norm2_g[0].reshape(1, D), sc2, sh2, gt2,
                 ffn_w1[0].astype(BF16), ffn_w3[0].astype(BF16), ffn_w2[0].astype(BF16))

    sh1, sc1, gt1, sh2, sc2, gt2 = mods[1]
    kw = GLA_HEADS * GLA_DK
    mult_row = jnp.concatenate([jnp.full((kw,), GLA_DK ** -0.5, F32), jnp.ones((3 * D - kw,), F32)]).reshape(1, -1)
    rank = gla_w_a1.shape[-1]
    wa1 = jnp.zeros((D, LANES), BF16).at[:, :rank].set(gla_w_a1[0].astype(BF16))
    wa2 = jnp.zeros((LANES, kw), BF16).at[:rank, :].set(gla_w_a2[0].astype(BF16))
    qkvr, la = _gla_inproj(x2d, S, norm1_g[1].reshape(1, D), sc1, sh1, gla_w_in[0].astype(BF16), mult_row,
                           wa1, wa2, gla_b_a[0].reshape(1, kw))
    o = _gla(qkvr.reshape(B, S, 3 * D), la.reshape(B, S, kw), gla_out_g[0].reshape(1, -1))
    router_pad = jnp.zeros((D, LANES), BF16).at[:, :N_EXPERTS].set(moe_router[0].astype(BF16))
    x3, hp, route, counts = _post1(x2d, o.reshape(N, D), S, gla_w_out[0].astype(BF16), gt1,
                                   norm2_g[1].reshape(1, D), sc2, sh2, router_pad)
    out = _moe(hp, route, counts, x3, S, gt2,
               moe_w1[0], moe_w3[0], moe_w2[0])
    return out.reshape(B, S, D)
```

```python
import functools
import math

import jax
import jax.numpy as jnp
from jax import lax
from jax.experimental import pallas as pl
from jax.experimental.pallas import tpu as pltpu

F32 = jnp.float32
BF16 = jnp.bfloat16
FP8 = jnp.float8_e4m3fn
FP8_TOP = 256.0
FP8_TINY = 1e-30

D = 1024
EPS = 1e-6
NEG = -1e30
CHUNK = 64
DA_HEADS = 8
DA_HEAD_DIM = 64
GLA_HEADS = 4
GLA_DK = 128
GLA_DV = 256
GLA_TAU = 16.0
N_EXPERTS = 8
LANES = 128
LOG2E = 1.4426950408889634
VT_ROWS = LANES + 16

VMEM_LIMIT = 56 * 1024 * 1024


def _cp(*sem):
    return pltpu.CompilerParams(dimension_semantics=sem, vmem_limit_bytes=VMEM_LIMIT)


def _sigmoid(x):
    return 1.0 / (1.0 + jnp.exp(-x))


def _modnorm(x, g, sc, sh):
    ms = jnp.mean(x * x, axis=-1, keepdims=True)
    return (x * lax.rsqrt(ms + EPS) * g) * (1.0 + sc) + sh


def _row_scale(v):
    amax = jnp.maximum(jnp.max(jnp.abs(v), axis=-1, keepdims=True), FP8_TINY)
    return FP8_TOP / amax, amax * (1.0 / FP8_TOP)


def _quantize_kernel(w_ref, q_ref, inv_ref):
    w = w_ref[...]
    amax = jnp.maximum(jnp.max(jnp.abs(w), axis=0, keepdims=True), FP8_TINY)
    q_ref[...] = (w * (FP8_TOP / amax)).astype(FP8)
    inv_ref[...] = amax * (1.0 / FP8_TOP)


def _quantize_columns(w, tn=512):
    E, K, n_cols = w.shape
    return pl.pallas_call(
        _quantize_kernel,
        grid=(E, n_cols // tn),
        in_specs=[pl.BlockSpec((None, K, tn), lambda e, n: (e, 0, n))],
        out_specs=[pl.BlockSpec((None, K, tn), lambda e, n: (e, 0, n)),
                   pl.BlockSpec((None, 1, tn), lambda e, n: (e, 0, n))],
        out_shape=[jax.ShapeDtypeStruct((E, K, n_cols), FP8), jax.ShapeDtypeStruct((E, 1, n_cols), F32)],
        compiler_params=_cp("arbitrary", "arbitrary"),
        name="quantize_columns",
    )(w)


def _adaln_kernel(c_ref, w_ref, b_ref, o_ref):
    c = c_ref[...]
    ca = (c * _sigmoid(c)).astype(BF16)
    o_ref[...] = jnp.dot(ca, w_ref[...].astype(BF16), preferred_element_type=F32) + b_ref[...]


def _adaln(c, ada_w, ada_b):
    L, _, n6 = ada_w.shape
    B = c.shape[0]
    tn = 1536
    return pl.pallas_call(
        _adaln_kernel,
        grid=(L, n6 // tn),
        in_specs=[pl.BlockSpec((B, D), lambda l, n: (0, 0)),
                  pl.BlockSpec((None, D, tn), lambda l, n: (l, 0, n)),
                  pl.BlockSpec((None, 1, tn), lambda l, n: (l, 0, n))],
        out_specs=pl.BlockSpec((None, B, tn), lambda l, n: (l, 0, n)),
        out_shape=jax.ShapeDtypeStruct((L, B, n6), F32),
        compiler_params=_cp("arbitrary", "arbitrary"),
        name="adaln",
    )(c, ada_w, ada_b.reshape(L, 1, n6))


def _pipeline(items, matmul, finish):
    prev = None
    for item in items:
        acc = matmul(item)
        if prev is not None:
            finish(*prev)
        prev = (item, acc)
    finish(*prev)


def _row_pipeline(parts, rows_total, matmul, finish):
    rp = rows_total // parts
    _pipeline([slice(p * rp, (p + 1) * rp) for p in range(parts)], matmul, finish)


def _da_inproj_kernel(x_ref, g_ref, sc_ref, sh_ref, w_ref, gain_ref, gsum_ref, o_ref, h_scr,
                      *, tn, n_qk_groups, parts):
    tm, n_out = o_ref.shape
    rp = tm // parts
    items = [(slice(p * rp, (p + 1) * rp), cg) for p in range(parts) for cg in range(n_out // tn)]

    def project(item):
        rows, cg = item
        if cg == 0:
            h = _modnorm(x_ref[rows, :], g_ref[...], sc_ref[...], sh_ref[...]).astype(BF16)
            h_scr[rows, :] = h
        else:
            h = h_scr[rows, :]
        return jnp.dot(h, w_ref[:, cg * tn:(cg + 1) * tn], preferred_element_type=F32)

    def finish(item, acc):
        rows, cg = item
        cols = slice(cg * tn, (cg + 1) * tn)
        if cg < n_qk_groups:
            y2 = (acc * acc).astype(BF16)
            gsum = gsum_ref[...]
            ms = jnp.concatenate(
                [jnp.dot(y2[:, c * 256:(c + 1) * 256], gsum, preferred_element_type=F32) for c in range(tn // 256)],
                axis=1)
            acc = acc * lax.rsqrt(ms + EPS)
        o_ref[rows, cols] = (acc * gain_ref[:, cols]).astype(BF16)

    _pipeline(items, project, finish)


def _da_inproj(x2d, S, g, sc, sh, w_bf, gain_row, tm=1024, tn=1024, parts=4):
    N = x2d.shape[0]
    n_out = w_bf.shape[1]
    r = jnp.arange(256) // DA_HEAD_DIM
    gsum = jnp.where(r[:, None] == r[None, :], 1.0 / DA_HEAD_DIM, 0.0).astype(BF16)
    vec = pl.BlockSpec((None, 1, D), lambda i: ((i * tm) // S, 0, 0))
    return pl.pallas_call(
        functools.partial(_da_inproj_kernel, tn=tn, n_qk_groups=(2 * D) // tn, parts=parts),
        grid=(N // tm,),
        in_specs=[pl.BlockSpec((tm, D), lambda i: (i, 0)),
                  pl.BlockSpec((1, D), lambda i: (0, 0)),
                  vec, vec,
                  pl.BlockSpec((D, n_out), lambda i: (0, 0)),
                  pl.BlockSpec((1, n_out), lambda i: (0, 0)),
                  pl.BlockSpec((256, 256), lambda i: (0, 0))],
        out_specs=pl.BlockSpec((tm, n_out), lambda i: (i, 0)),
        out_shape=jax.ShapeDtypeStruct((N, n_out), BF16),
        scratch_shapes=[pltpu.VMEM((tm, D), BF16)],
        compiler_params=_cp("arbitrary"),
        name="da_inproj",
    )(x2d, g, sc, sh, w_bf, gain_row, gsum)


def _da_attn_kernel(slope_ref, q_ref, k_ref, v_ref, lq1_ref, lk1_ref, lq2_ref, lk2_ref, subg_ref, o_ref,
                    k1_scr, k2_scr, vt_scr, corr_scr, aug_scr, s1e_scr, s2e_scr, s1o_scr, s2o_scr, qt1_scr, qt2_scr, a1, a2,
                    *, T, SUB, S, lambda_init):
    h = pl.program_id(0)
    slope2 = slope_ref[h] * LOG2E
    lane = lax.broadcasted_iota(jnp.int32, (T, LANES), 1)
    first_half = lane < DA_HEAD_DIM

    @pl.when(pl.program_id(1) == 0)
    def _():
        local = lax.broadcasted_iota(jnp.int32, (T, LANES), 0)
        lo = jnp.bitwise_and(local, 255).astype(F32)
        hi = (local - jnp.bitwise_and(local, 255)).astype(F32)

        def aug(base, vals):
            out = jnp.zeros((T, LANES), F32)
            for off, val in enumerate(vals):
                out = jnp.where(lane == base + off, val, out)
            return out.astype(BF16)

        c_hi = (slope2 + jnp.zeros((T, LANES), F32)).astype(BF16).astype(F32)
        c_mid = (slope2 - c_hi).astype(BF16).astype(F32)
        c_lo = slope2 - c_hi - c_mid
        k_vals = (lo, lo, lo, hi, hi, hi, -c_hi, -c_mid, -c_lo, -c_hi, -c_mid, -c_lo)
        q_vals = (c_hi, c_mid, c_lo, c_hi, c_mid, c_lo, lo, lo, lo, hi, hi, hi)
        aug_scr[0] = aug(DA_HEAD_DIM, k_vals)
        aug_scr[1] = aug(0, k_vals)
        aug_scr[2] = aug(DA_HEAD_DIM, q_vals)
        aug_scr[3] = aug(0, q_vals)
        kj = lax.broadcasted_iota(jnp.int32, (T, T), 0)
        qj = lax.broadcasted_iota(jnp.int32, (T, T), 1)
        visible = jnp.right_shift(kj, 6) <= jnp.right_shift(qj, 6)
        ahead = jnp.maximum(kj - qj, 0).astype(F32)
        corr_scr[...] = jnp.where(visible, -2.0 * slope2 * ahead, NEG)

    k_aug1, k_aug2, q_aug1, q_aug2 = aug_scr[0], aug_scr[1], aug_scr[2], aug_scr[3]
    ones_rows = jnp.where(lax.broadcasted_iota(jnp.int32, (VT_ROWS - LANES, T), 0) == 0, 1.0, 0.0).astype(BF16)

    def prep(j, carry):
        r0 = pl.multiple_of(j * T, T)
        k = k_ref[pl.ds(r0, T), :]
        k1_scr[pl.ds(r0, T), :] = jnp.where(first_half, k, k_aug1)
        k2_scr[pl.ds(r0, T), :] = jnp.where(first_half, k_aug2, k)
        vt_scr[:LANES, pl.ds(r0, T)] = v_ref[pl.ds(r0, T), :].astype(F32).T.astype(BF16)
        vt_scr[LANES:, pl.ds(r0, T)] = ones_rows
        return carry

    lax.fori_loop(0, S // T, prep, 0)

    lam =(jnp.exp(jnp.sum(lq1_ref[...] * lk1_ref[...], axis=-1, keepdims=True))
           - jnp.exp(jnp.sum(lq2_ref[...] * lk2_ref[...], axis=-1, keepdims=True)) + lambda_init)
    k_scrs = (k1_scr, k2_scr)
    s_scrs = ((s1e_scr, s2e_scr), (s1o_scr, s2o_scr))
    qt_scrs = (qt1_scr, qt2_scr)
    accs = (a1, a2)
    n_q = S // T
    TK = T // SUB
    neg = jnp.full((1, T), NEG, F32)

    def load_queries(qi):
        q = q_ref[pl.ds(pl.multiple_of(qi * T, T), T), :]
        for qt_scr, qa in zip(qt_scrs, (jnp.where(first_half, q, q_aug1), jnp.where(first_half, q_aug2, q))):
            qt_scr[...] = qa.astype(F32).T.astype(BF16)

    def block_shift(qi, j):
        return -slope2 * lax.convert_element_type((qi - j) * T, F32)

    def score_chunk(qi, par, j, part, ms, corr):
        k0 = pl.multiple_of(j * T + part * TK, TK)
        out = []
        for mp in range(2):
            s = jnp.dot(k_scrs[mp][pl.ds(k0, TK), :], qt_scrs[mp][...], preferred_element_type=F32)
            if corr is not None:
                s = s + corr[part * TK:(part + 1) * TK, :]
            s_scrs[par][mp][pl.ds(k0, TK), :] = s
            out.append(jnp.maximum(ms[mp], jnp.max(s, axis=0, keepdims=True) + block_shift(qi, j)))
        return tuple(out)

    def weight_chunk(qi, par, j, part, ms):
        k0 = pl.multiple_of(j * T + part * TK, TK)
        vt = vt_scr[:, pl.ds(k0, TK)]
        for mp in range(2):
            p = jnp.exp2(s_scrs[par][mp][pl.ds(k0, TK), :] + (block_shift(qi, j) - ms[mp]))
            accs[mp][...] += jnp.dot(vt, p.astype(BF16), preferred_element_type=F32)

    def finalize(qi):
        acc1 = a1[...]
        acc2 = a2[...]
        o = (acc1[:LANES] / acc1[LANES:LANES + 1] - lam * (acc2[:LANES] / acc2[LANES:LANES + 1])).T
        msq = jnp.mean(o * o, axis=-1, keepdims=True)
        o = o * lax.rsqrt(msq + EPS) * subg_ref[...] * (1.0 - lambda_init)
        o_ref[pl.ds(pl.multiple_of(qi * T, T), T), :] = o.astype(o_ref.dtype)
        a1[...] = jnp.zeros_like(a1)
        a2[...] = jnp.zeros_like(a2)

    a1[...] = jnp.zeros_like(a1)
    a2[...] = jnp.zeros_like(a2)
    load_queries(0)
    ms0 = (neg, neg)
    for part in range(SUB):
        ms0 = score_chunk(0, 0, 0, part, ms0, corr_scr)

    def q_block(qi, par, ms):
        load_queries(qi + 1)

        def chunk(j, ms_next):
            for part in range(SUB):
                ms_next = score_chunk(qi + 1, 1 - par, j, part, ms_next, None)
                weight_chunk(qi, par, j, part, ms)
            return ms_next

        def chunk_pair(jj, ms_next):
            return chunk(2 * jj + 1, chunk(2 * jj, ms_next))

        ms_next = lax.fori_loop(0, jnp.right_shift(qi + 1, 1), chunk_pair, (neg, neg))
        if par == 0:
            ms_next = chunk(qi, ms_next)
        for part in range(SUB):
            ms_next = score_chunk(qi + 1, 1 - par, qi + 1, part, ms_next, corr_scr)
        finalize(qi)
        return ms_next

    def q_pair(qq, ms):
        return q_block(2 * qq + 1, 1, q_block(2 * qq, 0, ms))

    ms_even = lax.fori_loop(0, n_q // 2 - 1, q_pair, ms0)
    ms_last = q_block(n_q - 2, 0, ms_even)

    group = 4 if n_q % 4 == 0 else 2

    def last_chunk_group(jj, carry):
        for j in range(group):
            for part in range(SUB):
                weight_chunk(n_q - 1, 1, group * jj + j, part, ms_last)
        return carry

    lax.fori_loop(0, n_q // group, last_chunk_group, 0)
    finalize(n_q - 1)


def _da_attention(qkv, slopes, lq1, lk1, lq2, lk2, subg, lambda_init, T=512, sub=2):
    B, S, _ = qkv.shape
    H = DA_HEADS
    vec64 = pl.BlockSpec((1, DA_HEAD_DIM), lambda h, b: (0, 0))
    return pl.pallas_call(
        functools.partial(_da_attn_kernel, T=T, SUB=sub, S=S, lambda_init=lambda_init),
        grid=(H, B),
        in_specs=[pl.BlockSpec(memory_space=pltpu.SMEM),
                  pl.BlockSpec((None, S, LANES), lambda h, b: (b, 0, h)),
                  pl.BlockSpec((None, S, LANES), lambda h, b: (b, 0, H + h)),
                  pl.BlockSpec((None, S, LANES), lambda h, b: (b, 0, 2 * H + h)),
                  vec64, vec64, vec64, vec64,
                  pl.BlockSpec((1, LANES), lambda h, b: (0, 0))],
        out_specs=pl.BlockSpec((None, S, LANES), lambda h, b: (b, 0, h)),
        out_shape=jax.ShapeDtypeStruct((B, S, H * LANES), BF16),
        scratch_shapes=[pltpu.VMEM((S, LANES), BF16), pltpu.VMEM((S, LANES), BF16), pltpu.VMEM((VT_ROWS, S), BF16),
                        pltpu.VMEM((T, T), F32), pltpu.VMEM((4, T, LANES), BF16)]
        + [pltpu.VMEM((S, T), F32) for _ in range(4)] + [
                        pltpu.VMEM((LANES, T), BF16), pltpu.VMEM((LANES, T), BF16),
                        pltpu.VMEM((VT_ROWS, T), F32), pltpu.VMEM((VT_ROWS, T), F32)],
        compiler_params=_cp("arbitrary", "arbitrary"),
        name="da_attention",
    )(slopes, qkv, qkv, qkv, lq1, lk1, lq2, lk2, subg)


def _post0_kernel(x_ref, o_ref, wo_ref, g1_ref, ng_ref, sc_ref, sh_ref, g2_ref, w1_ref, w3_ref, w2_ref,
                  out_ref, h_scr, acc_scr, *, parts, tf):
    tm = out_ref.shape[0]
    n_ft = w1_ref.shape[1] // tf
    rp = tm // parts

    def out_proj(rows):
        return jnp.dot(o_ref[rows, :], wo_ref[...], preferred_element_type=F32)

    def residual_norm(rows, y):
        x1 = x_ref[rows, :] + g1_ref[...] * y
        out_ref[rows, :] = x1
        h_scr[rows, :] = _modnorm(x1, ng_ref[...], sc_ref[...], sh_ref[...]).astype(BF16)

    _row_pipeline(parts, tm, out_proj, residual_norm)

    def up(item):
        rows, ft = item
        h = h_scr[rows, :]
        cols = slice(ft * tf, (ft + 1) * tf)
        return (jnp.dot(h, w1_ref[:, cols], preferred_element_type=F32),
                jnp.dot(h, w3_ref[:, cols], preferred_element_type=F32))

    def down(item, ab):
        rows, ft = item
        a, b = ab
        gact = (a * _sigmoid(a) * b).astype(BF16)
        part = jnp.dot(gact, w2_ref[ft * tf:(ft + 1) * tf, :], preferred_element_type=F32)
        total = part if ft == 0 else acc_scr[rows, :] + part
        if ft == n_ft - 1:
            out_ref[rows, :] = out_ref[rows, :] + g2_ref[...] * total
        else:
            acc_scr[rows, :] = total

    _pipeline([(slice(p * rp, (p + 1) * rp), ft) for p in range(parts) for ft in range(n_ft)], up, down)


def _post0(x2d, o2d, S, wo, g1, ng, sc, sh, g2, w1, w3, w2, tm=1024, tf=1408, parts=4):
    N = x2d.shape[0]
    F = w1.shape[1]
    vec = pl.BlockSpec((None, 1, D), lambda i: ((i * tm) // S, 0, 0))
    row = pl.BlockSpec((tm, D), lambda i: (i, 0))

    def resident(shape):
        return pl.BlockSpec(shape, lambda i: (0, 0), pipeline_mode=pl.Buffered(1))

    return pl.pallas_call(
        functools.partial(_post0_kernel, parts=parts, tf=tf),
        grid=(N // tm,),
        in_specs=[row, row,
                  resident((D, D)),
                  vec,
                  pl.BlockSpec((1, D), lambda i: (0, 0)),
                  vec, vec, vec,
                  resident((D, F)), resident((D, F)), resident((F, D))],
        out_specs=row,
        out_shape=jax.ShapeDtypeStruct((N, D), F32),
        scratch_shapes=[pltpu.VMEM((tm, D), BF16), pltpu.VMEM((tm, D), F32)],
        compiler_params=_cp("arbitrary"),
        name="post0_ffn",
    )(x2d, o2d, wo, g1, ng, sc, sh, g2, w1, w3, w2)


def _gla_inproj_kernel(x_ref, g_ref, sc_ref, sh_ref, w_ref, mult_ref, wa1_ref, wa2_ref, ba_ref,
                       o_ref, la_ref, h_scr, *, tn, n_plain_groups, parts):
    tm, n_out = o_ref.shape
    rp = tm // parts
    items = [(slice(p * rp, (p + 1) * rp), cg) for p in range(parts) for cg in range(n_out // tn)]

    def project(item):
        rows, cg = item
        if cg == 0:
            h = _modnorm(x_ref[rows, :], g_ref[...], sc_ref[...], sh_ref[...]).astype(BF16)
            h_scr[rows, :] = h
            low = jnp.dot(h, wa1_ref[...], preferred_element_type=F32).astype(BF16)
            z = jnp.dot(low, wa2_ref[...], preferred_element_type=F32) + ba_ref[...]
            log_sig = jnp.minimum(z, 0.0) - jnp.log(1.0 + jnp.exp(-jnp.abs(z)))
            la_ref[rows, :] = log_sig * (1.0 / GLA_TAU)
        else:
            h = h_scr[rows, :]
        return jnp.dot(h, w_ref[:, cg * tn:(cg + 1) * tn], preferred_element_type=F32)

    def finish(item, acc):
        rows, cg = item
        cols = slice(cg * tn, (cg + 1) * tn)
        if cg < n_plain_groups:
            o_ref[rows, cols] = (acc * mult_ref[:, cols]).astype(BF16)
        else:
            o_ref[rows, cols] = (acc * _sigmoid(acc)).astype(BF16)

    _pipeline(items, project, finish)


def _gla_inproj(x2d, S, g, sc, sh, w_bf, mult_row, wa1, wa2, ba, tm=1024, tn=1024, parts=4):
    N = x2d.shape[0]
    n_out = w_bf.shape[1]
    kw = wa2.shape[1]
    vec = pl.BlockSpec((None, 1, D), lambda i: ((i * tm) // S, 0, 0))
    return pl.pallas_call(
        functools.partial(_gla_inproj_kernel, tn=tn, n_plain_groups=(2 * D) // tn, parts=parts),
        grid=(N // tm,),
        in_specs=[pl.BlockSpec((tm, D), lambda i: (i, 0)),
                  pl.BlockSpec((1, D), lambda i: (0, 0)),
                  vec, vec,
                  pl.BlockSpec((D, n_out), lambda i: (0, 0)),
                  pl.BlockSpec((1, n_out), lambda i: (0, 0)),
                  pl.BlockSpec((D, LANES), lambda i: (0, 0)),
                  pl.BlockSpec((LANES, kw), lambda i: (0, 0)),
                  pl.BlockSpec((1, kw), lambda i: (0, 0))],
        out_specs=[pl.BlockSpec((tm, n_out), lambda i: (i, 0)),
                   pl.BlockSpec((tm, kw), lambda i: (i, 0))],
        out_shape=[jax.ShapeDtypeStruct((N, n_out), BF16), jax.ShapeDtypeStruct((N, kw), F32)],
        scratch_shapes=[pltpu.VMEM((tm, D), BF16)],
        compiler_params=_cp("arbitrary"),
        name="gla_inproj",
    )(x2d, g, sc, sh, w_bf, mult_row, wa1, wa2, ba)


def _gla_kernel(q_ref, k_ref, v_ref, r_ref, la_ref, tri_ref, og_ref, o_ref, state_scr, *, NB, SB):
    @pl.when(pl.program_id(1) == 0)
    def _():
        state_scr[...] = jnp.zeros_like(state_scr)

    tri = tri_ref[...]
    ti = lax.broadcasted_iota(jnp.int32, (CHUNK, CHUNK), 0)
    si = lax.broadcasted_iota(jnp.int32, (CHUNK, CHUNK), 1)
    causal = si <= ti
    og = og_ref[...]
    chains = [(nb, h) for nb in range(NB) for h in range(GLA_HEADS)]
    nt = (((1,), (1,)), ((), ()))
    tn = (((0,), (0,)), ((), ()))

    def chunk(c, carry):
        r0 = pl.multiple_of(c * CHUNK, CHUNK)
        rows = pl.ds(r0, CHUNK)
        kcol = [slice(h * GLA_DK, (h + 1) * GLA_DK) for _, h in chains]
        vcol = [slice(h * GLA_DV, (h + 1) * GLA_DV) for _, h in chains]
        las = [la_ref[nb, rows, kc] for (nb, _), kc in zip(chains, kcol)]
        his = [la.astype(BF16) for la in las]
        los = [(la - hi.astype(F32)).astype(BF16) for la, hi in zip(las, his)]
        bs = [jnp.dot(tri, hi, preferred_element_type=F32) + jnp.dot(tri, lo, preferred_element_type=F32)
              for hi, lo in zip(his, los)]
        b_lasts = [b[CHUNK - 1:CHUNK, :] for b in bs]
        qs = [q_ref[nb, rows, kc].astype(F32) for (nb, _), kc in zip(chains, kcol)]
        ks = [k_ref[nb, rows, kc].astype(F32) for (nb, _), kc in zip(chains, kcol)]
        vs = [v_ref[nb, rows, vc] for (nb, _), vc in zip(chains, vcol)]
        q_decs = [(q * jnp.exp(b)).astype(BF16) for q, b in zip(qs, bs)]
        k_invs = [(k * jnp.exp(-b)).astype(BF16) for k, b in zip(ks, bs)]
        k_decs = [(k * jnp.exp(bl - b)).astype(BF16) for k, b, bl in zip(ks, bs, b_lasts)]
        states = [state_scr[nb, h] for nb, h in chains]
        attns = [lax.dot_general(qd, ki, nt, preferred_element_type=F32) for qd, ki in zip(q_decs, k_invs)]
        o_inters = [lax.dot_general(qd, st.astype(BF16), nt, preferred_element_type=F32)
                    for qd, st in zip(q_decs, states)]
        kv_ts = [lax.dot_general(v, kd, tn, preferred_element_type=F32) for v, kd in zip(vs, k_decs)]
        attns = [jnp.where(causal, a, 0.0).astype(BF16) for a in attns]
        o_intras = [jnp.dot(a, v, preferred_element_type=F32) for a, v in zip(attns, vs)]
        for (nb, h), st, bl, kv_t, oi, oe, vc in zip(chains, states, b_lasts, kv_ts, o_intras, o_inters, vcol):
            state_scr[nb, h] = st * jnp.exp(bl) + kv_t
            o = oi + oe
            ms = jnp.mean(o * o, axis=-1, keepdims=True)
            o = o * lax.rsqrt(ms + EPS) * og * r_ref[nb, rows, vc].astype(F32)
            o_ref[nb, rows, vc] = o.astype(o_ref.dtype)
        return carry

    lax.fori_loop(0, SB // CHUNK, chunk, 0)


def _gla(qkvr, la, out_g, nb=4, sb=256):
    B, S, _ = qkvr.shape
    H = GLA_HEADS
    kw = H * GLA_DK
    vw = H * GLA_DV
    tri = (jnp.arange(CHUNK)[None, :] <= jnp.arange(CHUNK)[:, None]).astype(BF16)
    return pl.pallas_call(
        functools.partial(_gla_kernel, NB=nb, SB=sb),
        grid=(B // nb, S // sb),
        in_specs=[pl.BlockSpec((nb, sb, kw), lambda b, s: (b, s, 0)),
                  pl.BlockSpec((nb, sb, kw), lambda b, s: (b, s, 1)),
                  pl.BlockSpec((nb, sb, vw), lambda b, s: (b, s, (2 * kw) // vw)),
                  pl.BlockSpec((nb, sb, vw), lambda b, s: (b, s, (2 * kw) // vw + 1)),
                  pl.BlockSpec((nb, sb, kw), lambda b, s: (b, s, 0)),
                  pl.BlockSpec((CHUNK, CHUNK), lambda b, s: (0, 0)),
                  pl.BlockSpec((1, GLA_DV), lambda b, s: (0, 0))],
        out_specs=pl.BlockSpec((nb, sb, vw), lambda b, s: (b, s, 0)),
        out_shape=jax.ShapeDtypeStruct((B, S, vw), BF16),
        scratch_shapes=[pltpu.VMEM((nb, H, GLA_DV, GLA_DK), F32)],
        compiler_params=_cp("arbitrary", "arbitrary"),
        name="gla",
    )(qkvr, qkvr, qkvr, qkvr, la, tri, out_g)


def _post1_kernel(x_ref, o_ref, wo_ref, g1_ref, ng_ref, sc_ref, sh_ref, rw_ref, ltri_ref,
                  x3_ref, hp_ref, route_ref, cnt_ref, carry_scr, *, sub):
    i = pl.program_id(0)

    @pl.when(i == 0)
    def _():
        carry_scr[...] = jnp.zeros_like(carry_scr)

    tm = x3_ref.shape[0]
    lane = lax.broadcasted_iota(jnp.int32, (sub, LANES), 1)
    lanef = lane.astype(F32)

    def out_proj(rows):
        return jnp.dot(o_ref[rows, :], wo_ref[...], preferred_element_type=F32)

    def route_rows(rows, y):
        x3 = x_ref[rows, :] + g1_ref[...] * y
        x3_ref[rows, :] = x3
        hb = _modnorm(x3, ng_ref[...], sc_ref[...], sh_ref[...]).astype(BF16)

        half = D // 2
        lo = pltpu.bitcast(hb[:, :half].astype(F32), jnp.uint32)
        hi = pltpu.bitcast(hb[:, half:].astype(F32), jnp.uint32)
        hp_ref[rows, :] = jnp.right_shift(lo, jnp.uint32(16)) | (hi & jnp.uint32(0xFFFF0000))

        logits = jnp.dot(hb, rw_ref[...], preferred_element_type=F32)
        logits = jnp.where(lane < N_EXPERTS, logits, NEG)
        m1 = jnp.max(logits, axis=-1, keepdims=True)
        i1 = jnp.min(jnp.where(logits == m1, lanef, float(LANES)), axis=-1, keepdims=True)
        oh1 = lanef == i1
        rest = jnp.where(oh1, NEG, logits)
        m2 = jnp.max(rest, axis=-1, keepdims=True)
        i2 = jnp.min(jnp.where(rest == m2, lanef, float(LANES)), axis=-1, keepdims=True)
        oh2 = lanef == i2
        e = jnp.exp(m2 - m1)
        gate1 = 1.0 / (1.0 + e)
        gate2 = e / (1.0 + e)

        cnt = jnp.where(oh1, 1.0, 0.0) + jnp.where(oh2, 1.0, 0.0)
        carry = carry_scr[...]
        pre = jnp.dot(ltri_ref[...], cnt.astype(BF16), preferred_element_type=F32) + carry
        carry_scr[...] = carry + jnp.sum(cnt, axis=0, keepdims=True)
        r1 = jnp.sum(jnp.where(oh1, pre, 0.0), axis=-1, keepdims=True)
        r2 = jnp.sum(jnp.where(oh2, pre, 0.0), axis=-1, keepdims=True)
        route = jnp.where(lane == 0, i1, 0.0)
        for idx, val in ((1, i2), (2, r1), (3, r2), (4, gate1), (5, gate2)):
            route = jnp.where(lane == idx, val, route)
        route_ref[rows, :] = route

    _row_pipeline(tm // sub, tm, out_proj, route_rows)
    cnt_ref[...] = carry_scr[...]


def _post1(x2d, o2d, S, wo, g1, ng, sc, sh, router_pad, tm=1024, sub=256):
    N = x2d.shape[0]
    ltri = (jnp.arange(sub)[None, :] < jnp.arange(sub)[:, None]).astype(BF16)
    vec = pl.BlockSpec((None, 1, D), lambda i: ((i * tm) // S, 0, 0))
    row = pl.BlockSpec((tm, D), lambda i: (i, 0))
    return pl.pallas_call(
        functools.partial(_post1_kernel, sub=sub),
        grid=(N // tm,),
        in_specs=[row, row,
                  pl.BlockSpec((D, D), lambda i: (0, 0)),
                  vec,
                  pl.BlockSpec((1, D), lambda i: (0, 0)),
                  vec, vec,
                  pl.BlockSpec((D, LANES), lambda i: (0, 0)),
                  pl.BlockSpec((sub, sub), lambda i: (0, 0))],
        out_specs=[row,
                   pl.BlockSpec((tm, D // 2), lambda i: (i, 0)),
                   pl.BlockSpec((tm, LANES), lambda i: (i, 0)),
                   pl.BlockSpec((1, LANES), lambda i: (0, 0))],
        out_shape=[jax.ShapeDtypeStruct((N, D), F32),
                   jax.ShapeDtypeStruct((N, D // 2), jnp.uint32),
                   jax.ShapeDtypeStruct((N, LANES), F32),
                   jax.ShapeDtypeStruct((1, LANES), F32)],
        scratch_shapes=[pltpu.VMEM((1, LANES), F32)],
        compiler_params=_cp("arbitrary"),
        name="post1_router",
    )(x2d, o2d, wo, g1, ng, sc, sh, router_pad, ltri)


def _dispatch_kernel(pos_ref, h_ref, xs_in_ref, xs_ref, sem, *, td):
    del xs_in_ref

    def row(r, carry):
        for kk in range(2):
            p = pos_ref[0, 2 * r + kk]
            pltpu.make_async_copy(h_ref.at[pl.ds(r, 1)], xs_ref.at[pl.ds(p, 1)], sem).start(priority=kk)
        return carry

    lax.fori_loop(0, td, row, 0, unroll=8)
    for _ in range(2):
        pltpu.make_async_copy(h_ref, xs_ref.at[pl.ds(0, td)], sem).wait()


def _dispatch(hp, pos, P, td=1024):
    N, W = hp.shape
    xs0 = jnp.zeros((P, W), jnp.uint32)
    return pl.pallas_call(
        functools.partial(_dispatch_kernel, td=td),
        grid=(N // td,),
        in_specs=[pl.BlockSpec((None, 1, 2 * td), lambda i: (i, 0, 0), memory_space=pltpu.SMEM),
                  pl.BlockSpec((td, W), lambda i: (i, 0)),
                  pl.BlockSpec(memory_space=pl.ANY)],
        out_specs=pl.BlockSpec(memory_space=pl.ANY),
        out_shape=jax.ShapeDtypeStruct((P, W), jnp.uint32),
        scratch_shapes=[pltpu.SemaphoreType.DMA(())],
        input_output_aliases={2: 0},
        compiler_params=_cp("arbitrary"),
        name="moe_dispatch",
    )(pos.reshape(N // td, 1, 2 * td), hp, xs0)


def _experts_kernel(be_ref, nv_ref, xs_ref, w1_ref, w3_ref, w2_ref, c1_ref, c3_ref, c2_ref, y_ref,
                    xq_scr, xr_scr, *, parts, tf):
    i = pl.program_id(0)
    valid = i < nv_ref[0]
    tm = y_ref.shape[0]
    n_ft = w1_ref.shape[1] // tf
    rp = tm // parts
    half = D // 2

    @pl.when(jnp.logical_not(valid))
    def _():
        y_ref[...] = jnp.zeros_like(y_ref)

    def up(item):
        rows, ft = item
        if ft == 0:
            w = xs_ref[rows, :]
            lo = pltpu.bitcast(jnp.left_shift(w, jnp.uint32(16)), F32)
            hi = pltpu.bitcast(w & jnp.uint32(0xFFFF0000), F32)
            amax = jnp.maximum(jnp.maximum(jnp.max(jnp.abs(lo), axis=-1, keepdims=True),
                                           jnp.max(jnp.abs(hi), axis=-1, keepdims=True)), FP8_TINY)
            scale = FP8_TOP / amax
            xq_scr[rows, :half] = (lo * scale).astype(FP8)
            xq_scr[rows, half:] = (hi * scale).astype(FP8)
            xr_scr[rows, :] = jnp.broadcast_to(amax * (1.0 / FP8_TOP), (rp, LANES))
        xq = xq_scr[rows, :]
        cols = slice(ft * tf, (ft + 1) * tf)
        return (jnp.dot(xq, w1_ref[:, cols], preferred_element_type=F32),
                jnp.dot(xq, w3_ref[:, cols], preferred_element_type=F32))

    def down(item, ab):
        rows, ft = item
        cols = slice(ft * tf, (ft + 1) * tf)
        x_unscale = xr_scr[rows, :1]
        a = ab[0] * x_unscale * c1_ref[:, cols]
        b = ab[1] * x_unscale * c3_ref[:, cols]
        g = a * _sigmoid(a) * b
        g_scale, g_unscale = _row_scale(g)
        acc = jnp.dot((g * g_scale).astype(FP8), w2_ref[ft * tf:(ft + 1) * tf, :], preferred_element_type=F32)
        part = acc * g_unscale * c2_ref[...]
        y_ref[rows, :] = part if ft == 0 else y_ref[rows, :] + part

    @pl.when(valid)
    def _():
        _pipeline([(slice(p * rp, (p + 1) * rp), ft) for p in range(parts) for ft in range(n_ft)], up, down)


def _experts(xs, blk_expert, n_valid, w1, w3, w2, tm, tf=1792, parts=4):
    P, W = xs.shape
    E, _, F = w1.shape
    nblk = P // tm
    w1, c1 = _quantize_columns(w1, tf)
    w3, c3 = _quantize_columns(w3, tf)
    w2, c2 = _quantize_columns(w2)

    def _i(i, nv):
        return jnp.minimum(i, nv[0] - 1)

    grid_spec = pltpu.PrefetchScalarGridSpec(
        num_scalar_prefetch=2,
        grid=(nblk,),
        in_specs=[pl.BlockSpec((tm, W), lambda i, be, nv: (_i(i, nv), 0)),
                  pl.BlockSpec((None, D, F), lambda i, be, nv: (be[_i(i, nv)], 0, 0)),
                  pl.BlockSpec((None, D, F), lambda i, be, nv: (be[_i(i, nv)], 0, 0)),
                  pl.BlockSpec((None, F, D), lambda i, be, nv: (be[_i(i, nv)], 0, 0)),
                  pl.BlockSpec((None, 1, F), lambda i, be, nv: (be[_i(i, nv)], 0, 0)),
                  pl.BlockSpec((None, 1, F), lambda i, be, nv: (be[_i(i, nv)], 0, 0)),
                  pl.BlockSpec((None, 1, D), lambda i, be, nv: (be[_i(i, nv)], 0, 0))],
        out_specs=pl.BlockSpec((tm, D), lambda i, be, nv: (i, 0)),
        scratch_shapes=[pltpu.VMEM((tm, D), FP8), pltpu.VMEM((tm, LANES), F32)],
    )
    return pl.pallas_call(
        functools.partial(_experts_kernel, parts=parts, tf=tf),
        grid_spec=grid_spec,
        out_shape=jax.ShapeDtypeStruct((P, D), F32),
        compiler_params=_cp("arbitrary"),
        name="moe_experts",
    )(blk_expert, n_valid, xs, w1, w3, w2, c1, c3, c2)


def _combine_kernel(pos_ref, pos_next_ref, route_ref, x_ref, g2_ref, y_ref, out_ref, ybuf, sems, *, tc):
    i = pl.program_id(0)
    slot = jnp.bitwise_and(i, 1)

    def gather_copy(p, s, kk, r):
        return pltpu.make_async_copy(y_ref.at[pl.ds(p, 1)], ybuf.at[s, kk, pl.ds(r, 1)], sems.at[s])

    def start_gathers(table_ref, s):
        def row(r, carry):
            for kk in range(2):
                gather_copy(table_ref[0, 2 * r + kk], s, kk, r).start(priority=kk)
            return carry

        lax.fori_loop(0, tc, row, 0, unroll=8)

    @pl.when(i == 0)
    def _():
        start_gathers(pos_ref, 0)

    @pl.when(i + 1 < pl.num_programs(0))
    def _():
        start_gathers(pos_next_ref, 1 - slot)

    for kk in range(2):
        pltpu.make_async_copy(y_ref.at[pl.ds(0, tc)], ybuf.at[slot, kk], sems.at[slot]).wait()
    route = route_ref[...]
    moe = route[:, 4:5] * ybuf[slot, 0] + route[:, 5:6] * ybuf[slot, 1]
    out_ref[...] = x_ref[...] + g2_ref[...] * moe


def _combine(y, pos, route, x3, S, g2, tc=1024):
    N = x3.shape[0]
    n_steps = N // tc
    pos_blocks = pos.reshape(n_steps, 1, 2 * tc)
    return pl.pallas_call(
        functools.partial(_combine_kernel, tc=tc),
        grid=(n_steps,),
        in_specs=[pl.BlockSpec((None, 1, 2 * tc), lambda i: (i, 0, 0), memory_space=pltpu.SMEM),
                  pl.BlockSpec((None, 1, 2 * tc), lambda i: (jnp.minimum(i + 1, n_steps - 1), 0, 0),
                               memory_space=pltpu.SMEM),
                  pl.BlockSpec((tc, LANES), lambda i: (i, 0)),
                  pl.BlockSpec((tc, D), lambda i: (i, 0)),
                  pl.BlockSpec((None, 1, D), lambda i: ((i * tc) // S, 0, 0)),
                  pl.BlockSpec(memory_space=pl.ANY)],
        out_specs=pl.BlockSpec((tc, D), lambda i: (i, 0)),
        out_shape=jax.ShapeDtypeStruct((N, D), F32),
        scratch_shapes=[pltpu.VMEM((2, 2, tc, D), F32), pltpu.SemaphoreType.DMA((2,))],
        compiler_params=_cp("arbitrary"),
        name="moe_combine",
    )(pos_blocks, pos_blocks, route, x3, g2, y)


def _moe(hp, route, counts, x3, S, g2, w1, w3, w2, tm=1024):
    N = x3.shape[0]
    cnt = counts[0, :N_EXPERTS].astype(jnp.int32)
    nblk_e = (cnt + tm - 1) // tm
    blk_end = jnp.cumsum(nblk_e)
    row_start = (blk_end - nblk_e) * tm
    e_idx = route[:, 0:2].astype(jnp.int32)
    rank = route[:, 2:4].astype(jnp.int32)
    pos = row_start[e_idx] + rank
    P = 2 * N + N_EXPERTS * tm
    nblk = P // tm
    blk_expert = jnp.minimum(
        jnp.sum((jnp.arange(nblk, dtype=jnp.int32)[:, None] >= blk_end[None, :]).astype(jnp.int32), axis=1),
        N_EXPERTS - 1)
    n_valid = blk_end[-1:].astype(jnp.int32)
    xs = _dispatch(hp, pos, P)
    y = _experts(xs, blk_expert, n_valid, w1, w3, w2, tm)
    return _combine(y, pos, route, x3, S, g2)


def kernel(x, c, ada_w, ada_b, norm1_g, norm2_g, da_w_in, da_q_gain, da_k_gain, da_lam_q1, da_lam_k1, da_lam_q2,
           da_lam_k2, da_subln_g, da_w_out, gla_w_in, gla_w_a1, gla_w_a2, gla_b_a, gla_out_g, gla_w_out,
           ffn_w1, ffn_w3, ffn_w2, moe_router, moe_w1, moe_w3, moe_w2):
    B, S, _ = x.shape
    N = B * S
    mod = _adaln(c, ada_w, ada_b)
    mods = [[mod[l, :, k * D:(k + 1) * D].reshape(B, 1, D) for k in range(6)] for l in range(2)]
    x2d = x.reshape(N, D)

    sh1, sc1, gt1, sh2, sc2, gt2 = mods[0]
    lambda_init = 0.8 - 0.6 * math.exp(-0.3 * 0)
    qk_scale = DA_HEAD_DIM ** -0.5 * LOG2E
    gain_row = jnp.concatenate([jnp.tile(da_q_gain[0].reshape(-1) * qk_scale, DA_HEADS),
                                jnp.tile(da_k_gain[0].reshape(-1), DA_HEADS),
                                jnp.ones((D,), F32)]).reshape(1, 3 * D)
    qkv = _da_inproj(x2d, S, norm1_g[0].reshape(1, D), sc1, sh1, da_w_in[0].astype(BF16), gain_row)
    slopes = 2.0 ** (-8.0 * jnp.arange(1, DA_HEADS + 1, dtype=F32) / DA_HEADS)
    o = _da_attention(qkv.reshape(B, S, 3 * D), slopes,
                      da_lam_q1[0].reshape(1, -1), da_lam_k1[0].reshape(1, -1),
                      da_lam_q2[0].reshape(1, -1), da_lam_k2[0].reshape(1, -1),
                      da_subln_g[0].reshape(1, -1), lambda_init)
    x2d = _post0(x2d, o.reshape(N, D), S, da_w_out[0].astype(BF16), gt1, norm2_g[0].reshape(1, D), sc2, sh2, gt2,
                 ffn_w1[0].astype(BF16), ffn_w3[0].astype(BF16), ffn_w2[0].astype(BF16))

    sh1, sc1, gt1, sh2, sc2, gt2 = mods[1]
    kw = GLA_HEADS * GLA_DK
    mult_row = jnp.concatenate([jnp.full((kw,), GLA_DK ** -0.5, F32), jnp.ones((3 * D - kw,), F32)]).reshape(1, -1)
    rank = gla_w_a1.shape[-1]
    wa1 = jnp.zeros((D, LANES), BF16).at[:, :rank].set(gla_w_a1[0].astype(BF16))
    wa2 = jnp.zeros((LANES, kw), BF16).at[:rank, :].set(gla_w_a2[0].astype(BF16))
    qkvr, la = _gla_inproj(x2d, S, norm1_g[1].reshape(1, D), sc1, sh1, gla_w_in[0].astype(BF16), mult_row,
                           wa1, wa2, gla_b_a[0].reshape(1, kw))
    o = _gla(qkvr.reshape(B, S, 3 * D), la.reshape(B, S, kw), gla_out_g[0].reshape(1, -1))
    router_pad = jnp.zeros((D, LANES), BF16).at[:, :N_EXPERTS].set(moe_router[0].astype(BF16))
    x3, hp, route, counts = _post1(x2d, o.reshape(N, D), S, gla_w_out[0].astype(BF16), gt1,
                                   norm2_g[1].reshape(1, D), sc2, sh2, router_pad)
    out = _moe(hp, route, counts, x3, S, gt2,
               moe_w1[0], moe_w3[0], moe_w2[0])
    return out.reshape(B, S, D)
```
